```python
import math
import jax, jax.numpy as jnp
from jax import lax
import numpy as np

D_MODEL = 4096
BATCH = 4
SEQ = 4096
DEPTH = 1

CHUNK = 64
Q_BLOCK = 128
NORM_EPS = 1e-6
DA_HEADS = 16
DA_HEAD_DIM = 128
DA_QK = DA_HEADS * 2 * DA_HEAD_DIM
DA_V = DA_HEADS * 2 * DA_HEAD_DIM
HG_HEADS = 16
HG_KEY = 128
HG_VAL = 128
HG_K = HG_HEADS * HG_KEY
HG_V = HG_HEADS * HG_VAL
IN_COLS = 2 * DA_QK + DA_V + 2 * HG_K + 2 * HG_V
IN_SPLITS = [DA_QK, 2 * DA_QK, 2 * DA_QK + DA_V, 2 * DA_QK + DA_V + HG_K,
             2 * DA_QK + DA_V + 2 * HG_K, 2 * DA_QK + DA_V + 2 * HG_K + HG_V]
MEM_LEN = 256
X_HEADS = 4
X_HEAD_DIM = 256
X_WIDTH = X_HEADS * X_HEAD_DIM
N_GROUPS = 8
EXPERTS_PER_GROUP = 8
N_EXPERTS = N_GROUPS * EXPERTS_PER_GROUP
EXPERT_FF = 512
TOP_K = 2
MOE_BLOCK = 128

kernel_name = 'hybrid_diffattn_hgrn2_hmoe_block'


def rmsnorm(x, g):
    xf = x.astype(jnp.float32)
    y = xf * lax.rsqrt(jnp.mean(xf * xf, axis=-1, keepdims=True) + NORM_EPS)
    return (y * g.astype(jnp.float32)).astype(x.dtype)


def diff_lambda_init(layer):
    return 0.8 - 0.6 * math.exp(-0.3 * layer)


def differential_attention(q, k, v, g_q, g_k, lam_q1, lam_k1, lam_q2, lam_k2, g_sub, lam_init):
    b, s = q.shape[0], q.shape[1]
    q = rmsnorm(q, g_q)
    k = rmsnorm(k, g_k)
    lam = (jnp.exp(jnp.sum(lam_q1.astype(jnp.float32) * lam_k1.astype(jnp.float32)))
           - jnp.exp(jnp.sum(lam_q2.astype(jnp.float32) * lam_k2.astype(jnp.float32)))
           + lam_init)
    qh = jnp.transpose(q, (0, 2, 3, 1, 4))
    kh = jnp.transpose(k, (0, 2, 3, 1, 4))
    vh = jnp.transpose(v, (0, 2, 1, 3))
    n_blk = s // Q_BLOCK
    q_blocks = jnp.moveaxis(qh.reshape(b, DA_HEADS, 2, n_blk, Q_BLOCK, DA_HEAD_DIM), 3, 0)
    key_chunk = jnp.arange(s) // CHUNK
    scale = DA_HEAD_DIM ** -0.5

    def one_block(args):
        q_blk, blk = args
        logits = jnp.einsum('bhcqd,bhckd->bhcqk', q_blk, kh).astype(jnp.float32) * scale
        query_chunk = (blk * Q_BLOCK + jnp.arange(Q_BLOCK)) // CHUNK
        visible = key_chunk[None, :] <= query_chunk[:, None]
        probs = jax.nn.softmax(jnp.where(visible, logits, -jnp.inf), axis=-1)
        weights = probs[:, :, 0] - lam * probs[:, :, 1]
        return jnp.einsum('bhqk,bhkv->bhqv', weights.astype(vh.dtype), vh)

    o = lax.map(one_block, (q_blocks, jnp.arange(n_blk)))
    o = jnp.transpose(o, (1, 0, 3, 2, 4)).reshape(b, s, DA_HEADS, 2 * DA_HEAD_DIM)
    o = rmsnorm(o, g_sub) * (1.0 - lam_init)
    return o.reshape(b, s, DA_HEADS * 2 * DA_HEAD_DIM)


def hgrn2_recurrence(q, f_pre, i, g, lower_bound, g_out):
    b, s = q.shape[0], q.shape[1]
    n_chunks = s // CHUNK
    f = lower_bound + (1.0 - lower_bound) * jax.nn.sigmoid(f_pre.astype(jnp.float32))
    key = 1.0 - f
    log_f = jnp.log(f)

    def to_chunks(t):
        return jnp.transpose(t.astype(jnp.float32).reshape(b, n_chunks, CHUNK, HG_HEADS, -1), (1, 0, 3, 2, 4))

    causal = jnp.tril(jnp.ones((CHUNK, CHUNK), dtype=bool))

    def chunk_step(state, xs):
        qc, kc, vc, lc = xs
        cum = jnp.cumsum(lc, axis=2)
        rel = cum[:, :, :, None, :] - cum[:, :, None, :, :]
        decay = jnp.exp(jnp.where(causal[:, :, None], rel, -jnp.inf))
        scores = jnp.einsum('bhtk,bhsk,bhtsk->bhts', qc, kc, decay)
        out = (jnp.einsum('bhts,bhsv->bhtv', scores, vc)
               + jnp.einsum('bhtk,bhkv->bhtv', qc * jnp.exp(cum), state))
        last = cum[:, :, -1:, :]
        new_state = (jnp.exp(last)[:, :, 0, :, None] * state
                     + jnp.einsum('bhsk,bhsv->bhkv', kc * jnp.exp(last - cum), vc))
        return new_state, out

    state0 = jnp.zeros((b, HG_HEADS, HG_KEY, HG_VAL), jnp.float32)
    _, o = lax.scan(chunk_step, state0,
                    (to_chunks(q), to_chunks(key), to_chunks(i), to_chunks(log_f)))
    o = jnp.transpose(o, (1, 0, 3, 2, 4)).reshape(b, s, HG_HEADS, HG_VAL).astype(q.dtype)
    o = rmsnorm(o, g_out) * jax.nn.silu(g)
    return o.reshape(b, s, HG_HEADS * HG_VAL)


def memory_cross_attention(n, m, w_q, w_kv, g_q, g_k, w_o):
    b, s, _ = n.shape
    q = rmsnorm((n @ w_q).reshape(b, s, X_HEADS, X_HEAD_DIM), g_q)
    kv = (m @ w_kv).reshape(b, m.shape[1], 2, X_HEADS, X_HEAD_DIM)
    k = rmsnorm(kv[:, :, 0], g_k)
    v = kv[:, :, 1]
    logits = jnp.einsum('bshd,bmhd->bhsm', q, k).astype(jnp.float32) * (X_HEAD_DIM ** -0.5)
    probs = jax.nn.softmax(logits, axis=-1)
    o = jnp.einsum('bhsm,bmhd->bshd', probs.astype(v.dtype), v).reshape(b, s, X_WIDTH)
    return o @ w_o


def hierarchical_moe(n, w_grp, b_grp, w_router, b_router, w_gate_e, w_up_e, w_down_e):
    b, s, d = n.shape
    t = b * s
    xt = n.reshape(t, d)
    grp_logits = (xt @ w_grp + b_grp).astype(jnp.float32)
    grp_prob = jax.nn.softmax(grp_logits, axis=-1)
    grp = jnp.argmax(grp_logits, axis=-1).astype(jnp.int32)
    grp_w = jnp.take_along_axis(grp_prob, grp[:, None], axis=-1)
    exp_logits = (xt @ w_router + b_router).astype(jnp.float32).reshape(t, N_GROUPS, EXPERTS_PER_GROUP)
    in_grp = jnp.take_along_axis(exp_logits, grp[:, None, None], axis=1)[:, 0]
    top_val, top_idx = lax.top_k(in_grp, TOP_K)
    gate = grp_w * jax.nn.softmax(top_val, axis=-1)
    expert = (grp[:, None] * EXPERTS_PER_GROUP + top_idx.astype(jnp.int32)).reshape(-1)
    n_assign = t * TOP_K
    tok = jnp.repeat(jnp.arange(t, dtype=jnp.int32), TOP_K)
    wgt = gate.reshape(-1)
    order = jnp.argsort(expert)
    e_sorted = expert[order]
    tok_sorted = tok[order]
    w_sorted = wgt[order]
    counts = jnp.bincount(expert, length=N_EXPERTS)
    starts = jnp.cumsum(counts) - counts
    padded = (counts + MOE_BLOCK - 1) // MOE_BLOCK * MOE_BLOCK
    pad_ends = jnp.cumsum(padded)
    pad_starts = pad_ends - padded
    dest = pad_starts[e_sorted] + jnp.arange(n_assign) - starts[e_sorted]
    cap = n_assign + N_EXPERTS * MOE_BLOCK
    n_blk = cap // MOE_BLOCK
    buf_tok = jnp.zeros((cap,), jnp.int32).at[dest].set(tok_sorted)
    buf_w = jnp.zeros((cap,), jnp.float32).at[dest].set(w_sorted)
    blk_expert = jnp.minimum(jnp.searchsorted(pad_ends, jnp.arange(n_blk) * MOE_BLOCK, side='right'),
                             N_EXPERTS - 1)

    def expert_block(args):
        idx, e = args
        xb = xt[idx]
        hid = jax.nn.silu(xb @ w_gate_e[e]) * (xb @ w_up_e[e])
        return hid @ w_down_e[e]

    ys = lax.map(expert_block, (buf_tok.reshape(n_blk, MOE_BLOCK), blk_expert))
    ys = ys.reshape(cap, d) * buf_w[:, None].astype(ys.dtype)
    out = jnp.zeros((t, d), n.dtype).at[buf_tok].add(ys)
    return out.reshape(b, s, d)


def _normal(k, shape, scale):
    return scale * jax.random.normal(k, shape, jnp.float32)


def _gain(k, shape):
    return 1.0 + 0.02 * jax.random.normal(k, shape, jnp.float32)


def setup_inputs(seed: int = 0) -> dict:
    key = jax.random.key(seed)
    ks = jax.random.split(key, 40)
    L, D = DEPTH, D_MODEL
    return {
        'x': _normal(ks[0], (BATCH, SEQ, D), 1.0),
        'mem': _normal(ks[1], (BATCH, MEM_LEN, D), 1.0),
        'g_mix': _gain(ks[2], (L, D)),
        'w_in': _normal(ks[3], (L, D, IN_COLS), D ** -0.5),
        'g_da_q': _gain(ks[4], (L, 2, DA_HEAD_DIM)),
        'g_da_k': _gain(ks[5], (L, 2, DA_HEAD_DIM)),
        'lam_q1': _normal(ks[6], (L, DA_HEAD_DIM), 0.1),
        'lam_k1': _normal(ks[7], (L, DA_HEAD_DIM), 0.1),
        'lam_q2': _normal(ks[8], (L, DA_HEAD_DIM), 0.1),
        'lam_k2': _normal(ks[9], (L, DA_HEAD_DIM), 0.1),
        'g_da_sub': _gain(ks[10], (L, 2 * DA_HEAD_DIM)),
        'hg_lower': _normal(ks[11], (L + 1, HG_K), 0.1),
        'g_hg_out': _gain(ks[12], (L, HG_VAL)),
        'w_up_a': _normal(ks[13], (L, DA_V, D), DA_V ** -0.5),
        'w_up_b': _normal(ks[14], (L, HG_V, D), HG_V ** -0.5),
        'w_gate': _normal(ks[15], (L, D, 2 * D), D ** -0.5),
        'b_gate': _normal(ks[16], (L, 2 * D), 0.02),
        'w_out': _normal(ks[17], (L, D, D), D ** -0.5),
        'g_cross': _gain(ks[18], (L, D)),
        'g_mem': _gain(ks[19], (L, D)),
        'w_xq': _normal(ks[20], (L, D, X_WIDTH), D ** -0.5),
        'w_xkv': _normal(ks[21], (L, D, 2 * X_WIDTH), D ** -0.5),
        'g_xq': _gain(ks[22], (L, X_HEAD_DIM)),
        'g_xk': _gain(ks[23], (L, X_HEAD_DIM)),
        'w_xo': _normal(ks[24], (L, X_WIDTH, D), X_WIDTH ** -0.5),
        'g_ffn': _gain(ks[25], (L, D)),
        'w_grp': _normal(ks[26], (L, D, N_GROUPS), D ** -0.5),
        'b_grp': _normal(ks[27], (L, N_GROUPS), 0.01),
        'w_erouter': _normal(ks[28], (L, D, N_EXPERTS), D ** -0.5),
        'b_erouter': _normal(ks[29], (L, N_EXPERTS), 0.01),
        'w_e_gate': _normal(ks[30], (L, N_EXPERTS, D, EXPERT_FF), D ** -0.5),
        'w_e_up': _normal(ks[31], (L, N_EXPERTS, D, EXPERT_FF), D ** -0.5),
        'w_e_down': _normal(ks[32], (L, N_EXPERTS, EXPERT_FF, D), EXPERT_FF ** -0.5),
    }


def reference(x, mem, g_mix, w_in, g_da_q, g_da_k, lam_q1, lam_k1, lam_q2, lam_k2, g_da_sub,
              hg_lower, g_hg_out, w_up_a, w_up_b, w_gate, b_gate, w_out,
              g_cross, g_mem, w_xq, w_xkv, g_xq, g_xk, w_xo,
              g_ffn, w_grp, b_grp, w_erouter, b_erouter, w_e_gate, w_e_up, w_e_down):
    b, s, _ = x.shape
    lower_bounds = jnp.cumsum(jax.nn.softmax(hg_lower.astype(jnp.float32), axis=0), axis=0)
    h = x
    for l in range(DEPTH):
        n = rmsnorm(h, g_mix[l])
        proj = n @ w_in[l]
        da_q, da_k, da_v, hg_q, hg_f, hg_i, hg_g = jnp.split(proj, IN_SPLITS, axis=-1)
        y_a = differential_attention(
            da_q.reshape(b, s, DA_HEADS, 2, DA_HEAD_DIM),
            da_k.reshape(b, s, DA_HEADS, 2, DA_HEAD_DIM),
            da_v.reshape(b, s, DA_HEADS, 2 * DA_HEAD_DIM),
            g_da_q[l], g_da_k[l], lam_q1[l], lam_k1[l], lam_q2[l], lam_k2[l], g_da_sub[l],
            diff_lambda_init(l))
        y_b = hgrn2_recurrence(
            hg_q.reshape(b, s, HG_HEADS, HG_KEY),
            hg_f.reshape(b, s, HG_HEADS, HG_KEY),
            hg_i.reshape(b, s, HG_HEADS, HG_VAL),
            hg_g.reshape(b, s, HG_HEADS, HG_VAL),
            lower_bounds[l].reshape(HG_HEADS, HG_KEY), g_hg_out[l])
        gates = jax.nn.sigmoid((n @ w_gate[l] + b_gate[l]).astype(jnp.float32)).astype(h.dtype)
        merged = (gates[..., :D_MODEL] * (y_a @ w_up_a[l])
                  + gates[..., D_MODEL:] * (y_b @ w_up_b[l]))
        h = h + merged @ w_out[l]
        h = h + memory_cross_attention(rmsnorm(h, g_cross[l]), rmsnorm(mem, g_mem[l]),
                                       w_xq[l], w_xkv[l], g_xq[l], g_xk[l], w_xo[l])
        h = h + hierarchical_moe(rmsnorm(h, g_ffn[l]), w_grp[l], b_grp[l], w_erouter[l],
                                 b_erouter[l], w_e_gate[l], w_e_up[l], w_e_down[l])
    return h
```

```python
import functools
import math

import jax
import jax.numpy as jnp
from jax import lax
from jax.experimental import pallas as pl
from jax.experimental.pallas import tpu as pltpu

F32 = jnp.float32
BF16 = jnp.bfloat16

NORM_EPS = 1e-6
DA_HEAD_DIM = 128
DA_CHUNK = 64
HG_DIM = 128
HG_CHUNK = 64
HG_SUB = 16
X_HEAD_DIM = 256
TOP_K = 2
LANES = 128
MASK_VALUE = -1e30
V7X_VMEM_BYTES = 64 * 1024 * 1024
VMEM_CAP_BYTES = V7X_VMEM_BYTES - 6 * 1024 * 1024


def _vmem_limit(block_bytes):
    return int(min(VMEM_CAP_BYTES, max(32 * 1024 * 1024, 2 * block_bytes + 16 * 1024 * 1024)))


def _params(semantics, block_bytes):
    return pltpu.CompilerParams(dimension_semantics=semantics, vmem_limit_bytes=_vmem_limit(block_bytes))


def _nbytes(shape, dtype):
    return math.prod(shape) * jnp.dtype(dtype).itemsize


def _rms_kernel(x_ref, g_ref, o_ref):
    x = x_ref[...].astype(F32)
    ms = jnp.mean(x * x, axis=-1, keepdims=True)
    o_ref[...] = (x * lax.rsqrt(ms + NORM_EPS) * g_ref[...]).astype(o_ref.dtype)


def _rmsnorm(x, g, out_dtype, name):
    m, d = x.shape
    tm = min(256, m)
    return pl.pallas_call(
        _rms_kernel,
        out_shape=jax.ShapeDtypeStruct((m, d), out_dtype),
        grid=(m // tm,),
        in_specs=[pl.BlockSpec((tm, d), lambda i: (i, 0)), pl.BlockSpec((1, d), lambda i: (0, 0))],
        out_specs=pl.BlockSpec((tm, d), lambda i: (i, 0)),
        compiler_params=_params(("parallel",), _nbytes((tm, d), x.dtype) + _nbytes((tm, d), out_dtype)),
        name=name,
    )(x, g.reshape(1, d).astype(F32))


def _ep_plain(acc, o_ref):
    o_ref[...] = acc.astype(o_ref.dtype)


def _ep_group_norm(acc, o_ref, gain, *, group):
    for c in range(acc.shape[1] // group):
        sl = slice(c * group, (c + 1) * group)
        xg = acc[:, sl]
        ms = jnp.mean(xg * xg, axis=-1, keepdims=True)
        o_ref[:, sl] = (xg * lax.rsqrt(ms + NORM_EPS) * gain[:, sl]).astype(o_ref.dtype)


def _ep_bias_sigmoid(acc, o_ref, bias):
    o_ref[...] = jax.nn.sigmoid(acc + bias).astype(o_ref.dtype)


def _ep_residual(acc, o_ref, res):
    o_ref[...] = (res + acc).astype(o_ref.dtype)


def _mm_kernel(*refs, epilogue):
    a_ref, w_ref, *extra, o_ref = refs
    acc = jnp.dot(a_ref[...], w_ref[...], preferred_element_type=F32)
    epilogue(acc, o_ref, *[e[...] for e in extra])


def _matmul(a, w, *, col0=0, ncols=None, out_dtype, epilogue=_ep_plain, col_extras=(), full_extras=(), name):
    m, k = a.shape
    ncols = w.shape[1] - col0 if ncols is None else ncols
    tm = min(1024, m)
    tn = min(512, ncols)
    assert m % tm == 0 and ncols % tn == 0 and col0 % tn == 0
    in_specs = [pl.BlockSpec((tm, k), lambda j, i: (i, 0)),
                pl.BlockSpec((k, tn), lambda j, i, o=col0 // tn: (0, j + o))]
    operands = [a, w]
    block_bytes = _nbytes((tm, k), a.dtype) + _nbytes((k, tn), w.dtype) + _nbytes((tm, tn), out_dtype)
    for vec, c0 in col_extras:
        assert c0 % tn == 0
        in_specs.append(pl.BlockSpec((1, tn), lambda j, i, o=c0 // tn: (0, j + o)))
        operands.append(vec)
    for arr, c0 in full_extras:
        assert c0 % tn == 0
        in_specs.append(pl.BlockSpec((tm, tn), lambda j, i, o=c0 // tn: (i, j + o)))
        operands.append(arr)
        block_bytes += _nbytes((tm, tn), arr.dtype)
    return pl.pallas_call(
        functools.partial(_mm_kernel, epilogue=epilogue),
        out_shape=jax.ShapeDtypeStruct((m, ncols), out_dtype),
        grid=(ncols // tn, m // tm),
        in_specs=in_specs,
        out_specs=pl.BlockSpec((tm, tn), lambda j, i: (i, j)),
        compiler_params=_params(("parallel", "parallel"), block_bytes),
        name=name,
    )(*operands)


def _merge_kernel(ya_ref, wa_ref, yb_ref, wb_ref, ga_ref, gb_ref, o_ref):
    pa = jnp.dot(ya_ref[...], wa_ref[...], preferred_element_type=F32)
    pb = jnp.dot(yb_ref[...], wb_ref[...], preferred_element_type=F32)
    o_ref[...] = (ga_ref[...].astype(F32) * pa + gb_ref[...].astype(F32) * pb).astype(o_ref.dtype)


def _merge(ya, wa, yb, wb, gates, name):
    m, ka = ya.shape
    kb = yb.shape[1]
    d = wa.shape[1]
    tm = min(1024, m)
    tn = min(512, d)
    nb = d // tn
    block_bytes = (_nbytes((tm, ka), BF16) + _nbytes((ka, tn), BF16) + _nbytes((tm, kb), BF16)
                   + _nbytes((kb, tn), BF16) + 3 * _nbytes((tm, tn), BF16))
    return pl.pallas_call(
        _merge_kernel,
        out_shape=jax.ShapeDtypeStruct((m, d), BF16),
        grid=(nb, m // tm),
        in_specs=[pl.BlockSpec((tm, ka), lambda j, i: (i, 0)),
                  pl.BlockSpec((ka, tn), lambda j, i: (0, j)),
                  pl.BlockSpec((tm, kb), lambda j, i: (i, 0)),
                  pl.BlockSpec((kb, tn), lambda j, i: (0, j)),
                  pl.BlockSpec((tm, tn), lambda j, i: (i, j)),
                  pl.BlockSpec((tm, tn), lambda j, i: (i, j + nb))],
        out_specs=pl.BlockSpec((tm, tn), lambda j, i: (i, j)),
        compiler_params=_params(("parallel", "parallel"), block_bytes),
        name=name,
    )(ya, wa, yb, wb, gates, gates)


def _da_kernel(lamv_ref, gsub_ref, q_ref, k_ref, v_ref, o_ref, acc_ref, m_ref, l_ref, *, tq, lam_init):
    qi = pl.program_id(2)
    hd = DA_HEAD_DIM
    m_ref[...] = jnp.full(m_ref.shape, MASK_VALUE, F32)
    l_ref[...] = jnp.zeros(l_ref.shape, F32)
    acc_ref[...] = jnp.zeros(acc_ref.shape, F32)
    q = q_ref[0]
    row_chunk = lax.broadcasted_iota(jnp.int32, (tq, tq), 0) // DA_CHUNK
    col_chunk = lax.broadcasted_iota(jnp.int32, (tq, tq), 1) // DA_CHUNK
    visible = col_chunk <= row_chunk

    def update(j, masked):
        off = pl.multiple_of(j * tq, tq)
        kk = k_ref[0, pl.ds(off, tq), :]
        vv = v_ref[0, pl.ds(off, tq), :]
        for c in range(2):
            s = lax.dot_general(q[:, c * hd:(c + 1) * hd], kk[:, c * hd:(c + 1) * hd],
                                (((1,), (1,)), ((), ())), preferred_element_type=F32)
            if masked:
                s = jnp.where(visible, s, MASK_VALUE)
            m_old = m_ref[c]
            m_new = jnp.maximum(m_old, jnp.max(s, axis=-1, keepdims=True))
            p = jnp.exp(s - m_new)
            alpha = jnp.exp(m_old - m_new)
            l_ref[c] = alpha * l_ref[c] + jnp.sum(p, axis=-1, keepdims=True)
            acc_ref[c] = alpha * acc_ref[c] + jnp.dot(p.astype(BF16), vv, preferred_element_type=F32)
            m_ref[c] = m_new

    def body(j, carry):
        update(j, False)
        return carry

    lax.fori_loop(0, qi, body, 0)
    update(qi, True)

    lv = lamv_ref[...]
    lam = (jnp.exp(jnp.sum(lv[0:1] * lv[1:2], axis=-1, keepdims=True))
           - jnp.exp(jnp.sum(lv[2:3] * lv[3:4], axis=-1, keepdims=True)) + lam_init)
    o = acc_ref[0] / l_ref[0] - lam * (acc_ref[1] / l_ref[1])
    ms = jnp.mean(o * o, axis=-1, keepdims=True)
    o_ref[0] = ((o * lax.rsqrt(ms + NORM_EPS) * gsub_ref[...]) * (1.0 - lam_init)).astype(o_ref.dtype)


def _diff_attention(qk, v, lamv, g_sub, n_heads, lam_init, name):
    b, s, _ = v.shape
    hw = 2 * DA_HEAD_DIM
    tq = min(512, s)
    block_bytes = 2 * _nbytes((tq, hw), BF16) + 2 * _nbytes((s, hw), BF16) + 6 * _nbytes((tq, tq), F32)
    return pl.pallas_call(
        functools.partial(_da_kernel, tq=tq, lam_init=lam_init),
        out_shape=jax.ShapeDtypeStruct((b, s, n_heads * hw), BF16),
        grid=(b, n_heads, s // tq),
        in_specs=[pl.BlockSpec((4, DA_HEAD_DIM), lambda bi, h, i: (0, 0)),
                  pl.BlockSpec((1, hw), lambda bi, h, i: (0, 0)),
                  pl.BlockSpec((1, tq, hw), lambda bi, h, i: (bi, i, h)),
                  pl.BlockSpec((1, s, hw), lambda bi, h, i: (bi, 0, n_heads + h)),
                  pl.BlockSpec((1, s, hw), lambda bi, h, i: (bi, 0, h))],
        out_specs=pl.BlockSpec((1, tq, hw), lambda bi, h, i: (bi, i, h)),
        scratch_shapes=[pltpu.VMEM((2, tq, hw), F32), pltpu.VMEM((2, tq, 1), F32), pltpu.VMEM((2, tq, 1), F32)],
        compiler_params=_params(("parallel", "parallel", "arbitrary"), block_bytes),
        name=name,
    )(lamv, g_sub, qk, qk, v)


def _hg_kernel(lb_ref, gout_ref, q_ref, f_ref, i_ref, g_ref, o_ref, st_ref, *, heads, steps):
    t = pl.program_id(2)
    ch, sub, hd = HG_CHUNK, HG_SUB, HG_DIM

    @pl.when(t == 0)
    def _():
        st_ref[...] = jnp.zeros(st_ref.shape, F32)

    r = lax.broadcasted_iota(jnp.int32, (ch, ch), 0)
    c = lax.broadcasted_iota(jnp.int32, (ch, ch), 1)
    causal = c <= r
    cum_mat = jnp.concatenate([causal.astype(BF16), (c < (r // sub) * sub).astype(BF16)], axis=0)
    row = lax.broadcasted_iota(jnp.int32, (ch, hd), 0)
    nt = (((1,), (1,)), ((), ()))
    tn = (((0,), (0,)), ((), ()))

    def chunk(ci, carry):
        off = pl.multiple_of(ci * ch, ch)
        for h in range(heads):
            sl = slice(h * hd, (h + 1) * hd)
            q = q_ref[0, pl.ds(off, ch), sl]
            lb = lb_ref[:, sl]
            f = lb + (1.0 - lb) * jax.nn.sigmoid(f_ref[0, pl.ds(off, ch), sl])
            key = 1.0 - f
            lf = jnp.log(f)
            lf0 = lf.astype(BF16)
            r1 = lf - lf0.astype(F32)
            lf1 = r1.astype(BF16)
            lf2 = (r1 - lf1.astype(F32)).astype(BF16)
            cums = (jnp.dot(cum_mat, lf0, preferred_element_type=F32)
                    + jnp.dot(cum_mat, lf1, preferred_element_type=F32)
                    + jnp.dot(cum_mat, lf2, preferred_element_type=F32))
            cum = cums[:ch]
            cstart = cums[ch:]
            last = cum[ch - 1:ch]
            q_state = (q * jnp.exp(cum)).astype(BF16)
            q_blk = (q * jnp.exp(cum - cstart)).astype(BF16)
            k_state = (key * jnp.exp(last - cum)).astype(BF16)
            vals = i_ref[0, pl.ds(off, ch), sl].astype(BF16)
            parts = []
            for a in range(ch // sub):
                ref_a = cstart[a * sub:a * sub + 1]
                expo = jnp.where(row < (a + 1) * sub, ref_a - cum, MASK_VALUE)
                k_a = (key * jnp.exp(expo)).astype(BF16)
                parts.append(lax.dot_general(q_blk[a * sub:(a + 1) * sub], k_a, nt, preferred_element_type=F32))
            scores = jnp.where(causal, jnp.concatenate(parts, axis=0), 0.0).astype(BF16)
            st = st_ref[h]
            o = (jnp.dot(scores, vals, preferred_element_type=F32)
                 + lax.dot_general(q_state, st.astype(BF16), nt, preferred_element_type=F32))
            st_ref[h] = jnp.exp(last) * st + lax.dot_general(vals, k_state, tn, preferred_element_type=F32)
            ms = jnp.mean(o * o, axis=-1, keepdims=True)
            gate = g_ref[0, pl.ds(off, ch), sl]
            y = (o * lax.rsqrt(ms + NORM_EPS) * gout_ref[...]) * (gate * jax.nn.sigmoid(gate))
            o_ref[0, pl.ds(off, ch), sl] = y.astype(o_ref.dtype)
        return carry

    lax.fori_loop(0, steps // ch, chunk, 0)


def _hgrn2(hg, lower, g_out, n_heads, name):
    b, s, _ = hg.shape
    heads = min(2, n_heads)
    steps = min(512, s)
    nhb = n_heads // heads
    w = heads * HG_DIM
    block_bytes = 4 * _nbytes((steps, w), F32) + _nbytes((steps, w), BF16)

    def spec(part):
        return pl.BlockSpec((1, steps, w), lambda bi, h, t, p=part: (bi, t, p * nhb + h))

    return pl.pallas_call(
        functools.partial(_hg_kernel, heads=heads, steps=steps),
        out_shape=jax.ShapeDtypeStruct((b, s, n_heads * HG_DIM), BF16),
        grid=(b, nhb, s // steps),
        in_specs=[pl.BlockSpec((1, w), lambda bi, h, t: (0, h)),
                  pl.BlockSpec((1, HG_DIM), lambda bi, h, t: (0, 0)),
                  spec(0), spec(1), spec(2), spec(3)],
        out_specs=pl.BlockSpec((1, steps, w), lambda bi, h, t: (bi, t, h)),
        scratch_shapes=[pltpu.VMEM((heads, HG_DIM, HG_DIM), F32)],
        compiler_params=_params(("parallel", "parallel", "arbitrary"), block_bytes),
        name=name,
    )(lower, g_out, hg, hg, hg, hg)


def _xattn_kernel(q_ref, k_ref, v_ref, o_ref, *, n_heads):
    hd = X_HEAD_DIM
    for h in range(n_heads):
        sl = slice(h * hd, (h + 1) * hd)
        s = lax.dot_general(q_ref[0, :, sl], k_ref[0, :, sl], (((1,), (1,)), ((), ())),
                            preferred_element_type=F32)
        p = jnp.exp(s - jnp.max(s, axis=-1, keepdims=True))
        denom = jnp.sum(p, axis=-1, keepdims=True)
        o = jnp.dot(p.astype(BF16), v_ref[0, :, sl], preferred_element_type=F32)
        o_ref[0, :, sl] = (o / denom).astype(o_ref.dtype)


def _cross_attention(q, k, v, n_heads, name):
    b, s, w = q.shape
    mlen = k.shape[1]
    tq = min(1024, s)
    block_bytes = 2 * _nbytes((tq, w), BF16) + 2 * _nbytes((mlen, w), BF16) + 3 * _nbytes((tq, mlen), F32)
    return pl.pallas_call(
        functools.partial(_xattn_kernel, n_heads=n_heads),
        out_shape=jax.ShapeDtypeStruct((b, s, w), BF16),
        grid=(b, s // tq),
        in_specs=[pl.BlockSpec((1, tq, w), lambda bi, i: (bi, i, 0)),
                  pl.BlockSpec((1, mlen, w), lambda bi, i: (bi, 0, 0)),
                  pl.BlockSpec((1, mlen, w), lambda bi, i: (bi, 0, 0))],
        out_specs=pl.BlockSpec((1, tq, w), lambda bi, i: (bi, i, 0)),
        compiler_params=_params(("parallel", "parallel"), block_bytes),
        name=name,
    )(q, k, v)


def _router_kernel(h_ref, g_ref, whi_ref, wlo_ref, b_ref, n_ref, lg_ref):
    x = h_ref[...]
    ms = jnp.mean(x * x, axis=-1, keepdims=True)
    n = x * lax.rsqrt(ms + NORM_EPS) * g_ref[...]
    n_hi = n.astype(BF16)
    n_lo = (n - n_hi.astype(F32)).astype(BF16)
    n_ref[...] = n_hi
    lg_ref[...] = (jnp.dot(n_hi, whi_ref[...], preferred_element_type=F32)
                   + jnp.dot(n_lo, whi_ref[...], preferred_element_type=F32)
                   + jnp.dot(n_hi, wlo_ref[...], preferred_element_type=F32) + b_ref[...])


def _router(h, g, w_hi, w_lo, bias, name):
    m, d = h.shape
    nl = w_hi.shape[1]
    tm = min(256, m)
    block_bytes = _nbytes((tm, d), F32) + _nbytes((tm, d), BF16) + 2 * _nbytes((d, nl), BF16)
    return pl.pallas_call(
        _router_kernel,
        out_shape=(jax.ShapeDtypeStruct((m, d), BF16), jax.ShapeDtypeStruct((m, nl), F32)),
        grid=(m // tm,),
        in_specs=[pl.BlockSpec((tm, d), lambda i: (i, 0)),
                  pl.BlockSpec((1, d), lambda i: (0, 0)),
                  pl.BlockSpec((d, nl), lambda i: (0, 0)),
                  pl.BlockSpec((d, nl), lambda i: (0, 0)),
                  pl.BlockSpec((1, nl), lambda i: (0, 0))],
        out_specs=(pl.BlockSpec((tm, d), lambda i: (i, 0)), pl.BlockSpec((tm, nl), lambda i: (i, 0))),
        compiler_params=_params(("parallel",), block_bytes),
        name=name,
    )(h, g, w_hi, w_lo, bias)


def _gather_kernel(idx_ref, src_ref, out_ref, sem, *, rows):
    base = pl.program_id(0) * rows

    def start(r, carry):
        pltpu.make_async_copy(src_ref.at[idx_ref[base + r]], out_ref.at[base + r], sem).start()
        return carry

    def wait(r, carry):
        pltpu.make_async_copy(src_ref.at[idx_ref[base + r]], out_ref.at[base + r], sem).wait()
        return carry

    lax.fori_loop(0, rows, start, 0)
    lax.fori_loop(0, rows, wait, 0)


def _gather_rows(src, idx, name):
    n_out = idx.shape[0]
    rows = min(1024, n_out)
    assert n_out % rows == 0
    return pl.pallas_call(
        functools.partial(_gather_kernel, rows=rows),
        out_shape=jax.ShapeDtypeStruct((n_out,) + src.shape[1:], src.dtype),
        grid_spec=pltpu.PrefetchScalarGridSpec(
            num_scalar_prefetch=1,
            grid=(n_out // rows,),
            in_specs=[pl.BlockSpec(memory_space=pl.ANY)],
            out_specs=pl.BlockSpec(memory_space=pl.ANY),
            scratch_shapes=[pltpu.SemaphoreType.DMA(())]),
        compiler_params=pltpu.CompilerParams(dimension_semantics=("arbitrary",)),
        name=name,
    )(idx, src)


def _expert_kernel(be_ref, nreal_ref, x_ref, rw_ref, wg_ref, wu_ref, wd_ref, o_ref):
    i = pl.program_id(0)

    @pl.when(i < nreal_ref[0])
    def _():
        x = x_ref[...]
        hg = jnp.dot(x, wg_ref[0], preferred_element_type=F32)
        hu = jnp.dot(x, wu_ref[0], preferred_element_type=F32)
        hid = (hg * jax.nn.sigmoid(hg) * hu).astype(BF16)
        y = jnp.dot(hid, wd_ref[0], preferred_element_type=F32)
        o_ref[...] = (y * rw_ref[...]).astype(o_ref.dtype)

    @pl.when(i >= nreal_ref[0])
    def _():
        o_ref[...] = jnp.zeros(o_ref.shape, o_ref.dtype)


def _experts(xs, row_w, blk_expert, n_real, wg, wu, wd, blk, name):
    cap, d = xs.shape
    ff = wg.shape[2]
    block_bytes = 2 * _nbytes((blk, d), BF16) + 3 * _nbytes((d, ff), BF16) + 2 * _nbytes((blk, ff), F32)
    return pl.pallas_call(
        _expert_kernel,
        out_shape=jax.ShapeDtypeStruct((cap, d), BF16),
        grid_spec=pltpu.PrefetchScalarGridSpec(
            num_scalar_prefetch=2,
            grid=(cap // blk,),
            in_specs=[pl.BlockSpec((blk, d), lambda i, be, nr: (i, 0)),
                      pl.BlockSpec((blk, 1), lambda i, be, nr: (i, 0)),
                      pl.BlockSpec((1, d, ff), lambda i, be, nr: (be[i], 0, 0)),
                      pl.BlockSpec((1, d, ff), lambda i, be, nr: (be[i], 0, 0)),
                      pl.BlockSpec((1, ff, d), lambda i, be, nr: (be[i], 0, 0))],
            out_specs=pl.BlockSpec((blk, d), lambda i, be, nr: (i, 0))),
        compiler_params=_params(("arbitrary",), block_bytes),
        name=name,
    )(blk_expert, n_real, xs, row_w, wg, wu, wd)


def _combine_kernel(h_ref, y_ref, o_ref):
    d = h_ref.shape[1]
    o_ref[...] = h_ref[...] + y_ref[:, :d].astype(F32) + y_ref[:, d:].astype(F32)


def _combine(h, pairs, name):
    m, d = h.shape
    tm = min(256, m)
    block_bytes = 2 * _nbytes((tm, d), F32) + _nbytes((tm, 2 * d), BF16)
    return pl.pallas_call(
        _combine_kernel,
        out_shape=jax.ShapeDtypeStruct((m, d), F32),
        grid=(m // tm,),
        in_specs=[pl.BlockSpec((tm, d), lambda i: (i, 0)), pl.BlockSpec((tm, 2 * d), lambda i: (i, 0))],
        out_specs=pl.BlockSpec((tm, d), lambda i: (i, 0)),
        compiler_params=_params(("parallel",), block_bytes),
        name=name,
    )(h, pairs)


def _route(logits, n_groups, n_experts, blk):
    t = logits.shape[0]
    epg = n_experts // n_groups
    grp_logits = logits[:, :n_groups]
    grp_prob = jax.nn.softmax(grp_logits, axis=-1)
    grp = jnp.argmax(grp_logits, axis=-1).astype(jnp.int32)
    grp_w = jnp.take_along_axis(grp_prob, grp[:, None], axis=-1)
    exp_logits = logits[:, n_groups:n_groups + n_experts].reshape(t, n_groups, epg)
    in_grp = jnp.take_along_axis(exp_logits, grp[:, None, None], axis=1)[:, 0]
    top_val, top_idx = lax.top_k(in_grp, TOP_K)
    gate = grp_w * jax.nn.softmax(top_val, axis=-1)
    expert = (grp[:, None] * epg + top_idx.astype(jnp.int32)).reshape(-1)
    n_assign = t * TOP_K
    tok = jnp.repeat(jnp.arange(t, dtype=jnp.int32), TOP_K)
    order = jnp.argsort(expert)
    e_sorted = expert[order]
    counts = jnp.bincount(expert, length=n_experts)
    starts = jnp.cumsum(counts) - counts
    padded = (counts + blk - 1) // blk * blk
    pad_ends = jnp.cumsum(padded)
    pad_starts = pad_ends - padded
    dest = (pad_starts[e_sorted] + jnp.arange(n_assign) - starts[e_sorted]).astype(jnp.int32)
    cap = n_assign + n_experts * blk
    n_blk = cap // blk
    buf_tok = jnp.zeros((cap,), jnp.int32).at[dest].set(tok[order])
    buf_w = jnp.zeros((cap,), F32).at[dest].set(gate.reshape(-1)[order])
    blk_expert = jnp.minimum(jnp.searchsorted(pad_ends, jnp.arange(n_blk) * blk, side='right'),
                             n_experts - 1).astype(jnp.int32)
    n_real = (pad_ends[-1] // blk).astype(jnp.int32).reshape(1)
    slot = jnp.zeros((n_assign,), jnp.int32).at[order].set(dest)
    return buf_tok, buf_w, blk_expert, n_real, slot


def _diff_lambda_init(layer):
    return 0.8 - 0.6 * math.exp(-0.3 * layer)


def _tile_cols(v, n):
    return jnp.tile(v.reshape(1, -1).astype(F32), (1, n // v.size))


def kernel(x, mem, g_mix, w_in, g_da_q, g_da_k, lam_q1, lam_k1, lam_q2, lam_k2, g_da_sub, hg_lower, g_hg_out,
           w_up_a, w_up_b, w_gate, b_gate, w_out, g_cross, g_mem, w_xq, w_xkv, g_xq, g_xk, w_xo, g_ffn, w_grp,
           b_grp, w_erouter, b_erouter, w_e_gate, w_e_up, w_e_down):
    b, s, d = x.shape
    t = b * s
    depth = g_mix.shape[0]
    da_w = w_up_a.shape[1]
    hg_w = w_up_b.shape[1]
    da_heads = da_w // (2 * DA_HEAD_DIM)
    hg_heads = hg_w // HG_DIM
    x_w = w_xq.shape[2]
    x_heads = x_w // X_HEAD_DIM
    n_groups = w_grp.shape[2]
    n_experts = w_erouter.shape[2]
    moe_blk = min(256, t)

    lower_bounds = jnp.cumsum(jax.nn.softmax(hg_lower.astype(F32), axis=0), axis=0)
    h = x.reshape(t, d)
    mem2 = mem.reshape(b * mem.shape[1], d)
    for l in range(depth):
        lam_init = _diff_lambda_init(l)
        w_in_l = w_in[l].astype(BF16)
        n = _rmsnorm(h, g_mix[l], BF16, "rms_mix")
        qk_gain = jnp.concatenate([_tile_cols(g_da_q[l], da_w) * DA_HEAD_DIM ** -0.5, _tile_cols(g_da_k[l], da_w)],
                                  axis=1)
        qk = _matmul(n, w_in_l, col0=0, ncols=2 * da_w, out_dtype=BF16,
                     epilogue=functools.partial(_ep_group_norm, group=DA_HEAD_DIM),
                     col_extras=[(qk_gain, 0)], name="proj_qk")
        v = _matmul(n, w_in_l, col0=2 * da_w, ncols=da_w, out_dtype=BF16, name="proj_v")
        hg = _matmul(n, w_in_l, col0=3 * da_w, ncols=4 * hg_w, out_dtype=F32, name="proj_hg")
        gates = _matmul(n, w_gate[l].astype(BF16), out_dtype=BF16, epilogue=_ep_bias_sigmoid,
                        col_extras=[(b_gate[l].reshape(1, -1).astype(F32), 0)], name="proj_gate")
        lamv = jnp.stack([lam_q1[l], lam_k1[l], lam_q2[l], lam_k2[l]]).astype(F32)
        y_a = _diff_attention(qk.reshape(b, s, 2 * da_w), v.reshape(b, s, da_w), lamv,
                              g_da_sub[l].reshape(1, -1).astype(F32), da_heads, lam_init, "diff_attn")
        y_b = _hgrn2(hg.reshape(b, s, 4 * hg_w), lower_bounds[l].reshape(1, hg_w),
                     g_hg_out[l].reshape(1, HG_DIM).astype(F32), hg_heads, "hgrn2")
        merged = _merge(y_a.reshape(t, da_w), w_up_a[l].astype(BF16), y_b.reshape(t, hg_w),
                        w_up_b[l].astype(BF16), gates, "merge")
        h = _matmul(merged, w_out[l].astype(BF16), out_dtype=F32, epilogue=_ep_residual,
                    full_extras=[(h, 0)], name="mix_out")
        nc = _rmsnorm(h, g_cross[l], BF16, "rms_cross")
        xq = _matmul(nc, w_xq[l].astype(BF16), out_dtype=BF16,
                     epilogue=functools.partial(_ep_group_norm, group=X_HEAD_DIM),
                     col_extras=[(_tile_cols(g_xq[l], x_w) * X_HEAD_DIM ** -0.5, 0)], name="xattn_q")
        nm = _rmsnorm(mem2, g_mem[l], BF16, "rms_mem")
        w_xkv_l = w_xkv[l].astype(BF16)
        xk = _matmul(nm, w_xkv_l, col0=0, ncols=x_w, out_dtype=BF16,
                     epilogue=functools.partial(_ep_group_norm, group=X_HEAD_DIM),
                     col_extras=[(_tile_cols(g_xk[l], x_w), 0)], name="xattn_k")
        xv = _matmul(nm, w_xkv_l, col0=x_w, ncols=x_w, out_dtype=BF16, name="xattn_v")
        xo = _cross_attention(xq.reshape(b, s, x_w), xk.reshape(b, -1, x_w), xv.reshape(b, -1, x_w), x_heads,
                              "xattn")
        h = _matmul(xo.reshape(t, x_w), w_xo[l].astype(BF16), out_dtype=F32, epilogue=_ep_residual,
                    full_extras=[(h, 0)], name="xattn_out")
        n_logit = -(-(n_groups + n_experts) // LANES) * LANES
        w_r = jnp.zeros((d, n_logit), F32).at[:, :n_groups].set(w_grp[l]).at[:, n_groups:n_groups + n_experts].set(
            w_erouter[l])
        b_r = jnp.zeros((1, n_logit), F32).at[0, :n_groups].set(b_grp[l]).at[0, n_groups:n_groups + n_experts].set(
            b_erouter[l])
        w_r_hi = w_r.astype(BF16)
        w_r_lo = (w_r - w_r_hi.astype(F32)).astype(BF16)
        nf, logits = _router(h, g_ffn[l].reshape(1, d).astype(F32), w_r_hi, w_r_lo, b_r, "router")
        buf_tok, buf_w, blk_expert, n_real, slot = _route(logits, n_groups, n_experts, moe_blk)
        xs = _gather_rows(nf.reshape(t, d // LANES, LANES), buf_tok, "moe_gather")
        ys = _experts(xs.reshape(-1, d), buf_w.reshape(-1, 1), blk_expert, n_real, w_e_gate[l].astype(BF16),
                      w_e_up[l].astype(BF16), w_e_down[l].astype(BF16), moe_blk, "moe_experts")
        pairs = _gather_rows(ys.reshape(-1, d // LANES, LANES), slot, "moe_ungather")
        h = _combine(h, pairs.reshape(t, TOP_K * d), "moe_combine")
    return h.reshape(b, s, d)
```

```python
import functools
import math

import jax
import jax.numpy as jnp
from jax import lax
from jax.experimental import pallas as pl
from jax.experimental.pallas import tpu as pltpu

F32 = jnp.float32
BF16 = jnp.bfloat16

NORM_EPS = 1e-6
DA_HEAD_DIM = 128
DA_CHUNK = 64
HG_DIM = 128
HG_CHUNK = 64
HG_SUB = 16
X_HEAD_DIM = 256
TOP_K = 2
LANES = 128
MASK_VALUE = -1e30
V7X_VMEM_BYTES = 64 * 1024 * 1024
VMEM_CAP_BYTES = V7X_VMEM_BYTES - 6 * 1024 * 1024


def _vmem_limit(block_bytes):
    return int(min(VMEM_CAP_BYTES, max(32 * 1024 * 1024, 2 * block_bytes + 16 * 1024 * 1024)))


def _params(semantics, block_bytes):
    return pltpu.CompilerParams(dimension_semantics=semantics, vmem_limit_bytes=_vmem_limit(block_bytes))


def _nbytes(shape, dtype):
    return math.prod(shape) * jnp.dtype(dtype).itemsize


def _rms_kernel(x_ref, g_ref, o_ref):
    x = x_ref[...].astype(F32)
    ms = jnp.mean(x * x, axis=-1, keepdims=True)
    o_ref[...] = (x * lax.rsqrt(ms + NORM_EPS) * g_ref[...]).astype(o_ref.dtype)


def _rmsnorm(x, g, out_dtype, name):
    m, d = x.shape
    tm = min(256, m)
    return pl.pallas_call(
        _rms_kernel,
        out_shape=jax.ShapeDtypeStruct((m, d), out_dtype),
        grid=(m // tm,),
        in_specs=[pl.BlockSpec((tm, d), lambda i: (i, 0)), pl.BlockSpec((1, d), lambda i: (0, 0))],
        out_specs=pl.BlockSpec((tm, d), lambda i: (i, 0)),
        compiler_params=_params(("parallel",), _nbytes((tm, d), x.dtype) + _nbytes((tm, d), out_dtype)),
        name=name,
    )(x, g.reshape(1, d).astype(F32))


def _ep_plain(acc, o_ref):
    o_ref[...] = acc.astype(o_ref.dtype)


def _ep_group_norm(acc, o_ref, gain, *, group):
    for c in range(acc.shape[1] // group):
        sl = slice(c * group, (c + 1) * group)
        xg = acc[:, sl]
        ms = jnp.mean(xg * xg, axis=-1, keepdims=True)
        o_ref[:, sl] = (xg * lax.rsqrt(ms + NORM_EPS) * gain[:, sl]).astype(o_ref.dtype)


def _ep_bias_sigmoid(acc, o_ref, bias):
    o_ref[...] = jax.nn.sigmoid(acc + bias).astype(o_ref.dtype)


def _ep_residual(acc, o_ref, res):
    o_ref[...] = (res + acc).astype(o_ref.dtype)


def _mm_kernel(*refs, epilogue):
    a_ref, w_ref, *extra, o_ref = refs
    acc = jnp.dot(a_ref[...], w_ref[...], preferred_element_type=F32)
    epilogue(acc, o_ref, *[e[...] for e in extra])


def _matmul(a, w, *, col0=0, ncols=None, out_dtype, epilogue=_ep_plain, col_extras=(), full_extras=(), name):
    m, k = a.shape
    ncols = w.shape[1] - col0 if ncols is None else ncols
    tm = min(1024, m)
    tn = min(512, ncols)
    assert m % tm == 0 and ncols % tn == 0 and col0 % tn == 0
    in_specs = [pl.BlockSpec((tm, k), lambda j, i: (i, 0)),
                pl.BlockSpec((k, tn), lambda j, i, o=col0 // tn: (0, j + o))]
    operands = [a, w]
    block_bytes = _nbytes((tm, k), a.dtype) + _nbytes((k, tn), w.dtype) + _nbytes((tm, tn), out_dtype)
    for vec, c0 in col_extras:
        assert c0 % tn == 0
        in_specs.append(pl.BlockSpec((1, tn), lambda j, i, o=c0 // tn: (0, j + o)))
        operands.append(vec)
    for arr, c0 in full_extras:
        assert c0 % tn == 0
        in_specs.append(pl.BlockSpec((tm, tn), lambda j, i, o=c0 // tn: (i, j + o)))
        operands.append(arr)
        block_bytes += _nbytes((tm, tn), arr.dtype)
    return pl.pallas_call(
        functools.partial(_mm_kernel, epilogue=epilogue),
        out_shape=jax.ShapeDtypeStruct((m, ncols), out_dtype),
        grid=(ncols // tn, m // tm),
        in_specs=in_specs,
        out_specs=pl.BlockSpec((tm, tn), lambda j, i: (i, j)),
        compiler_params=_params(("parallel", "parallel"), block_bytes),
        name=name,
    )(*operands)


def _merge_kernel(ya_ref, wa_ref, yb_ref, wb_ref, ga_ref, gb_ref, o_ref):
    pa = jnp.dot(ya_ref[...], wa_ref[...], preferred_element_type=F32)
    pb = jnp.dot(yb_ref[...], wb_ref[...], preferred_element_type=F32)
    o_ref[...] = (ga_ref[...].astype(F32) * pa + gb_ref[...].astype(F32) * pb).astype(o_ref.dtype)


def _merge(ya, wa, yb, wb, gates, name):
    m, ka = ya.shape
    kb = yb.shape[1]
    d = wa.shape[1]
    tm = min(1024, m)
    tn = min(512, d)
    nb = d // tn
    block_bytes = (_nbytes((tm, ka), BF16) + _nbytes((ka, tn), BF16) + _nbytes((tm, kb), BF16)
                   + _nbytes((kb, tn), BF16) + 3 * _nbytes((tm, tn), BF16))
    return pl.pallas_call(
        _merge_kernel,
        out_shape=jax.ShapeDtypeStruct((m, d), BF16),
        grid=(nb, m // tm),
        in_specs=[pl.BlockSpec((tm, ka), lambda j, i: (i, 0)),
                  pl.BlockSpec((ka, tn), lambda j, i: (0, j)),
                  pl.BlockSpec((tm, kb), lambda j, i: (i, 0)),
                  pl.BlockSpec((kb, tn), lambda j, i: (0, j)),
                  pl.BlockSpec((tm, tn), lambda j, i: (i, j)),
                  pl.BlockSpec((tm, tn), lambda j, i: (i, j + nb))],
        out_specs=pl.BlockSpec((tm, tn), lambda j, i: (i, j)),
        compiler_params=_params(("parallel", "parallel"), block_bytes),
        name=name,
    )(ya, wa, yb, wb, gates, gates)


def _da_kernel(lamv_ref, gsub_ref, q_ref, k_ref, v_ref, o_ref, acc_ref, m_ref, l_ref, *, tq, lam_init):
    qi = pl.program_id(2)
    hd = DA_HEAD_DIM
    m_ref[...] = jnp.full(m_ref.shape, MASK_VALUE, F32)
    l_ref[...] = jnp.zeros(l_ref.shape, F32)
    acc_ref[...] = jnp.zeros(acc_ref.shape, F32)
    q = q_ref[0]
    row_chunk = lax.broadcasted_iota(jnp.int32, (tq, tq), 0) // DA_CHUNK
    col_chunk = lax.broadcasted_iota(jnp.int32, (tq, tq), 1) // DA_CHUNK
    visible = col_chunk <= row_chunk

    def update(j, masked):
        off = pl.multiple_of(j * tq, tq)
        kk = k_ref[0, pl.ds(off, tq), :]
        vv = v_ref[0, pl.ds(off, tq), :]
        for c in range(2):
            s = lax.dot_general(q[:, c * hd:(c + 1) * hd], kk[:, c * hd:(c + 1) * hd],
                                (((1,), (1,)), ((), ())), preferred_element_type=F32)
            if masked:
                s = jnp.where(visible, s, MASK_VALUE)
            m_old = m_ref[c]
            m_new = jnp.maximum(m_old, jnp.max(s, axis=-1, keepdims=True))
            p = jnp.exp(s - m_new)
            alpha = jnp.exp(m_old - m_new)
            l_ref[c] = alpha * l_ref[c] + jnp.sum(p, axis=-1, keepdims=True)
            acc_ref[c] = alpha * acc_ref[c] + jnp.dot(p.astype(BF16), vv, preferred_element_type=F32)
            m_ref[c] = m_new

    def body(j, carry):
        update(j, False)
        return carry

    lax.fori_loop(0, qi, body, 0)
    update(qi, True)

    lv = lamv_ref[...]
    lam = (jnp.exp(jnp.sum(lv[0:1] * lv[1:2], axis=-1, keepdims=True))
           - jnp.exp(jnp.sum(lv[2:3] * lv[3:4], axis=-1, keepdims=True)) + lam_init)
    o = acc_ref[0] / l_ref[0] - lam * (acc_ref[1] / l_ref[1])
    ms = jnp.mean(o * o, axis=-1, keepdims=True)
    o_ref[0] = ((o * lax.rsqrt(ms + NORM_EPS) * gsub_ref[...]) * (1.0 - lam_init)).astype(o_ref.dtype)


def _diff_attention(qk, v, lamv, g_sub, n_heads, lam_init, name):
    b, s, _ = v.shape
    hw = 2 * DA_HEAD_DIM
    tq = min(512, s)
    block_bytes = 2 * _nbytes((tq, hw), BF16) + 2 * _nbytes((s, hw), BF16) + 6 * _nbytes((tq, tq), F32)
    return pl.pallas_call(
        functools.partial(_da_kernel, tq=tq, lam_init=lam_init),
        out_shape=jax.ShapeDtypeStruct((b, s, n_heads * hw), BF16),
        grid=(b, n_heads, s // tq),
        in_specs=[pl.BlockSpec((4, DA_HEAD_DIM), lambda bi, h, i: (0, 0)),
                  pl.BlockSpec((1, hw), lambda bi, h, i: (0, 0)),
                  pl.BlockSpec((1, tq, hw), lambda bi, h, i: (bi, i, h)),
                  pl.BlockSpec((1, s, hw), lambda bi, h, i: (bi, 0, n_heads + h)),
                  pl.BlockSpec((1, s, hw), lambda bi, h, i: (bi, 0, h))],
        out_specs=pl.BlockSpec((1, tq, hw), lambda bi, h, i: (bi, i, h)),
        scratch_shapes=[pltpu.VMEM((2, tq, hw), F32), pltpu.VMEM((2, tq, 1), F32), pltpu.VMEM((2, tq, 1), F32)],
        compiler_params=_params(("parallel", "parallel", "arbitrary"), block_bytes),
        name=name,
    )(lamv, g_sub, qk, qk, v)


def _hg_kernel(lb_ref, gout_ref, q_ref, f_ref, i_ref, g_ref, o_ref, st_ref, *, heads, steps):
    t = pl.program_id(2)
    ch, sub, hd = HG_CHUNK, HG_SUB, HG_DIM

    @pl.when(t == 0)
    def _():
        st_ref[...] = jnp.zeros(st_ref.shape, F32)

    r = lax.broadcasted_iota(jnp.int32, (ch, ch), 0)
    c = lax.broadcasted_iota(jnp.int32, (ch, ch), 1)
    causal = c <= r
    cum_mat = jnp.concatenate([causal.astype(BF16), (c < (r // sub) * sub).astype(BF16)], axis=0)
    row = lax.broadcasted_iota(jnp.int32, (ch, hd), 0)
    nt = (((1,), (1,)), ((), ()))
    tn = (((0,), (0,)), ((), ()))

    def chunk(ci, carry):
        off = pl.multiple_of(ci * ch, ch)
        for h in range(heads):
            sl = slice(h * hd, (h + 1) * hd)
            q = q_ref[0, pl.ds(off, ch), sl]
            lb = lb_ref[:, sl]
            f = lb + (1.0 - lb) * jax.nn.sigmoid(f_ref[0, pl.ds(off, ch), sl])
            key = 1.0 - f
            lf = jnp.log(f)
            lf0 = lf.astype(BF16)
            r1 = lf - lf0.astype(F32)
            lf1 = r1.astype(BF16)
            lf2 = (r1 - lf1.astype(F32)).astype(BF16)
            cums = (jnp.dot(cum_mat, lf0, preferred_element_type=F32)
                    + jnp.dot(cum_mat, lf1, preferred_element_type=F32)
                    + jnp.dot(cum_mat, lf2, preferred_element_type=F32))
            cum = cums[:ch]
            cstart = cums[ch:]
            last = cum[ch - 1:ch]
            q_state = (q * jnp.exp(cum)).astype(BF16)
            q_blk = (q * jnp.exp(cum - cstart)).astype(BF16)
            k_state = (key * jnp.exp(last - cum)).astype(BF16)
            vals = i_ref[0, pl.ds(off, ch), sl].astype(BF16)
            parts = []
            for a in range(ch // sub):
                ref_a = cstart[a * sub:a * sub + 1]
                expo = jnp.where(row < (a + 1) * sub, ref_a - cum, MASK_VALUE)
                k_a = (key * jnp.exp(expo)).astype(BF16)
                parts.append(lax.dot_general(q_blk[a * sub:(a + 1) * sub], k_a, nt, preferred_element_type=F32))
            scores = jnp.where(causal, jnp.concatenate(parts, axis=0), 0.0).astype(BF16)
            st = st_ref[h]
            o = (jnp.dot(scores, vals, preferred_element_type=F32)
                 + lax.dot_general(q_state, st.astype(BF16), nt, preferred_element_type=F32))
            st_ref[h] = jnp.exp(last) * st + lax.dot_general(vals, k_state, tn, preferred_element_type=F32)
            ms = jnp.mean(o * o, axis=-1, keepdims=True)
            gate = g_ref[0, pl.ds(off, ch), sl]
            y = (o * lax.rsqrt(ms + NORM_EPS) * gout_ref[...]) * (gate * jax.nn.sigmoid(gate))
            o_ref[0, pl.ds(off, ch), sl] = y.astype(o_ref.dtype)
        return carry

    lax.fori_loop(0, steps // ch, chunk, 0)


def _hgrn2(hg, lower, g_out, n_heads, name):
    b, s, _ = hg.shape
    heads = min(2, n_heads)
    steps = min(512, s)
    nhb = n_heads // heads
    w = heads * HG_DIM
    block_bytes = 4 * _nbytes((steps, w), F32) + _nbytes((steps, w), BF16)

    def spec(part):
        return pl.BlockSpec((1, steps, w), lambda bi, h, t, p=part: (bi, t, p * nhb + h))

    return pl.pallas_call(
        functools.partial(_hg_kernel, heads=heads, steps=steps),
        out_shape=jax.ShapeDtypeStruct((b, s, n_heads * HG_DIM), BF16),
        grid=(b, nhb, s // steps),
        in_specs=[pl.BlockSpec((1, w), lambda bi, h, t: (0, h)),
                  pl.BlockSpec((1, HG_DIM), lambda bi, h, t: (0, 0)),
                  spec(0), spec(1), spec(2), spec(3)],
        out_specs=pl.BlockSpec((1, steps, w), lambda bi, h, t: (bi, t, h)),
        scratch_shapes=[pltpu.VMEM((heads, HG_DIM, HG_DIM), F32)],
        compiler_params=_params(("parallel", "parallel", "arbitrary"), block_bytes),
        name=name,
    )(lower, g_out, hg, hg, hg, hg)


def _xattn_kernel(q_ref, k_ref, v_ref, o_ref, *, n_heads):
    hd = X_HEAD_DIM
    for h in range(n_heads):
        sl = slice(h * hd, (h + 1) * hd)
        s = lax.dot_general(q_ref[0, :, sl], k_ref[0, :, sl], (((1,), (1,)), ((), ())),
                            preferred_element_type=F32)
        p = jnp.exp(s - jnp.max(s, axis=-1, keepdims=True))
        denom = jnp.sum(p, axis=-1, keepdims=True)
        o = jnp.dot(p.astype(BF16), v_ref[0, :, sl], preferred_element_type=F32)
        o_ref[0, :, sl] = (o / denom).astype(o_ref.dtype)


def _cross_attention(q, k, v, n_heads, name):
    b, s, w = q.shape
    mlen = k.shape[1]
    tq = min(1024, s)
    block_bytes = 2 * _nbytes((tq, w), BF16) + 2 * _nbytes((mlen, w), BF16) + 3 * _nbytes((tq, mlen), F32)
    return pl.pallas_call(
        functools.partial(_xattn_kernel, n_heads=n_heads),
        out_shape=jax.ShapeDtypeStruct((b, s, w), BF16),
        grid=(b, s // tq),
        in_specs=[pl.BlockSpec((1, tq, w), lambda bi, i: (bi, i, 0)),
                  pl.BlockSpec((1, mlen, w), lambda bi, i: (bi, 0, 0)),
                  pl.BlockSpec((1, mlen, w), lambda bi, i: (bi, 0, 0))],
        out_specs=pl.BlockSpec((1, tq, w), lambda bi, i: (bi, i, 0)),
        compiler_params=_params(("parallel", "parallel"), block_bytes),
        name=name,
    )(q, k, v)


def _router_kernel(h_ref, g_ref, whi_ref, wlo_ref, b_ref, n_ref, lg_ref):
    x = h_ref[...]
    ms = jnp.mean(x * x, axis=-1, keepdims=True)
    n = x * lax.rsqrt(ms + NORM_EPS) * g_ref[...]
    n_hi = n.astype(BF16)
    n_lo = (n - n_hi.astype(F32)).astype(BF16)
    n_ref[...] = n
    lg_ref[...] = (jnp.dot(n_hi, whi_ref[...], preferred_element_type=F32)
                   + jnp.dot(n_lo, whi_ref[...], preferred_element_type=F32)
                   + jnp.dot(n_hi, wlo_ref[...], preferred_element_type=F32) + b_ref[...])


def _router(h, g, w_hi, w_lo, bias, name):
    m, d = h.shape
    nl = w_hi.shape[1]
    tm = min(256, m)
    block_bytes = 2 * _nbytes((tm, d), F32) + 2 * _nbytes((d, nl), BF16)
    return pl.pallas_call(
        _router_kernel,
        out_shape=(jax.ShapeDtypeStruct((m, d), F32), jax.ShapeDtypeStruct((m, nl), F32)),
        grid=(m // tm,),
        in_specs=[pl.BlockSpec((tm, d), lambda i: (i, 0)),
                  pl.BlockSpec((1, d), lambda i: (0, 0)),
                  pl.BlockSpec((d, nl), lambda i: (0, 0)),
                  pl.BlockSpec((d, nl), lambda i: (0, 0)),
                  pl.BlockSpec((1, nl), lambda i: (0, 0))],
        out_specs=(pl.BlockSpec((tm, d), lambda i: (i, 0)), pl.BlockSpec((tm, nl), lambda i: (i, 0))),
        compiler_params=_params(("parallel",), block_bytes),
        name=name,
    )(h, g, w_hi, w_lo, bias)


def _row_copies(src_ref, dst_ref, idx_ref, idx0, idx_stride, dst0, n, sem, *, start):
    def body(r, carry):
        cp = pltpu.make_async_copy(src_ref.at[pl.ds(idx_ref[idx0 + r * idx_stride], 1)],
                                   dst_ref.at[pl.ds(dst0 + r, 1)], sem)
        if start:
            cp.start()
        else:
            cp.wait()
        return carry

    lax.fori_loop(0, n, body, 0)


def _expert_kernel(tok_ref, be_ref, nreal_ref, nf_ref, rw_ref, wg_ref, wu_ref, wd_ref, o_ref, xbuf, sems, *, blk):
    i = pl.program_id(0)
    n_real = nreal_ref[0]

    def gather(block, start):
        slot = block % 2
        _row_copies(nf_ref, xbuf.at[slot], tok_ref, block * blk, 1, 0, blk, sems.at[slot], start=start)

    @pl.when(jnp.logical_and(i == 0, n_real > 0))
    def _():
        gather(i, True)

    @pl.when(i + 1 < n_real)
    def _():
        gather(i + 1, True)

    @pl.when(i < n_real)
    def _():
        gather(i, False)
        x = xbuf[i % 2].astype(BF16)
        hg = jnp.dot(x, wg_ref[0], preferred_element_type=F32)
        hu = jnp.dot(x, wu_ref[0], preferred_element_type=F32)
        hid = (hg * jax.nn.sigmoid(hg) * hu).astype(BF16)
        y = jnp.dot(hid, wd_ref[0], preferred_element_type=F32)
        o_ref[...] = (y * rw_ref[...]).astype(o_ref.dtype)

    @pl.when(i >= n_real)
    def _():
        o_ref[...] = jnp.zeros(o_ref.shape, o_ref.dtype)


def _experts(nf, buf_tok, row_w, blk_expert, n_real, wg, wu, wd, blk, name):
    cap = buf_tok.shape[0]
    d = nf.shape[1]
    ff = wg.shape[2]
    block_bytes = 2 * _nbytes((blk, d), F32) + 3 * _nbytes((d, ff), BF16) + _nbytes((blk, d), F32)
    return pl.pallas_call(
        functools.partial(_expert_kernel, blk=blk),
        out_shape=jax.ShapeDtypeStruct((cap, d), F32),
        grid_spec=pltpu.PrefetchScalarGridSpec(
            num_scalar_prefetch=3,
            grid=(cap // blk,),
            in_specs=[pl.BlockSpec(memory_space=pl.ANY),
                      pl.BlockSpec((blk, 1), lambda i, tk, be, nr: (i, 0)),
                      pl.BlockSpec((1, d, ff), lambda i, tk, be, nr: (be[i], 0, 0)),
                      pl.BlockSpec((1, d, ff), lambda i, tk, be, nr: (be[i], 0, 0)),
                      pl.BlockSpec((1, ff, d), lambda i, tk, be, nr: (be[i], 0, 0))],
            out_specs=pl.BlockSpec((blk, d), lambda i, tk, be, nr: (i, 0)),
            scratch_shapes=[pltpu.VMEM((2, blk, d), F32), pltpu.SemaphoreType.DMA((2,))]),
        compiler_params=_params(("arbitrary",), block_bytes),
        name=name,
    )(buf_tok, blk_expert, n_real, nf, row_w, wg, wu, wd)


def _combine_kernel(slot_ref, h_ref, ys_ref, o_ref, ybuf, sems, *, tm):
    i = pl.program_id(0)

    def gather(block, start):
        s = block % 2
        for k in range(TOP_K):
            _row_copies(ys_ref, ybuf.at[s], slot_ref, block * tm * TOP_K + k, TOP_K, k * tm, tm, sems.at[s],
                        start=start)

    @pl.when(i == 0)
    def _():
        gather(i, True)

    @pl.when(i + 1 < pl.num_programs(0))
    def _():
        gather(i + 1, True)

    gather(i, False)
    acc = h_ref[...]
    for k in range(TOP_K):
        acc = acc + ybuf[i % 2, pl.ds(k * tm, tm), :]
    o_ref[...] = acc


def _combine(h, ys, slot, name):
    m, d = h.shape
    tm = min(128, m)
    block_bytes = 2 * _nbytes((tm, d), F32) + TOP_K * _nbytes((tm, d), F32)
    return pl.pallas_call(
        functools.partial(_combine_kernel, tm=tm),
        out_shape=jax.ShapeDtypeStruct((m, d), F32),
        grid_spec=pltpu.PrefetchScalarGridSpec(
            num_scalar_prefetch=1,
            grid=(m // tm,),
            in_specs=[pl.BlockSpec((tm, d), lambda i, sl: (i, 0)), pl.BlockSpec(memory_space=pl.ANY)],
            out_specs=pl.BlockSpec((tm, d), lambda i, sl: (i, 0)),
            scratch_shapes=[pltpu.VMEM((2, TOP_K * tm, d), F32), pltpu.SemaphoreType.DMA((2,))]),
        compiler_params=_params(("arbitrary",), block_bytes),
        name=name,
    )(slot, h, ys)


def _route(logits, n_groups, n_experts, blk):
    t = logits.shape[0]
    epg = n_experts // n_groups
    grp_logits = logits[:, :n_groups]
    grp_prob = jax.nn.softmax(grp_logits, axis=-1)
    grp = jnp.argmax(grp_logits, axis=-1).astype(jnp.int32)
    grp_w = jnp.take_along_axis(grp_prob, grp[:, None], axis=-1)
    exp_logits = logits[:, n_groups:n_groups + n_experts].reshape(t, n_groups, epg)
    in_grp = jnp.take_along_axis(exp_logits, grp[:, None, None], axis=1)[:, 0]
    top_val, top_idx = lax.top_k(in_grp, TOP_K)
    gate = grp_w * jax.nn.softmax(top_val, axis=-1)
    expert = (grp[:, None] * epg + top_idx.astype(jnp.int32)).reshape(-1)
    n_assign = t * TOP_K
    tok = jnp.repeat(jnp.arange(t, dtype=jnp.int32), TOP_K)
    order = jnp.argsort(expert)
    e_sorted = expert[order]
    counts = jnp.bincount(expert, length=n_experts)
    starts = jnp.cumsum(counts) - counts
    padded = (counts + blk - 1) // blk * blk
    pad_ends = jnp.cumsum(padded)
    pad_starts = pad_ends - padded
    dest = (pad_starts[e_sorted] + jnp.arange(n_assign) - starts[e_sorted]).astype(jnp.int32)
    cap = n_assign + n_experts * blk
    n_blk = cap // blk
    buf_tok = jnp.zeros((cap,), jnp.int32).at[dest].set(tok[order])
    buf_w = jnp.zeros((cap,), F32).at[dest].set(gate.reshape(-1)[order])
    blk_expert = jnp.minimum(jnp.searchsorted(pad_ends, jnp.arange(n_blk) * blk, side='right'),
                             n_experts - 1).astype(jnp.int32)
    n_real = (pad_ends[-1] // blk).astype(jnp.int32).reshape(1)
    slot = jnp.zeros((n_assign,), jnp.int32).at[order].set(dest)
    return buf_tok, buf_w, blk_expert, n_real, slot


def _diff_lambda_init(layer):
    return 0.8 - 0.6 * math.exp(-0.3 * layer)


def _tile_cols(v, n):
    return jnp.tile(v.reshape(1, -1).astype(F32), (1, n // v.size))


def kernel(x, mem, g_mix, w_in, g_da_q, g_da_k, lam_q1, lam_k1, lam_q2, lam_k2, g_da_sub, hg_lower, g_hg_out,
           w_up_a, w_up_b, w_gate, b_gate, w_out, g_cross, g_mem, w_xq, w_xkv, g_xq, g_xk, w_xo, g_ffn, w_grp,
           b_grp, w_erouter, b_erouter, w_e_gate, w_e_up, w_e_down):
    b, s, d = x.shape
    t = b * s
    depth = g_mix.shape[0]
    da_w = w_up_a.shape[1]
    hg_w = w_up_b.shape[1]
    da_heads = da_w // (2 * DA_HEAD_DIM)
    hg_heads = hg_w // HG_DIM
    x_w = w_xq.shape[2]
    x_heads = x_w // X_HEAD_DIM
    n_groups = w_grp.shape[2]
    n_experts = w_erouter.shape[2]
    moe_blk = min(256, t)

    lower_bounds = jnp.cumsum(jax.nn.softmax(hg_lower.astype(F32), axis=0), axis=0)
    h = x.reshape(t, d)
    mem2 = mem.reshape(b * mem.shape[1], d)
    for l in range(depth):
        lam_init = _diff_lambda_init(l)
        w_in_l = w_in[l].astype(BF16)
        n = _rmsnorm(h, g_mix[l], BF16, "rms_mix")
        qk_gain = jnp.concatenate([_tile_cols(g_da_q[l], da_w) * DA_HEAD_DIM ** -0.5, _tile_cols(g_da_k[l], da_w)],
                                  axis=1)
        qk = _matmul(n, w_in_l, col0=0, ncols=2 * da_w, out_dtype=BF16,
                     epilogue=functools.partial(_ep_group_norm, group=DA_HEAD_DIM),
                     col_extras=[(qk_gain, 0)], name="proj_qk")
        v = _matmul(n, w_in_l, col0=2 * da_w, ncols=da_w, out_dtype=BF16, name="proj_v")
        hg = _matmul(n, w_in_l, col0=3 * da_w, ncols=4 * hg_w, out_dtype=F32, name="proj_hg")
        gates = _matmul(n, w_gate[l].astype(BF16), out_dtype=BF16, epilogue=_ep_bias_sigmoid,
                        col_extras=[(b_gate[l].reshape(1, -1).astype(F32), 0)], name="proj_gate")
        lamv = jnp.stack([lam_q1[l], lam_k1[l], lam_q2[l], lam_k2[l]]).astype(F32)
        y_a = _diff_attention(qk.reshape(b, s, 2 * da_w), v.reshape(b, s, da_w), lamv,
                              g_da_sub[l].reshape(1, -1).astype(F32), da_heads, lam_init, "diff_attn")
        y_b = _hgrn2(hg.reshape(b, s, 4 * hg_w), lower_bounds[l].reshape(1, hg_w),
                     g_hg_out[l].reshape(1, HG_DIM).astype(F32), hg_heads, "hgrn2")
        merged = _merge(y_a.reshape(t, da_w), w_up_a[l].astype(BF16), y_b.reshape(t, hg_w),
                        w_up_b[l].astype(BF16), gates, "merge")
        h = _matmul(merged, w_out[l].astype(BF16), out_dtype=F32, epilogue=_ep_residual,
                    full_extras=[(h, 0)], name="mix_out")
        nc = _rmsnorm(h, g_cross[l], BF16, "rms_cross")
        xq = _matmul(nc, w_xq[l].astype(BF16), out_dtype=BF16,
                     epilogue=functools.partial(_ep_group_norm, group=X_HEAD_DIM),
                     col_extras=[(_tile_cols(g_xq[l], x_w) * X_HEAD_DIM ** -0.5, 0)], name="xattn_q")
        nm = _rmsnorm(mem2, g_mem[l], BF16, "rms_mem")
        w_xkv_l = w_xkv[l].astype(BF16)
        xk = _matmul(nm, w_xkv_l, col0=0, ncols=x_w, out_dtype=BF16,
                     epilogue=functools.partial(_ep_group_norm, group=X_HEAD_DIM),
                     col_extras=[(_tile_cols(g_xk[l], x_w), 0)], name="xattn_k")
        xv = _matmul(nm, w_xkv_l, col0=x_w, ncols=x_w, out_dtype=BF16, name="xattn_v")
        xo = _cross_attention(xq.reshape(b, s, x_w), xk.reshape(b, -1, x_w), xv.reshape(b, -1, x_w), x_heads,
                              "xattn")
        h = _matmul(xo.reshape(t, x_w), w_xo[l].astype(BF16), out_dtype=F32, epilogue=_ep_residual,
                    full_extras=[(h, 0)], name="xattn_out")
        n_logit = -(-(n_groups + n_experts) // LANES) * LANES
        w_r = jnp.zeros((d, n_logit), F32).at[:, :n_groups].set(w_grp[l]).at[:, n_groups:n_groups + n_experts].set(
            w_erouter[l])
        b_r = jnp.zeros((1, n_logit), F32).at[0, :n_groups].set(b_grp[l]).at[0, n_groups:n_groups + n_experts].set(
            b_erouter[l])
        w_r_hi = w_r.astype(BF16)
        w_r_lo = (w_r - w_r_hi.astype(F32)).astype(BF16)
        nf, logits = _router(h, g_ffn[l].reshape(1, d).astype(F32), w_r_hi, w_r_lo, b_r, "router")
        buf_tok, buf_w, blk_expert, n_real, slot = _route(logits, n_groups, n_experts, moe_blk)
        ys = _experts(nf, buf_tok, buf_w.reshape(-1, 1), blk_expert, n_real, w_e_gate[l].astype(BF16),
                      w_e_up[l].astype(BF16), w_e_down[l].astype(BF16), moe_blk, "moe_experts")
        h = _combine(h, ys, slot, "moe_combine")
    return h.reshape(b, s, d)
```

```python
import functools
import math

import jax
import jax.numpy as jnp
from jax import lax
from jax.experimental import pallas as pl
from jax.experimental.pallas import tpu as pltpu

F32 = jnp.float32
BF16 = jnp.bfloat16

NORM_EPS = 1e-6
DA_HEAD_DIM = 128
DA_CHUNK = 64
HG_DIM = 128
HG_CHUNK = 64
HG_FAST_MAX_DECAY = 60.0
X_HEAD_DIM = 256
TOP_K = 2
LANES = 128
MASK_VALUE = -1e30
V7X_VMEM_BYTES = 64 * 1024 * 1024
VMEM_CAP_BYTES = V7X_VMEM_BYTES - 6 * 1024 * 1024


def _vmem_limit(block_bytes, scratch_bytes):
    return int(min(VMEM_CAP_BYTES, max(32 * 1024 * 1024, 2 * block_bytes + scratch_bytes + 16 * 1024 * 1024)))


def _params(semantics, block_bytes, scratch_bytes=0):
    return pltpu.CompilerParams(dimension_semantics=semantics,
                                vmem_limit_bytes=_vmem_limit(block_bytes, scratch_bytes))


def _nbytes(shape, dtype):
    return math.prod(shape) * jnp.dtype(dtype).itemsize


def _rms_kernel(x_ref, g_ref, o_ref):
    x = x_ref[...].astype(F32)
    ms = jnp.mean(x * x, axis=-1, keepdims=True)
    o_ref[...] = (x * lax.rsqrt(ms + NORM_EPS) * g_ref[...]).astype(o_ref.dtype)


def _rmsnorm(x, g, out_dtype, name):
    m, d = x.shape
    tm = min(256, m)
    return pl.pallas_call(
        _rms_kernel,
        out_shape=jax.ShapeDtypeStruct((m, d), out_dtype),
        grid=(m // tm,),
        in_specs=[pl.BlockSpec((tm, d), lambda i: (i, 0)), pl.BlockSpec((1, d), lambda i: (0, 0))],
        out_specs=pl.BlockSpec((tm, d), lambda i: (i, 0)),
        compiler_params=_params(("parallel",), _nbytes((tm, d), x.dtype) + _nbytes((tm, d), out_dtype)),
        name=name,
    )(x, g.reshape(1, d).astype(F32))


def _ep_plain(acc, o_ref):
    o_ref[...] = acc.astype(o_ref.dtype)


def _ep_group_norm(acc, o_ref, gain, *, group):
    for c in range(acc.shape[1] // group):
        sl = slice(c * group, (c + 1) * group)
        xg = acc[:, sl]
        ms = jnp.mean(xg * xg, axis=-1, keepdims=True)
        o_ref[:, sl] = (xg * lax.rsqrt(ms + NORM_EPS) * gain[:, sl]).astype(o_ref.dtype)


def _ep_bias_sigmoid(acc, o_ref, bias):
    o_ref[...] = jax.nn.sigmoid(acc + bias).astype(o_ref.dtype)


def _ep_residual(acc, o_ref, res):
    o_ref[...] = (res + acc).astype(o_ref.dtype)


def _bf16_weight(w_ref, wb_ref):
    if wb_ref is None:
        return w_ref[...]

    @pl.when(pl.program_id(1) == 0)
    def _():
        wb_ref[...] = w_ref[...].astype(BF16)

    return wb_ref[...]


def _mm_kernel(*refs, epilogue, cast_w):
    if cast_w:
        a_ref, w_ref, *extra, o_ref, wb_ref = refs
    else:
        (a_ref, w_ref, *extra, o_ref), wb_ref = refs, None
    acc = jnp.dot(a_ref[...], _bf16_weight(w_ref, wb_ref), preferred_element_type=F32)
    epilogue(acc, o_ref, *[e[...] for e in extra])


def _matmul(a, w, *, col0=0, ncols=None, out_dtype, epilogue=_ep_plain, col_extras=(), full_extras=(), name):
    m, k = a.shape
    ncols = w.shape[1] - col0 if ncols is None else ncols
    tm = min(1024, m)
    tn = min(512, ncols)
    assert m % tm == 0 and ncols % tn == 0 and col0 % tn == 0
    in_specs = [pl.BlockSpec((tm, k), lambda j, i: (i, 0)),
                pl.BlockSpec((k, tn), lambda j, i, o=col0 // tn: (0, j + o))]
    operands = [a, w]
    block_bytes = _nbytes((tm, k), a.dtype) + _nbytes((k, tn), w.dtype) + _nbytes((tm, tn), out_dtype)
    for vec, c0 in col_extras:
        assert c0 % tn == 0
        in_specs.append(pl.BlockSpec((1, tn), lambda j, i, o=c0 // tn: (0, j + o)))
        operands.append(vec)
    for arr, c0 in full_extras:
        assert c0 % tn == 0
        in_specs.append(pl.BlockSpec((tm, tn), lambda j, i, o=c0 // tn: (i, j + o)))
        operands.append(arr)
        block_bytes += _nbytes((tm, tn), arr.dtype)
    cast_w = w.dtype != BF16
    return pl.pallas_call(
        functools.partial(_mm_kernel, epilogue=epilogue, cast_w=cast_w),
        out_shape=jax.ShapeDtypeStruct((m, ncols), out_dtype),
        grid=(ncols // tn, m // tm),
        in_specs=in_specs,
        out_specs=pl.BlockSpec((tm, tn), lambda j, i: (i, j)),
        scratch_shapes=[pltpu.VMEM((k, tn), BF16)] if cast_w else [],
        compiler_params=_params(("parallel", "arbitrary"), block_bytes, _nbytes((k, tn), BF16) if cast_w else 0),
        name=name,
    )(*operands)


def _merge_kernel(ya_ref, wa_ref, yb_ref, wb_ref, ga_ref, gb_ref, o_ref, wa_bf_ref, wb_bf_ref):
    pa = jnp.dot(ya_ref[...], _bf16_weight(wa_ref, wa_bf_ref), preferred_element_type=F32)
    pb = jnp.dot(yb_ref[...], _bf16_weight(wb_ref, wb_bf_ref), preferred_element_type=F32)
    o_ref[...] = (ga_ref[...].astype(F32) * pa + gb_ref[...].astype(F32) * pb).astype(o_ref.dtype)


def _merge(ya, wa, yb, wb, gates, name):
    m, ka = ya.shape
    kb = yb.shape[1]
    d = wa.shape[1]
    tm = min(512, m)
    tn = min(512, d)
    nb = d // tn
    block_bytes = (_nbytes((tm, ka), BF16) + _nbytes((ka, tn), F32) + _nbytes((tm, kb), BF16)
                   + _nbytes((kb, tn), F32) + 3 * _nbytes((tm, tn), BF16))
    return pl.pallas_call(
        _merge_kernel,
        out_shape=jax.ShapeDtypeStruct((m, d), BF16),
        grid=(nb, m // tm),
        in_specs=[pl.BlockSpec((tm, ka), lambda j, i: (i, 0)),
                  pl.BlockSpec((ka, tn), lambda j, i: (0, j)),
                  pl.BlockSpec((tm, kb), lambda j, i: (i, 0)),
                  pl.BlockSpec((kb, tn), lambda j, i: (0, j)),
                  pl.BlockSpec((tm, tn), lambda j, i: (i, j)),
                  pl.BlockSpec((tm, tn), lambda j, i: (i, j + nb))],
        out_specs=pl.BlockSpec((tm, tn), lambda j, i: (i, j)),
        scratch_shapes=[pltpu.VMEM((ka, tn), BF16), pltpu.VMEM((kb, tn), BF16)],
        compiler_params=_params(("parallel", "arbitrary"), block_bytes, _nbytes((ka + kb, tn), BF16)),
        name=name,
    )(ya, wa, yb, wb, gates, gates)


def _da_kernel(lamv_ref, gsub_ref, q_ref, k_ref, v_ref, o_ref, acc_ref, m_ref, l_ref, *, tq, lam_init):
    qi = pl.program_id(2)
    hd = DA_HEAD_DIM
    m_ref[...] = jnp.full(m_ref.shape, MASK_VALUE, F32)
    l_ref[...] = jnp.zeros(l_ref.shape, F32)
    acc_ref[...] = jnp.zeros(acc_ref.shape, F32)
    q = q_ref[0]
    row_chunk = lax.broadcasted_iota(jnp.int32, (tq, tq), 0) // DA_CHUNK
    col_chunk = lax.broadcasted_iota(jnp.int32, (tq, tq), 1) // DA_CHUNK
    visible = col_chunk <= row_chunk

    def widen(x, width):
        return jnp.concatenate([x] * (width // LANES), axis=-1)

    def update(off, width, masked):
        kk = k_ref[0, pl.ds(off, width), :]
        vv = v_ref[0, pl.ds(off, width), :]
        for c in range(2):
            s = lax.dot_general(q[:, c * hd:(c + 1) * hd], kk[:, c * hd:(c + 1) * hd],
                                (((1,), (1,)), ((), ())), preferred_element_type=F32)
            if masked:
                s = jnp.where(visible, s, MASK_VALUE)
            m_old = m_ref[c]
            m_new = jnp.maximum(m_old, jnp.max(s, axis=-1, keepdims=True))
            p = jnp.exp2(s - widen(m_new, width))
            alpha = jnp.exp2(m_old - m_new)
            psum = p[:, :LANES]
            for g in range(1, width // LANES):
                psum = psum + p[:, g * LANES:(g + 1) * LANES]
            l_ref[c] = alpha * l_ref[c] + psum
            acc_ref[c] = widen(alpha, 2 * hd) * acc_ref[c] + jnp.dot(p.astype(BF16), vv,
                                                                     preferred_element_type=F32)
            m_ref[c] = m_new

    def wide_body(j, carry):
        update(pl.multiple_of(j * 2 * tq, 2 * tq), 2 * tq, False)
        return carry

    lax.fori_loop(0, qi // 2, wide_body, 0)

    @pl.when(qi % 2 == 1)
    def _():
        update(pl.multiple_of((qi - 1) * tq, tq), tq, False)

    update(pl.multiple_of(qi * tq, tq), tq, True)

    lv = lamv_ref[...]
    lam = (jnp.exp(jnp.sum(lv[0:1] * lv[1:2], axis=-1, keepdims=True))
           - jnp.exp(jnp.sum(lv[2:3] * lv[3:4], axis=-1, keepdims=True)) + lam_init)
    l1 = jnp.sum(l_ref[0], axis=-1, keepdims=True)
    l2 = jnp.sum(l_ref[1], axis=-1, keepdims=True)
    o = acc_ref[0] / l1 - lam * (acc_ref[1] / l2)
    ms = jnp.mean(o * o, axis=-1, keepdims=True)
    o_ref[0] = ((o * lax.rsqrt(ms + NORM_EPS) * gsub_ref[...]) * (1.0 - lam_init)).astype(o_ref.dtype)


def _diff_attention(qk, v, lamv, g_sub, n_heads, lam_init, name):
    b, s, _ = v.shape
    hw = 2 * DA_HEAD_DIM
    tq = min(512, s)
    block_bytes = 2 * _nbytes((tq, hw), BF16) + 2 * _nbytes((s, hw), BF16) + 6 * _nbytes((tq, 2 * tq), F32)
    return pl.pallas_call(
        functools.partial(_da_kernel, tq=tq, lam_init=lam_init),
        out_shape=jax.ShapeDtypeStruct((b, s, n_heads * hw), BF16),
        grid=(b, n_heads, s // tq),
        in_specs=[pl.BlockSpec((4, DA_HEAD_DIM), lambda bi, h, i: (0, 0)),
                  pl.BlockSpec((1, hw), lambda bi, h, i: (0, 0)),
                  pl.BlockSpec((1, tq, hw), lambda bi, h, i: (bi, i, h)),
                  pl.BlockSpec((1, s, hw), lambda bi, h, i: (bi, 0, n_heads + h)),
                  pl.BlockSpec((1, s, hw), lambda bi, h, i: (bi, 0, h))],
        out_specs=pl.BlockSpec((1, tq, hw), lambda bi, h, i: (bi, i, h)),
        scratch_shapes=[pltpu.VMEM((2, tq, hw), F32), pltpu.VMEM((2, tq, LANES), F32),
                        pltpu.VMEM((2, tq, LANES), F32)],
        compiler_params=_params(("parallel", "parallel", "arbitrary"), block_bytes),
        name=name,
    )(lamv, g_sub, qk, qk, v)


def _hg_forget(lb_ref, f_ref, rows, sl):
    lb = lb_ref[:, sl]
    return lb + (1.0 - lb) * jax.nn.sigmoid(f_ref[0, rows, sl])


def _hg_finish(o, gout_ref, gate):
    ms = jnp.mean(o * o, axis=-1, keepdims=True)
    return (o * lax.rsqrt(ms + NORM_EPS) * gout_ref[...]) * (gate * jax.nn.sigmoid(gate))


def _hg_decay_kernel(lb_ref, f_ref, o_ref, *, steps):
    lb = lb_ref[...]
    nl = -jnp.log(lb + (1.0 - lb) * jax.nn.sigmoid(f_ref[...]))
    worst = jnp.zeros((1, nl.shape[1]), F32)
    for c in range(steps // HG_CHUNK):
        worst = jnp.maximum(worst, jnp.sum(nl[c * HG_CHUNK:(c + 1) * HG_CHUNK], axis=0, keepdims=True))
    o_ref[...] = jnp.broadcast_to(worst, o_ref.shape)


def _hg_max_decay(hg2, lower, hg_w, steps, name):
    t = hg2.shape[0]
    w = min(1024, hg_w)
    ncb = hg_w // w
    sub = 8
    out = pl.pallas_call(
        functools.partial(_hg_decay_kernel, steps=steps),
        out_shape=jax.ShapeDtypeStruct((t // steps * sub, hg_w), F32),
        grid=(t // steps, ncb),
        in_specs=[pl.BlockSpec((1, w), lambda i, j: (0, j)),
                  pl.BlockSpec((steps, w), lambda i, j: (i, ncb + j))],
        out_specs=pl.BlockSpec((sub, w), lambda i, j: (i, j)),
        compiler_params=_params(("parallel", "parallel"), 4 * _nbytes((steps, w), F32)),
        name=name,
    )(lower, hg2)
    return jnp.max(out)


def _hg_kernel(lb_ref, gout_ref, q_ref, f_ref, i_ref, g_ref, o_ref, st_ref, *, heads, steps):
    t = pl.program_id(2)
    ch, hd = HG_CHUNK, HG_DIM

    @pl.when(t == 0)
    def _():
        st_ref[...] = jnp.zeros(st_ref.shape, F32)

    r = lax.broadcasted_iota(jnp.int32, (ch, ch), 0)
    c = lax.broadcasted_iota(jnp.int32, (ch, ch), 1)
    causal = c <= r
    cum_mat = jnp.concatenate([causal.astype(BF16)] * 3, axis=1)
    nt = (((1,), (1,)), ((), ()))
    tn = (((0,), (0,)), ((), ()))

    def chunk(rows, sl, st):
        q = q_ref[0, rows, sl]
        f = _hg_forget(lb_ref, f_ref, rows, sl)
        key = 1.0 - f
        lf = jnp.log(f)
        lf0 = lf.astype(BF16)
        r1 = lf - lf0.astype(F32)
        lf1 = r1.astype(BF16)
        lf2 = (r1 - lf1.astype(F32)).astype(BF16)
        cum = jnp.dot(cum_mat, jnp.concatenate([lf0, lf1, lf2], axis=0), preferred_element_type=F32)
        last = cum[ch - 1:ch]
        q_dec = (q * jnp.exp(cum)).astype(BF16)
        k_inv = (key * jnp.exp(-cum)).astype(BF16)
        k_state = (key * jnp.exp(last - cum)).astype(BF16)
        vals = i_ref[0, rows, sl].astype(BF16)
        scores = jnp.where(causal, lax.dot_general(q_dec, k_inv, nt, preferred_element_type=F32), 0.0)
        o = (jnp.dot(scores.astype(BF16), vals, preferred_element_type=F32)
             + lax.dot_general(q_dec, st.astype(BF16), nt, preferred_element_type=F32))
        st_next = jnp.exp(last) * st + lax.dot_general(vals, k_state, tn, preferred_element_type=F32)
        return _hg_finish(o, gout_ref, g_ref[0, rows, sl]), st_next

    for h in range(heads):
        sl = slice(h * hd, (h + 1) * hd)
        st = st_ref[h]
        for ci in range(steps // ch):
            rows = slice(ci * ch, (ci + 1) * ch)
            y, st = chunk(rows, sl, st)
            o_ref[0, rows, sl] = y.astype(o_ref.dtype)
        st_ref[h] = st


def _hg_seq_kernel(lb_ref, gout_ref, q_ref, f_ref, i_ref, g_ref, o_ref, st_ref, *, steps):
    t = pl.program_id(2)
    hd = HG_DIM

    @pl.when(t == 0)
    def _():
        st_ref[...] = jnp.zeros(st_ref.shape, F32)

    lane = lax.broadcasted_iota(jnp.int32, (hd, hd), 1)
    sl = slice(0, hd)

    def block(bi, carry):
        rows = pl.ds(pl.multiple_of(bi * hd, hd), hd)
        q = q_ref[0, rows, :]
        f = _hg_forget(lb_ref, f_ref, rows, sl)
        key = 1.0 - f
        vals_t = i_ref[0, rows, :].T
        st = st_ref[...]
        out_t = jnp.zeros((hd, hd), F32)
        for s in range(hd):
            st = f[s:s + 1] * st + vals_t[:, s:s + 1] * key[s:s + 1]
            out_t = jnp.where(lane == s, jnp.sum(st * q[s:s + 1], axis=1, keepdims=True), out_t)
        st_ref[...] = st
        o_ref[0, rows, :] = _hg_finish(out_t.T, gout_ref, g_ref[0, rows, :]).astype(o_ref.dtype)
        return carry

    lax.fori_loop(0, steps // hd, block, 0)


def _hgrn2(hg, lower, g_out, n_heads, heads, kernel_fn, name):
    b, s, _ = hg.shape
    steps = min(512, s)
    nhb = n_heads // heads
    w = heads * HG_DIM
    block_bytes = 4 * _nbytes((steps, w), F32) + _nbytes((steps, w), BF16)

    def spec(part):
        return pl.BlockSpec((1, steps, w), lambda bi, h, t, p=part: (bi, t, p * nhb + h))

    return pl.pallas_call(
        functools.partial(kernel_fn, steps=steps),
        out_shape=jax.ShapeDtypeStruct((b, s, n_heads * HG_DIM), BF16),
        grid=(b, nhb, s // steps),
        in_specs=[pl.BlockSpec((1, w), lambda bi, h, t: (0, h)),
                  pl.BlockSpec((1, HG_DIM), lambda bi, h, t: (0, 0)),
                  spec(0), spec(1), spec(2), spec(3)],
        out_specs=pl.BlockSpec((1, steps, w), lambda bi, h, t: (bi, t, h)),
        scratch_shapes=[pltpu.VMEM((heads, HG_DIM, HG_DIM) if heads > 1 else (HG_DIM, HG_DIM), F32)],
        compiler_params=_params(("parallel", "parallel", "arbitrary"), block_bytes),
        name=name,
    )(lower, g_out, hg, hg, hg, hg)


def _hgrn2_any(hg, lower, g_out, n_heads):
    b, s, c4 = hg.shape
    hg_w = c4 // 4
    heads = min(4, n_heads)
    decay = _hg_max_decay(hg.reshape(b * s, c4), lower, hg_w, min(512, s), "hgrn2_decay")
    return lax.cond(
        decay < HG_FAST_MAX_DECAY,
        lambda: _hgrn2(hg, lower, g_out, n_heads, heads, functools.partial(_hg_kernel, heads=heads), "hgrn2"),
        lambda: _hgrn2(hg, lower, g_out, n_heads, 1, _hg_seq_kernel, "hgrn2_stepwise"))


def _xattn_kernel(q_ref, k_ref, v_ref, o_ref, *, n_heads):
    hd = X_HEAD_DIM
    for h in range(n_heads):
        sl = slice(h * hd, (h + 1) * hd)
        s = lax.dot_general(q_ref[0, :, sl], k_ref[0, :, sl], (((1,), (1,)), ((), ())),
                            preferred_element_type=F32)
        p = jnp.exp(s - jnp.max(s, axis=-1, keepdims=True))
        denom = jnp.sum(p, axis=-1, keepdims=True)
        o = jnp.dot(p.astype(BF16), v_ref[0, :, sl], preferred_element_type=F32)
        o_ref[0, :, sl] = (o / denom).astype(o_ref.dtype)


def _cross_attention(q, k, v, n_heads, name):
    b, s, w = q.shape
    mlen = k.shape[1]
    tq = min(1024, s)
    block_bytes = 2 * _nbytes((tq, w), BF16) + 2 * _nbytes((mlen, w), BF16) + 3 * _nbytes((tq, mlen), F32)
    return pl.pallas_call(
        functools.partial(_xattn_kernel, n_heads=n_heads),
        out_shape=jax.ShapeDtypeStruct((b, s, w), BF16),
        grid=(b, s // tq),
        in_specs=[pl.BlockSpec((1, tq, w), lambda bi, i: (bi, i, 0)),
                  pl.BlockSpec((1, mlen, w), lambda bi, i: (bi, 0, 0)),
                  pl.BlockSpec((1, mlen, w), lambda bi, i: (bi, 0, 0))],
        out_specs=pl.BlockSpec((1, tq, w), lambda bi, i: (bi, i, 0)),
        compiler_params=_params(("parallel", "parallel"), block_bytes),
        name=name,
    )(q, k, v)


def _router_kernel(h_ref, g_ref, whi_ref, wlo_ref, b_ref, n_ref, lg_ref):
    x = h_ref[...]
    ms = jnp.mean(x * x, axis=-1, keepdims=True)
    n = x * lax.rsqrt(ms + NORM_EPS) * g_ref[...]
    n_hi = n.astype(BF16)
    n_lo = (n - n_hi.astype(F32)).astype(BF16)
    n_ref[...] = n
    lg_ref[...] = (jnp.dot(n_hi, whi_ref[...], preferred_element_type=F32)
                   + jnp.dot(n_lo, whi_ref[...], preferred_element_type=F32)
                   + jnp.dot(n_hi, wlo_ref[...], preferred_element_type=F32) + b_ref[...])


def _router(h, g, w_hi, w_lo, bias, name):
    m, d = h.shape
    nl = w_hi.shape[1]
    tm = min(256, m)
    block_bytes = 2 * _nbytes((tm, d), F32) + 2 * _nbytes((d, nl), BF16)
    return pl.pallas_call(
        _router_kernel,
        out_shape=(jax.ShapeDtypeStruct((m, d), F32), jax.ShapeDtypeStruct((m, nl), F32)),
        grid=(m // tm,),
        in_specs=[pl.BlockSpec((tm, d), lambda i: (i, 0)),
                  pl.BlockSpec((1, d), lambda i: (0, 0)),
                  pl.BlockSpec((d, nl), lambda i: (0, 0)),
                  pl.BlockSpec((d, nl), lambda i: (0, 0)),
                  pl.BlockSpec((1, nl), lambda i: (0, 0))],
        out_specs=(pl.BlockSpec((tm, d), lambda i: (i, 0)), pl.BlockSpec((tm, nl), lambda i: (i, 0))),
        compiler_params=_params(("parallel",), block_bytes),
        name=name,
    )(h, g, w_hi, w_lo, bias)


def _row_copies(src_ref, dst_ref, idx_ref, idx0, idx_stride, dst0, n, sem, *, start):
    def body(r, carry):
        cp = pltpu.make_async_copy(src_ref.at[pl.ds(idx_ref[idx0 + r * idx_stride], 1)],
                                   dst_ref.at[pl.ds(dst0 + r, 1)], sem)
        if start:
            cp.start()
        else:
            cp.wait()
        return carry

    lax.fori_loop(0, n, body, 0, unroll=8)


def _expert_kernel(tok_ref, be_ref, nreal_ref, nf_ref, wg_ref, wu_ref, wd_ref, o_ref, xbuf, sems, *, blk):
    i = pl.program_id(0)
    n_real = nreal_ref[0]

    def gather(block, start):
        slot = block % 2
        _row_copies(nf_ref, xbuf.at[slot], tok_ref, block * blk, 1, 0, blk, sems.at[slot], start=start)

    @pl.when(jnp.logical_and(i == 0, n_real > 0))
    def _():
        gather(i, True)

    @pl.when(i + 1 < n_real)
    def _():
        gather(i + 1, True)

    @pl.when(i < n_real)
    def _():
        gather(i, False)
        x = xbuf[i % 2].astype(BF16)
        hg = jnp.dot(x, wg_ref[0], preferred_element_type=F32)
        hu = jnp.dot(x, wu_ref[0], preferred_element_type=F32)
        hid = (hg * jax.nn.sigmoid(hg) * hu).astype(BF16)
        o_ref[...] = jnp.dot(hid, wd_ref[0], preferred_element_type=F32).astype(o_ref.dtype)

    @pl.when(i >= n_real)
    def _():
        o_ref[...] = jnp.zeros(o_ref.shape, o_ref.dtype)


def _experts(nf, buf_tok, blk_expert, n_real, wg, wu, wd, blk, name):
    cap = buf_tok.shape[0]
    d = nf.shape[1]
    ff = wg.shape[2]
    block_bytes = 2 * _nbytes((blk, d), F32) + 3 * _nbytes((d, ff), BF16) + _nbytes((blk, d), F32)
    return pl.pallas_call(
        functools.partial(_expert_kernel, blk=blk),
        out_shape=jax.ShapeDtypeStruct((cap, d), F32),
        grid_spec=pltpu.PrefetchScalarGridSpec(
            num_scalar_prefetch=3,
            grid=(cap // blk,),
            in_specs=[pl.BlockSpec(memory_space=pl.ANY),
                      pl.BlockSpec((1, d, ff), lambda i, tk, be, nr: (be[i], 0, 0)),
                      pl.BlockSpec((1, d, ff), lambda i, tk, be, nr: (be[i], 0, 0)),
                      pl.BlockSpec((1, ff, d), lambda i, tk, be, nr: (be[i], 0, 0))],
            out_specs=pl.BlockSpec((blk, d), lambda i, tk, be, nr: (i, 0)),
            scratch_shapes=[pltpu.VMEM((2, blk, d), F32), pltpu.SemaphoreType.DMA((2,))]),
        compiler_params=_params(("arbitrary",), block_bytes),
        name=name,
    )(buf_tok, blk_expert, n_real, nf, wg, wu, wd)


def _combine_kernel(slot_ref, h_ref, gate_ref, ys_ref, o_ref, ybuf, sems, *, tm):
    i = pl.program_id(0)

    def gather(block, start):
        s = block % 2
        for k in range(TOP_K):
            _row_copies(ys_ref, ybuf.at[s], slot_ref, block * tm * TOP_K + k, TOP_K, k * tm, tm, sems.at[s],
                        start=start)

    @pl.when(i == 0)
    def _():
        gather(i, True)

    @pl.when(i + 1 < pl.num_programs(0))
    def _():
        gather(i + 1, True)

    gather(i, False)
    acc = h_ref[...]
    gate = gate_ref[...]
    for k in range(TOP_K):
        acc = acc + gate[:, k:k + 1] * ybuf[i % 2, pl.ds(k * tm, tm), :]
    o_ref[...] = acc


def _combine(h, gate, ys, slot, name):
    m, d = h.shape
    tm = min(128, m)
    block_bytes = 2 * _nbytes((tm, d), F32) + TOP_K * _nbytes((tm, d), F32)
    return pl.pallas_call(
        functools.partial(_combine_kernel, tm=tm),
        out_shape=jax.ShapeDtypeStruct((m, d), F32),
        grid_spec=pltpu.PrefetchScalarGridSpec(
            num_scalar_prefetch=1,
            grid=(m // tm,),
            in_specs=[pl.BlockSpec((tm, d), lambda i, sl: (i, 0)),
                      pl.BlockSpec((tm, TOP_K), lambda i, sl: (i, 0)),
                      pl.BlockSpec(memory_space=pl.ANY)],
            out_specs=pl.BlockSpec((tm, d), lambda i, sl: (i, 0)),
            scratch_shapes=[pltpu.VMEM((2, TOP_K * tm, d), F32), pltpu.SemaphoreType.DMA((2,))]),
        compiler_params=_params(("arbitrary",), block_bytes),
        name=name,
    )(slot, h, gate, ys)


def _route(logits, n_groups, n_experts, blk):
    t = logits.shape[0]
    epg = n_experts // n_groups
    grp_logits = logits[:, :n_groups]
    grp_prob = jax.nn.softmax(grp_logits, axis=-1)
    grp = jnp.argmax(grp_logits, axis=-1).astype(jnp.int32)
    grp_w = jnp.take_along_axis(grp_prob, grp[:, None], axis=-1)
    exp_logits = logits[:, n_groups:n_groups + n_experts].reshape(t, n_groups, epg)
    in_grp = jnp.take_along_axis(exp_logits, grp[:, None, None], axis=1)[:, 0]
    top_val, top_idx = lax.top_k(in_grp, TOP_K)
    gate = grp_w * jax.nn.softmax(top_val, axis=-1)
    expert = (grp[:, None] * epg + top_idx.astype(jnp.int32)).reshape(-1)
    n_assign = t * TOP_K
    e_sorted, order = lax.sort_key_val(expert, jnp.arange(n_assign, dtype=jnp.int32))
    experts = jnp.arange(n_experts, dtype=jnp.int32)
    starts = jnp.searchsorted(e_sorted, experts, side='left').astype(jnp.int32)
    counts = jnp.searchsorted(e_sorted, experts, side='right').astype(jnp.int32) - starts
    padded = (counts + blk - 1) // blk * blk
    pad_ends = jnp.cumsum(padded)
    pad_starts = pad_ends - padded
    dest = pad_starts[e_sorted] + jnp.arange(n_assign, dtype=jnp.int32) - starts[e_sorted]
    _, slot = lax.sort_key_val(order, dest)
    cap = n_assign + n_experts * blk
    n_blk = cap // blk
    blk_expert = jnp.minimum(jnp.searchsorted(pad_ends, jnp.arange(n_blk, dtype=jnp.int32) * blk, side='right'),
                             n_experts - 1).astype(jnp.int32)
    row = jnp.arange(cap, dtype=jnp.int32)
    row_expert = jnp.repeat(blk_expert, blk)
    within = row - pad_starts[row_expert]
    src = jnp.clip(within + starts[row_expert], 0, n_assign - 1)
    buf_tok = jnp.where(within < counts[row_expert], order[src] // TOP_K, 0).astype(jnp.int32)
    n_real = (pad_ends[-1] // blk).astype(jnp.int32).reshape(1)
    return gate, buf_tok, blk_expert, n_real, slot.astype(jnp.int32)


def _diff_lambda_init(layer):
    return 0.8 - 0.6 * math.exp(-0.3 * layer)


def _tile_cols(v, n):
    return jnp.tile(v.reshape(1, -1).astype(F32), (1, n // v.size))


def kernel(x, mem, g_mix, w_in, g_da_q, g_da_k, lam_q1, lam_k1, lam_q2, lam_k2, g_da_sub, hg_lower, g_hg_out,
           w_up_a, w_up_b, w_gate, b_gate, w_out, g_cross, g_mem, w_xq, w_xkv, g_xq, g_xk, w_xo, g_ffn, w_grp,
           b_grp, w_erouter, b_erouter, w_e_gate, w_e_up, w_e_down):
    b, s, d = x.shape
    t = b * s
    depth = g_mix.shape[0]
    da_w = w_up_a.shape[1]
    hg_w = w_up_b.shape[1]
    da_heads = da_w // (2 * DA_HEAD_DIM)
    hg_heads = hg_w // HG_DIM
    x_w = w_xq.shape[2]
    x_heads = x_w // X_HEAD_DIM
    n_groups = w_grp.shape[2]
    n_experts = w_erouter.shape[2]
    moe_blk = min(256, t)

    lower_bounds = jnp.cumsum(jax.nn.softmax(hg_lower.astype(F32), axis=0), axis=0)
    h = x.reshape(t, d)
    mem2 = mem.reshape(b * mem.shape[1], d)
    for l in range(depth):
        lam_init = _diff_lambda_init(l)
        w_in_l = w_in[l]
        n = _rmsnorm(h, g_mix[l], BF16, "rms_mix")
        q_scale = DA_HEAD_DIM ** -0.5 * math.log2(math.e)
        qk_gain = jnp.concatenate([_tile_cols(g_da_q[l], da_w) * q_scale, _tile_cols(g_da_k[l], da_w)], axis=1)
        qk = _matmul(n, w_in_l, col0=0, ncols=2 * da_w, out_dtype=BF16,
                     epilogue=functools.partial(_ep_group_norm, group=DA_HEAD_DIM),
                     col_extras=[(qk_gain, 0)], name="proj_qk")
        v = _matmul(n, w_in_l, col0=2 * da_w, ncols=da_w, out_dtype=BF16, name="proj_v")
        hg = _matmul(n, w_in_l, col0=3 * da_w, ncols=4 * hg_w, out_dtype=F32, name="proj_hg")
        gates = _matmul(n, w_gate[l], out_dtype=BF16, epilogue=_ep_bias_sigmoid,
                        col_extras=[(b_gate[l].reshape(1, -1).astype(F32), 0)], name="proj_gate")
        lamv = jnp.stack([lam_q1[l], lam_k1[l], lam_q2[l], lam_k2[l]]).astype(F32)
        y_a = _diff_attention(qk.reshape(b, s, 2 * da_w), v.reshape(b, s, da_w), lamv,
                              g_da_sub[l].reshape(1, -1).astype(F32), da_heads, lam_init, "diff_attn")
        y_b = _hgrn2_any(hg.reshape(b, s, 4 * hg_w), lower_bounds[l].reshape(1, hg_w),
                         g_hg_out[l].reshape(1, HG_DIM).astype(F32), hg_heads)
        merged = _merge(y_a.reshape(t, da_w), w_up_a[l], y_b.reshape(t, hg_w), w_up_b[l], gates, "merge")
        h = _matmul(merged, w_out[l], out_dtype=F32, epilogue=_ep_residual,
                    full_extras=[(h, 0)], name="mix_out")
        nc = _rmsnorm(h, g_cross[l], BF16, "rms_cross")
        xq = _matmul(nc, w_xq[l], out_dtype=BF16,
                     epilogue=functools.partial(_ep_group_norm, group=X_HEAD_DIM),
                     col_extras=[(_tile_cols(g_xq[l], x_w) * X_HEAD_DIM ** -0.5, 0)], name="xattn_q")
        nm = _rmsnorm(mem2, g_mem[l], BF16, "rms_mem")
        w_xkv_l = w_xkv[l]
        xk = _matmul(nm, w_xkv_l, col0=0, ncols=x_w, out_dtype=BF16,
                     epilogue=functools.partial(_ep_group_norm, group=X_HEAD_DIM),
                     col_extras=[(_tile_cols(g_xk[l], x_w), 0)], name="xattn_k")
        xv = _matmul(nm, w_xkv_l, col0=x_w, ncols=x_w, out_dtype=BF16, name="xattn_v")
        xo = _cross_attention(xq.reshape(b, s, x_w), xk.reshape(b, -1, x_w), xv.reshape(b, -1, x_w), x_heads,
                              "xattn")
        h = _matmul(xo.reshape(t, x_w), w_xo[l], out_dtype=F32, epilogue=_ep_residual,
                    full_extras=[(h, 0)], name="xattn_out")
        n_logit = -(-(n_groups + n_experts) // LANES) * LANES
        w_r = jnp.zeros((d, n_logit), F32).at[:, :n_groups].set(w_grp[l]).at[:, n_groups:n_groups + n_experts].set(
            w_erouter[l])
        b_r = jnp.zeros((1, n_logit), F32).at[0, :n_groups].set(b_grp[l]).at[0, n_groups:n_groups + n_experts].set(
            b_erouter[l])
        w_r_hi = w_r.astype(BF16)
        w_r_lo = (w_r - w_r_hi.astype(F32)).astype(BF16)
        nf, logits = _router(h, g_ffn[l].reshape(1, d).astype(F32), w_r_hi, w_r_lo, b_r, "router")
        gate, buf_tok, blk_expert, n_real, slot = _route(logits, n_groups, n_experts, moe_blk)
        ys = _experts(nf, buf_tok, blk_expert, n_real, w_e_gate[l].astype(BF16), w_e_up[l].astype(BF16),
                      w_e_down[l].astype(BF16), moe_blk, "moe_experts")
        h = _combine(h, gate, ys, slot, "moe_combine")
    return h.reshape(b, s, d)
```

```python
import functools
import math

import jax
import jax.numpy as jnp
from jax import lax
from jax.experimental import pallas as pl
from jax.experimental.pallas import tpu as pltpu

F32 = jnp.float32
BF16 = jnp.bfloat16

NORM_EPS = 1e-6
DA_HEAD_DIM = 128
DA_CHUNK = 64
HG_DIM = 128
HG_CHUNK = 64
HG_FAST_MAX_DECAY = 60.0
X_HEAD_DIM = 256
TOP_K = 2
ROW_COPY_UNROLL = 8
LANES = 128
MASK_VALUE = -1e30
V7X_VMEM_BYTES = 64 * 1024 * 1024
VMEM_CAP_BYTES = V7X_VMEM_BYTES - 6 * 1024 * 1024


def _vmem_limit(block_bytes, scratch_bytes):
    return int(min(VMEM_CAP_BYTES, max(32 * 1024 * 1024, 2 * block_bytes + scratch_bytes + 16 * 1024 * 1024)))


def _params(semantics, block_bytes, scratch_bytes=0):
    return pltpu.CompilerParams(dimension_semantics=semantics,
                                vmem_limit_bytes=_vmem_limit(block_bytes, scratch_bytes))


def _nbytes(shape, dtype):
    return math.prod(shape) * jnp.dtype(dtype).itemsize


def _rms_kernel(x_ref, g_ref, o_ref):
    x = x_ref[...].astype(F32)
    ms = jnp.mean(x * x, axis=-1, keepdims=True)
    o_ref[...] = (x * lax.rsqrt(ms + NORM_EPS) * g_ref[...]).astype(o_ref.dtype)


def _rmsnorm(x, g, out_dtype, name):
    m, d = x.shape
    tm = min(256, m)
    return pl.pallas_call(
        _rms_kernel,
        out_shape=jax.ShapeDtypeStruct((m, d), out_dtype),
        grid=(m // tm,),
        in_specs=[pl.BlockSpec((tm, d), lambda i: (i, 0)), pl.BlockSpec((1, d), lambda i: (0, 0))],
        out_specs=pl.BlockSpec((tm, d), lambda i: (i, 0)),
        compiler_params=_params(("parallel",), _nbytes((tm, d), x.dtype) + _nbytes((tm, d), out_dtype)),
        name=name,
    )(x, g.reshape(1, d).astype(F32))


def _ep_plain(acc, o_ref):
    o_ref[...] = acc.astype(o_ref.dtype)


def _ep_group_norm(acc, o_ref, gain, *, group):
    for c in range(acc.shape[1] // group):
        sl = slice(c * group, (c + 1) * group)
        xg = acc[:, sl]
        ms = jnp.mean(xg * xg, axis=-1, keepdims=True)
        o_ref[:, sl] = (xg * lax.rsqrt(ms + NORM_EPS) * gain[:, sl]).astype(o_ref.dtype)


def _ep_bias_sigmoid(acc, o_ref, bias):
    o_ref[...] = jax.nn.sigmoid(acc + bias).astype(o_ref.dtype)


def _ep_residual(acc, o_ref, res):
    o_ref[...] = (res + acc).astype(o_ref.dtype)


def _row_block(i):
    return jnp.maximum(i - 1, 0)


def _mm_kernel(a_ref, w_ref, *rest, epilogue):
    *extra, o_ref, wb_ref = rest
    i = pl.program_id(1)

    @pl.when(i == 0)
    def _():
        wb_ref[...] = w_ref[...].astype(BF16)

    @pl.when(i > 0)
    def _():
        acc = jnp.dot(a_ref[...], wb_ref[...], preferred_element_type=F32)
        epilogue(acc, o_ref, *[e[...] for e in extra])


def _matmul(a, w, *, col0=0, ncols=None, out_dtype, epilogue=_ep_plain, col_extras=(), full_extras=(), name):
    m, k = a.shape
    ncols = w.shape[1] - col0 if ncols is None else ncols
    tm = min(1024, m)
    tn = min(512, ncols)
    assert m % tm == 0 and ncols % tn == 0 and col0 % tn == 0
    in_specs = [pl.BlockSpec((tm, k), lambda j, i: (_row_block(i), 0)),
                pl.BlockSpec((k, tn), lambda j, i, o=col0 // tn: (0, j + o))]
    operands = [a, w]
    block_bytes = _nbytes((tm, k), a.dtype) + _nbytes((k, tn), w.dtype) + _nbytes((tm, tn), out_dtype)
    for vec, c0 in col_extras:
        assert c0 % tn == 0
        in_specs.append(pl.BlockSpec((1, tn), lambda j, i, o=c0 // tn: (0, j + o)))
        operands.append(vec)
    for arr, c0 in full_extras:
        assert c0 % tn == 0
        in_specs.append(pl.BlockSpec((tm, tn), lambda j, i, o=c0 // tn: (_row_block(i), j + o)))
        operands.append(arr)
        block_bytes += _nbytes((tm, tn), arr.dtype)
    return pl.pallas_call(
        functools.partial(_mm_kernel, epilogue=epilogue),
        out_shape=jax.ShapeDtypeStruct((m, ncols), out_dtype),
        grid=(ncols // tn, m // tm + 1),
        in_specs=in_specs,
        out_specs=pl.BlockSpec((tm, tn), lambda j, i: (_row_block(i), j)),
        scratch_shapes=[pltpu.VMEM((k, tn), BF16)],
        compiler_params=_params(("parallel", "arbitrary"), block_bytes, _nbytes((k, tn), BF16)),
        name=name,
    )(*operands)


def _merge_kernel(ya_ref, wa_ref, yb_ref, wb_ref, ga_ref, gb_ref, o_ref, wa_bf_ref, wb_bf_ref):
    i = pl.program_id(1)

    @pl.when(i == 0)
    def _():
        wa_bf_ref[...] = wa_ref[...].astype(BF16)
        wb_bf_ref[...] = wb_ref[...].astype(BF16)

    @pl.when(i > 0)
    def _():
        pa = jnp.dot(ya_ref[...], wa_bf_ref[...], preferred_element_type=F32)
        pb = jnp.dot(yb_ref[...], wb_bf_ref[...], preferred_element_type=F32)
        o_ref[...] = (ga_ref[...].astype(F32) * pa + gb_ref[...].astype(F32) * pb).astype(o_ref.dtype)


def _merge(ya, wa, yb, wb, gates, name):
    m, ka = ya.shape
    kb = yb.shape[1]
    d = wa.shape[1]
    tm = min(512, m)
    tn = min(512, d)
    nb = d // tn
    block_bytes = (_nbytes((tm, ka), BF16) + _nbytes((ka, tn), F32) + _nbytes((tm, kb), BF16)
                   + _nbytes((kb, tn), F32) + 3 * _nbytes((tm, tn), BF16))
    return pl.pallas_call(
        _merge_kernel,
        out_shape=jax.ShapeDtypeStruct((m, d), BF16),
        grid=(nb, m // tm + 1),
        in_specs=[pl.BlockSpec((tm, ka), lambda j, i: (_row_block(i), 0)),
                  pl.BlockSpec((ka, tn), lambda j, i: (0, j)),
                  pl.BlockSpec((tm, kb), lambda j, i: (_row_block(i), 0)),
                  pl.BlockSpec((kb, tn), lambda j, i: (0, j)),
                  pl.BlockSpec((tm, tn), lambda j, i: (_row_block(i), j)),
                  pl.BlockSpec((tm, tn), lambda j, i: (_row_block(i), j + nb))],
        out_specs=pl.BlockSpec((tm, tn), lambda j, i: (_row_block(i), j)),
        scratch_shapes=[pltpu.VMEM((ka, tn), BF16), pltpu.VMEM((kb, tn), BF16)],
        compiler_params=_params(("parallel", "arbitrary"), block_bytes, _nbytes((ka + kb, tn), BF16)),
        name=name,
    )(ya, wa, yb, wb, gates, gates)


def _da_kernel(lamv_ref, gsub_ref, q_ref, k_ref, v_ref, o_ref, acc_ref, m_ref, l_ref, *, tq, lam_init):
    qi = pl.program_id(2)
    hd = DA_HEAD_DIM
    m_ref[...] = jnp.full(m_ref.shape, MASK_VALUE, F32)
    l_ref[...] = jnp.zeros(l_ref.shape, F32)
    acc_ref[...] = jnp.zeros(acc_ref.shape, F32)
    q = q_ref[0]
    row_chunk = lax.broadcasted_iota(jnp.int32, (tq, tq), 0) // DA_CHUNK
    col_chunk = lax.broadcasted_iota(jnp.int32, (tq, tq), 1) // DA_CHUNK
    visible = col_chunk <= row_chunk

    def widen(x, width):
        return jnp.concatenate([x] * (width // LANES), axis=-1)

    def update(off, width, masked):
        kk = k_ref[0, pl.ds(off, width), :]
        vv = v_ref[0, pl.ds(off, width), :]
        for c in range(2):
            s = lax.dot_general(q[:, c * hd:(c + 1) * hd], kk[:, c * hd:(c + 1) * hd],
                                (((1,), (1,)), ((), ())), preferred_element_type=F32)
            if masked:
                s = jnp.where(visible, s, MASK_VALUE)
            m_old = m_ref[c]
            m_new = jnp.maximum(m_old, jnp.max(s, axis=-1, keepdims=True))
            p = jnp.exp2(s - widen(m_new, width))
            alpha = jnp.exp2(m_old - m_new)
            psum = p[:, :LANES]
            for g in range(1, width // LANES):
                psum = psum + p[:, g * LANES:(g + 1) * LANES]
            l_ref[c] = alpha * l_ref[c] + psum
            acc_ref[c] = widen(alpha, 2 * hd) * acc_ref[c] + jnp.dot(p.astype(BF16), vv,
                                                                     preferred_element_type=F32)
            m_ref[c] = m_new

    def wide_body(j, carry):
        update(pl.multiple_of(j * 2 * tq, 2 * tq), 2 * tq, False)
        return carry

    lax.fori_loop(0, qi // 2, wide_body, 0)

    @pl.when(qi % 2 == 1)
    def _():
        update(pl.multiple_of((qi - 1) * tq, tq), tq, False)

    update(pl.multiple_of(qi * tq, tq), tq, True)

    lv = lamv_ref[...]
    lam = (jnp.exp(jnp.sum(lv[0:1] * lv[1:2], axis=-1, keepdims=True))
           - jnp.exp(jnp.sum(lv[2:3] * lv[3:4], axis=-1, keepdims=True)) + lam_init)
    l1 = jnp.sum(l_ref[0], axis=-1, keepdims=True)
    l2 = jnp.sum(l_ref[1], axis=-1, keepdims=True)
    o = acc_ref[0] / l1 - lam * (acc_ref[1] / l2)
    ms = jnp.mean(o * o, axis=-1, keepdims=True)
    o_ref[0] = ((o * lax.rsqrt(ms + NORM_EPS) * gsub_ref[...]) * (1.0 - lam_init)).astype(o_ref.dtype)


def _diff_attention(qk, v, lamv, g_sub, n_heads, lam_init, name):
    b, s, _ = v.shape
    hw = 2 * DA_HEAD_DIM
    tq = min(512, s)
    block_bytes = 2 * _nbytes((tq, hw), BF16) + 2 * _nbytes((s, hw), BF16) + 6 * _nbytes((tq, 2 * tq), F32)
    return pl.pallas_call(
        functools.partial(_da_kernel, tq=tq, lam_init=lam_init),
        out_shape=jax.ShapeDtypeStruct((b, s, n_heads * hw), BF16),
        grid=(b, n_heads, s // tq),
        in_specs=[pl.BlockSpec((4, DA_HEAD_DIM), lambda bi, h, i: (0, 0)),
                  pl.BlockSpec((1, hw), lambda bi, h, i: (0, 0)),
                  pl.BlockSpec((1, tq, hw), lambda bi, h, i: (bi, i, h)),
                  pl.BlockSpec((1, s, hw), lambda bi, h, i: (bi, 0, n_heads + h)),
                  pl.BlockSpec((1, s, hw), lambda bi, h, i: (bi, 0, h))],
        out_specs=pl.BlockSpec((1, tq, hw), lambda bi, h, i: (bi, i, h)),
        scratch_shapes=[pltpu.VMEM((2, tq, hw), F32), pltpu.VMEM((2, tq, LANES), F32),
                        pltpu.VMEM((2, tq, LANES), F32)],
        compiler_params=_params(("parallel", "parallel", "arbitrary"), block_bytes),
        name=name,
    )(lamv, g_sub, qk, qk, v)


def _hg_forget(lb_ref, f_ref, rows, sl):
    lb = lb_ref[:, sl]
    return lb + (1.0 - lb) * jax.nn.sigmoid(f_ref[0, rows, sl])


def _hg_finish(o, gout_ref, gate):
    ms = jnp.mean(o * o, axis=-1, keepdims=True)
    return (o * lax.rsqrt(ms + NORM_EPS) * gout_ref[...]) * (gate * jax.nn.sigmoid(gate))


def _hg_decay_kernel(lb_ref, f_ref, o_ref, *, steps):
    lb = lb_ref[...]
    nl = -jnp.log(lb + (1.0 - lb) * jax.nn.sigmoid(f_ref[...]))
    worst = jnp.zeros((1, nl.shape[1]), F32)
    for c in range(steps // HG_CHUNK):
        worst = jnp.maximum(worst, jnp.sum(nl[c * HG_CHUNK:(c + 1) * HG_CHUNK], axis=0, keepdims=True))
    o_ref[...] = jnp.broadcast_to(worst, o_ref.shape)


def _hg_max_decay(hg2, lower, hg_w, steps, name):
    t = hg2.shape[0]
    w = min(1024, hg_w)
    ncb = hg_w // w
    sub = 8
    out = pl.pallas_call(
        functools.partial(_hg_decay_kernel, steps=steps),
        out_shape=jax.ShapeDtypeStruct((t // steps * sub, hg_w), F32),
        grid=(t // steps, ncb),
        in_specs=[pl.BlockSpec((1, w), lambda i, j: (0, j)),
                  pl.BlockSpec((steps, w), lambda i, j: (i, ncb + j))],
        out_specs=pl.BlockSpec((sub, w), lambda i, j: (i, j)),
        compiler_params=_params(("parallel", "parallel"), 4 * _nbytes((steps, w), F32)),
        name=name,
    )(lower, hg2)
    return jnp.max(out)


def _hg_kernel(lb_ref, gout_ref, q_ref, f_ref, i_ref, g_ref, o_ref, st_ref, *, heads, steps):
    t = pl.program_id(2)
    ch, hd = HG_CHUNK, HG_DIM

    @pl.when(t == 0)
    def _():
        st_ref[...] = jnp.zeros(st_ref.shape, F32)

    r = lax.broadcasted_iota(jnp.int32, (ch, ch), 0)
    c = lax.broadcasted_iota(jnp.int32, (ch, ch), 1)
    causal = c <= r
    cum_mat = jnp.concatenate([causal.astype(BF16)] * 3, axis=1)
    nt = (((1,), (1,)), ((), ()))
    tn = (((0,), (0,)), ((), ()))

    def chunk(rows, sl, st):
        q = q_ref[0, rows, sl]
        f = _hg_forget(lb_ref, f_ref, rows, sl)
        key = 1.0 - f
        lf = jnp.log(f)
        lf0 = lf.astype(BF16)
        r1 = lf - lf0.astype(F32)
        lf1 = r1.astype(BF16)
        lf2 = (r1 - lf1.astype(F32)).astype(BF16)
        cum = jnp.dot(cum_mat, jnp.concatenate([lf0, lf1, lf2], axis=0), preferred_element_type=F32)
        last = cum[ch - 1:ch]
        q_dec = (q * jnp.exp(cum)).astype(BF16)
        k_inv = (key * jnp.exp(-cum)).astype(BF16)
        k_state = (key * jnp.exp(last - cum)).astype(BF16)
        vals = i_ref[0, rows, sl].astype(BF16)
        scores = jnp.where(causal, lax.dot_general(q_dec, k_inv, nt, preferred_element_type=F32), 0.0)
        o = (jnp.dot(scores.astype(BF16), vals, preferred_element_type=F32)
             + lax.dot_general(q_dec, st.astype(BF16), nt, preferred_element_type=F32))
        st_next = jnp.exp(last) * st + lax.dot_general(vals, k_state, tn, preferred_element_type=F32)
        return _hg_finish(o, gout_ref, g_ref[0, rows, sl]), st_next

    for h in range(heads):
        sl = slice(h * hd, (h + 1) * hd)
        st = st_ref[h]
        for ci in range(steps // ch):
            rows = slice(ci * ch, (ci + 1) * ch)
            y, st = chunk(rows, sl, st)
            o_ref[0, rows, sl] = y.astype(o_ref.dtype)
        st_ref[h] = st


def _hg_seq_kernel(lb_ref, gout_ref, q_ref, f_ref, i_ref, g_ref, o_ref, st_ref, *, steps):
    t = pl.program_id(2)
    hd = HG_DIM

    @pl.when(t == 0)
    def _():
        st_ref[...] = jnp.zeros(st_ref.shape, F32)

    lane = lax.broadcasted_iota(jnp.int32, (hd, hd), 1)
    sl = slice(0, hd)

    def block(bi, carry):
        rows = pl.ds(pl.multiple_of(bi * hd, hd), hd)
        q = q_ref[0, rows, :]
        f = _hg_forget(lb_ref, f_ref, rows, sl)
        key = 1.0 - f
        vals_t = i_ref[0, rows, :].T
        st = st_ref[...]
        out_t = jnp.zeros((hd, hd), F32)
        for s in range(hd):
            st = f[s:s + 1] * st + vals_t[:, s:s + 1] * key[s:s + 1]
            out_t = jnp.where(lane == s, jnp.sum(st * q[s:s + 1], axis=1, keepdims=True), out_t)
        st_ref[...] = st
        o_ref[0, rows, :] = _hg_finish(out_t.T, gout_ref, g_ref[0, rows, :]).astype(o_ref.dtype)
        return carry

    lax.fori_loop(0, steps // hd, block, 0)


def _hgrn2(hg, lower, g_out, n_heads, heads, kernel_fn, name):
    b, s, _ = hg.shape
    steps = min(512, s)
    nhb = n_heads // heads
    w = heads * HG_DIM
    block_bytes = 4 * _nbytes((steps, w), F32) + _nbytes((steps, w), BF16)

    def spec(part):
        return pl.BlockSpec((1, steps, w), lambda bi, h, t, p=part: (bi, t, p * nhb + h))

    return pl.pallas_call(
        functools.partial(kernel_fn, steps=steps),
        out_shape=jax.ShapeDtypeStruct((b, s, n_heads * HG_DIM), BF16),
        grid=(b, nhb, s // steps),
        in_specs=[pl.BlockSpec((1, w), lambda bi, h, t: (0, h)),
                  pl.BlockSpec((1, HG_DIM), lambda bi, h, t: (0, 0)),
                  spec(0), spec(1), spec(2), spec(3)],
        out_specs=pl.BlockSpec((1, steps, w), lambda bi, h, t: (bi, t, h)),
        scratch_shapes=[pltpu.VMEM((heads, HG_DIM, HG_DIM) if heads > 1 else (HG_DIM, HG_DIM), F32)],
        compiler_params=_params(("parallel", "parallel", "arbitrary"), block_bytes),
        name=name,
    )(lower, g_out, hg, hg, hg, hg)


def _hgrn2_any(hg, lower, g_out, n_heads):
    b, s, c4 = hg.shape
    hg_w = c4 // 4
    heads = min(4, n_heads)
    decay = _hg_max_decay(hg.reshape(b * s, c4), lower, hg_w, min(512, s), "hgrn2_decay")
    return lax.cond(
        decay < HG_FAST_MAX_DECAY,
        lambda: _hgrn2(hg, lower, g_out, n_heads, heads, functools.partial(_hg_kernel, heads=heads), "hgrn2"),
        lambda: _hgrn2(hg, lower, g_out, n_heads, 1, _hg_seq_kernel, "hgrn2_stepwise"))


def _xattn_kernel(q_ref, k_ref, v_ref, o_ref, *, n_heads):
    hd = X_HEAD_DIM
    for h in range(n_heads):
        sl = slice(h * hd, (h + 1) * hd)
        s = lax.dot_general(q_ref[0, :, sl], k_ref[0, :, sl], (((1,), (1,)), ((), ())),
                            preferred_element_type=F32)
        p = jnp.exp(s - jnp.max(s, axis=-1, keepdims=True))
        denom = jnp.sum(p, axis=-1, keepdims=True)
        o = jnp.dot(p.astype(BF16), v_ref[0, :, sl], preferred_element_type=F32)
        o_ref[0, :, sl] = (o / denom).astype(o_ref.dtype)


def _cross_attention(q, k, v, n_heads, name):
    b, s, w = q.shape
    mlen = k.shape[1]
    tq = min(1024, s)
    block_bytes = 2 * _nbytes((tq, w), BF16) + 2 * _nbytes((mlen, w), BF16) + 3 * _nbytes((tq, mlen), F32)
    return pl.pallas_call(
        functools.partial(_xattn_kernel, n_heads=n_heads),
        out_shape=jax.ShapeDtypeStruct((b, s, w), BF16),
        grid=(b, s // tq),
        in_specs=[pl.BlockSpec((1, tq, w), lambda bi, i: (bi, i, 0)),
                  pl.BlockSpec((1, mlen, w), lambda bi, i: (bi, 0, 0)),
                  pl.BlockSpec((1, mlen, w), lambda bi, i: (bi, 0, 0))],
        out_specs=pl.BlockSpec((1, tq, w), lambda bi, i: (bi, i, 0)),
        compiler_params=_params(("parallel", "parallel"), block_bytes),
        name=name,
    )(q, k, v)


def _router_kernel(h_ref, g_ref, whi_ref, wlo_ref, b_ref, n_ref, lg_ref):
    x = h_ref[...]
    ms = jnp.mean(x * x, axis=-1, keepdims=True)
    n = x * lax.rsqrt(ms + NORM_EPS) * g_ref[...]
    n_hi = n.astype(BF16)
    n_lo = (n - n_hi.astype(F32)).astype(BF16)
    n_ref[...] = n
    lg_ref[...] = (jnp.dot(n_hi, whi_ref[...], preferred_element_type=F32)
                   + jnp.dot(n_lo, whi_ref[...], preferred_element_type=F32)
                   + jnp.dot(n_hi, wlo_ref[...], preferred_element_type=F32) + b_ref[...])


def _router(h, g, w_hi, w_lo, bias, name):
    m, d = h.shape
    nl = w_hi.shape[1]
    tm = min(256, m)
    block_bytes = 2 * _nbytes((tm, d), F32) + 2 * _nbytes((d, nl), BF16)
    return pl.pallas_call(
        _router_kernel,
        out_shape=(jax.ShapeDtypeStruct((m, d), F32), jax.ShapeDtypeStruct((m, nl), F32)),
        grid=(m // tm,),
        in_specs=[pl.BlockSpec((tm, d), lambda i: (i, 0)),
                  pl.BlockSpec((1, d), lambda i: (0, 0)),
                  pl.BlockSpec((d, nl), lambda i: (0, 0)),
                  pl.BlockSpec((d, nl), lambda i: (0, 0)),
                  pl.BlockSpec((1, nl), lambda i: (0, 0))],
        out_specs=(pl.BlockSpec((tm, d), lambda i: (i, 0)), pl.BlockSpec((tm, nl), lambda i: (i, 0))),
        compiler_params=_params(("parallel",), block_bytes),
        name=name,
    )(h, g, w_hi, w_lo, bias)


def _row_copies(src_ref, dst_ref, idx_ref, idx0, idx_stride, dst0, n, sem, *, start):
    assert n % ROW_COPY_UNROLL == 0

    def body(g, carry):
        for u in range(ROW_COPY_UNROLL):
            r = g * ROW_COPY_UNROLL + u
            cp = pltpu.make_async_copy(src_ref.at[pl.ds(idx_ref[idx0 + r * idx_stride], 1)],
                                       dst_ref.at[pl.ds(dst0 + r, 1)], sem)
            if start:
                cp.start(priority=u % 2)
            else:
                cp.wait()
        return carry

    lax.fori_loop(0, n // ROW_COPY_UNROLL, body, 0)


def _expert_kernel(tok_ref, be_ref, nreal_ref, nf_ref, wg_ref, wu_ref, wd_ref, o_ref, xbuf, sems, *, blk):
    i = pl.program_id(0)
    n_real = nreal_ref[0]

    def gather(block, start):
        slot = block % 2
        _row_copies(nf_ref, xbuf.at[slot], tok_ref, block * blk, 1, 0, blk, sems.at[slot], start=start)

    @pl.when(jnp.logical_and(i == 0, n_real > 0))
    def _():
        gather(i, True)

    @pl.when(i + 1 < n_real)
    def _():
        gather(i + 1, True)

    @pl.when(i < n_real)
    def _():
        gather(i, False)
        x = xbuf[i % 2].astype(BF16)
        hg = jnp.dot(x, wg_ref[0], preferred_element_type=F32)
        hu = jnp.dot(x, wu_ref[0], preferred_element_type=F32)
        hid = (hg * jax.nn.sigmoid(hg) * hu).astype(BF16)
        o_ref[...] = jnp.dot(hid, wd_ref[0], preferred_element_type=F32).astype(o_ref.dtype)

    @pl.when(i >= n_real)
    def _():
        o_ref[...] = jnp.zeros(o_ref.shape, o_ref.dtype)


def _experts(nf, buf_tok, blk_expert, n_real, wg, wu, wd, blk, name):
    cap = buf_tok.shape[0]
    d = nf.shape[1]
    ff = wg.shape[2]
    block_bytes = 2 * _nbytes((blk, d), F32) + 3 * _nbytes((d, ff), BF16) + _nbytes((blk, d), F32)
    return pl.pallas_call(
        functools.partial(_expert_kernel, blk=blk),
        out_shape=jax.ShapeDtypeStruct((cap, d), F32),
        grid_spec=pltpu.PrefetchScalarGridSpec(
            num_scalar_prefetch=3,
            grid=(cap // blk,),
            in_specs=[pl.BlockSpec(memory_space=pl.ANY),
                      pl.BlockSpec((1, d, ff), lambda i, tk, be, nr: (be[i], 0, 0)),
                      pl.BlockSpec((1, d, ff), lambda i, tk, be, nr: (be[i], 0, 0)),
                      pl.BlockSpec((1, ff, d), lambda i, tk, be, nr: (be[i], 0, 0))],
            out_specs=pl.BlockSpec((blk, d), lambda i, tk, be, nr: (i, 0)),
            scratch_shapes=[pltpu.VMEM((2, blk, d), F32), pltpu.SemaphoreType.DMA((2,))]),
        compiler_params=_params(("arbitrary",), block_bytes),
        name=name,
    )(buf_tok, blk_expert, n_real, nf, wg, wu, wd)


def _combine_kernel(slot_ref, h_ref, gate_ref, ys_ref, o_ref, ybuf, sems, *, tm):
    i = pl.program_id(0)

    def gather(block, start):
        s = block % 2
        for k in range(TOP_K):
            _row_copies(ys_ref, ybuf.at[s], slot_ref, block * tm * TOP_K + k, TOP_K, k * tm, tm, sems.at[s],
                        start=start)

    @pl.when(i == 0)
    def _():
        gather(i, True)

    @pl.when(i + 1 < pl.num_programs(0))
    def _():
        gather(i + 1, True)

    gather(i, False)
    acc = h_ref[...]
    gate = gate_ref[...]
    for k in range(TOP_K):
        acc = acc + gate[:, k:k + 1] * ybuf[i % 2, pl.ds(k * tm, tm), :]
    o_ref[...] = acc


def _combine(h, gate, ys, slot, name):
    m, d = h.shape
    tm = min(128, m)
    block_bytes = 2 * _nbytes((tm, d), F32) + TOP_K * _nbytes((tm, d), F32)
    return pl.pallas_call(
        functools.partial(_combine_kernel, tm=tm),
        out_shape=jax.ShapeDtypeStruct((m, d), F32),
        grid_spec=pltpu.PrefetchScalarGridSpec(
            num_scalar_prefetch=1,
            grid=(m // tm,),
            in_specs=[pl.BlockSpec((tm, d), lambda i, sl: (i, 0)),
                      pl.BlockSpec((tm, TOP_K), lambda i, sl: (i, 0)),
                      pl.BlockSpec(memory_space=pl.ANY)],
            out_specs=pl.BlockSpec((tm, d), lambda i, sl: (i, 0)),
            scratch_shapes=[pltpu.VMEM((2, TOP_K * tm, d), F32), pltpu.SemaphoreType.DMA((2,))]),
        compiler_params=_params(("arbitrary",), block_bytes),
        name=name,
    )(slot, h, gate, ys)


def _route(logits, n_groups, n_experts, blk):
    t = logits.shape[0]
    epg = n_experts // n_groups
    grp_logits = logits[:, :n_groups]
    grp_prob = jax.nn.softmax(grp_logits, axis=-1)
    grp = jnp.argmax(grp_logits, axis=-1).astype(jnp.int32)
    grp_w = jnp.take_along_axis(grp_prob, grp[:, None], axis=-1)
    exp_logits = logits[:, n_groups:n_groups + n_experts].reshape(t, n_groups, epg)
    in_grp = jnp.take_along_axis(exp_logits, grp[:, None, None], axis=1)[:, 0]
    top_val, top_idx = lax.top_k(in_grp, TOP_K)
    gate = grp_w * jax.nn.softmax(top_val, axis=-1)
    expert = (grp[:, None] * epg + top_idx.astype(jnp.int32)).reshape(-1)
    n_assign = t * TOP_K
    e_sorted, order = lax.sort_key_val(expert, jnp.arange(n_assign, dtype=jnp.int32))
    experts = jnp.arange(n_experts, dtype=jnp.int32)
    starts = jnp.searchsorted(e_sorted, experts, side='left').astype(jnp.int32)
    counts = jnp.searchsorted(e_sorted, experts, side='right').astype(jnp.int32) - starts
    padded = (counts + blk - 1) // blk * blk
    pad_ends = jnp.cumsum(padded)
    pad_starts = pad_ends - padded
    dest = pad_starts[e_sorted] + jnp.arange(n_assign, dtype=jnp.int32) - starts[e_sorted]
    _, slot = lax.sort_key_val(order, dest)
    cap = n_assign + n_experts * blk
    n_blk = cap // blk
    blk_expert = jnp.minimum(jnp.searchsorted(pad_ends, jnp.arange(n_blk, dtype=jnp.int32) * blk, side='right'),
                             n_experts - 1).astype(jnp.int32)
    row = jnp.arange(cap, dtype=jnp.int32)
    row_expert = jnp.repeat(blk_expert, blk)
    within = row - pad_starts[row_expert]
    src = jnp.clip(within + starts[row_expert], 0, n_assign - 1)
    buf_tok = jnp.where(within < counts[row_expert], order[src] // TOP_K, 0).astype(jnp.int32)
    n_real = (pad_ends[-1] // blk).astype(jnp.int32).reshape(1)
    return gate, buf_tok, blk_expert, n_real, slot.astype(jnp.int32)


def _diff_lambda_init(layer):
    return 0.8 - 0.6 * math.exp(-0.3 * layer)


def _tile_cols(v, n):
    return jnp.tile(v.reshape(1, -1).astype(F32), (1, n // v.size))


def kernel(x, mem, g_mix, w_in, g_da_q, g_da_k, lam_q1, lam_k1, lam_q2, lam_k2, g_da_sub, hg_lower, g_hg_out,
           w_up_a, w_up_b, w_gate, b_gate, w_out, g_cross, g_mem, w_xq, w_xkv, g_xq, g_xk, w_xo, g_ffn, w_grp,
           b_grp, w_erouter, b_erouter, w_e_gate, w_e_up, w_e_down):
    b, s, d = x.shape
    t = b * s
    depth = g_mix.shape[0]
    da_w = w_up_a.shape[1]
    hg_w = w_up_b.shape[1]
    da_heads = da_w // (2 * DA_HEAD_DIM)
    hg_heads = hg_w // HG_DIM
    x_w = w_xq.shape[2]
    x_heads = x_w // X_HEAD_DIM
    n_groups = w_grp.shape[2]
    n_experts = w_erouter.shape[2]
    moe_blk = min(256, t)

    lower_bounds = jnp.cumsum(jax.nn.softmax(hg_lower.astype(F32), axis=0), axis=0)
    h = x.reshape(t, d)
    mem2 = mem.reshape(b * mem.shape[1], d)
    for l in range(depth):
        lam_init = _diff_lambda_init(l)
        w_in_l = w_in[l]
        n = _rmsnorm(h, g_mix[l], BF16, "rms_mix")
        q_scale = DA_HEAD_DIM ** -0.5 * math.log2(math.e)
        qk_gain = jnp.concatenate([_tile_cols(g_da_q[l], da_w) * q_scale, _tile_cols(g_da_k[l], da_w)], axis=1)
        qk = _matmul(n, w_in_l, col0=0, ncols=2 * da_w, out_dtype=BF16,
                     epilogue=functools.partial(_ep_group_norm, group=DA_HEAD_DIM),
                     col_extras=[(qk_gain, 0)], name="proj_qk")
        v = _matmul(n, w_in_l, col0=2 * da_w, ncols=da_w, out_dtype=BF16, name="proj_v")
        hg = _matmul(n, w_in_l, col0=3 * da_w, ncols=4 * hg_w, out_dtype=F32, name="proj_hg")
        gates = _matmul(n, w_gate[l], out_dtype=BF16, epilogue=_ep_bias_sigmoid,
                        col_extras=[(b_gate[l].reshape(1, -1).astype(F32), 0)], name="proj_gate")
        lamv = jnp.stack([lam_q1[l], lam_k1[l], lam_q2[l], lam_k2[l]]).astype(F32)
        y_a = _diff_attention(qk.reshape(b, s, 2 * da_w), v.reshape(b, s, da_w), lamv,
                              g_da_sub[l].reshape(1, -1).astype(F32), da_heads, lam_init, "diff_attn")
        y_b = _hgrn2_any(hg.reshape(b, s, 4 * hg_w), lower_bounds[l].reshape(1, hg_w),
                         g_hg_out[l].reshape(1, HG_DIM).astype(F32), hg_heads)
        merged = _merge(y_a.reshape(t, da_w), w_up_a[l], y_b.reshape(t, hg_w), w_up_b[l], gates, "merge")
        h = _matmul(merged, w_out[l], out_dtype=F32, epilogue=_ep_residual,
                    full_extras=[(h, 0)], name="mix_out")
        nc = _rmsnorm(h, g_cross[l], BF16, "rms_cross")
        xq = _matmul(nc, w_xq[l], out_dtype=BF16,
                     epilogue=functools.partial(_ep_group_norm, group=X_HEAD_DIM),
                     col_extras=[(_tile_cols(g_xq[l], x_w) * X_HEAD_DIM ** -0.5, 0)], name="xattn_q")
        nm = _rmsnorm(mem2, g_mem[l], BF16, "rms_mem")
        w_xkv_l = w_xkv[l]
        xk = _matmul(nm, w_xkv_l, col0=0, ncols=x_w, out_dtype=BF16,
                     epilogue=functools.partial(_ep_group_norm, group=X_HEAD_DIM),
                     col_extras=[(_tile_cols(g_xk[l], x_w), 0)], name="xattn_k")
        xv = _matmul(nm, w_xkv_l, col0=x_w, ncols=x_w, out_dtype=BF16, name="xattn_v")
        xo = _cross_attention(xq.reshape(b, s, x_w), xk.reshape(b, -1, x_w), xv.reshape(b, -1, x_w), x_heads,
                              "xattn")
        h = _matmul(xo.reshape(t, x_w), w_xo[l], out_dtype=F32, epilogue=_ep_residual,
                    full_extras=[(h, 0)], name="xattn_out")
        n_logit = -(-(n_groups + n_experts) // LANES) * LANES
        w_r = jnp.zeros((d, n_logit), F32).at[:, :n_groups].set(w_grp[l]).at[:, n_groups:n_groups + n_experts].set(
            w_erouter[l])
        b_r = jnp.zeros((1, n_logit), F32).at[0, :n_groups].set(b_grp[l]).at[0, n_groups:n_groups + n_experts].set(
            b_erouter[l])
        w_r_hi = w_r.astype(BF16)
        w_r_lo = (w_r - w_r_hi.astype(F32)).astype(BF16)
        nf, logits = _router(h, g_ffn[l].reshape(1, d).astype(F32), w_r_hi, w_r_lo, b_r, "router")
        gate, buf_tok, blk_expert, n_real, slot = _route(logits, n_groups, n_experts, moe_blk)
        ys = _experts(nf, buf_tok, blk_expert, n_real, w_e_gate[l].astype(BF16), w_e_up[l].astype(BF16),
                      w_e_down[l].astype(BF16), moe_blk, "moe_experts")
        h = _combine(h, gate, ys, slot, "moe_combine")
    return h.reshape(b, s, d)
```

```python
import functools
import math

import jax
import jax.numpy as jnp
from jax import lax
from jax.experimental import pallas as pl
from jax.experimental.pallas import tpu as pltpu

F32 = jnp.float32
BF16 = jnp.bfloat16

NORM_EPS = 1e-6
DA_HEAD_DIM = 128
DA_CHUNK = 64
DA_MAX_BOUNDED_LOGIT = 50.0
HG_DIM = 128
HG_CHUNK = 64
HG_FAST_MAX_DECAY = 60.0
X_HEAD_DIM = 256
TOP_K = 2
ROW_COPY_UNROLL = 8
LANES = 128
MASK_VALUE = -1e30
V7X_VMEM_BYTES = 64 * 1024 * 1024
VMEM_CAP_BYTES = V7X_VMEM_BYTES - 6 * 1024 * 1024


def _vmem_limit(block_bytes, scratch_bytes):
    return int(min(VMEM_CAP_BYTES, max(32 * 1024 * 1024, 2 * block_bytes + scratch_bytes + 16 * 1024 * 1024)))


def _params(semantics, block_bytes, scratch_bytes=0):
    return pltpu.CompilerParams(dimension_semantics=semantics,
                                vmem_limit_bytes=_vmem_limit(block_bytes, scratch_bytes))


def _nbytes(shape, dtype):
    return math.prod(shape) * jnp.dtype(dtype).itemsize


def _rms_kernel(x_ref, g_ref, o_ref):
    x = x_ref[...].astype(F32)
    ms = jnp.mean(x * x, axis=-1, keepdims=True)
    o_ref[...] = (x * lax.rsqrt(ms + NORM_EPS) * g_ref[...]).astype(o_ref.dtype)


def _rmsnorm(x, g, out_dtype, name):
    m, d = x.shape
    tm = min(256, m)
    return pl.pallas_call(
        _rms_kernel,
        out_shape=jax.ShapeDtypeStruct((m, d), out_dtype),
        grid=(m // tm,),
        in_specs=[pl.BlockSpec((tm, d), lambda i: (i, 0)), pl.BlockSpec((1, d), lambda i: (0, 0))],
        out_specs=pl.BlockSpec((tm, d), lambda i: (i, 0)),
        compiler_params=_params(("parallel",), _nbytes((tm, d), x.dtype) + _nbytes((tm, d), out_dtype)),
        name=name,
    )(x, g.reshape(1, d).astype(F32))


def _ep_plain(acc, o_ref):
    o_ref[...] = acc.astype(o_ref.dtype)


def _ep_group_norm(acc, o_ref, gain, *, group):
    for c in range(acc.shape[1] // group):
        sl = slice(c * group, (c + 1) * group)
        xg = acc[:, sl]
        ms = jnp.mean(xg * xg, axis=-1, keepdims=True)
        o_ref[:, sl] = (xg * lax.rsqrt(ms + NORM_EPS) * gain[:, sl]).astype(o_ref.dtype)


def _ep_bias_sigmoid(acc, o_ref, bias):
    o_ref[...] = jax.nn.sigmoid(acc + bias).astype(o_ref.dtype)


def _ep_residual(acc, o_ref, res):
    o_ref[...] = (res + acc).astype(o_ref.dtype)


def _row_block(i):
    return jnp.maximum(i - 1, 0)


def _mm_kernel(a_ref, w_ref, *rest, epilogue):
    *extra, o_ref, wb_ref = rest
    i = pl.program_id(1)

    @pl.when(i == 0)
    def _():
        wb_ref[...] = w_ref[...].astype(BF16)

    @pl.when(i > 0)
    def _():
        acc = jnp.dot(a_ref[...], wb_ref[...], preferred_element_type=F32)
        epilogue(acc, o_ref, *[e[...] for e in extra])


def _matmul(a, w, *, col0=0, ncols=None, out_dtype, epilogue=_ep_plain, col_extras=(), full_extras=(), name):
    m, k = a.shape
    ncols = w.shape[1] - col0 if ncols is None else ncols
    tm = min(1024, m)
    tn = min(512, ncols)
    assert m % tm == 0 and ncols % tn == 0 and col0 % tn == 0
    in_specs = [pl.BlockSpec((tm, k), lambda j, i: (_row_block(i), 0)),
                pl.BlockSpec((k, tn), lambda j, i, o=col0 // tn: (0, j + o))]
    operands = [a, w]
    block_bytes = _nbytes((tm, k), a.dtype) + _nbytes((k, tn), w.dtype) + _nbytes((tm, tn), out_dtype)
    for vec, c0 in col_extras:
        assert c0 % tn == 0
        in_specs.append(pl.BlockSpec((1, tn), lambda j, i, o=c0 // tn: (0, j + o)))
        operands.append(vec)
    for arr, c0 in full_extras:
        assert c0 % tn == 0
        in_specs.append(pl.BlockSpec((tm, tn), lambda j, i, o=c0 // tn: (_row_block(i), j + o)))
        operands.append(arr)
        block_bytes += _nbytes((tm, tn), arr.dtype)
    return pl.pallas_call(
        functools.partial(_mm_kernel, epilogue=epilogue),
        out_shape=jax.ShapeDtypeStruct((m, ncols), out_dtype),
        grid=(ncols // tn, m // tm + 1),
        in_specs=in_specs,
        out_specs=pl.BlockSpec((tm, tn), lambda j, i: (_row_block(i), j)),
        scratch_shapes=[pltpu.VMEM((k, tn), BF16)],
        compiler_params=_params(("parallel", "arbitrary"), block_bytes, _nbytes((k, tn), BF16)),
        name=name,
    )(*operands)


def _merge_kernel(ya_ref, wa_ref, yb_ref, wb_ref, ga_ref, gb_ref, o_ref, wa_bf_ref, wb_bf_ref):
    i = pl.program_id(1)

    @pl.when(i == 0)
    def _():
        wa_bf_ref[...] = wa_ref[...].astype(BF16)
        wb_bf_ref[...] = wb_ref[...].astype(BF16)

    @pl.when(i > 0)
    def _():
        pa = jnp.dot(ya_ref[...], wa_bf_ref[...], preferred_element_type=F32)
        pb = jnp.dot(yb_ref[...], wb_bf_ref[...], preferred_element_type=F32)
        o_ref[...] = (ga_ref[...].astype(F32) * pa + gb_ref[...].astype(F32) * pb).astype(o_ref.dtype)


def _merge(ya, wa, yb, wb, gates, name):
    m, ka = ya.shape
    kb = yb.shape[1]
    d = wa.shape[1]
    tm = min(512, m)
    tn = min(512, d)
    nb = d // tn
    block_bytes = (_nbytes((tm, ka), BF16) + _nbytes((ka, tn), F32) + _nbytes((tm, kb), BF16)
                   + _nbytes((kb, tn), F32) + 3 * _nbytes((tm, tn), BF16))
    return pl.pallas_call(
        _merge_kernel,
        out_shape=jax.ShapeDtypeStruct((m, d), BF16),
        grid=(nb, m // tm + 1),
        in_specs=[pl.BlockSpec((tm, ka), lambda j, i: (_row_block(i), 0)),
                  pl.BlockSpec((ka, tn), lambda j, i: (0, j)),
                  pl.BlockSpec((tm, kb), lambda j, i: (_row_block(i), 0)),
                  pl.BlockSpec((kb, tn), lambda j, i: (0, j)),
                  pl.BlockSpec((tm, tn), lambda j, i: (_row_block(i), j)),
                  pl.BlockSpec((tm, tn), lambda j, i: (_row_block(i), j + nb))],
        out_specs=pl.BlockSpec((tm, tn), lambda j, i: (_row_block(i), j)),
        scratch_shapes=[pltpu.VMEM((ka, tn), BF16), pltpu.VMEM((kb, tn), BF16)],
        compiler_params=_params(("parallel", "arbitrary"), block_bytes, _nbytes((ka + kb, tn), BF16)),
        name=name,
    )(ya, wa, yb, wb, gates, gates)


def _da_kernel(lamv_ref, gsub_ref, q_ref, k_ref, v_ref, o_ref, acc_ref, l_ref, *m_ref, tq, lam_init, bounded):
    qi = pl.program_id(2)
    hd = DA_HEAD_DIM
    m_ref = None if bounded else m_ref[0]
    if not bounded:
        m_ref[...] = jnp.full(m_ref.shape, MASK_VALUE, F32)
    l_ref[...] = jnp.zeros(l_ref.shape, F32)
    acc_ref[...] = jnp.zeros(acc_ref.shape, F32)
    q = q_ref[0]
    row_chunk = lax.broadcasted_iota(jnp.int32, (tq, tq), 0) // DA_CHUNK
    col_chunk = lax.broadcasted_iota(jnp.int32, (tq, tq), 1) // DA_CHUNK
    visible = col_chunk <= row_chunk

    def widen(x, width):
        return jnp.concatenate([x] * (width // LANES), axis=-1)

    def update(off, width, masked):
        kk = k_ref[0, pl.ds(off, width), :]
        vv = v_ref[0, pl.ds(off, width), :]
        for c in range(2):
            s = lax.dot_general(q[:, c * hd:(c + 1) * hd], kk[:, c * hd:(c + 1) * hd],
                                (((1,), (1,)), ((), ())), preferred_element_type=F32)
            if masked:
                s = jnp.where(visible, s, MASK_VALUE)
            if bounded:
                p = jnp.exp2(s)
            else:
                m_old = m_ref[c]
                m_new = jnp.maximum(m_old, jnp.max(s, axis=-1, keepdims=True))
                p = jnp.exp2(s - widen(m_new, width))
                alpha = jnp.exp2(m_old - m_new)
                m_ref[c] = m_new
            psum = p[:, :LANES]
            for g in range(1, width // LANES):
                psum = psum + p[:, g * LANES:(g + 1) * LANES]
            pv = jnp.dot(p.astype(BF16), vv, preferred_element_type=F32)
            if bounded:
                l_ref[c] = l_ref[c] + psum
                acc_ref[c] = acc_ref[c] + pv
            else:
                l_ref[c] = alpha * l_ref[c] + psum
                acc_ref[c] = widen(alpha, 2 * hd) * acc_ref[c] + pv

    def wide_body(j, carry):
        update(pl.multiple_of(j * 2 * tq, 2 * tq), 2 * tq, False)
        return carry

    lax.fori_loop(0, qi // 2, wide_body, 0)

    @pl.when(qi % 2 == 1)
    def _():
        update(pl.multiple_of((qi - 1) * tq, tq), tq, False)

    update(pl.multiple_of(qi * tq, tq), tq, True)

    lv = lamv_ref[...]
    lam = (jnp.exp(jnp.sum(lv[0:1] * lv[1:2], axis=-1, keepdims=True))
           - jnp.exp(jnp.sum(lv[2:3] * lv[3:4], axis=-1, keepdims=True)) + lam_init)
    l1 = jnp.sum(l_ref[0], axis=-1, keepdims=True)
    l2 = jnp.sum(l_ref[1], axis=-1, keepdims=True)
    o = acc_ref[0] / l1 - lam * (acc_ref[1] / l2)
    ms = jnp.mean(o * o, axis=-1, keepdims=True)
    o_ref[0] = ((o * lax.rsqrt(ms + NORM_EPS) * gsub_ref[...]) * (1.0 - lam_init)).astype(o_ref.dtype)


def _diff_attention(qk, v, lamv, g_sub, n_heads, lam_init, bounded, name):
    b, s, _ = v.shape
    hw = 2 * DA_HEAD_DIM
    tq = min(512, s)
    block_bytes = 2 * _nbytes((tq, hw), BF16) + 2 * _nbytes((s, hw), BF16) + 6 * _nbytes((tq, 2 * tq), F32)
    stats = [pltpu.VMEM((2, tq, LANES), F32)] * (1 if bounded else 2)
    return pl.pallas_call(
        functools.partial(_da_kernel, tq=tq, lam_init=lam_init, bounded=bounded),
        out_shape=jax.ShapeDtypeStruct((b, s, n_heads * hw), BF16),
        grid=(b, n_heads, s // tq),
        in_specs=[pl.BlockSpec((4, DA_HEAD_DIM), lambda bi, h, i: (0, 0)),
                  pl.BlockSpec((1, hw), lambda bi, h, i: (0, 0)),
                  pl.BlockSpec((1, tq, hw), lambda bi, h, i: (bi, i, h)),
                  pl.BlockSpec((1, s, hw), lambda bi, h, i: (bi, 0, n_heads + h)),
                  pl.BlockSpec((1, s, hw), lambda bi, h, i: (bi, 0, h))],
        out_specs=pl.BlockSpec((1, tq, hw), lambda bi, h, i: (bi, i, h)),
        scratch_shapes=[pltpu.VMEM((2, tq, hw), F32)] + stats,
        compiler_params=_params(("parallel", "parallel", "arbitrary"), block_bytes),
        name=name,
    )(lamv, g_sub, qk, qk, v)


def _diff_attention_any(qk, v, lamv, g_sub, g_q, g_k, q_scale, n_heads, lam_init):
    bound = 1.01 * DA_HEAD_DIM * q_scale * jnp.max(jnp.abs(g_q)) * jnp.max(jnp.abs(g_k))
    return lax.cond(
        bound < DA_MAX_BOUNDED_LOGIT,
        lambda: _diff_attention(qk, v, lamv, g_sub, n_heads, lam_init, True, "diff_attn"),
        lambda: _diff_attention(qk, v, lamv, g_sub, n_heads, lam_init, False, "diff_attn_online"))


def _hg_forget(lb_ref, f_ref, rows, sl):
    lb = lb_ref[:, sl]
    return lb + (1.0 - lb) * jax.nn.sigmoid(f_ref[0, rows, sl])


def _hg_finish(o, gout_ref, gate):
    ms = jnp.mean(o * o, axis=-1, keepdims=True)
    return (o * lax.rsqrt(ms + NORM_EPS) * gout_ref[...]) * (gate * jax.nn.sigmoid(gate))


def _hg_decay_kernel(lb_ref, f_ref, o_ref, *, steps):
    lb = lb_ref[...]
    nl = -jnp.log(lb + (1.0 - lb) * jax.nn.sigmoid(f_ref[...]))
    worst = jnp.zeros((1, nl.shape[1]), F32)
    for c in range(steps // HG_CHUNK):
        worst = jnp.maximum(worst, jnp.sum(nl[c * HG_CHUNK:(c + 1) * HG_CHUNK], axis=0, keepdims=True))
    o_ref[...] = jnp.broadcast_to(worst, o_ref.shape)


def _hg_max_decay(hg2, lower, hg_w, steps, name):
    t = hg2.shape[0]
    w = min(1024, hg_w)
    ncb = hg_w // w
    sub = 8
    out = pl.pallas_call(
        functools.partial(_hg_decay_kernel, steps=steps),
        out_shape=jax.ShapeDtypeStruct((t // steps * sub, hg_w), F32),
        grid=(t // steps, ncb),
        in_specs=[pl.BlockSpec((1, w), lambda i, j: (0, j)),
                  pl.BlockSpec((steps, w), lambda i, j: (i, ncb + j))],
        out_specs=pl.BlockSpec((sub, w), lambda i, j: (i, j)),
        compiler_params=_params(("parallel", "parallel"), 4 * _nbytes((steps, w), F32)),
        name=name,
    )(lower, hg2)
    return jnp.max(out)


def _hg_kernel(lb_ref, gout_ref, q_ref, f_ref, i_ref, g_ref, o_ref, st_ref, *, heads, steps):
    t = pl.program_id(2)
    ch, hd = HG_CHUNK, HG_DIM

    @pl.when(t == 0)
    def _():
        st_ref[...] = jnp.zeros(st_ref.shape, F32)

    r = lax.broadcasted_iota(jnp.int32, (ch, ch), 0)
    c = lax.broadcasted_iota(jnp.int32, (ch, ch), 1)
    causal = c <= r
    cum_mat = jnp.concatenate([causal.astype(BF16)] * 3, axis=1)
    nt = (((1,), (1,)), ((), ()))
    tn = (((0,), (0,)), ((), ()))

    def chunk(rows, sl, st):
        q = q_ref[0, rows, sl]
        f = _hg_forget(lb_ref, f_ref, rows, sl)
        key = 1.0 - f
        lf = jnp.log(f)
        lf0 = lf.astype(BF16)
        r1 = lf - lf0.astype(F32)
        lf1 = r1.astype(BF16)
        lf2 = (r1 - lf1.astype(F32)).astype(BF16)
        cum = jnp.dot(cum_mat, jnp.concatenate([lf0, lf1, lf2], axis=0), preferred_element_type=F32)
        last = cum[ch - 1:ch]
        q_dec = (q * jnp.exp(cum)).astype(BF16)
        k_inv = (key * jnp.exp(-cum)).astype(BF16)
        k_state = (key * jnp.exp(last - cum)).astype(BF16)
        vals = i_ref[0, rows, sl].astype(BF16)
        scores = jnp.where(causal, lax.dot_general(q_dec, k_inv, nt, preferred_element_type=F32), 0.0)
        o = (jnp.dot(scores.astype(BF16), vals, preferred_element_type=F32)
             + lax.dot_general(q_dec, st.astype(BF16), nt, preferred_element_type=F32))
        st_next = jnp.exp(last) * st + lax.dot_general(vals, k_state, tn, preferred_element_type=F32)
        return _hg_finish(o, gout_ref, g_ref[0, rows, sl]), st_next

    for h in range(heads):
        sl = slice(h * hd, (h + 1) * hd)
        st = st_ref[h]
        for ci in range(steps // ch):
            rows = slice(ci * ch, (ci + 1) * ch)
            y, st = chunk(rows, sl, st)
            o_ref[0, rows, sl] = y.astype(o_ref.dtype)
        st_ref[h] = st


def _hg_seq_kernel(lb_ref, gout_ref, q_ref, f_ref, i_ref, g_ref, o_ref, st_ref, *, steps):
    t = pl.program_id(2)
    hd = HG_DIM

    @pl.when(t == 0)
    def _():
        st_ref[...] = jnp.zeros(st_ref.shape, F32)

    lane = lax.broadcasted_iota(jnp.int32, (hd, hd), 1)
    sl = slice(0, hd)

    def block(bi, carry):
        rows = pl.ds(pl.multiple_of(bi * hd, hd), hd)
        q = q_ref[0, rows, :]
        f = _hg_forget(lb_ref, f_ref, rows, sl)
        key = 1.0 - f
        vals_t = i_ref[0, rows, :].T
        st = st_ref[...]
        out_t = jnp.zeros((hd, hd), F32)
        for s in range(hd):
            st = f[s:s + 1] * st + vals_t[:, s:s + 1] * key[s:s + 1]
            out_t = jnp.where(lane == s, jnp.sum(st * q[s:s + 1], axis=1, keepdims=True), out_t)
        st_ref[...] = st
        o_ref[0, rows, :] = _hg_finish(out_t.T, gout_ref, g_ref[0, rows, :]).astype(o_ref.dtype)
        return carry

    lax.fori_loop(0, steps // hd, block, 0)


def _hgrn2(hg, lower, g_out, n_heads, heads, kernel_fn, name):
    b, s, _ = hg.shape
    steps = min(512, s)
    nhb = n_heads // heads
    w = heads * HG_DIM
    block_bytes = 4 * _nbytes((steps, w), F32) + _nbytes((steps, w), BF16)

    def spec(part):
        return pl.BlockSpec((1, steps, w), lambda bi, h, t, p=part: (bi, t, p * nhb + h))

    return pl.pallas_call(
        functools.partial(kernel_fn, steps=steps),
        out_shape=jax.ShapeDtypeStruct((b, s, n_heads * HG_DIM), BF16),
        grid=(b, nhb, s // steps),
        in_specs=[pl.BlockSpec((1, w), lambda bi, h, t: (0, h)),
                  pl.BlockSpec((1, HG_DIM), lambda bi, h, t: (0, 0)),
                  spec(0), spec(1), spec(2), spec(3)],
        out_specs=pl.BlockSpec((1, steps, w), lambda bi, h, t: (bi, t, h)),
        scratch_shapes=[pltpu.VMEM((heads, HG_DIM, HG_DIM) if heads > 1 else (HG_DIM, HG_DIM), F32)],
        compiler_params=_params(("parallel", "parallel", "arbitrary"), block_bytes),
        name=name,
    )(lower, g_out, hg, hg, hg, hg)


def _hgrn2_any(hg, lower, g_out, n_heads):
    b, s, c4 = hg.shape
    hg_w = c4 // 4
    heads = min(4, n_heads)
    decay = _hg_max_decay(hg.reshape(b * s, c4), lower, hg_w, min(512, s), "hgrn2_decay")
    return lax.cond(
        decay < HG_FAST_MAX_DECAY,
        lambda: _hgrn2(hg, lower, g_out, n_heads, heads, functools.partial(_hg_kernel, heads=heads), "hgrn2"),
        lambda: _hgrn2(hg, lower, g_out, n_heads, 1, _hg_seq_kernel, "hgrn2_stepwise"))


def _xattn_kernel(q_ref, k_ref, v_ref, o_ref, *, n_heads):
    hd = X_HEAD_DIM
    for h in range(n_heads):
        sl = slice(h * hd, (h + 1) * hd)
        s = lax.dot_general(q_ref[0, :, sl], k_ref[0, :, sl], (((1,), (1,)), ((), ())),
                            preferred_element_type=F32)
        p = jnp.exp(s - jnp.max(s, axis=-1, keepdims=True))
        denom = jnp.sum(p, axis=-1, keepdims=True)
        o = jnp.dot(p.astype(BF16), v_ref[0, :, sl], preferred_element_type=F32)
        o_ref[0, :, sl] = (o / denom).astype(o_ref.dtype)


def _cross_attention(q, k, v, n_heads, name):
    b, s, w = q.shape
    mlen = k.shape[1]
    tq = min(1024, s)
    block_bytes = 2 * _nbytes((tq, w), BF16) + 2 * _nbytes((mlen, w), BF16) + 3 * _nbytes((tq, mlen), F32)
    return pl.pallas_call(
        functools.partial(_xattn_kernel, n_heads=n_heads),
        out_shape=jax.ShapeDtypeStruct((b, s, w), BF16),
        grid=(b, s // tq),
        in_specs=[pl.BlockSpec((1, tq, w), lambda bi, i: (bi, i, 0)),
                  pl.BlockSpec((1, mlen, w), lambda bi, i: (bi, 0, 0)),
                  pl.BlockSpec((1, mlen, w), lambda bi, i: (bi, 0, 0))],
        out_specs=pl.BlockSpec((1, tq, w), lambda bi, i: (bi, i, 0)),
        compiler_params=_params(("parallel", "parallel"), block_bytes),
        name=name,
    )(q, k, v)


def _router_kernel(h_ref, g_ref, whi_ref, wlo_ref, b_ref, n_ref, lg_ref):
    x = h_ref[...]
    ms = jnp.mean(x * x, axis=-1, keepdims=True)
    n = x * lax.rsqrt(ms + NORM_EPS) * g_ref[...]
    n_hi = n.astype(BF16)
    n_lo = (n - n_hi.astype(F32)).astype(BF16)
    n_ref[...] = n
    lg_ref[...] = (jnp.dot(n_hi, whi_ref[...], preferred_element_type=F32)
                   + jnp.dot(n_lo, whi_ref[...], preferred_element_type=F32)
                   + jnp.dot(n_hi, wlo_ref[...], preferred_element_type=F32) + b_ref[...])


def _router(h, g, w_hi, w_lo, bias, name):
    m, d = h.shape
    nl = w_hi.shape[1]
    tm = min(256, m)
    block_bytes = 2 * _nbytes((tm, d), F32) + 2 * _nbytes((d, nl), BF16)
    return pl.pallas_call(
        _router_kernel,
        out_shape=(jax.ShapeDtypeStruct((m, d), F32), jax.ShapeDtypeStruct((m, nl), F32)),
        grid=(m // tm,),
        in_specs=[pl.BlockSpec((tm, d), lambda i: (i, 0)),
                  pl.BlockSpec((1, d), lambda i: (0, 0)),
                  pl.BlockSpec((d, nl), lambda i: (0, 0)),
                  pl.BlockSpec((d, nl), lambda i: (0, 0)),
                  pl.BlockSpec((1, nl), lambda i: (0, 0))],
        out_specs=(pl.BlockSpec((tm, d), lambda i: (i, 0)), pl.BlockSpec((tm, nl), lambda i: (i, 0))),
        compiler_params=_params(("parallel",), block_bytes),
        name=name,
    )(h, g, w_hi, w_lo, bias)


def _row_copies(src_ref, dst_ref, idx_ref, idx0, idx_stride, dst0, n, sem, *, start):
    assert n % ROW_COPY_UNROLL == 0

    def body(g, carry):
        for u in range(ROW_COPY_UNROLL):
            r = g * ROW_COPY_UNROLL + u
            cp = pltpu.make_async_copy(src_ref.at[pl.ds(idx_ref[idx0 + r * idx_stride], 1)],
                                       dst_ref.at[pl.ds(dst0 + r, 1)], sem)
            if start:
                cp.start(priority=1)
            else:
                cp.wait()
        return carry

    lax.fori_loop(0, n // ROW_COPY_UNROLL, body, 0)


def _expert_kernel(tok_ref, be_ref, nreal_ref, nf_ref, wg_ref, wu_ref, wd_ref, o_ref, xbuf, sems, *, blk):
    i = pl.program_id(0)
    n_real = nreal_ref[0]

    def gather(block, start):
        slot = block % 2
        _row_copies(nf_ref, xbuf.at[slot], tok_ref, block * blk, 1, 0, blk, sems.at[slot], start=start)

    @pl.when(jnp.logical_and(i == 0, n_real > 0))
    def _():
        gather(i, True)

    @pl.when(i + 1 < n_real)
    def _():
        gather(i + 1, True)

    @pl.when(i < n_real)
    def _():
        gather(i, False)
        x = xbuf[i % 2].astype(BF16)
        hg = jnp.dot(x, wg_ref[0], preferred_element_type=F32)
        hu = jnp.dot(x, wu_ref[0], preferred_element_type=F32)
        hid = (hg * jax.nn.sigmoid(hg) * hu).astype(BF16)
        o_ref[...] = jnp.dot(hid, wd_ref[0], preferred_element_type=F32).astype(o_ref.dtype)

    @pl.when(i >= n_real)
    def _():
        o_ref[...] = jnp.zeros(o_ref.shape, o_ref.dtype)


def _experts(nf, buf_tok, blk_expert, n_real, wg, wu, wd, blk, name):
    cap = buf_tok.shape[0]
    d = nf.shape[1]
    ff = wg.shape[2]
    block_bytes = 2 * _nbytes((blk, d), F32) + 3 * _nbytes((d, ff), BF16) + _nbytes((blk, d), F32)
    return pl.pallas_call(
        functools.partial(_expert_kernel, blk=blk),
        out_shape=jax.ShapeDtypeStruct((cap, d), F32),
        grid_spec=pltpu.PrefetchScalarGridSpec(
            num_scalar_prefetch=3,
            grid=(cap // blk,),
            in_specs=[pl.BlockSpec(memory_space=pl.ANY),
                      pl.BlockSpec((1, d, ff), lambda i, tk, be, nr: (be[i], 0, 0)),
                      pl.BlockSpec((1, d, ff), lambda i, tk, be, nr: (be[i], 0, 0)),
                      pl.BlockSpec((1, ff, d), lambda i, tk, be, nr: (be[i], 0, 0))],
            out_specs=pl.BlockSpec((blk, d), lambda i, tk, be, nr: (i, 0)),
            scratch_shapes=[pltpu.VMEM((2, blk, d), F32), pltpu.SemaphoreType.DMA((2,))]),
        compiler_params=_params(("arbitrary",), block_bytes),
        name=name,
    )(buf_tok, blk_expert, n_real, nf, wg, wu, wd)


def _combine_kernel(slot_ref, h_ref, gate_ref, ys_ref, o_ref, ybuf, sems, *, tm):
    i = pl.program_id(0)

    def gather(block, start):
        s = block % 2
        for k in range(TOP_K):
            _row_copies(ys_ref, ybuf.at[s], slot_ref, block * tm * TOP_K + k, TOP_K, k * tm, tm, sems.at[s],
                        start=start)

    @pl.when(i == 0)
    def _():
        gather(i, True)

    @pl.when(i + 1 < pl.num_programs(0))
    def _():
        gather(i + 1, True)

    gather(i, False)
    acc = h_ref[...]
    gate = gate_ref[...]
    for k in range(TOP_K):
        acc = acc + gate[:, k:k + 1] * ybuf[i % 2, pl.ds(k * tm, tm), :]
    o_ref[...] = acc


def _combine(h, gate, ys, slot, name):
    m, d = h.shape
    tm = min(128, m)
    block_bytes = 2 * _nbytes((tm, d), F32) + TOP_K * _nbytes((tm, d), F32)
    return pl.pallas_call(
        functools.partial(_combine_kernel, tm=tm),
        out_shape=jax.ShapeDtypeStruct((m, d), F32),
        grid_spec=pltpu.PrefetchScalarGridSpec(
            num_scalar_prefetch=1,
            grid=(m // tm,),
            in_specs=[pl.BlockSpec((tm, d), lambda i, sl: (i, 0)),
                      pl.BlockSpec((tm, TOP_K), lambda i, sl: (i, 0)),
                      pl.BlockSpec(memory_space=pl.ANY)],
            out_specs=pl.BlockSpec((tm, d), lambda i, sl: (i, 0)),
            scratch_shapes=[pltpu.VMEM((2, TOP_K * tm, d), F32), pltpu.SemaphoreType.DMA((2,))]),
        compiler_params=_params(("arbitrary",), block_bytes),
        name=name,
    )(slot, h, gate, ys)


def _route(logits, n_groups, n_experts, blk):
    t = logits.shape[0]
    epg = n_experts // n_groups
    grp_logits = logits[:, :n_groups]
    grp_prob = jax.nn.softmax(grp_logits, axis=-1)
    grp = jnp.argmax(grp_logits, axis=-1).astype(jnp.int32)
    grp_w = jnp.take_along_axis(grp_prob, grp[:, None], axis=-1)
    exp_logits = logits[:, n_groups:n_groups + n_experts].reshape(t, n_groups, epg)
    in_grp = jnp.take_along_axis(exp_logits, grp[:, None, None], axis=1)[:, 0]
    top_val, top_idx = lax.top_k(in_grp, TOP_K)
    gate = grp_w * jax.nn.softmax(top_val, axis=-1)
    expert = (grp[:, None] * epg + top_idx.astype(jnp.int32)).reshape(-1)
    n_assign = t * TOP_K
    e_sorted, order = lax.sort_key_val(expert, jnp.arange(n_assign, dtype=jnp.int32))
    experts = jnp.arange(n_experts, dtype=jnp.int32)
    starts = jnp.searchsorted(e_sorted, experts, side='left').astype(jnp.int32)
    counts = jnp.searchsorted(e_sorted, experts, side='right').astype(jnp.int32) - starts
    padded = (counts + blk - 1) // blk * blk
    pad_ends = jnp.cumsum(padded)
    pad_starts = pad_ends - padded
    dest = pad_starts[e_sorted] + jnp.arange(n_assign, dtype=jnp.int32) - starts[e_sorted]
    _, slot = lax.sort_key_val(order, dest)
    cap = n_assign + n_experts * blk
    n_blk = cap // blk
    blk_expert = jnp.minimum(jnp.searchsorted(pad_ends, jnp.arange(n_blk, dtype=jnp.int32) * blk, side='right'),
                             n_experts - 1).astype(jnp.int32)
    row = jnp.arange(cap, dtype=jnp.int32)
    row_expert = jnp.repeat(blk_expert, blk)
    within = row - pad_starts[row_expert]
    src = jnp.clip(within + starts[row_expert], 0, n_assign - 1)
    buf_tok = jnp.where(within < counts[row_expert], order[src] // TOP_K, 0).astype(jnp.int32)
    n_real = (pad_ends[-1] // blk).astype(jnp.int32).reshape(1)
    return gate, buf_tok, blk_expert, n_real, slot.astype(jnp.int32)


def _diff_lambda_init(layer):
    return 0.8 - 0.6 * math.exp(-0.3 * layer)


def _tile_cols(v, n):
    return jnp.tile(v.reshape(1, -1).astype(F32), (1, n // v.size))


def kernel(x, mem, g_mix, w_in, g_da_q, g_da_k, lam_q1, lam_k1, lam_q2, lam_k2, g_da_sub, hg_lower, g_hg_out,
           w_up_a, w_up_b, w_gate, b_gate, w_out, g_cross, g_mem, w_xq, w_xkv, g_xq, g_xk, w_xo, g_ffn, w_grp,
           b_grp, w_erouter, b_erouter, w_e_gate, w_e_up, w_e_down):
    b, s, d = x.shape
    t = b * s
    depth = g_mix.shape[0]
    da_w = w_up_a.shape[1]
    hg_w = w_up_b.shape[1]
    da_heads = da_w // (2 * DA_HEAD_DIM)
    hg_heads = hg_w // HG_DIM
    x_w = w_xq.shape[2]
    x_heads = x_w // X_HEAD_DIM
    n_groups = w_grp.shape[2]
    n_experts = w_erouter.shape[2]
    moe_blk = min(256, t)

    lower_bounds = jnp.cumsum(jax.nn.softmax(hg_lower.astype(F32), axis=0), axis=0)
    h = x.reshape(t, d)
    mem2 = mem.reshape(b * mem.shape[1], d)
    for l in range(depth):
        lam_init = _diff_lambda_init(l)
        w_in_l = w_in[l]
        n = _rmsnorm(h, g_mix[l], BF16, "rms_mix")
        q_scale = DA_HEAD_DIM ** -0.5 * math.log2(math.e)
        qk_gain = jnp.concatenate([_tile_cols(g_da_q[l], da_w) * q_scale, _tile_cols(g_da_k[l], da_w)], axis=1)
        qk = _matmul(n, w_in_l, col0=0, ncols=2 * da_w, out_dtype=BF16,
                     epilogue=functools.partial(_ep_group_norm, group=DA_HEAD_DIM),
                     col_extras=[(qk_gain, 0)], name="proj_qk")
        v = _matmul(n, w_in_l, col0=2 * da_w, ncols=da_w, out_dtype=BF16, name="proj_v")
        hg = _matmul(n, w_in_l, col0=3 * da_w, ncols=4 * hg_w, out_dtype=F32, name="proj_hg")
        gates = _matmul(n, w_gate[l], out_dtype=BF16, epilogue=_ep_bias_sigmoid,
                        col_extras=[(b_gate[l].reshape(1, -1).astype(F32), 0)], name="proj_gate")
        lamv = jnp.stack([lam_q1[l], lam_k1[l], lam_q2[l], lam_k2[l]]).astype(F32)
        y_a = _diff_attention_any(qk.reshape(b, s, 2 * da_w), v.reshape(b, s, da_w), lamv,
                                  g_da_sub[l].reshape(1, -1).astype(F32), g_da_q[l], g_da_k[l], q_scale,
                                  da_heads, lam_init)
        y_b = _hgrn2_any(hg.reshape(b, s, 4 * hg_w), lower_bounds[l].reshape(1, hg_w),
                         g_hg_out[l].reshape(1, HG_DIM).astype(F32), hg_heads)
        merged = _merge(y_a.reshape(t, da_w), w_up_a[l], y_b.reshape(t, hg_w), w_up_b[l], gates, "merge")
        h = _matmul(merged, w_out[l], out_dtype=F32, epilogue=_ep_residual,
                    full_extras=[(h, 0)], name="mix_out")
        nc = _rmsnorm(h, g_cross[l], BF16, "rms_cross")
        xq = _matmul(nc, w_xq[l], out_dtype=BF16,
                     epilogue=functools.partial(_ep_group_norm, group=X_HEAD_DIM),
                     col_extras=[(_tile_cols(g_xq[l], x_w) * X_HEAD_DIM ** -0.5, 0)], name="xattn_q")
        nm = _rmsnorm(mem2, g_mem[l], BF16, "rms_mem")
        w_xkv_l = w_xkv[l]
        xk = _matmul(nm, w_xkv_l, col0=0, ncols=x_w, out_dtype=BF16,
                     epilogue=functools.partial(_ep_group_norm, group=X_HEAD_DIM),
                     col_extras=[(_tile_cols(g_xk[l], x_w), 0)], name="xattn_k")
        xv = _matmul(nm, w_xkv_l, col0=x_w, ncols=x_w, out_dtype=BF16, name="xattn_v")
        xo = _cross_attention(xq.reshape(b, s, x_w), xk.reshape(b, -1, x_w), xv.reshape(b, -1, x_w), x_heads,
                              "xattn")
        h = _matmul(xo.reshape(t, x_w), w_xo[l], out_dtype=F32, epilogue=_ep_residual,
                    full_extras=[(h, 0)], name="xattn_out")
        n_logit = -(-(n_groups + n_experts) // LANES) * LANES
        w_r = jnp.zeros((d, n_logit), F32).at[:, :n_groups].set(w_grp[l]).at[:, n_groups:n_groups + n_experts].set(
            w_erouter[l])
        b_r = jnp.zeros((1, n_logit), F32).at[0, :n_groups].set(b_grp[l]).at[0, n_groups:n_groups + n_experts].set(
            b_erouter[l])
        w_r_hi = w_r.astype(BF16)
        w_r_lo = (w_r - w_r_hi.astype(F32)).astype(BF16)
        nf, logits = _router(h, g_ffn[l].reshape(1, d).astype(F32), w_r_hi, w_r_lo, b_r, "router")
        gate, buf_tok, blk_expert, n_real, slot = _route(logits, n_groups, n_experts, moe_blk)
        ys = _experts(nf, buf_tok, blk_expert, n_real, w_e_gate[l].astype(BF16), w_e_up[l].astype(BF16),
                      w_e_down[l].astype(BF16), moe_blk, "moe_experts")
        h = _combine(h, gate, ys, slot, "moe_combine")
    return h.reshape(b, s, d)
```

```python
import functools
import math

import jax
import jax.numpy as jnp
from jax import lax
from jax.experimental import pallas as pl
from jax.experimental.pallas import tpu as pltpu

F32 = jnp.float32
BF16 = jnp.bfloat16

NORM_EPS = 1e-6
DA_HEAD_DIM = 128
DA_CHUNK = 64
DA_MAX_BOUNDED_LOGIT = 50.0
HG_DIM = 128
HG_CHUNK = 64
HG_FAST_MAX_DECAY = 60.0
X_HEAD_DIM = 256
TOP_K = 2
ROW_COPY_UNROLL = 8
LANES = 128
MASK_VALUE = -1e30
V7X_VMEM_BYTES = 64 * 1024 * 1024
VMEM_CAP_BYTES = V7X_VMEM_BYTES - 6 * 1024 * 1024


def _vmem_limit(block_bytes, scratch_bytes):
    return int(min(VMEM_CAP_BYTES, max(32 * 1024 * 1024, 2 * block_bytes + scratch_bytes + 16 * 1024 * 1024)))


def _params(semantics, block_bytes, scratch_bytes=0):
    return pltpu.CompilerParams(dimension_semantics=semantics,
                                vmem_limit_bytes=_vmem_limit(block_bytes, scratch_bytes))


def _nbytes(shape, dtype):
    return math.prod(shape) * jnp.dtype(dtype).itemsize


def _rms_kernel(x_ref, g_ref, o_ref):
    x = x_ref[...].astype(F32)
    ms = jnp.mean(x * x, axis=-1, keepdims=True)
    o_ref[...] = (x * lax.rsqrt(ms + NORM_EPS) * g_ref[...]).astype(o_ref.dtype)


def _rmsnorm(x, g, out_dtype, name):
    m, d = x.shape
    tm = min(256, m)
    return pl.pallas_call(
        _rms_kernel,
        out_shape=jax.ShapeDtypeStruct((m, d), out_dtype),
        grid=(m // tm,),
        in_specs=[pl.BlockSpec((tm, d), lambda i: (i, 0)), pl.BlockSpec((1, d), lambda i: (0, 0))],
        out_specs=pl.BlockSpec((tm, d), lambda i: (i, 0)),
        compiler_params=_params(("parallel",), _nbytes((tm, d), x.dtype) + _nbytes((tm, d), out_dtype)),
        name=name,
    )(x, g.reshape(1, d).astype(F32))


def _ep_plain(acc, o_ref):
    o_ref[...] = acc.astype(o_ref.dtype)


def _ep_group_norm(acc, o_ref, gain, *, group):
    for c in range(acc.shape[1] // group):
        sl = slice(c * group, (c + 1) * group)
        xg = acc[:, sl]
        ms = jnp.mean(xg * xg, axis=-1, keepdims=True)
        o_ref[:, sl] = (xg * lax.rsqrt(ms + NORM_EPS) * gain[:, sl]).astype(o_ref.dtype)


def _ep_bias_sigmoid(acc, o_ref, bias):
    o_ref[...] = jax.nn.sigmoid(acc + bias).astype(o_ref.dtype)


def _ep_residual(acc, o_ref, res):
    o_ref[...] = (res + acc).astype(o_ref.dtype)


def _row_block(i):
    return jnp.maximum(i - 1, 0)


def _mm_kernel(a_ref, w_ref, *rest, epilogue):
    *extra, o_ref, wb_ref = rest
    i = pl.program_id(1)

    @pl.when(i == 0)
    def _():
        wb_ref[...] = w_ref[...].astype(BF16)

    @pl.when(i > 0)
    def _():
        acc = jnp.dot(a_ref[...], wb_ref[...], preferred_element_type=F32)
        epilogue(acc, o_ref, *[e[...] for e in extra])


def _matmul(a, w, *, col0=0, ncols=None, out_dtype, epilogue=_ep_plain, col_extras=(), full_extras=(), name):
    m, k = a.shape
    ncols = w.shape[1] - col0 if ncols is None else ncols
    tm = min(1024, m)
    tn = min(512, ncols)
    assert m % tm == 0 and ncols % tn == 0 and col0 % tn == 0
    in_specs = [pl.BlockSpec((tm, k), lambda j, i: (_row_block(i), 0)),
                pl.BlockSpec((k, tn), lambda j, i, o=col0 // tn: (0, j + o))]
    operands = [a, w]
    block_bytes = _nbytes((tm, k), a.dtype) + _nbytes((k, tn), w.dtype) + _nbytes((tm, tn), out_dtype)
    for vec, c0 in col_extras:
        assert c0 % tn == 0
        in_specs.append(pl.BlockSpec((1, tn), lambda j, i, o=c0 // tn: (0, j + o)))
        operands.append(vec)
    for arr, c0 in full_extras:
        assert c0 % tn == 0
        in_specs.append(pl.BlockSpec((tm, tn), lambda j, i, o=c0 // tn: (_row_block(i), j + o)))
        operands.append(arr)
        block_bytes += _nbytes((tm, tn), arr.dtype)
    return pl.pallas_call(
        functools.partial(_mm_kernel, epilogue=epilogue),
        out_shape=jax.ShapeDtypeStruct((m, ncols), out_dtype),
        grid=(ncols // tn, m // tm + 1),
        in_specs=in_specs,
        out_specs=pl.BlockSpec((tm, tn), lambda j, i: (_row_block(i), j)),
        scratch_shapes=[pltpu.VMEM((k, tn), BF16)],
        compiler_params=_params(("parallel", "arbitrary"), block_bytes, _nbytes((k, tn), BF16)),
        name=name,
    )(*operands)


def _merge_kernel(ya_ref, wa_ref, yb_ref, wb_ref, ga_ref, gb_ref, o_ref, wa_bf_ref, wb_bf_ref):
    i = pl.program_id(1)

    @pl.when(i == 0)
    def _():
        wa_bf_ref[...] = wa_ref[...].astype(BF16)
        wb_bf_ref[...] = wb_ref[...].astype(BF16)

    @pl.when(i > 0)
    def _():
        pa = jnp.dot(ya_ref[...], wa_bf_ref[...], preferred_element_type=F32)
        pb = jnp.dot(yb_ref[...], wb_bf_ref[...], preferred_element_type=F32)
        o_ref[...] = (ga_ref[...].astype(F32) * pa + gb_ref[...].astype(F32) * pb).astype(o_ref.dtype)


def _merge(ya, wa, yb, wb, gates, name):
    m, ka = ya.shape
    kb = yb.shape[1]
    d = wa.shape[1]
    tm = min(512, m)
    tn = min(512, d)
    nb = d // tn
    block_bytes = (_nbytes((tm, ka), BF16) + _nbytes((ka, tn), F32) + _nbytes((tm, kb), BF16)
                   + _nbytes((kb, tn), F32) + 3 * _nbytes((tm, tn), BF16))
    return pl.pallas_call(
        _merge_kernel,
        out_shape=jax.ShapeDtypeStruct((m, d), BF16),
        grid=(nb, m // tm + 1),
        in_specs=[pl.BlockSpec((tm, ka), lambda j, i: (_row_block(i), 0)),
                  pl.BlockSpec((ka, tn), lambda j, i: (0, j)),
                  pl.BlockSpec((tm, kb), lambda j, i: (_row_block(i), 0)),
                  pl.BlockSpec((kb, tn), lambda j, i: (0, j)),
                  pl.BlockSpec((tm, tn), lambda j, i: (_row_block(i), j)),
                  pl.BlockSpec((tm, tn), lambda j, i: (_row_block(i), j + nb))],
        out_specs=pl.BlockSpec((tm, tn), lambda j, i: (_row_block(i), j)),
        scratch_shapes=[pltpu.VMEM((ka, tn), BF16), pltpu.VMEM((kb, tn), BF16)],
        compiler_params=_params(("parallel", "arbitrary"), block_bytes, _nbytes((ka + kb, tn), BF16)),
        name=name,
    )(ya, wa, yb, wb, gates, gates)


def _da_kernel(lamv_ref, gsub_ref, q_ref, k_ref, v_ref, o_ref, acc_ref, l_ref, *m_ref, tq, lam_init, bounded):
    qi = pl.program_id(2)
    hd = DA_HEAD_DIM
    m_ref = None if bounded else m_ref[0]
    if not bounded:
        m_ref[...] = jnp.full(m_ref.shape, MASK_VALUE, F32)
    l_ref[...] = jnp.zeros(l_ref.shape, F32)
    acc_ref[...] = jnp.zeros(acc_ref.shape, F32)
    q = q_ref[0]
    row_chunk = lax.broadcasted_iota(jnp.int32, (tq, tq), 0) // DA_CHUNK
    col_chunk = lax.broadcasted_iota(jnp.int32, (tq, tq), 1) // DA_CHUNK
    visible = col_chunk <= row_chunk

    def widen(x, width):
        return jnp.concatenate([x] * (width // LANES), axis=-1)

    def update(off, width, masked):
        kk = k_ref[0, pl.ds(off, width), :]
        vv = v_ref[0, pl.ds(off, width), :]
        for c in range(2):
            s = lax.dot_general(q[:, c * hd:(c + 1) * hd], kk[:, c * hd:(c + 1) * hd],
                                (((1,), (1,)), ((), ())), preferred_element_type=F32)
            if masked:
                s = jnp.where(visible, s, MASK_VALUE)
            if bounded:
                p = jnp.exp2(s)
            else:
                m_old = m_ref[c]
                m_new = jnp.maximum(m_old, jnp.max(s, axis=-1, keepdims=True))
                p = jnp.exp2(s - widen(m_new, width))
                alpha = jnp.exp2(m_old - m_new)
                m_ref[c] = m_new
            psum = p[:, :LANES]
            for g in range(1, width // LANES):
                psum = psum + p[:, g * LANES:(g + 1) * LANES]
            pv = jnp.dot(p.astype(BF16), vv, preferred_element_type=F32)
            if bounded:
                l_ref[c] = l_ref[c] + psum
                acc_ref[c] = acc_ref[c] + pv
            else:
                l_ref[c] = alpha * l_ref[c] + psum
                acc_ref[c] = widen(alpha, 2 * hd) * acc_ref[c] + pv

    def wide_body(j, carry):
        update(pl.multiple_of(j * 2 * tq, 2 * tq), 2 * tq, False)
        return carry

    lax.fori_loop(0, qi // 2, wide_body, 0)

    @pl.when(qi % 2 == 1)
    def _():
        update(pl.multiple_of((qi - 1) * tq, tq), tq, False)

    update(pl.multiple_of(qi * tq, tq), tq, True)

    lv = lamv_ref[...]
    lam = (jnp.exp(jnp.sum(lv[0:1] * lv[1:2], axis=-1, keepdims=True))
           - jnp.exp(jnp.sum(lv[2:3] * lv[3:4], axis=-1, keepdims=True)) + lam_init)
    l1 = jnp.sum(l_ref[0], axis=-1, keepdims=True)
    l2 = jnp.sum(l_ref[1], axis=-1, keepdims=True)
    o = acc_ref[0] / l1 - lam * (acc_ref[1] / l2)
    ms = jnp.mean(o * o, axis=-1, keepdims=True)
    o_ref[0] = ((o * lax.rsqrt(ms + NORM_EPS) * gsub_ref[...]) * (1.0 - lam_init)).astype(o_ref.dtype)


def _diff_attention(qk, v, lamv, g_sub, n_heads, lam_init, bounded, name):
    b, s, _ = v.shape
    hw = 2 * DA_HEAD_DIM
    tq = min(512, s)
    block_bytes = 2 * _nbytes((tq, hw), BF16) + 2 * _nbytes((s, hw), BF16) + 6 * _nbytes((tq, 2 * tq), F32)
    stats = [pltpu.VMEM((2, tq, LANES), F32)] * (1 if bounded else 2)
    return pl.pallas_call(
        functools.partial(_da_kernel, tq=tq, lam_init=lam_init, bounded=bounded),
        out_shape=jax.ShapeDtypeStruct((b, s, n_heads * hw), BF16),
        grid=(b, n_heads, s // tq),
        in_specs=[pl.BlockSpec((4, DA_HEAD_DIM), lambda bi, h, i: (0, 0)),
                  pl.BlockSpec((1, hw), lambda bi, h, i: (0, 0)),
                  pl.BlockSpec((1, tq, hw), lambda bi, h, i: (bi, i, h)),
                  pl.BlockSpec((1, s, hw), lambda bi, h, i: (bi, 0, n_heads + h)),
                  pl.BlockSpec((1, s, hw), lambda bi, h, i: (bi, 0, h))],
        out_specs=pl.BlockSpec((1, tq, hw), lambda bi, h, i: (bi, i, h)),
        scratch_shapes=[pltpu.VMEM((2, tq, hw), F32)] + stats,
        compiler_params=_params(("parallel", "parallel", "arbitrary"), block_bytes),
        name=name,
    )(lamv, g_sub, qk, qk, v)


def _diff_attention_any(qk, v, lamv, g_sub, g_q, g_k, q_scale, n_heads, lam_init):
    bound = 1.01 * DA_HEAD_DIM * q_scale * jnp.max(jnp.abs(g_q)) * jnp.max(jnp.abs(g_k))
    return lax.cond(
        bound < DA_MAX_BOUNDED_LOGIT,
        lambda: _diff_attention(qk, v, lamv, g_sub, n_heads, lam_init, True, "diff_attn"),
        lambda: _diff_attention(qk, v, lamv, g_sub, n_heads, lam_init, False, "diff_attn_online"))


def _hg_forget(lb_ref, f_ref, rows, sl):
    lb = lb_ref[:, sl]
    return lb + (1.0 - lb) * jax.nn.sigmoid(f_ref[0, rows, sl])


def _hg_finish(o, gout_ref, gate):
    ms = jnp.mean(o * o, axis=-1, keepdims=True)
    return (o * lax.rsqrt(ms + NORM_EPS) * gout_ref[...]) * (gate * jax.nn.sigmoid(gate))


def _hg_decay_kernel(lb_ref, f_ref, o_ref, *, steps):
    lb = lb_ref[...]
    nl = -jnp.log(lb + (1.0 - lb) * jax.nn.sigmoid(f_ref[...]))
    worst = jnp.zeros((1, nl.shape[1]), F32)
    for c in range(steps // HG_CHUNK):
        worst = jnp.maximum(worst, jnp.sum(nl[c * HG_CHUNK:(c + 1) * HG_CHUNK], axis=0, keepdims=True))
    o_ref[...] = jnp.broadcast_to(worst, o_ref.shape)


def _hg_max_decay(hg2, lower, hg_w, steps, name):
    t = hg2.shape[0]
    w = min(1024, hg_w)
    ncb = hg_w // w
    sub = 8
    out = pl.pallas_call(
        functools.partial(_hg_decay_kernel, steps=steps),
        out_shape=jax.ShapeDtypeStruct((t // steps * sub, hg_w), F32),
        grid=(t // steps, ncb),
        in_specs=[pl.BlockSpec((1, w), lambda i, j: (0, j)),
                  pl.BlockSpec((steps, w), lambda i, j: (i, ncb + j))],
        out_specs=pl.BlockSpec((sub, w), lambda i, j: (i, j)),
        compiler_params=_params(("parallel", "parallel"), 4 * _nbytes((steps, w), F32)),
        name=name,
    )(lower, hg2)
    return jnp.max(out)


def _hg_kernel(lb_ref, gout_ref, q_ref, f_ref, i_ref, g_ref, o_ref, st_ref, *, heads, steps):
    t = pl.program_id(2)
    ch, hd = HG_CHUNK, HG_DIM

    @pl.when(t == 0)
    def _():
        st_ref[...] = jnp.zeros(st_ref.shape, F32)

    r = lax.broadcasted_iota(jnp.int32, (ch, ch), 0)
    c = lax.broadcasted_iota(jnp.int32, (ch, ch), 1)
    causal = c <= r
    row = lax.broadcasted_iota(jnp.int32, (ch, hd), 0)
    nt = (((1,), (1,)), ((), ()))
    tn = (((0,), (0,)), ((), ()))

    def chunk(rows, sl, st):
        q = q_ref[0, rows, sl]
        f = _hg_forget(lb_ref, f_ref, rows, sl)
        key = 1.0 - f
        lf = jnp.log(f)
        cum = lf
        for sh in (1, 2, 4, 8, 16, 32):
            cum = cum + jnp.where(row >= sh, pltpu.roll(cum, sh, axis=0), 0.0)
        last = cum[ch - 1:ch]
        q_dec = (q * jnp.exp(cum)).astype(BF16)
        k_inv = (key * jnp.exp(-cum)).astype(BF16)
        k_state = (key * jnp.exp(last - cum)).astype(BF16)
        vals = i_ref[0, rows, sl].astype(BF16)
        scores = jnp.where(causal, lax.dot_general(q_dec, k_inv, nt, preferred_element_type=F32), 0.0)
        o = (jnp.dot(scores.astype(BF16), vals, preferred_element_type=F32)
             + lax.dot_general(q_dec, st.astype(BF16), nt, preferred_element_type=F32))
        st_next = jnp.exp(last) * st + lax.dot_general(vals, k_state, tn, preferred_element_type=F32)
        return _hg_finish(o, gout_ref, g_ref[0, rows, sl]), st_next

    for h in range(heads):
        sl = slice(h * hd, (h + 1) * hd)
        st = st_ref[h]
        for ci in range(steps // ch):
            rows = slice(ci * ch, (ci + 1) * ch)
            y, st = chunk(rows, sl, st)
            o_ref[0, rows, sl] = y.astype(o_ref.dtype)
        st_ref[h] = st


def _hg_seq_kernel(lb_ref, gout_ref, q_ref, f_ref, i_ref, g_ref, o_ref, st_ref, *, steps):
    t = pl.program_id(2)
    hd = HG_DIM

    @pl.when(t == 0)
    def _():
        st_ref[...] = jnp.zeros(st_ref.shape, F32)

    lane = lax.broadcasted_iota(jnp.int32, (hd, hd), 1)
    sl = slice(0, hd)

    def block(bi, carry):
        rows = pl.ds(pl.multiple_of(bi * hd, hd), hd)
        q = q_ref[0, rows, :]
        f = _hg_forget(lb_ref, f_ref, rows, sl)
        key = 1.0 - f
        vals_t = i_ref[0, rows, :].T
        st = st_ref[...]
        out_t = jnp.zeros((hd, hd), F32)
        for s in range(hd):
            st = f[s:s + 1] * st + vals_t[:, s:s + 1] * key[s:s + 1]
            out_t = jnp.where(lane == s, jnp.sum(st * q[s:s + 1], axis=1, keepdims=True), out_t)
        st_ref[...] = st
        o_ref[0, rows, :] = _hg_finish(out_t.T, gout_ref, g_ref[0, rows, :]).astype(o_ref.dtype)
        return carry

    lax.fori_loop(0, steps // hd, block, 0)


def _hgrn2(hg, lower, g_out, n_heads, heads, kernel_fn, name):
    b, s, _ = hg.shape
    steps = min(512, s)
    nhb = n_heads // heads
    w = heads * HG_DIM
    block_bytes = 4 * _nbytes((steps, w), F32) + _nbytes((steps, w), BF16)

    def spec(part):
        return pl.BlockSpec((1, steps, w), lambda bi, h, t, p=part: (bi, t, p * nhb + h))

    return pl.pallas_call(
        functools.partial(kernel_fn, steps=steps),
        out_shape=jax.ShapeDtypeStruct((b, s, n_heads * HG_DIM), BF16),
        grid=(b, nhb, s // steps),
        in_specs=[pl.BlockSpec((1, w), lambda bi, h, t: (0, h)),
                  pl.BlockSpec((1, HG_DIM), lambda bi, h, t: (0, 0)),
                  spec(0), spec(1), spec(2), spec(3)],
        out_specs=pl.BlockSpec((1, steps, w), lambda bi, h, t: (bi, t, h)),
        scratch_shapes=[pltpu.VMEM((heads, HG_DIM, HG_DIM) if heads > 1 else (HG_DIM, HG_DIM), F32)],
        compiler_params=_params(("parallel", "parallel", "arbitrary"), block_bytes),
        name=name,
    )(lower, g_out, hg, hg, hg, hg)


def _hgrn2_any(hg, lower, g_out, n_heads):
    b, s, c4 = hg.shape
    hg_w = c4 // 4
    heads = min(4, n_heads)
    decay = _hg_max_decay(hg.reshape(b * s, c4), lower, hg_w, min(512, s), "hgrn2_decay")
    return lax.cond(
        decay < HG_FAST_MAX_DECAY,
        lambda: _hgrn2(hg, lower, g_out, n_heads, heads, functools.partial(_hg_kernel, heads=heads), "hgrn2"),
        lambda: _hgrn2(hg, lower, g_out, n_heads, 1, _hg_seq_kernel, "hgrn2_stepwise"))


def _xattn_kernel(q_ref, k_ref, v_ref, o_ref, *, n_heads):
    hd = X_HEAD_DIM
    for h in range(n_heads):
        sl = slice(h * hd, (h + 1) * hd)
        s = lax.dot_general(q_ref[0, :, sl], k_ref[0, :, sl], (((1,), (1,)), ((), ())),
                            preferred_element_type=F32)
        p = jnp.exp(s - jnp.max(s, axis=-1, keepdims=True))
        denom = jnp.sum(p, axis=-1, keepdims=True)
        o = jnp.dot(p.astype(BF16), v_ref[0, :, sl], preferred_element_type=F32)
        o_ref[0, :, sl] = (o / denom).astype(o_ref.dtype)


def _cross_attention(q, k, v, n_heads, name):
    b, s, w = q.shape
    mlen = k.shape[1]
    tq = min(1024, s)
    block_bytes = 2 * _nbytes((tq, w), BF16) + 2 * _nbytes((mlen, w), BF16) + 3 * _nbytes((tq, mlen), F32)
    return pl.pallas_call(
        functools.partial(_xattn_kernel, n_heads=n_heads),
        out_shape=jax.ShapeDtypeStruct((b, s, w), BF16),
        grid=(b, s // tq),
        in_specs=[pl.BlockSpec((1, tq, w), lambda bi, i: (bi, i, 0)),
                  pl.BlockSpec((1, mlen, w), lambda bi, i: (bi, 0, 0)),
                  pl.BlockSpec((1, mlen, w), lambda bi, i: (bi, 0, 0))],
        out_specs=pl.BlockSpec((1, tq, w), lambda bi, i: (bi, i, 0)),
        compiler_params=_params(("parallel", "parallel"), block_bytes),
        name=name,
    )(q, k, v)


def _pack_bf16_pair(lo, hi):
    lo_bits = lax.bitcast_convert_type(lo.astype(BF16).astype(F32), jnp.uint32)
    hi_bits = lax.bitcast_convert_type(hi.astype(BF16).astype(F32), jnp.uint32)
    return (lo_bits >> 16) | (hi_bits & jnp.uint32(0xFFFF0000))


def _unpack_bf16_pair(words):
    lo = lax.bitcast_convert_type(words << 16, F32)
    hi = lax.bitcast_convert_type(words & jnp.uint32(0xFFFF0000), F32)
    return lo, hi


def _router_kernel(h_ref, g_ref, whi_ref, wlo_ref, b_ref, n_ref, lg_ref):
    x = h_ref[...]
    ms = jnp.mean(x * x, axis=-1, keepdims=True)
    n = x * lax.rsqrt(ms + NORM_EPS) * g_ref[...]
    n_hi = n.astype(BF16)
    n_lo = (n - n_hi.astype(F32)).astype(BF16)
    half = n.shape[1] // 2
    n_ref[...] = _pack_bf16_pair(n[:, :half], n[:, half:])
    lg_ref[...] = (jnp.dot(n_hi, whi_ref[...], preferred_element_type=F32)
                   + jnp.dot(n_lo, whi_ref[...], preferred_element_type=F32)
                   + jnp.dot(n_hi, wlo_ref[...], preferred_element_type=F32) + b_ref[...])


def _router(h, g, w_hi, w_lo, bias, name):
    m, d = h.shape
    nl = w_hi.shape[1]
    tm = min(256, m)
    block_bytes = 2 * _nbytes((tm, d), F32) + 2 * _nbytes((d, nl), BF16)
    return pl.pallas_call(
        _router_kernel,
        out_shape=(jax.ShapeDtypeStruct((m, d // 2), jnp.uint32), jax.ShapeDtypeStruct((m, nl), F32)),
        grid=(m // tm,),
        in_specs=[pl.BlockSpec((tm, d), lambda i: (i, 0)),
                  pl.BlockSpec((1, d), lambda i: (0, 0)),
                  pl.BlockSpec((d, nl), lambda i: (0, 0)),
                  pl.BlockSpec((d, nl), lambda i: (0, 0)),
                  pl.BlockSpec((1, nl), lambda i: (0, 0))],
        out_specs=(pl.BlockSpec((tm, d // 2), lambda i: (i, 0)), pl.BlockSpec((tm, nl), lambda i: (i, 0))),
        compiler_params=_params(("parallel",), block_bytes),
        name=name,
    )(h, g, w_hi, w_lo, bias)


def _row_copies(src_ref, dst_ref, idx_ref, idx0, idx_stride, dst0, n, sem, *, start):
    assert n % ROW_COPY_UNROLL == 0

    def body(g, carry):
        for u in range(ROW_COPY_UNROLL):
            r = g * ROW_COPY_UNROLL + u
            cp = pltpu.make_async_copy(src_ref.at[pl.ds(idx_ref[idx0 + r * idx_stride], 1)],
                                       dst_ref.at[pl.ds(dst0 + r, 1)], sem)
            if start:
                cp.start(priority=1)
            else:
                cp.wait()
        return carry

    lax.fori_loop(0, n // ROW_COPY_UNROLL, body, 0)


def _expert_kernel(tok_ref, be_ref, nreal_ref, nf_ref, wg_ref, wu_ref, wd_ref, o_ref, xbuf, sems, *, blk):
    i = pl.program_id(0)
    n_real = nreal_ref[0]

    def gather(block, start):
        slot = block % 2
        _row_copies(nf_ref, xbuf.at[slot], tok_ref, block * blk, 1, 0, blk, sems.at[slot], start=start)

    @pl.when(jnp.logical_and(i == 0, n_real > 0))
    def _():
        gather(i, True)

    @pl.when(i + 1 < n_real)
    def _():
        gather(i + 1, True)

    @pl.when(i < n_real)
    def _():
        gather(i, False)
        x_lo, x_hi = _unpack_bf16_pair(xbuf[i % 2])
        x_lo, x_hi = x_lo.astype(BF16), x_hi.astype(BF16)
        half = x_lo.shape[1]

        def up(w_ref):
            return (jnp.dot(x_lo, w_ref[0, :half, :], preferred_element_type=F32)
                    + jnp.dot(x_hi, w_ref[0, half:, :], preferred_element_type=F32))

        hg = up(wg_ref)
        hid = (hg * jax.nn.sigmoid(hg) * up(wu_ref)).astype(BF16)
        y = jnp.dot(hid, wd_ref[0], preferred_element_type=F32)
        o_ref[...] = _pack_bf16_pair(y[:, :half], y[:, half:])

    @pl.when(i >= n_real)
    def _():
        o_ref[...] = jnp.zeros(o_ref.shape, o_ref.dtype)


def _experts(nf, buf_tok, blk_expert, n_real, wg, wu, wd, blk, name):
    cap = buf_tok.shape[0]
    d = wg.shape[1]
    ff = wg.shape[2]
    half = d // 2
    block_bytes = 3 * _nbytes((d, ff), BF16) + _nbytes((blk, half), jnp.uint32) + 3 * _nbytes((blk, d), F32)
    return pl.pallas_call(
        functools.partial(_expert_kernel, blk=blk),
        out_shape=jax.ShapeDtypeStruct((cap, half), jnp.uint32),
        grid_spec=pltpu.PrefetchScalarGridSpec(
            num_scalar_prefetch=3,
            grid=(cap // blk,),
            in_specs=[pl.BlockSpec(memory_space=pl.ANY),
                      pl.BlockSpec((1, d, ff), lambda i, tk, be, nr: (be[i], 0, 0)),
                      pl.BlockSpec((1, d, ff), lambda i, tk, be, nr: (be[i], 0, 0)),
                      pl.BlockSpec((1, ff, d), lambda i, tk, be, nr: (be[i], 0, 0))],
            out_specs=pl.BlockSpec((blk, half), lambda i, tk, be, nr: (i, 0)),
            scratch_shapes=[pltpu.VMEM((2, blk, half), jnp.uint32), pltpu.SemaphoreType.DMA((2,))]),
        compiler_params=_params(("arbitrary",), block_bytes, 2 * _nbytes((blk, half), jnp.uint32)),
        name=name,
    )(buf_tok, blk_expert, n_real, nf, wg, wu, wd)


def _combine_kernel(slot_ref, h_ref, gate_ref, ys_ref, o_ref, ybuf, sems, *, tm):
    i = pl.program_id(0)

    def gather(block, start):
        s = block % 2
        for k in range(TOP_K):
            _row_copies(ys_ref, ybuf.at[s], slot_ref, block * tm * TOP_K + k, TOP_K, k * tm, tm, sems.at[s],
                        start=start)

    @pl.when(i == 0)
    def _():
        gather(i, True)

    @pl.when(i + 1 < pl.num_programs(0))
    def _():
        gather(i + 1, True)

    gather(i, False)
    half = ybuf.shape[2]
    gate = gate_ref[...]
    acc_lo = h_ref[:, :half]
    acc_hi = h_ref[:, half:]
    for k in range(TOP_K):
        y_lo, y_hi = _unpack_bf16_pair(ybuf[i % 2, pl.ds(k * tm, tm), :])
        acc_lo = acc_lo + gate[:, k:k + 1] * y_lo
        acc_hi = acc_hi + gate[:, k:k + 1] * y_hi
    o_ref[:, :half] = acc_lo
    o_ref[:, half:] = acc_hi


def _combine(h, gate, ys, slot, name):
    m, d = h.shape
    tm = min(128, m)
    block_bytes = 2 * _nbytes((tm, d), F32) + TOP_K * _nbytes((tm, d), F32)
    return pl.pallas_call(
        functools.partial(_combine_kernel, tm=tm),
        out_shape=jax.ShapeDtypeStruct((m, d), F32),
        grid_spec=pltpu.PrefetchScalarGridSpec(
            num_scalar_prefetch=1,
            grid=(m // tm,),
            in_specs=[pl.BlockSpec((tm, d), lambda i, sl: (i, 0)),
                      pl.BlockSpec((tm, TOP_K), lambda i, sl: (i, 0)),
                      pl.BlockSpec(memory_space=pl.ANY)],
            out_specs=pl.BlockSpec((tm, d), lambda i, sl: (i, 0)),
            scratch_shapes=[pltpu.VMEM((2, TOP_K * tm, d // 2), jnp.uint32), pltpu.SemaphoreType.DMA((2,))]),
        compiler_params=_params(("arbitrary",), block_bytes),
        name=name,
    )(slot, h, gate, ys)


def _route(logits, n_groups, n_experts, blk):
    t = logits.shape[0]
    epg = n_experts // n_groups
    grp_logits = logits[:, :n_groups]
    grp_prob = jax.nn.softmax(grp_logits, axis=-1)
    grp = jnp.argmax(grp_logits, axis=-1).astype(jnp.int32)
    grp_w = jnp.take_along_axis(grp_prob, grp[:, None], axis=-1)
    exp_logits = logits[:, n_groups:n_groups + n_experts].reshape(t, n_groups, epg)
    in_grp = jnp.take_along_axis(exp_logits, grp[:, None, None], axis=1)[:, 0]
    top_val, top_idx = lax.top_k(in_grp, TOP_K)
    gate = grp_w * jax.nn.softmax(top_val, axis=-1)
    expert = (grp[:, None] * epg + top_idx.astype(jnp.int32)).reshape(-1)
    n_assign = t * TOP_K
    e_sorted, order = lax.sort_key_val(expert, jnp.arange(n_assign, dtype=jnp.int32))
    experts = jnp.arange(n_experts, dtype=jnp.int32)
    starts = jnp.searchsorted(e_sorted, experts, side='left').astype(jnp.int32)
    counts = jnp.searchsorted(e_sorted, experts, side='right').astype(jnp.int32) - starts
    padded = (counts + blk - 1) // blk * blk
    pad_ends = jnp.cumsum(padded)
    pad_starts = pad_ends - padded
    dest = pad_starts[e_sorted] + jnp.arange(n_assign, dtype=jnp.int32) - starts[e_sorted]
    _, slot = lax.sort_key_val(order, dest)
    cap = n_assign + n_experts * blk
    n_blk = cap // blk
    blk_expert = jnp.minimum(jnp.searchsorted(pad_ends, jnp.arange(n_blk, dtype=jnp.int32) * blk, side='right'),
                             n_experts - 1).astype(jnp.int32)
    row = jnp.arange(cap, dtype=jnp.int32)
    row_expert = jnp.repeat(blk_expert, blk)
    within = row - pad_starts[row_expert]
    src = jnp.clip(within + starts[row_expert], 0, n_assign - 1)
    buf_tok = jnp.where(within < counts[row_expert], order[src] // TOP_K, 0).astype(jnp.int32)
    n_real = (pad_ends[-1] // blk).astype(jnp.int32).reshape(1)
    return gate, buf_tok, blk_expert, n_real, slot.astype(jnp.int32)


def _diff_lambda_init(layer):
    return 0.8 - 0.6 * math.exp(-0.3 * layer)


def _tile_cols(v, n):
    return jnp.tile(v.reshape(1, -1).astype(F32), (1, n // v.size))


def kernel(x, mem, g_mix, w_in, g_da_q, g_da_k, lam_q1, lam_k1, lam_q2, lam_k2, g_da_sub, hg_lower, g_hg_out,
           w_up_a, w_up_b, w_gate, b_gate, w_out, g_cross, g_mem, w_xq, w_xkv, g_xq, g_xk, w_xo, g_ffn, w_grp,
           b_grp, w_erouter, b_erouter, w_e_gate, w_e_up, w_e_down):
    b, s, d = x.shape
    t = b * s
    depth = g_mix.shape[0]
    da_w = w_up_a.shape[1]
    hg_w = w_up_b.shape[1]
    da_heads = da_w // (2 * DA_HEAD_DIM)
    hg_heads = hg_w // HG_DIM
    x_w = w_xq.shape[2]
    x_heads = x_w // X_HEAD_DIM
    n_groups = w_grp.shape[2]
    n_experts = w_erouter.shape[2]
    moe_blk = min(256, t)

    lower_bounds = jnp.cumsum(jax.nn.softmax(hg_lower.astype(F32), axis=0), axis=0)
    h = x.reshape(t, d)
    mem2 = mem.reshape(b * mem.shape[1], d)
    for l in range(depth):
        lam_init = _diff_lambda_init(l)
        w_in_l = w_in[l]
        n = _rmsnorm(h, g_mix[l], BF16, "rms_mix")
        q_scale = DA_HEAD_DIM ** -0.5 * math.log2(math.e)
        qk_gain = jnp.concatenate([_tile_cols(g_da_q[l], da_w) * q_scale, _tile_cols(g_da_k[l], da_w)], axis=1)
        qk = _matmul(n, w_in_l, col0=0, ncols=2 * da_w, out_dtype=BF16,
                     epilogue=functools.partial(_ep_group_norm, group=DA_HEAD_DIM),
                     col_extras=[(qk_gain, 0)], name="proj_qk")
        v = _matmul(n, w_in_l, col0=2 * da_w, ncols=da_w, out_dtype=BF16, name="proj_v")
        hg = _matmul(n, w_in_l, col0=3 * da_w, ncols=4 * hg_w, out_dtype=F32, name="proj_hg")
        gates = _matmul(n, w_gate[l], out_dtype=BF16, epilogue=_ep_bias_sigmoid,
                        col_extras=[(b_gate[l].reshape(1, -1).astype(F32), 0)], name="proj_gate")
        lamv = jnp.stack([lam_q1[l], lam_k1[l], lam_q2[l], lam_k2[l]]).astype(F32)
        y_a = _diff_attention_any(qk.reshape(b, s, 2 * da_w), v.reshape(b, s, da_w), lamv,
                                  g_da_sub[l].reshape(1, -1).astype(F32), g_da_q[l], g_da_k[l], q_scale,
                                  da_heads, lam_init)
        y_b = _hgrn2_any(hg.reshape(b, s, 4 * hg_w), lower_bounds[l].reshape(1, hg_w),
                         g_hg_out[l].reshape(1, HG_DIM).astype(F32), hg_heads)
        merged = _merge(y_a.reshape(t, da_w), w_up_a[l], y_b.reshape(t, hg_w), w_up_b[l], gates, "merge")
        h = _matmul(merged, w_out[l], out_dtype=F32, epilogue=_ep_residual,
                    full_extras=[(h, 0)], name="mix_out")
        nc = _rmsnorm(h, g_cross[l], BF16, "rms_cross")
        xq = _matmul(nc, w_xq[l], out_dtype=BF16,
                     epilogue=functools.partial(_ep_group_norm, group=X_HEAD_DIM),
                     col_extras=[(_tile_cols(g_xq[l], x_w) * X_HEAD_DIM ** -0.5, 0)], name="xattn_q")
        nm = _rmsnorm(mem2, g_mem[l], BF16, "rms_mem")
        w_xkv_l = w_xkv[l]
        xk = _matmul(nm, w_xkv_l, col0=0, ncols=x_w, out_dtype=BF16,
                     epilogue=functools.partial(_ep_group_norm, group=X_HEAD_DIM),
                     col_extras=[(_tile_cols(g_xk[l], x_w), 0)], name="xattn_k")
        xv = _matmul(nm, w_xkv_l, col0=x_w, ncols=x_w, out_dtype=BF16, name="xattn_v")
        xo = _cross_attention(xq.reshape(b, s, x_w), xk.reshape(b, -1, x_w), xv.reshape(b, -1, x_w), x_heads,
                              "xattn")
        h = _matmul(xo.reshape(t, x_w), w_xo[l], out_dtype=F32, epilogue=_ep_residual,
                    full_extras=[(h, 0)], name="xattn_out")
        n_logit = -(-(n_groups + n_experts) // LANES) * LANES
        w_r = jnp.zeros((d, n_logit), F32).at[:, :n_groups].set(w_grp[l]).at[:, n_groups:n_groups + n_experts].set(
            w_erouter[l])
        b_r = jnp.zeros((1, n_logit), F32).at[0, :n_groups].set(b_grp[l]).at[0, n_groups:n_groups + n_experts].set(
            b_erouter[l])
        w_r_hi = w_r.astype(BF16)
        w_r_lo = (w_r - w_r_hi.astype(F32)).astype(BF16)
        nf, logits = _router(h, g_ffn[l].reshape(1, d).astype(F32), w_r_hi, w_r_lo, b_r, "router")
        gate, buf_tok, blk_expert, n_real, slot = _route(logits, n_groups, n_experts, moe_blk)
        ys = _experts(nf, buf_tok, blk_expert, n_real, w_e_gate[l].astype(BF16), w_e_up[l].astype(BF16),
                      w_e_down[l].astype(BF16), moe_blk, "moe_experts")
        h = _combine(h, gate, ys, slot, "moe_combine")
    return h.reshape(b, s, d)
```

```python
import functools
import math

import jax
import jax.numpy as jnp
from jax import lax
from jax.experimental import pallas as pl
from jax.experimental.pallas import tpu as pltpu

F32 = jnp.float32
BF16 = jnp.bfloat16

NORM_EPS = 1e-6
DA_HEAD_DIM = 128
DA_CHUNK = 64
DA_MAX_BOUNDED_LOGIT = 50.0
HG_DIM = 128
HG_CHUNK = 64
HG_FAST_MAX_DECAY = 60.0
X_HEAD_DIM = 256
TOP_K = 2
ROW_COPY_UNROLL = 8
LANES = 128
MASK_VALUE = -1e30
V7X_VMEM_BYTES = 64 * 1024 * 1024
VMEM_CAP_BYTES = V7X_VMEM_BYTES - 6 * 1024 * 1024


def _vmem_limit(block_bytes, scratch_bytes):
    return int(min(VMEM_CAP_BYTES, max(32 * 1024 * 1024, 2 * block_bytes + scratch_bytes + 16 * 1024 * 1024)))


def _params(semantics, block_bytes, scratch_bytes=0):
    return pltpu.CompilerParams(dimension_semantics=semantics,
                                vmem_limit_bytes=_vmem_limit(block_bytes, scratch_bytes))


def _nbytes(shape, dtype):
    return math.prod(shape) * jnp.dtype(dtype).itemsize


def _rms_kernel(x_ref, g_ref, o_ref):
    x = x_ref[...].astype(F32)
    ms = jnp.mean(x * x, axis=-1, keepdims=True)
    o_ref[...] = (x * lax.rsqrt(ms + NORM_EPS) * g_ref[...]).astype(o_ref.dtype)


def _rmsnorm(x, g, out_dtype, name):
    m, d = x.shape
    tm = min(256, m)
    return pl.pallas_call(
        _rms_kernel,
        out_shape=jax.ShapeDtypeStruct((m, d), out_dtype),
        grid=(m // tm,),
        in_specs=[pl.BlockSpec((tm, d), lambda i: (i, 0)), pl.BlockSpec((1, d), lambda i: (0, 0))],
        out_specs=pl.BlockSpec((tm, d), lambda i: (i, 0)),
        compiler_params=_params(("parallel",), _nbytes((tm, d), x.dtype) + _nbytes((tm, d), out_dtype)),
        name=name,
    )(x, g.reshape(1, d).astype(F32))


def _ep_plain(acc, o_ref):
    o_ref[...] = acc.astype(o_ref.dtype)


def _ep_group_norm(acc, o_ref, gain, *, group):
    for c in range(acc.shape[1] // group):
        sl = slice(c * group, (c + 1) * group)
        xg = acc[:, sl]
        ms = jnp.mean(xg * xg, axis=-1, keepdims=True)
        o_ref[:, sl] = (xg * lax.rsqrt(ms + NORM_EPS) * gain[:, sl]).astype(o_ref.dtype)


def _ep_bias_sigmoid(acc, o_ref, bias):
    o_ref[...] = jax.nn.sigmoid(acc + bias).astype(o_ref.dtype)


def _ep_residual(acc, o_ref, res):
    o_ref[...] = (res + acc).astype(o_ref.dtype)


def _row_block(i):
    return jnp.maximum(i - 1, 0)


def _mm_kernel(a_ref, w_ref, *rest, epilogue, side):
    if side:
        *extra, side_src_ref, o_ref, side_out_ref, wb_ref = rest
    else:
        *extra, o_ref, wb_ref = rest
    i = pl.program_id(1)

    @pl.when(i == 0)
    def _():
        wb_ref[...] = w_ref[...].astype(BF16)

    @pl.when(i > 0)
    def _():
        acc = jnp.dot(a_ref[...], wb_ref[...], preferred_element_type=F32)
        epilogue(acc, o_ref, *[e[...] for e in extra])
        if side:
            side_out_ref[...] = side_src_ref[...].astype(BF16)


def _matmul(a, w, *, col0=0, ncols=None, out_dtype, epilogue=_ep_plain, col_extras=(), full_extras=(),
            side_cast=None, name):
    m, k = a.shape
    ncols = w.shape[1] - col0 if ncols is None else ncols
    tm = min(1024, m)
    tn = min(512, ncols)
    assert m % tm == 0 and ncols % tn == 0 and col0 % tn == 0
    n_rb = m // tm
    in_specs = [pl.BlockSpec((tm, k), lambda j, i: (_row_block(i), 0)),
                pl.BlockSpec((k, tn), lambda j, i, o=col0 // tn: (0, j + o))]
    operands = [a, w]
    block_bytes = _nbytes((tm, k), a.dtype) + _nbytes((k, tn), w.dtype) + _nbytes((tm, tn), out_dtype)
    for vec, c0 in col_extras:
        assert c0 % tn == 0
        in_specs.append(pl.BlockSpec((1, tn), lambda j, i, o=c0 // tn: (0, j + o)))
        operands.append(vec)
    for arr, c0 in full_extras:
        assert c0 % tn == 0
        in_specs.append(pl.BlockSpec((tm, tn), lambda j, i, o=c0 // tn: (_row_block(i), j + o)))
        operands.append(arr)
        block_bytes += _nbytes((tm, tn), arr.dtype)
    out_shape = jax.ShapeDtypeStruct((m, ncols), out_dtype)
    out_specs = pl.BlockSpec((tm, tn), lambda j, i: (_row_block(i), j))
    if side_cast is not None:
        rows, cols = side_cast.shape
        n_steps = (ncols // tn) * n_rb
        chunk = rows // n_steps
        assert chunk * n_steps == rows and chunk % 16 == 0, (rows, n_steps)
        side_spec = pl.BlockSpec((chunk, cols), lambda j, i: (j * n_rb + _row_block(i), 0))
        in_specs.append(side_spec)
        operands.append(side_cast)
        out_shape = (out_shape, jax.ShapeDtypeStruct((rows, cols), BF16))
        out_specs = (out_specs, side_spec)
        block_bytes += _nbytes((chunk, cols), F32) + _nbytes((chunk, cols), BF16)
    return pl.pallas_call(
        functools.partial(_mm_kernel, epilogue=epilogue, side=side_cast is not None),
        out_shape=out_shape,
        grid=(ncols // tn, n_rb + 1),
        in_specs=in_specs,
        out_specs=out_specs,
        scratch_shapes=[pltpu.VMEM((k, tn), BF16)],
        compiler_params=_params(("parallel", "arbitrary"), block_bytes, _nbytes((k, tn), BF16)),
        name=name,
    )(*operands)


def _merge_kernel(ya_ref, wa_ref, yb_ref, wb_ref, ga_ref, gb_ref, o_ref, wa_bf_ref, wb_bf_ref):
    i = pl.program_id(1)

    @pl.when(i == 0)
    def _():
        wa_bf_ref[...] = wa_ref[...].astype(BF16)
        wb_bf_ref[...] = wb_ref[...].astype(BF16)

    @pl.when(i > 0)
    def _():
        pa = jnp.dot(ya_ref[...], wa_bf_ref[...], preferred_element_type=F32)
        pb = jnp.dot(yb_ref[...], wb_bf_ref[...], preferred_element_type=F32)
        o_ref[...] = (ga_ref[...].astype(F32) * pa + gb_ref[...].astype(F32) * pb).astype(o_ref.dtype)


def _merge(ya, wa, yb, wb, gates, name):
    m, ka = ya.shape
    kb = yb.shape[1]
    d = wa.shape[1]
    tm = min(512, m)
    tn = min(512, d)
    nb = d // tn
    block_bytes = (_nbytes((tm, ka), BF16) + _nbytes((ka, tn), F32) + _nbytes((tm, kb), BF16)
                   + _nbytes((kb, tn), F32) + 3 * _nbytes((tm, tn), BF16))
    return pl.pallas_call(
        _merge_kernel,
        out_shape=jax.ShapeDtypeStruct((m, d), BF16),
        grid=(nb, m // tm + 1),
        in_specs=[pl.BlockSpec((tm, ka), lambda j, i: (_row_block(i), 0)),
                  pl.BlockSpec((ka, tn), lambda j, i: (0, j)),
                  pl.BlockSpec((tm, kb), lambda j, i: (_row_block(i), 0)),
                  pl.BlockSpec((kb, tn), lambda j, i: (0, j)),
                  pl.BlockSpec((tm, tn), lambda j, i: (_row_block(i), j)),
                  pl.BlockSpec((tm, tn), lambda j, i: (_row_block(i), j + nb))],
        out_specs=pl.BlockSpec((tm, tn), lambda j, i: (_row_block(i), j)),
        scratch_shapes=[pltpu.VMEM((ka, tn), BF16), pltpu.VMEM((kb, tn), BF16)],
        compiler_params=_params(("parallel", "arbitrary"), block_bytes, _nbytes((ka + kb, tn), BF16)),
        name=name,
    )(ya, wa, yb, wb, gates, gates)


def _da_kernel(lamv_ref, gsub_ref, q_ref, k_ref, v_ref, o_ref, acc_ref, l_ref, *m_ref, tq, lam_init, bounded):
    qi = pl.program_id(2)
    hd = DA_HEAD_DIM
    m_ref = None if bounded else m_ref[0]
    if not bounded:
        m_ref[...] = jnp.full(m_ref.shape, MASK_VALUE, F32)
    l_ref[...] = jnp.zeros(l_ref.shape, F32)
    acc_ref[...] = jnp.zeros(acc_ref.shape, F32)
    q = q_ref[0]
    row_chunk = lax.broadcasted_iota(jnp.int32, (tq, tq), 0) // DA_CHUNK
    col_chunk = lax.broadcasted_iota(jnp.int32, (tq, tq), 1) // DA_CHUNK
    visible = col_chunk <= row_chunk

    def widen(x, width):
        return jnp.concatenate([x] * (width // LANES), axis=-1)

    def update(off, width, masked):
        kk = k_ref[0, pl.ds(off, width), :]
        vv = v_ref[0, pl.ds(off, width), :]
        for c in range(2):
            s = lax.dot_general(q[:, c * hd:(c + 1) * hd], kk[:, c * hd:(c + 1) * hd],
                                (((1,), (1,)), ((), ())), preferred_element_type=F32)
            if masked:
                s = jnp.where(visible, s, MASK_VALUE)
            if bounded:
                p = jnp.exp2(s)
            else:
                m_old = m_ref[c]
                m_new = jnp.maximum(m_old, jnp.max(s, axis=-1, keepdims=True))
                p = jnp.exp2(s - widen(m_new, width))
                alpha = jnp.exp2(m_old - m_new)
                m_ref[c] = m_new
            psum = p[:, :LANES]
            for g in range(1, width // LANES):
                psum = psum + p[:, g * LANES:(g + 1) * LANES]
            pv = jnp.dot(p.astype(BF16), vv, preferred_element_type=F32)
            if bounded:
                l_ref[c] = l_ref[c] + psum
                acc_ref[c] = acc_ref[c] + pv
            else:
                l_ref[c] = alpha * l_ref[c] + psum
                acc_ref[c] = widen(alpha, 2 * hd) * acc_ref[c] + pv

    def wide_body(j, carry):
        update(pl.multiple_of(j * 2 * tq, 2 * tq), 2 * tq, False)
        return carry

    lax.fori_loop(0, qi // 2, wide_body, 0)

    @pl.when(qi % 2 == 1)
    def _():
        update(pl.multiple_of((qi - 1) * tq, tq), tq, False)

    update(pl.multiple_of(qi * tq, tq), tq, True)

    lv = lamv_ref[...]
    lam = (jnp.exp(jnp.sum(lv[0:1] * lv[1:2], axis=-1, keepdims=True))
           - jnp.exp(jnp.sum(lv[2:3] * lv[3:4], axis=-1, keepdims=True)) + lam_init)
    l1 = jnp.sum(l_ref[0], axis=-1, keepdims=True)
    l2 = jnp.sum(l_ref[1], axis=-1, keepdims=True)
    o = acc_ref[0] / l1 - lam * (acc_ref[1] / l2)
    ms = jnp.mean(o * o, axis=-1, keepdims=True)
    o_ref[0] = ((o * lax.rsqrt(ms + NORM_EPS) * gsub_ref[...]) * (1.0 - lam_init)).astype(o_ref.dtype)


def _diff_attention(qk, v, lamv, g_sub, n_heads, lam_init, bounded, name):
    b, s, _ = v.shape
    hw = 2 * DA_HEAD_DIM
    tq = min(512, s)
    block_bytes = 2 * _nbytes((tq, hw), BF16) + 2 * _nbytes((s, hw), BF16) + 6 * _nbytes((tq, 2 * tq), F32)
    stats = [pltpu.VMEM((2, tq, LANES), F32)] * (1 if bounded else 2)
    return pl.pallas_call(
        functools.partial(_da_kernel, tq=tq, lam_init=lam_init, bounded=bounded),
        out_shape=jax.ShapeDtypeStruct((b, s, n_heads * hw), BF16),
        grid=(b, n_heads, s // tq),
        in_specs=[pl.BlockSpec((4, DA_HEAD_DIM), lambda bi, h, i: (0, 0)),
                  pl.BlockSpec((1, hw), lambda bi, h, i: (0, 0)),
                  pl.BlockSpec((1, tq, hw), lambda bi, h, i: (bi, i, h)),
                  pl.BlockSpec((1, s, hw), lambda bi, h, i: (bi, 0, n_heads + h)),
                  pl.BlockSpec((1, s, hw), lambda bi, h, i: (bi, 0, h))],
        out_specs=pl.BlockSpec((1, tq, hw), lambda bi, h, i: (bi, i, h)),
        scratch_shapes=[pltpu.VMEM((2, tq, hw), F32)] + stats,
        compiler_params=_params(("parallel", "parallel", "arbitrary"), block_bytes),
        name=name,
    )(lamv, g_sub, qk, qk, v)


def _diff_attention_any(qk, v, lamv, g_sub, g_q, g_k, q_scale, n_heads, lam_init):
    bound = 1.01 * DA_HEAD_DIM * q_scale * jnp.max(jnp.abs(g_q)) * jnp.max(jnp.abs(g_k))
    return lax.cond(
        bound < DA_MAX_BOUNDED_LOGIT,
        lambda: _diff_attention(qk, v, lamv, g_sub, n_heads, lam_init, True, "diff_attn"),
        lambda: _diff_attention(qk, v, lamv, g_sub, n_heads, lam_init, False, "diff_attn_online"))


def _hg_forget(lb_ref, f_ref, rows, sl):
    lb = lb_ref[:, sl]
    return lb + (1.0 - lb) * jax.nn.sigmoid(f_ref[0, rows, sl])


def _hg_finish(o, gout_ref, gate):
    ms = jnp.mean(o * o, axis=-1, keepdims=True)
    return (o * lax.rsqrt(ms + NORM_EPS) * gout_ref[...]) * (gate * jax.nn.sigmoid(gate))


def _hg_decay_kernel(lb_ref, f_ref, o_ref, *, steps):
    lb = lb_ref[...]
    nl = -jnp.log(lb + (1.0 - lb) * jax.nn.sigmoid(f_ref[...]))
    worst = jnp.zeros((1, nl.shape[1]), F32)
    for c in range(steps // HG_CHUNK):
        worst = jnp.maximum(worst, jnp.sum(nl[c * HG_CHUNK:(c + 1) * HG_CHUNK], axis=0, keepdims=True))
    o_ref[...] = jnp.broadcast_to(worst, o_ref.shape)


def _hg_max_decay(hg2, lower, hg_w, steps, name):
    t = hg2.shape[0]
    w = min(1024, hg_w)
    ncb = hg_w // w
    sub = 8
    out = pl.pallas_call(
        functools.partial(_hg_decay_kernel, steps=steps),
        out_shape=jax.ShapeDtypeStruct((t // steps * sub, hg_w), F32),
        grid=(t // steps, ncb),
        in_specs=[pl.BlockSpec((1, w), lambda i, j: (0, j)),
                  pl.BlockSpec((steps, w), lambda i, j: (i, ncb + j))],
        out_specs=pl.BlockSpec((sub, w), lambda i, j: (i, j)),
        compiler_params=_params(("parallel", "parallel"), 4 * _nbytes((steps, w), F32)),
        name=name,
    )(lower, hg2)
    return jnp.max(out)


def _hg_kernel(lb_ref, gout_ref, q_ref, f_ref, i_ref, g_ref, o_ref, st_ref, *, heads, steps):
    t = pl.program_id(2)
    ch, hd = HG_CHUNK, HG_DIM

    @pl.when(t == 0)
    def _():
        st_ref[...] = jnp.zeros(st_ref.shape, F32)

    r = lax.broadcasted_iota(jnp.int32, (ch, ch), 0)
    c = lax.broadcasted_iota(jnp.int32, (ch, ch), 1)
    causal = c <= r
    row = lax.broadcasted_iota(jnp.int32, (ch, hd), 0)
    nt = (((1,), (1,)), ((), ()))
    tn = (((0,), (0,)), ((), ()))

    def chunk(rows, sl, st):
        q = q_ref[0, rows, sl]
        f = _hg_forget(lb_ref, f_ref, rows, sl)
        key = 1.0 - f
        lf = jnp.log(f)
        cum = lf
        for sh in (1, 2, 4, 8, 16, 32):
            cum = cum + jnp.where(row >= sh, pltpu.roll(cum, sh, axis=0), 0.0)
        last = cum[ch - 1:ch]
        q_dec = (q * jnp.exp(cum)).astype(BF16)
        k_inv = (key * jnp.exp(-cum)).astype(BF16)
        k_state = (key * jnp.exp(last - cum)).astype(BF16)
        vals = i_ref[0, rows, sl].astype(BF16)
        scores = jnp.where(causal, lax.dot_general(q_dec, k_inv, nt, preferred_element_type=F32), 0.0)
        o = (jnp.dot(scores.astype(BF16), vals, preferred_element_type=F32)
             + lax.dot_general(q_dec, st.astype(BF16), nt, preferred_element_type=F32))
        st_next = jnp.exp(last) * st + lax.dot_general(vals, k_state, tn, preferred_element_type=F32)
        return _hg_finish(o, gout_ref, g_ref[0, rows, sl]), st_next

    for h in range(heads):
        sl = slice(h * hd, (h + 1) * hd)
        st = st_ref[h]
        for ci in range(steps // ch):
            rows = slice(ci * ch, (ci + 1) * ch)
            y, st = chunk(rows, sl, st)
            o_ref[0, rows, sl] = y.astype(o_ref.dtype)
        st_ref[h] = st


def _hg_seq_kernel(lb_ref, gout_ref, q_ref, f_ref, i_ref, g_ref, o_ref, st_ref, *, steps):
    t = pl.program_id(2)
    hd = HG_DIM

    @pl.when(t == 0)
    def _():
        st_ref[...] = jnp.zeros(st_ref.shape, F32)

    lane = lax.broadcasted_iota(jnp.int32, (hd, hd), 1)
    sl = slice(0, hd)

    def block(bi, carry):
        rows = pl.ds(pl.multiple_of(bi * hd, hd), hd)
        q = q_ref[0, rows, :]
        f = _hg_forget(lb_ref, f_ref, rows, sl)
        key = 1.0 - f
        vals_t = i_ref[0, rows, :].T
        st = st_ref[...]
        out_t = jnp.zeros((hd, hd), F32)
        for s in range(hd):
            st = f[s:s + 1] * st + vals_t[:, s:s + 1] * key[s:s + 1]
            out_t = jnp.where(lane == s, jnp.sum(st * q[s:s + 1], axis=1, keepdims=True), out_t)
        st_ref[...] = st
        o_ref[0, rows, :] = _hg_finish(out_t.T, gout_ref, g_ref[0, rows, :]).astype(o_ref.dtype)
        return carry

    lax.fori_loop(0, steps // hd, block, 0)


def _hgrn2(hg, lower, g_out, n_heads, heads, kernel_fn, name):
    b, s, _ = hg.shape
    steps = min(512, s)
    nhb = n_heads // heads
    w = heads * HG_DIM
    block_bytes = 4 * _nbytes((steps, w), F32) + _nbytes((steps, w), BF16)

    def spec(part):
        return pl.BlockSpec((1, steps, w), lambda bi, h, t, p=part: (bi, t, p * nhb + h))

    return pl.pallas_call(
        functools.partial(kernel_fn, steps=steps),
        out_shape=jax.ShapeDtypeStruct((b, s, n_heads * HG_DIM), BF16),
        grid=(b, nhb, s // steps),
        in_specs=[pl.BlockSpec((1, w), lambda bi, h, t: (0, h)),
                  pl.BlockSpec((1, HG_DIM), lambda bi, h, t: (0, 0)),
                  spec(0), spec(1), spec(2), spec(3)],
        out_specs=pl.BlockSpec((1, steps, w), lambda bi, h, t: (bi, t, h)),
        scratch_shapes=[pltpu.VMEM((heads, HG_DIM, HG_DIM) if heads > 1 else (HG_DIM, HG_DIM), F32)],
        compiler_params=_params(("parallel", "parallel", "arbitrary"), block_bytes),
        name=name,
    )(lower, g_out, hg, hg, hg, hg)


def _hgrn2_any(hg, lower, g_out, n_heads):
    b, s, c4 = hg.shape
    hg_w = c4 // 4
    heads = min(4, n_heads)
    decay = _hg_max_decay(hg.reshape(b * s, c4), lower, hg_w, min(512, s), "hgrn2_decay")
    return lax.cond(
        decay < HG_FAST_MAX_DECAY,
        lambda: _hgrn2(hg, lower, g_out, n_heads, heads, functools.partial(_hg_kernel, heads=heads), "hgrn2"),
        lambda: _hgrn2(hg, lower, g_out, n_heads, 1, _hg_seq_kernel, "hgrn2_stepwise"))


def _xattn_kernel(q_ref, k_ref, v_ref, o_ref, *, n_heads):
    hd = X_HEAD_DIM
    for h in range(n_heads):
        sl = slice(h * hd, (h + 1) * hd)
        s = lax.dot_general(q_ref[0, :, sl], k_ref[0, :, sl], (((1,), (1,)), ((), ())),
                            preferred_element_type=F32)
        p = jnp.exp(s - jnp.max(s, axis=-1, keepdims=True))
        denom = jnp.sum(p, axis=-1, keepdims=True)
        o = jnp.dot(p.astype(BF16), v_ref[0, :, sl], preferred_element_type=F32)
        o_ref[0, :, sl] = (o / denom).astype(o_ref.dtype)


def _cross_attention(q, k, v, n_heads, name):
    b, s, w = q.shape
    mlen = k.shape[1]
    tq = min(1024, s)
    block_bytes = 2 * _nbytes((tq, w), BF16) + 2 * _nbytes((mlen, w), BF16) + 3 * _nbytes((tq, mlen), F32)
    return pl.pallas_call(
        functools.partial(_xattn_kernel, n_heads=n_heads),
        out_shape=jax.ShapeDtypeStruct((b, s, w), BF16),
        grid=(b, s // tq),
        in_specs=[pl.BlockSpec((1, tq, w), lambda bi, i: (bi, i, 0)),
                  pl.BlockSpec((1, mlen, w), lambda bi, i: (bi, 0, 0)),
                  pl.BlockSpec((1, mlen, w), lambda bi, i: (bi, 0, 0))],
        out_specs=pl.BlockSpec((1, tq, w), lambda bi, i: (bi, i, 0)),
        compiler_params=_params(("parallel", "parallel"), block_bytes),
        name=name,
    )(q, k, v)


def _pack_bf16_pair(lo, hi):
    lo_bits = lax.bitcast_convert_type(lo.astype(BF16).astype(F32), jnp.uint32)
    hi_bits = lax.bitcast_convert_type(hi.astype(BF16).astype(F32), jnp.uint32)
    return (lo_bits >> 16) | (hi_bits & jnp.uint32(0xFFFF0000))


def _unpack_bf16_pair(words):
    lo = lax.bitcast_convert_type(words << 16, F32)
    hi = lax.bitcast_convert_type(words & jnp.uint32(0xFFFF0000), F32)
    return lo, hi


def _router_kernel(h_ref, g_ref, whi_ref, wlo_ref, b_ref, n_ref, lg_ref):
    x = h_ref[...]
    ms = jnp.mean(x * x, axis=-1, keepdims=True)
    n = x * lax.rsqrt(ms + NORM_EPS) * g_ref[...]
    n_hi = n.astype(BF16)
    n_lo = (n - n_hi.astype(F32)).astype(BF16)
    half = n.shape[1] // 2
    n_ref[...] = _pack_bf16_pair(n[:, :half], n[:, half:])
    lg_ref[...] = (jnp.dot(n_hi, whi_ref[...], preferred_element_type=F32)
                   + jnp.dot(n_lo, whi_ref[...], preferred_element_type=F32)
                   + jnp.dot(n_hi, wlo_ref[...], preferred_element_type=F32) + b_ref[...])


def _router(h, g, w_hi, w_lo, bias, name):
    m, d = h.shape
    nl = w_hi.shape[1]
    tm = min(256, m)
    block_bytes = 2 * _nbytes((tm, d), F32) + 2 * _nbytes((d, nl), BF16)
    return pl.pallas_call(
        _router_kernel,
        out_shape=(jax.ShapeDtypeStruct((m, d // 2), jnp.uint32), jax.ShapeDtypeStruct((m, nl), F32)),
        grid=(m // tm,),
        in_specs=[pl.BlockSpec((tm, d), lambda i: (i, 0)),
                  pl.BlockSpec((1, d), lambda i: (0, 0)),
                  pl.BlockSpec((d, nl), lambda i: (0, 0)),
                  pl.BlockSpec((d, nl), lambda i: (0, 0)),
                  pl.BlockSpec((1, nl), lambda i: (0, 0))],
        out_specs=(pl.BlockSpec((tm, d // 2), lambda i: (i, 0)), pl.BlockSpec((tm, nl), lambda i: (i, 0))),
        compiler_params=_params(("parallel",), block_bytes),
        name=name,
    )(h, g, w_hi, w_lo, bias)


def _row_copies(src_ref, dst_ref, idx_ref, idx0, idx_stride, dst0, n, sem, *, start):
    assert n % ROW_COPY_UNROLL == 0

    def body(g, carry):
        for u in range(ROW_COPY_UNROLL):
            r = g * ROW_COPY_UNROLL + u
            cp = pltpu.make_async_copy(src_ref.at[pl.ds(idx_ref[idx0 + r * idx_stride], 1)],
                                       dst_ref.at[pl.ds(dst0 + r, 1)], sem)
            if start:
                cp.start(priority=1)
            else:
                cp.wait()
        return carry

    lax.fori_loop(0, n // ROW_COPY_UNROLL, body, 0)


def _expert_kernel(tok_ref, be_ref, nreal_ref, nf_ref, wg_ref, wu_ref, wd_ref, o_ref, xbuf, sems, *, blk):
    i = pl.program_id(0)
    n_real = nreal_ref[0]

    def gather(block, start):
        slot = block % 2
        _row_copies(nf_ref, xbuf.at[slot], tok_ref, block * blk, 1, 0, blk, sems.at[slot], start=start)

    @pl.when(jnp.logical_and(i == 0, n_real > 0))
    def _():
        gather(i, True)

    @pl.when(i + 1 < n_real)
    def _():
        gather(i + 1, True)

    @pl.when(i < n_real)
    def _():
        gather(i, False)
        x_lo, x_hi = _unpack_bf16_pair(xbuf[i % 2])
        x_lo, x_hi = x_lo.astype(BF16), x_hi.astype(BF16)
        half = x_lo.shape[1]

        def up(w_ref):
            return (jnp.dot(x_lo, w_ref[0, :half, :], preferred_element_type=F32)
                    + jnp.dot(x_hi, w_ref[0, half:, :], preferred_element_type=F32))

        hg = up(wg_ref)
        hid = (hg * jax.nn.sigmoid(hg) * up(wu_ref)).astype(BF16)
        y = jnp.dot(hid, wd_ref[0], preferred_element_type=F32)
        o_ref[...] = _pack_bf16_pair(y[:, :half], y[:, half:])

    @pl.when(i >= n_real)
    def _():
        o_ref[...] = jnp.zeros(o_ref.shape, o_ref.dtype)


def _experts(nf, buf_tok, blk_expert, n_real, wg, wu, wd, blk, name):
    cap = buf_tok.shape[0]
    d = wg.shape[1]
    ff = wg.shape[2]
    half = d // 2
    block_bytes = 3 * _nbytes((d, ff), BF16) + _nbytes((blk, half), jnp.uint32) + 3 * _nbytes((blk, d), F32)
    return pl.pallas_call(
        functools.partial(_expert_kernel, blk=blk),
        out_shape=jax.ShapeDtypeStruct((cap, half), jnp.uint32),
        grid_spec=pltpu.PrefetchScalarGridSpec(
            num_scalar_prefetch=3,
            grid=(cap // blk,),
            in_specs=[pl.BlockSpec(memory_space=pl.ANY),
                      pl.BlockSpec((1, d, ff), lambda i, tk, be, nr: (be[i], 0, 0)),
                      pl.BlockSpec((1, d, ff), lambda i, tk, be, nr: (be[i], 0, 0)),
                      pl.BlockSpec((1, ff, d), lambda i, tk, be, nr: (be[i], 0, 0))],
            out_specs=pl.BlockSpec((blk, half), lambda i, tk, be, nr: (i, 0)),
            scratch_shapes=[pltpu.VMEM((2, blk, half), jnp.uint32), pltpu.SemaphoreType.DMA((2,))]),
        compiler_params=_params(("arbitrary",), block_bytes, 2 * _nbytes((blk, half), jnp.uint32)),
        name=name,
    )(buf_tok, blk_expert, n_real, nf, wg, wu, wd)


def _combine_kernel(slot_ref, h_ref, gate_ref, ys_ref, o_ref, ybuf, sems, *, tm):
    i = pl.program_id(0)

    def gather(block, start):
        s = block % 2
        for k in range(TOP_K):
            _row_copies(ys_ref, ybuf.at[s], slot_ref, block * tm * TOP_K + k, TOP_K, k * tm, tm, sems.at[s],
                        start=start)

    @pl.when(i == 0)
    def _():
        gather(i, True)

    @pl.when(i + 1 < pl.num_programs(0))
    def _():
        gather(i + 1, True)

    gather(i, False)
    half = ybuf.shape[2]
    gate = gate_ref[...]
    acc_lo = h_ref[:, :half]
    acc_hi = h_ref[:, half:]
    for k in range(TOP_K):
        y_lo, y_hi = _unpack_bf16_pair(ybuf[i % 2, pl.ds(k * tm, tm), :])
        acc_lo = acc_lo + gate[:, k:k + 1] * y_lo
        acc_hi = acc_hi + gate[:, k:k + 1] * y_hi
    o_ref[:, :half] = acc_lo
    o_ref[:, half:] = acc_hi


def _combine(h, gate, ys, slot, name):
    m, d = h.shape
    tm = min(128, m)
    block_bytes = 2 * _nbytes((tm, d), F32) + TOP_K * _nbytes((tm, d), F32)
    return pl.pallas_call(
        functools.partial(_combine_kernel, tm=tm),
        out_shape=jax.ShapeDtypeStruct((m, d), F32),
        grid_spec=pltpu.PrefetchScalarGridSpec(
            num_scalar_prefetch=1,
            grid=(m // tm,),
            in_specs=[pl.BlockSpec((tm, d), lambda i, sl: (i, 0)),
                      pl.BlockSpec((tm, TOP_K), lambda i, sl: (i, 0)),
                      pl.BlockSpec(memory_space=pl.ANY)],
            out_specs=pl.BlockSpec((tm, d), lambda i, sl: (i, 0)),
            scratch_shapes=[pltpu.VMEM((2, TOP_K * tm, d // 2), jnp.uint32), pltpu.SemaphoreType.DMA((2,))]),
        compiler_params=_params(("arbitrary",), block_bytes),
        name=name,
    )(slot, h, gate, ys)


def _route(logits, n_groups, n_experts, blk):
    t = logits.shape[0]
    epg = n_experts // n_groups
    grp_logits = logits[:, :n_groups]
    grp_prob = jax.nn.softmax(grp_logits, axis=-1)
    grp = jnp.argmax(grp_logits, axis=-1).astype(jnp.int32)
    grp_w = jnp.take_along_axis(grp_prob, grp[:, None], axis=-1)
    exp_logits = logits[:, n_groups:n_groups + n_experts].reshape(t, n_groups, epg)
    in_grp = jnp.take_along_axis(exp_logits, grp[:, None, None], axis=1)[:, 0]
    top_val, top_idx = lax.top_k(in_grp, TOP_K)
    gate = grp_w * jax.nn.softmax(top_val, axis=-1)
    expert = (grp[:, None] * epg + top_idx.astype(jnp.int32)).reshape(-1)
    n_assign = t * TOP_K
    e_sorted, order = lax.sort_key_val(expert, jnp.arange(n_assign, dtype=jnp.int32))
    experts = jnp.arange(n_experts, dtype=jnp.int32)
    starts = jnp.searchsorted(e_sorted, experts, side='left').astype(jnp.int32)
    counts = jnp.searchsorted(e_sorted, experts, side='right').astype(jnp.int32) - starts
    padded = (counts + blk - 1) // blk * blk
    pad_ends = jnp.cumsum(padded)
    pad_starts = pad_ends - padded
    dest = pad_starts[e_sorted] + jnp.arange(n_assign, dtype=jnp.int32) - starts[e_sorted]
    _, slot = lax.sort_key_val(order, dest)
    cap = n_assign + n_experts * blk
    n_blk = cap // blk
    blk_expert = jnp.minimum(jnp.searchsorted(pad_ends, jnp.arange(n_blk, dtype=jnp.int32) * blk, side='right'),
                             n_experts - 1).astype(jnp.int32)
    row = jnp.arange(cap, dtype=jnp.int32)
    row_expert = jnp.repeat(blk_expert, blk)
    within = row - pad_starts[row_expert]
    src = jnp.clip(within + starts[row_expert], 0, n_assign - 1)
    buf_tok = jnp.where(within < counts[row_expert], order[src] // TOP_K, 0).astype(jnp.int32)
    n_real = (pad_ends[-1] // blk).astype(jnp.int32).reshape(1)
    return gate, buf_tok, blk_expert, n_real, slot.astype(jnp.int32)


def _diff_lambda_init(layer):
    return 0.8 - 0.6 * math.exp(-0.3 * layer)


def _tile_cols(v, n):
    return jnp.tile(v.reshape(1, -1).astype(F32), (1, n // v.size))


def kernel(x, mem, g_mix, w_in, g_da_q, g_da_k, lam_q1, lam_k1, lam_q2, lam_k2, g_da_sub, hg_lower, g_hg_out,
           w_up_a, w_up_b, w_gate, b_gate, w_out, g_cross, g_mem, w_xq, w_xkv, g_xq, g_xk, w_xo, g_ffn, w_grp,
           b_grp, w_erouter, b_erouter, w_e_gate, w_e_up, w_e_down):
    b, s, d = x.shape
    t = b * s
    depth = g_mix.shape[0]
    da_w = w_up_a.shape[1]
    hg_w = w_up_b.shape[1]
    da_heads = da_w // (2 * DA_HEAD_DIM)
    hg_heads = hg_w // HG_DIM
    x_w = w_xq.shape[2]
    x_heads = x_w // X_HEAD_DIM
    n_groups = w_grp.shape[2]
    n_experts = w_erouter.shape[2]
    moe_blk = min(256, t)

    lower_bounds = jnp.cumsum(jax.nn.softmax(hg_lower.astype(F32), axis=0), axis=0)
    h = x.reshape(t, d)
    mem2 = mem.reshape(b * mem.shape[1], d)
    for l in range(depth):
        lam_init = _diff_lambda_init(l)
        w_in_l = w_in[l]
        n = _rmsnorm(h, g_mix[l], BF16, "rms_mix")
        q_scale = DA_HEAD_DIM ** -0.5 * math.log2(math.e)
        qk_gain = jnp.concatenate([_tile_cols(g_da_q[l], da_w) * q_scale, _tile_cols(g_da_k[l], da_w)], axis=1)
        ff = w_e_gate.shape[3]
        qk, we_gate = _matmul(n, w_in_l, col0=0, ncols=2 * da_w, out_dtype=BF16,
                              epilogue=functools.partial(_ep_group_norm, group=DA_HEAD_DIM),
                              col_extras=[(qk_gain, 0)], side_cast=w_e_gate[l].reshape(n_experts * d, ff),
                              name="proj_qk")
        v = _matmul(n, w_in_l, col0=2 * da_w, ncols=da_w, out_dtype=BF16, name="proj_v")
        hg, we_down = _matmul(n, w_in_l, col0=3 * da_w, ncols=4 * hg_w, out_dtype=F32,
                              side_cast=w_e_down[l].reshape(n_experts * ff, d), name="proj_hg")
        gates, we_up = _matmul(n, w_gate[l], out_dtype=BF16, epilogue=_ep_bias_sigmoid,
                               col_extras=[(b_gate[l].reshape(1, -1).astype(F32), 0)],
                               side_cast=w_e_up[l].reshape(n_experts * d, ff), name="proj_gate")
        lamv = jnp.stack([lam_q1[l], lam_k1[l], lam_q2[l], lam_k2[l]]).astype(F32)
        y_a = _diff_attention_any(qk.reshape(b, s, 2 * da_w), v.reshape(b, s, da_w), lamv,
                                  g_da_sub[l].reshape(1, -1).astype(F32), g_da_q[l], g_da_k[l], q_scale,
                                  da_heads, lam_init)
        y_b = _hgrn2_any(hg.reshape(b, s, 4 * hg_w), lower_bounds[l].reshape(1, hg_w),
                         g_hg_out[l].reshape(1, HG_DIM).astype(F32), hg_heads)
        merged = _merge(y_a.reshape(t, da_w), w_up_a[l], y_b.reshape(t, hg_w), w_up_b[l], gates, "merge")
        h = _matmul(merged, w_out[l], out_dtype=F32, epilogue=_ep_residual,
                    full_extras=[(h, 0)], name="mix_out")
        nc = _rmsnorm(h, g_cross[l], BF16, "rms_cross")
        xq = _matmul(nc, w_xq[l], out_dtype=BF16,
                     epilogue=functools.partial(_ep_group_norm, group=X_HEAD_DIM),
                     col_extras=[(_tile_cols(g_xq[l], x_w) * X_HEAD_DIM ** -0.5, 0)], name="xattn_q")
        nm = _rmsnorm(mem2, g_mem[l], BF16, "rms_mem")
        w_xkv_l = w_xkv[l]
        xk = _matmul(nm, w_xkv_l, col0=0, ncols=x_w, out_dtype=BF16,
                     epilogue=functools.partial(_ep_group_norm, group=X_HEAD_DIM),
                     col_extras=[(_tile_cols(g_xk[l], x_w), 0)], name="xattn_k")
        xv = _matmul(nm, w_xkv_l, col0=x_w, ncols=x_w, out_dtype=BF16, name="xattn_v")
        xo = _cross_attention(xq.reshape(b, s, x_w), xk.reshape(b, -1, x_w), xv.reshape(b, -1, x_w), x_heads,
                              "xattn")
        h = _matmul(xo.reshape(t, x_w), w_xo[l], out_dtype=F32, epilogue=_ep_residual,
                    full_extras=[(h, 0)], name="xattn_out")
        n_logit = -(-(n_groups + n_experts) // LANES) * LANES
        w_r = jnp.zeros((d, n_logit), F32).at[:, :n_groups].set(w_grp[l]).at[:, n_groups:n_groups + n_experts].set(
            w_erouter[l])
        b_r = jnp.zeros((1, n_logit), F32).at[0, :n_groups].set(b_grp[l]).at[0, n_groups:n_groups + n_experts].set(
            b_erouter[l])
        w_r_hi = w_r.astype(BF16)
        w_r_lo = (w_r - w_r_hi.astype(F32)).astype(BF16)
        nf, logits = _router(h, g_ffn[l].reshape(1, d).astype(F32), w_r_hi, w_r_lo, b_r, "router")
        gate, buf_tok, blk_expert, n_real, slot = _route(logits, n_groups, n_experts, moe_blk)
        ys = _experts(nf, buf_tok, blk_expert, n_real, we_gate.reshape(n_experts, d, ff),
                      we_up.reshape(n_experts, d, ff), we_down.reshape(n_experts, ff, d), moe_blk, "moe_experts")
        h = _combine(h, gate, ys, slot, "moe_combine")
    return h.reshape(b, s, d)
```

```python
import functools
import math

import jax
import jax.numpy as jnp
from jax import lax
from jax.experimental import pallas as pl
from jax.experimental.pallas import tpu as pltpu

F32 = jnp.float32
BF16 = jnp.bfloat16

NORM_EPS = 1e-6
DA_HEAD_DIM = 128
DA_CHUNK = 64
DA_MAX_BOUNDED_LOGIT = 50.0
HG_DIM = 128
HG_CHUNK = 64
HG_FAST_MAX_DECAY = 60.0
X_HEAD_DIM = 256
TOP_K = 2
ROW_COPY_UNROLL = 8
LANES = 128
MASK_VALUE = -1e30
V7X_VMEM_BYTES = 64 * 1024 * 1024
VMEM_CAP_BYTES = V7X_VMEM_BYTES - 6 * 1024 * 1024


def _vmem_limit(block_bytes, scratch_bytes):
    return int(min(VMEM_CAP_BYTES, max(32 * 1024 * 1024, 2 * block_bytes + scratch_bytes + 16 * 1024 * 1024)))


def _params(semantics, block_bytes, scratch_bytes=0):
    return pltpu.CompilerParams(dimension_semantics=semantics,
                                vmem_limit_bytes=_vmem_limit(block_bytes, scratch_bytes))


def _nbytes(shape, dtype):
    return math.prod(shape) * jnp.dtype(dtype).itemsize


def _rms_kernel(x_ref, g_ref, o_ref):
    x = x_ref[...].astype(F32)
    ms = jnp.mean(x * x, axis=-1, keepdims=True)
    o_ref[...] = (x * lax.rsqrt(ms + NORM_EPS) * g_ref[...]).astype(o_ref.dtype)


def _rmsnorm(x, g, out_dtype, name):
    m, d = x.shape
    tm = min(256, m)
    return pl.pallas_call(
        _rms_kernel,
        out_shape=jax.ShapeDtypeStruct((m, d), out_dtype),
        grid=(m // tm,),
        in_specs=[pl.BlockSpec((tm, d), lambda i: (i, 0)), pl.BlockSpec((1, d), lambda i: (0, 0))],
        out_specs=pl.BlockSpec((tm, d), lambda i: (i, 0)),
        compiler_params=_params(("parallel",), _nbytes((tm, d), x.dtype) + _nbytes((tm, d), out_dtype)),
        name=name,
    )(x, g.reshape(1, d).astype(F32))


def _ep_plain(acc, o_ref):
    o_ref[...] = acc.astype(o_ref.dtype)


def _ep_group_norm(acc, o_ref, gain, *, group):
    for c in range(acc.shape[1] // group):
        sl = slice(c * group, (c + 1) * group)
        xg = acc[:, sl]
        ms = jnp.mean(xg * xg, axis=-1, keepdims=True)
        o_ref[:, sl] = (xg * lax.rsqrt(ms + NORM_EPS) * gain[:, sl]).astype(o_ref.dtype)


def _ep_bias_sigmoid(acc, o_ref, bias):
    o_ref[...] = jax.nn.sigmoid(acc + bias).astype(o_ref.dtype)


def _ep_residual(acc, o_ref, res):
    o_ref[...] = (res + acc).astype(o_ref.dtype)


def _row_block(i):
    return jnp.maximum(i - 1, 0)


def _mm_kernel(a_ref, w_ref, *rest, epilogue, side):
    if side:
        *extra, side_src_ref, o_ref, side_out_ref, wb_ref = rest
    else:
        *extra, o_ref, wb_ref = rest
    i = pl.program_id(1)

    @pl.when(i == 0)
    def _():
        wb_ref[...] = w_ref[...].astype(BF16)

    @pl.when(i > 0)
    def _():
        acc = jnp.dot(a_ref[...], wb_ref[...], preferred_element_type=F32)
        epilogue(acc, o_ref, *[e[...] for e in extra])
        if side:
            side_out_ref[...] = side_src_ref[...].astype(BF16)


def _matmul(a, w, *, col0=0, ncols=None, out_dtype, epilogue=_ep_plain, col_extras=(), full_extras=(),
            side_cast=None, name):
    m, k = a.shape
    ncols = w.shape[1] - col0 if ncols is None else ncols
    tm = min(1024, m)
    tn = min(512, ncols)
    assert m % tm == 0 and ncols % tn == 0 and col0 % tn == 0
    n_rb = m // tm
    in_specs = [pl.BlockSpec((tm, k), lambda j, i: (_row_block(i), 0)),
                pl.BlockSpec((k, tn), lambda j, i, o=col0 // tn: (0, j + o))]
    operands = [a, w]
    block_bytes = _nbytes((tm, k), a.dtype) + _nbytes((k, tn), w.dtype) + _nbytes((tm, tn), out_dtype)
    for vec, c0 in col_extras:
        assert c0 % tn == 0
        in_specs.append(pl.BlockSpec((1, tn), lambda j, i, o=c0 // tn: (0, j + o)))
        operands.append(vec)
    for arr, c0 in full_extras:
        assert c0 % tn == 0
        in_specs.append(pl.BlockSpec((tm, tn), lambda j, i, o=c0 // tn: (_row_block(i), j + o)))
        operands.append(arr)
        block_bytes += _nbytes((tm, tn), arr.dtype)
    out_shape = jax.ShapeDtypeStruct((m, ncols), out_dtype)
    out_specs = pl.BlockSpec((tm, tn), lambda j, i: (_row_block(i), j))
    if side_cast is not None:
        rows, cols = side_cast.shape
        n_steps = (ncols // tn) * n_rb
        chunk = rows // n_steps
        assert chunk * n_steps == rows and chunk % 16 == 0, (rows, n_steps)
        side_spec = pl.BlockSpec((chunk, cols), lambda j, i: (j * n_rb + _row_block(i), 0))
        in_specs.append(side_spec)
        operands.append(side_cast)
        out_shape = (out_shape, jax.ShapeDtypeStruct((rows, cols), BF16))
        out_specs = (out_specs, side_spec)
        block_bytes += _nbytes((chunk, cols), F32) + _nbytes((chunk, cols), BF16)
    return pl.pallas_call(
        functools.partial(_mm_kernel, epilogue=epilogue, side=side_cast is not None),
        out_shape=out_shape,
        grid=(ncols // tn, n_rb + 1),
        in_specs=in_specs,
        out_specs=out_specs,
        scratch_shapes=[pltpu.VMEM((k, tn), BF16)],
        compiler_params=_params(("parallel", "arbitrary"), block_bytes, _nbytes((k, tn), BF16)),
        name=name,
    )(*operands)


def _merge_kernel(ya_ref, wa_ref, yb_ref, wb_ref, ga_ref, gb_ref, o_ref, wa_bf_ref, wb_bf_ref):
    i = pl.program_id(1)

    @pl.when(i == 0)
    def _():
        wa_bf_ref[...] = wa_ref[...].astype(BF16)
        wb_bf_ref[...] = wb_ref[...].astype(BF16)

    @pl.when(i > 0)
    def _():
        pa = jnp.dot(ya_ref[...], wa_bf_ref[...], preferred_element_type=F32)
        pb = jnp.dot(yb_ref[...], wb_bf_ref[...], preferred_element_type=F32)
        o_ref[...] = (ga_ref[...].astype(F32) * pa + gb_ref[...].astype(F32) * pb).astype(o_ref.dtype)


def _merge(ya, wa, yb, wb, gates, name):
    m, ka = ya.shape
    kb = yb.shape[1]
    d = wa.shape[1]
    tm = min(512, m)
    tn = min(512, d)
    nb = d // tn
    block_bytes = (_nbytes((tm, ka), BF16) + _nbytes((ka, tn), F32) + _nbytes((tm, kb), BF16)
                   + _nbytes((kb, tn), F32) + 3 * _nbytes((tm, tn), BF16))
    return pl.pallas_call(
        _merge_kernel,
        out_shape=jax.ShapeDtypeStruct((m, d), BF16),
        grid=(nb, m // tm + 1),
        in_specs=[pl.BlockSpec((tm, ka), lambda j, i: (_row_block(i), 0)),
                  pl.BlockSpec((ka, tn), lambda j, i: (0, j)),
                  pl.BlockSpec((tm, kb), lambda j, i: (_row_block(i), 0)),
                  pl.BlockSpec((kb, tn), lambda j, i: (0, j)),
                  pl.BlockSpec((tm, tn), lambda j, i: (_row_block(i), j)),
                  pl.BlockSpec((tm, tn), lambda j, i: (_row_block(i), j + nb))],
        out_specs=pl.BlockSpec((tm, tn), lambda j, i: (_row_block(i), j)),
        scratch_shapes=[pltpu.VMEM((ka, tn), BF16), pltpu.VMEM((kb, tn), BF16)],
        compiler_params=_params(("parallel", "arbitrary"), block_bytes, _nbytes((ka + kb, tn), BF16)),
        name=name,
    )(ya, wa, yb, wb, gates, gates)


def _da_kernel(lamv_ref, gsub_ref, q_ref, k_ref, v_ref, o_ref, acc_ref, l_ref, *m_ref, tq, lam_init, bounded):
    qi = pl.program_id(2)
    hd = DA_HEAD_DIM
    m_ref = None if bounded else m_ref[0]
    if not bounded:
        m_ref[...] = jnp.full(m_ref.shape, MASK_VALUE, F32)
    l_ref[...] = jnp.zeros(l_ref.shape, F32)
    acc_ref[...] = jnp.zeros(acc_ref.shape, F32)
    q = q_ref[0]
    def visible(width):
        row_chunk = lax.broadcasted_iota(jnp.int32, (tq, width), 0) // DA_CHUNK
        col_chunk = lax.broadcasted_iota(jnp.int32, (tq, width), 1) // DA_CHUNK - (width - tq) // DA_CHUNK
        return col_chunk <= row_chunk

    def widen(x, width):
        return jnp.concatenate([x] * (width // LANES), axis=-1)

    def update(off, width, masked):
        kk = k_ref[0, pl.ds(off, width), :]
        vv = v_ref[0, pl.ds(off, width), :]
        for c in range(2):
            s = lax.dot_general(q[:, c * hd:(c + 1) * hd], kk[:, c * hd:(c + 1) * hd],
                                (((1,), (1,)), ((), ())), preferred_element_type=F32)
            if masked:
                s = jnp.where(visible(width), s, MASK_VALUE)
            if bounded:
                p = jnp.exp2(s)
            else:
                m_old = m_ref[c]
                m_new = jnp.maximum(m_old, jnp.max(s, axis=-1, keepdims=True))
                p = jnp.exp2(s - widen(m_new, width))
                alpha = jnp.exp2(m_old - m_new)
                m_ref[c] = m_new
            psum = p[:, :LANES]
            for g in range(1, width // LANES):
                psum = psum + p[:, g * LANES:(g + 1) * LANES]
            pv = jnp.dot(p.astype(BF16), vv, preferred_element_type=F32)
            if bounded:
                l_ref[c] = l_ref[c] + psum
                acc_ref[c] = acc_ref[c] + pv
            else:
                l_ref[c] = alpha * l_ref[c] + psum
                acc_ref[c] = widen(alpha, 2 * hd) * acc_ref[c] + pv

    def wide_body(j, carry):
        update(pl.multiple_of(j * 2 * tq, 2 * tq), 2 * tq, False)
        return carry

    lax.fori_loop(0, qi // 2, wide_body, 0)

    @pl.when(qi % 2 == 1)
    def _():
        update(pl.multiple_of((qi - 1) * tq, tq), tq, False)

    update(pl.multiple_of(qi * tq, tq), tq, True)

    lv = lamv_ref[...]
    lam = (jnp.exp(jnp.sum(lv[0:1] * lv[1:2], axis=-1, keepdims=True))
           - jnp.exp(jnp.sum(lv[2:3] * lv[3:4], axis=-1, keepdims=True)) + lam_init)
    l1 = jnp.sum(l_ref[0], axis=-1, keepdims=True)
    l2 = jnp.sum(l_ref[1], axis=-1, keepdims=True)
    o = acc_ref[0] / l1 - lam * (acc_ref[1] / l2)
    ms = jnp.mean(o * o, axis=-1, keepdims=True)
    o_ref[0] = ((o * lax.rsqrt(ms + NORM_EPS) * gsub_ref[...]) * (1.0 - lam_init)).astype(o_ref.dtype)


def _diff_attention(qk, v, lamv, g_sub, n_heads, lam_init, bounded, name):
    b, s, _ = v.shape
    hw = 2 * DA_HEAD_DIM
    tq = min(512, s)
    block_bytes = 2 * _nbytes((tq, hw), BF16) + 2 * _nbytes((s, hw), BF16) + 6 * _nbytes((tq, 2 * tq), F32)
    stats = [pltpu.VMEM((2, tq, LANES), F32)] * (1 if bounded else 2)
    return pl.pallas_call(
        functools.partial(_da_kernel, tq=tq, lam_init=lam_init, bounded=bounded),
        out_shape=jax.ShapeDtypeStruct((b, s, n_heads * hw), BF16),
        grid=(b, n_heads, s // tq),
        in_specs=[pl.BlockSpec((4, DA_HEAD_DIM), lambda bi, h, i: (0, 0)),
                  pl.BlockSpec((1, hw), lambda bi, h, i: (0, 0)),
                  pl.BlockSpec((1, tq, hw), lambda bi, h, i: (bi, i, h)),
                  pl.BlockSpec((1, s, hw), lambda bi, h, i: (bi, 0, n_heads + h)),
                  pl.BlockSpec((1, s, hw), lambda bi, h, i: (bi, 0, h))],
        out_specs=pl.BlockSpec((1, tq, hw), lambda bi, h, i: (bi, i, h)),
        scratch_shapes=[pltpu.VMEM((2, tq, hw), F32)] + stats,
        compiler_params=_params(("parallel", "parallel", "arbitrary"), block_bytes),
        name=name,
    )(lamv, g_sub, qk, qk, v)


def _diff_attention_any(qk, v, lamv, g_sub, g_q, g_k, q_scale, n_heads, lam_init):
    bound = 1.01 * DA_HEAD_DIM * q_scale * jnp.max(jnp.abs(g_q)) * jnp.max(jnp.abs(g_k))
    return lax.cond(
        bound < DA_MAX_BOUNDED_LOGIT,
        lambda: _diff_attention(qk, v, lamv, g_sub, n_heads, lam_init, True, "diff_attn"),
        lambda: _diff_attention(qk, v, lamv, g_sub, n_heads, lam_init, False, "diff_attn_online"))


def _hg_forget(lb_ref, f_ref, rows, sl):
    lb = lb_ref[:, sl]
    return lb + (1.0 - lb) * jax.nn.sigmoid(f_ref[0, rows, sl])


def _hg_finish(o, gout_ref, gate):
    ms = jnp.mean(o * o, axis=-1, keepdims=True)
    return (o * lax.rsqrt(ms + NORM_EPS) * gout_ref[...]) * (gate * jax.nn.sigmoid(gate))


def _hg_decay_kernel(lb_ref, f_ref, o_ref, *, steps):
    lb = lb_ref[...]
    nl = -jnp.log(lb + (1.0 - lb) * jax.nn.sigmoid(f_ref[...]))
    worst = jnp.zeros((1, nl.shape[1]), F32)
    for c in range(steps // HG_CHUNK):
        worst = jnp.maximum(worst, jnp.sum(nl[c * HG_CHUNK:(c + 1) * HG_CHUNK], axis=0, keepdims=True))
    o_ref[...] = jnp.broadcast_to(worst, o_ref.shape)


def _hg_max_decay(hg2, lower, hg_w, steps, name):
    t = hg2.shape[0]
    w = min(1024, hg_w)
    ncb = hg_w // w
    sub = 8
    out = pl.pallas_call(
        functools.partial(_hg_decay_kernel, steps=steps),
        out_shape=jax.ShapeDtypeStruct((t // steps * sub, hg_w), F32),
        grid=(t // steps, ncb),
        in_specs=[pl.BlockSpec((1, w), lambda i, j: (0, j)),
                  pl.BlockSpec((steps, w), lambda i, j: (i, ncb + j))],
        out_specs=pl.BlockSpec((sub, w), lambda i, j: (i, j)),
        compiler_params=_params(("parallel", "parallel"), 4 * _nbytes((steps, w), F32)),
        name=name,
    )(lower, hg2)
    return jnp.max(out)


def _hg_kernel(lb_ref, gout_ref, q_ref, f_ref, i_ref, g_ref, o_ref, st_ref, *, heads, steps):
    t = pl.program_id(2)
    ch, hd = HG_CHUNK, HG_DIM

    @pl.when(t == 0)
    def _():
        st_ref[...] = jnp.zeros(st_ref.shape, F32)

    r = lax.broadcasted_iota(jnp.int32, (ch, ch), 0)
    c = lax.broadcasted_iota(jnp.int32, (ch, ch), 1)
    causal = c <= r
    row = lax.broadcasted_iota(jnp.int32, (ch, hd), 0)
    nt = (((1,), (1,)), ((), ()))
    tn = (((0,), (0,)), ((), ()))

    def chunk(rows, sl, st):
        q = q_ref[0, rows, sl]
        f = _hg_forget(lb_ref, f_ref, rows, sl)
        key = 1.0 - f
        lf = jnp.log(f)
        cum = lf
        for sh in (1, 2, 4, 8, 16, 32):
            cum = cum + jnp.where(row >= sh, pltpu.roll(cum, sh, axis=0), 0.0)
        last = cum[ch - 1:ch]
        q_dec = (q * jnp.exp(cum)).astype(BF16)
        k_inv = (key * jnp.exp(-cum)).astype(BF16)
        k_state = (key * jnp.exp(last - cum)).astype(BF16)
        vals = i_ref[0, rows, sl].astype(BF16)
        scores = jnp.where(causal, lax.dot_general(q_dec, k_inv, nt, preferred_element_type=F32), 0.0)
        o = (jnp.dot(scores.astype(BF16), vals, preferred_element_type=F32)
             + lax.dot_general(q_dec, st.astype(BF16), nt, preferred_element_type=F32))
        st_next = jnp.exp(last) * st + lax.dot_general(vals, k_state, tn, preferred_element_type=F32)
        return _hg_finish(o, gout_ref, g_ref[0, rows, sl]), st_next

    for h in range(heads):
        sl = slice(h * hd, (h + 1) * hd)
        st = st_ref[h]
        for ci in range(steps // ch):
            rows = slice(ci * ch, (ci + 1) * ch)
            y, st = chunk(rows, sl, st)
            o_ref[0, rows, sl] = y.astype(o_ref.dtype)
        st_ref[h] = st


def _hg_seq_kernel(lb_ref, gout_ref, q_ref, f_ref, i_ref, g_ref, o_ref, st_ref, *, steps):
    t = pl.program_id(2)
    hd = HG_DIM

    @pl.when(t == 0)
    def _():
        st_ref[...] = jnp.zeros(st_ref.shape, F32)

    lane = lax.broadcasted_iota(jnp.int32, (hd, hd), 1)
    sl = slice(0, hd)

    def block(bi, carry):
        rows = pl.ds(pl.multiple_of(bi * hd, hd), hd)
        q = q_ref[0, rows, :]
        f = _hg_forget(lb_ref, f_ref, rows, sl)
        key = 1.0 - f
        vals_t = i_ref[0, rows, :].T
        st = st_ref[...]
        out_t = jnp.zeros((hd, hd), F32)
        for s in range(hd):
            st = f[s:s + 1] * st + vals_t[:, s:s + 1] * key[s:s + 1]
            out_t = jnp.where(lane == s, jnp.sum(st * q[s:s + 1], axis=1, keepdims=True), out_t)
        st_ref[...] = st
        o_ref[0, rows, :] = _hg_finish(out_t.T, gout_ref, g_ref[0, rows, :]).astype(o_ref.dtype)
        return carry

    lax.fori_loop(0, steps // hd, block, 0)


def _hgrn2(hg, lower, g_out, n_heads, heads, kernel_fn, name):
    b, s, _ = hg.shape
    steps = min(512, s)
    nhb = n_heads // heads
    w = heads * HG_DIM
    block_bytes = 4 * _nbytes((steps, w), F32) + _nbytes((steps, w), BF16)

    def spec(part):
        return pl.BlockSpec((1, steps, w), lambda bi, h, t, p=part: (bi, t, p * nhb + h))

    return pl.pallas_call(
        functools.partial(kernel_fn, steps=steps),
        out_shape=jax.ShapeDtypeStruct((b, s, n_heads * HG_DIM), BF16),
        grid=(b, nhb, s // steps),
        in_specs=[pl.BlockSpec((1, w), lambda bi, h, t: (0, h)),
                  pl.BlockSpec((1, HG_DIM), lambda bi, h, t: (0, 0)),
                  spec(0), spec(1), spec(2), spec(3)],
        out_specs=pl.BlockSpec((1, steps, w), lambda bi, h, t: (bi, t, h)),
        scratch_shapes=[pltpu.VMEM((heads, HG_DIM, HG_DIM) if heads > 1 else (HG_DIM, HG_DIM), F32)],
        compiler_params=_params(("parallel", "parallel", "arbitrary"), block_bytes),
        name=name,
    )(lower, g_out, hg, hg, hg, hg)


def _hgrn2_any(hg, lower, g_out, n_heads):
    b, s, c4 = hg.shape
    hg_w = c4 // 4
    heads = min(4, n_heads)
    decay = _hg_max_decay(hg.reshape(b * s, c4), lower, hg_w, min(512, s), "hgrn2_decay")
    return lax.cond(
        decay < HG_FAST_MAX_DECAY,
        lambda: _hgrn2(hg, lower, g_out, n_heads, heads, functools.partial(_hg_kernel, heads=heads), "hgrn2"),
        lambda: _hgrn2(hg, lower, g_out, n_heads, 1, _hg_seq_kernel, "hgrn2_stepwise"))


def _xattn_kernel(q_ref, k_ref, v_ref, o_ref, *, n_heads):
    hd = X_HEAD_DIM
    for h in range(n_heads):
        sl = slice(h * hd, (h + 1) * hd)
        s = lax.dot_general(q_ref[0, :, sl], k_ref[0, :, sl], (((1,), (1,)), ((), ())),
                            preferred_element_type=F32)
        p = jnp.exp(s - jnp.max(s, axis=-1, keepdims=True))
        denom = jnp.sum(p, axis=-1, keepdims=True)
        o = jnp.dot(p.astype(BF16), v_ref[0, :, sl], preferred_element_type=F32)
        o_ref[0, :, sl] = (o / denom).astype(o_ref.dtype)


def _cross_attention(q, k, v, n_heads, name):
    b, s, w = q.shape
    mlen = k.shape[1]
    tq = min(1024, s)
    block_bytes = 2 * _nbytes((tq, w), BF16) + 2 * _nbytes((mlen, w), BF16) + 3 * _nbytes((tq, mlen), F32)
    return pl.pallas_call(
        functools.partial(_xattn_kernel, n_heads=n_heads),
        out_shape=jax.ShapeDtypeStruct((b, s, w), BF16),
        grid=(b, s // tq),
        in_specs=[pl.BlockSpec((1, tq, w), lambda bi, i: (bi, i, 0)),
                  pl.BlockSpec((1, mlen, w), lambda bi, i: (bi, 0, 0)),
                  pl.BlockSpec((1, mlen, w), lambda bi, i: (bi, 0, 0))],
        out_specs=pl.BlockSpec((1, tq, w), lambda bi, i: (bi, i, 0)),
        compiler_params=_params(("parallel", "parallel"), block_bytes),
        name=name,
    )(q, k, v)


def _pack_bf16_pair(lo, hi):
    lo_bits = lax.bitcast_convert_type(lo.astype(BF16).astype(F32), jnp.uint32)
    hi_bits = lax.bitcast_convert_type(hi.astype(BF16).astype(F32), jnp.uint32)
    return (lo_bits >> 16) | (hi_bits & jnp.uint32(0xFFFF0000))


def _unpack_bf16_pair(words):
    lo = lax.bitcast_convert_type(words << 16, F32)
    hi = lax.bitcast_convert_type(words & jnp.uint32(0xFFFF0000), F32)
    return lo, hi


def _router_kernel(h_ref, g_ref, whi_ref, wlo_ref, b_ref, n_ref, rt_ref, *, n_groups, n_experts):
    x = h_ref[...]
    ms = jnp.mean(x * x, axis=-1, keepdims=True)
    n = x * lax.rsqrt(ms + NORM_EPS) * g_ref[...]
    n_hi = n.astype(BF16)
    n_lo = (n - n_hi.astype(F32)).astype(BF16)
    half = n.shape[1] // 2
    n_ref[...] = _pack_bf16_pair(n[:, :half], n[:, half:])
    logits = (jnp.dot(n_hi, whi_ref[...], preferred_element_type=F32)
              + jnp.dot(n_lo, whi_ref[...], preferred_element_type=F32)
              + jnp.dot(n_hi, wlo_ref[...], preferred_element_type=F32) + b_ref[...])
    epg = n_experts // n_groups
    lane = lax.broadcasted_iota(jnp.int32, logits.shape, 1)
    n_lanes = logits.shape[1]

    def top1(vals):
        best = jnp.max(vals, axis=-1, keepdims=True)
        return best, jnp.min(jnp.where(vals == best, lane, n_lanes), axis=-1, keepdims=True)

    is_grp = lane < n_groups
    g_max, grp = top1(jnp.where(is_grp, logits, -jnp.inf))
    grp_w = 1.0 / jnp.sum(jnp.where(is_grp, jnp.exp(logits - g_max), 0.0), axis=-1, keepdims=True)
    first = n_groups + grp * epg
    in_grp = jnp.where(jnp.logical_and(lane >= first, lane < first + epg), logits, -jnp.inf)
    v1, i1 = top1(in_grp)
    v2, i2 = top1(jnp.where(lane == i1, -jnp.inf, in_grp))
    e2 = jnp.exp(v2 - v1)
    p1 = 1.0 / (1.0 + e2)
    cols = [grp_w * p1, grp_w * (e2 * p1), (i1 - n_groups).astype(F32), (i2 - n_groups).astype(F32)]
    rt = jnp.zeros(logits.shape, F32)
    for c, val in enumerate(cols):
        rt = jnp.where(lane == c, val, rt)
    rt_ref[...] = rt


def _router(h, g, w_hi, w_lo, bias, n_groups, n_experts, name):
    m, d = h.shape
    nl = w_hi.shape[1]
    tm = min(256, m)
    block_bytes = 2 * _nbytes((tm, d), F32) + 2 * _nbytes((d, nl), BF16)
    return pl.pallas_call(
        functools.partial(_router_kernel, n_groups=n_groups, n_experts=n_experts),
        out_shape=(jax.ShapeDtypeStruct((m, d // 2), jnp.uint32), jax.ShapeDtypeStruct((m, nl), F32)),
        grid=(m // tm,),
        in_specs=[pl.BlockSpec((tm, d), lambda i: (i, 0)),
                  pl.BlockSpec((1, d), lambda i: (0, 0)),
                  pl.BlockSpec((d, nl), lambda i: (0, 0)),
                  pl.BlockSpec((d, nl), lambda i: (0, 0)),
                  pl.BlockSpec((1, nl), lambda i: (0, 0))],
        out_specs=(pl.BlockSpec((tm, d // 2), lambda i: (i, 0)), pl.BlockSpec((tm, nl), lambda i: (i, 0))),
        compiler_params=_params(("parallel",), block_bytes),
        name=name,
    )(h, g, w_hi, w_lo, bias)


def _row_copies(src_ref, dst_ref, idx_ref, idx0, idx_stride, dst0, n_groups, sem, *, start):
    def body(g, carry):
        for u in range(ROW_COPY_UNROLL):
            r = g * ROW_COPY_UNROLL + u
            cp = pltpu.make_async_copy(src_ref.at[pl.ds(idx_ref[idx0 + r * idx_stride], 1)],
                                       dst_ref.at[pl.ds(dst0 + r, 1)], sem)
            if start:
                cp.start(priority=1)
            else:
                cp.wait()
        return carry

    lax.fori_loop(0, n_groups, body, 0)


def _expert_kernel(tok_ref, be_ref, ng_ref, nreal_ref, nf_ref, wg_ref, wu_ref, wd_ref, o_ref, xbuf, sems, *, blk):
    i = pl.program_id(0)
    n_real = nreal_ref[0]

    def gather(block, start):
        slot = block % 2
        _row_copies(nf_ref, xbuf.at[slot], tok_ref, block * blk, 1, 0, ng_ref[block], sems.at[slot], start=start)

    @pl.when(i == 0)
    def _():
        xbuf[...] = jnp.zeros(xbuf.shape, xbuf.dtype)

    @pl.when(jnp.logical_and(i == 0, n_real > 0))
    def _():
        gather(i, True)

    @pl.when(i + 1 < n_real)
    def _():
        gather(i + 1, True)

    @pl.when(i < n_real)
    def _():
        gather(i, False)
        x_lo, x_hi = _unpack_bf16_pair(xbuf[i % 2])
        x_lo, x_hi = x_lo.astype(BF16), x_hi.astype(BF16)
        half = x_lo.shape[1]

        def up(w_ref):
            return (jnp.dot(x_lo, w_ref[0, :half, :], preferred_element_type=F32)
                    + jnp.dot(x_hi, w_ref[0, half:, :], preferred_element_type=F32))

        hg = up(wg_ref)
        hid = (hg * jax.nn.sigmoid(hg) * up(wu_ref)).astype(BF16)
        y = jnp.dot(hid, wd_ref[0], preferred_element_type=F32)
        o_ref[...] = _pack_bf16_pair(y[:, :half], y[:, half:])

    @pl.when(i >= n_real)
    def _():
        o_ref[...] = jnp.zeros(o_ref.shape, o_ref.dtype)


def _experts(nf, buf_tok, blk_expert, blk_groups, n_real, wg, wu, wd, blk, name):
    cap = buf_tok.shape[0]
    d = wg.shape[1]
    ff = wg.shape[2]
    half = d // 2
    block_bytes = 3 * _nbytes((d, ff), BF16) + _nbytes((blk, half), jnp.uint32) + 3 * _nbytes((blk, d), F32)
    return pl.pallas_call(
        functools.partial(_expert_kernel, blk=blk),
        out_shape=jax.ShapeDtypeStruct((cap, half), jnp.uint32),
        grid_spec=pltpu.PrefetchScalarGridSpec(
            num_scalar_prefetch=4,
            grid=(cap // blk,),
            in_specs=[pl.BlockSpec(memory_space=pl.ANY),
                      pl.BlockSpec((1, d, ff), lambda i, tk, be, ng, nr: (be[i], 0, 0)),
                      pl.BlockSpec((1, d, ff), lambda i, tk, be, ng, nr: (be[i], 0, 0)),
                      pl.BlockSpec((1, ff, d), lambda i, tk, be, ng, nr: (be[i], 0, 0))],
            out_specs=pl.BlockSpec((blk, half), lambda i, tk, be, ng, nr: (i, 0)),
            scratch_shapes=[pltpu.VMEM((2, blk, half), jnp.uint32), pltpu.SemaphoreType.DMA((2,))]),
        compiler_params=_params(("arbitrary",), block_bytes, 2 * _nbytes((blk, half), jnp.uint32)),
        name=name,
    )(buf_tok, blk_expert, blk_groups, n_real, nf, wg, wu, wd)


def _combine_kernel(slot_ref, h_ref, gate_ref, ys_ref, o_ref, ybuf, sems, *, tm):
    i = pl.program_id(0)

    def gather(block, start):
        s = block % 2
        for k in range(TOP_K):
            _row_copies(ys_ref, ybuf.at[s], slot_ref, block * tm * TOP_K + k, TOP_K, k * tm,
                        tm // ROW_COPY_UNROLL, sems.at[s], start=start)

    @pl.when(i == 0)
    def _():
        gather(i, True)

    @pl.when(i + 1 < pl.num_programs(0))
    def _():
        gather(i + 1, True)

    gather(i, False)
    half = ybuf.shape[2]
    gate = gate_ref[...]
    acc_lo = h_ref[:, :half]
    acc_hi = h_ref[:, half:]
    for k in range(TOP_K):
        y_lo, y_hi = _unpack_bf16_pair(ybuf[i % 2, pl.ds(k * tm, tm), :])
        acc_lo = acc_lo + gate[:, k:k + 1] * y_lo
        acc_hi = acc_hi + gate[:, k:k + 1] * y_hi
    o_ref[:, :half] = acc_lo
    o_ref[:, half:] = acc_hi


def _combine(h, gate, ys, slot, name):
    m, d = h.shape
    tm = min(128, m)
    block_bytes = 2 * _nbytes((tm, d), F32) + TOP_K * _nbytes((tm, d), F32)
    return pl.pallas_call(
        functools.partial(_combine_kernel, tm=tm),
        out_shape=jax.ShapeDtypeStruct((m, d), F32),
        grid_spec=pltpu.PrefetchScalarGridSpec(
            num_scalar_prefetch=1,
            grid=(m // tm,),
            in_specs=[pl.BlockSpec((tm, d), lambda i, sl: (i, 0)),
                      pl.BlockSpec((tm, TOP_K), lambda i, sl: (i, 0)),
                      pl.BlockSpec(memory_space=pl.ANY)],
            out_specs=pl.BlockSpec((tm, d), lambda i, sl: (i, 0)),
            scratch_shapes=[pltpu.VMEM((2, TOP_K * tm, d // 2), jnp.uint32), pltpu.SemaphoreType.DMA((2,))]),
        compiler_params=_params(("arbitrary",), block_bytes),
        name=name,
    )(slot, h, gate, ys)


def _layout(expert, n_experts, blk):
    n_assign = expert.shape[0]
    e_sorted, order = lax.sort_key_val(expert, jnp.arange(n_assign, dtype=jnp.int32))
    experts = jnp.arange(n_experts, dtype=jnp.int32)
    starts = jnp.searchsorted(e_sorted, experts, side='left').astype(jnp.int32)
    counts = jnp.searchsorted(e_sorted, experts, side='right').astype(jnp.int32) - starts
    padded = (counts + blk - 1) // blk * blk
    pad_ends = jnp.cumsum(padded)
    pad_starts = pad_ends - padded
    dest = pad_starts[e_sorted] + jnp.arange(n_assign, dtype=jnp.int32) - starts[e_sorted]
    _, slot = lax.sort_key_val(order, dest)
    cap = n_assign + n_experts * blk
    n_blk = cap // blk
    blk_expert = jnp.minimum(jnp.searchsorted(pad_ends, jnp.arange(n_blk, dtype=jnp.int32) * blk, side='right'),
                             n_experts - 1).astype(jnp.int32)
    row = jnp.arange(cap, dtype=jnp.int32)
    row_expert = jnp.repeat(blk_expert, blk)
    within = row - pad_starts[row_expert]
    src = jnp.clip(within + starts[row_expert], 0, n_assign - 1)
    buf_tok = jnp.where(within < counts[row_expert], order[src] // TOP_K, 0).astype(jnp.int32)
    n_real = (pad_ends[-1] // blk).astype(jnp.int32)
    blk_row0 = jnp.arange(n_blk, dtype=jnp.int32) * blk
    blk_rows = jnp.clip(counts[blk_expert] - (blk_row0 - pad_starts[blk_expert]), 0, blk)
    blk_rows = jnp.where(jnp.arange(n_blk) < n_real, blk_rows, 0)
    blk_groups = ((blk_rows + ROW_COPY_UNROLL - 1) // ROW_COPY_UNROLL).astype(jnp.int32)
    return buf_tok, blk_expert, blk_groups, n_real.reshape(1), slot.astype(jnp.int32)


def _diff_lambda_init(layer):
    return 0.8 - 0.6 * math.exp(-0.3 * layer)


def _tile_cols(v, n):
    return jnp.tile(v.reshape(1, -1).astype(F32), (1, n // v.size))


def kernel(x, mem, g_mix, w_in, g_da_q, g_da_k, lam_q1, lam_k1, lam_q2, lam_k2, g_da_sub, hg_lower, g_hg_out,
           w_up_a, w_up_b, w_gate, b_gate, w_out, g_cross, g_mem, w_xq, w_xkv, g_xq, g_xk, w_xo, g_ffn, w_grp,
           b_grp, w_erouter, b_erouter, w_e_gate, w_e_up, w_e_down):
    b, s, d = x.shape
    t = b * s
    depth = g_mix.shape[0]
    da_w = w_up_a.shape[1]
    hg_w = w_up_b.shape[1]
    da_heads = da_w // (2 * DA_HEAD_DIM)
    hg_heads = hg_w // HG_DIM
    x_w = w_xq.shape[2]
    x_heads = x_w // X_HEAD_DIM
    n_groups = w_grp.shape[2]
    n_experts = w_erouter.shape[2]
    moe_blk = min(256, t)

    lower_bounds = jnp.cumsum(jax.nn.softmax(hg_lower.astype(F32), axis=0), axis=0)
    h = x.reshape(t, d)
    mem2 = mem.reshape(b * mem.shape[1], d)
    for l in range(depth):
        lam_init = _diff_lambda_init(l)
        w_in_l = w_in[l]
        n = _rmsnorm(h, g_mix[l], BF16, "rms_mix")
        q_scale = DA_HEAD_DIM ** -0.5 * math.log2(math.e)
        qk_gain = jnp.concatenate([_tile_cols(g_da_q[l], da_w) * q_scale, _tile_cols(g_da_k[l], da_w)], axis=1)
        ff = w_e_gate.shape[3]
        qk, we_gate = _matmul(n, w_in_l, col0=0, ncols=2 * da_w, out_dtype=BF16,
                              epilogue=functools.partial(_ep_group_norm, group=DA_HEAD_DIM),
                              col_extras=[(qk_gain, 0)], side_cast=w_e_gate[l].reshape(n_experts * d, ff),
                              name="proj_qk")
        v = _matmul(n, w_in_l, col0=2 * da_w, ncols=da_w, out_dtype=BF16, name="proj_v")
        hg, we_down = _matmul(n, w_in_l, col0=3 * da_w, ncols=4 * hg_w, out_dtype=F32,
                              side_cast=w_e_down[l].reshape(n_experts * ff, d), name="proj_hg")
        gates, we_up = _matmul(n, w_gate[l], out_dtype=BF16, epilogue=_ep_bias_sigmoid,
                               col_extras=[(b_gate[l].reshape(1, -1).astype(F32), 0)],
                               side_cast=w_e_up[l].reshape(n_experts * d, ff), name="proj_gate")
        lamv = jnp.stack([lam_q1[l], lam_k1[l], lam_q2[l], lam_k2[l]]).astype(F32)
        y_a = _diff_attention_any(qk.reshape(b, s, 2 * da_w), v.reshape(b, s, da_w), lamv,
                                  g_da_sub[l].reshape(1, -1).astype(F32), g_da_q[l], g_da_k[l], q_scale,
                                  da_heads, lam_init)
        y_b = _hgrn2_any(hg.reshape(b, s, 4 * hg_w), lower_bounds[l].reshape(1, hg_w),
                         g_hg_out[l].reshape(1, HG_DIM).astype(F32), hg_heads)
        merged = _merge(y_a.reshape(t, da_w), w_up_a[l], y_b.reshape(t, hg_w), w_up_b[l], gates, "merge")
        h = _matmul(merged, w_out[l], out_dtype=F32, epilogue=_ep_residual,
                    full_extras=[(h, 0)], name="mix_out")
        nc = _rmsnorm(h, g_cross[l], BF16, "rms_cross")
        xq = _matmul(nc, w_xq[l], out_dtype=BF16,
                     epilogue=functools.partial(_ep_group_norm, group=X_HEAD_DIM),
                     col_extras=[(_tile_cols(g_xq[l], x_w) * X_HEAD_DIM ** -0.5, 0)], name="xattn_q")
        nm = _rmsnorm(mem2, g_mem[l], BF16, "rms_mem")
        w_xkv_l = w_xkv[l]
        xk = _matmul(nm, w_xkv_l, col0=0, ncols=x_w, out_dtype=BF16,
                     epilogue=functools.partial(_ep_group_norm, group=X_HEAD_DIM),
                     col_extras=[(_tile_cols(g_xk[l], x_w), 0)], name="xattn_k")
        xv = _matmul(nm, w_xkv_l, col0=x_w, ncols=x_w, out_dtype=BF16, name="xattn_v")
        xo = _cross_attention(xq.reshape(b, s, x_w), xk.reshape(b, -1, x_w), xv.reshape(b, -1, x_w), x_heads,
                              "xattn")
        h = _matmul(xo.reshape(t, x_w), w_xo[l], out_dtype=F32, epilogue=_ep_residual,
                    full_extras=[(h, 0)], name="xattn_out")
        n_logit = -(-(n_groups + n_experts) // LANES) * LANES
        w_r = jnp.zeros((d, n_logit), F32).at[:, :n_groups].set(w_grp[l]).at[:, n_groups:n_groups + n_experts].set(
            w_erouter[l])
        b_r = jnp.zeros((1, n_logit), F32).at[0, :n_groups].set(b_grp[l]).at[0, n_groups:n_groups + n_experts].set(
            b_erouter[l])
        w_r_hi = w_r.astype(BF16)
        w_r_lo = (w_r - w_r_hi.astype(F32)).astype(BF16)
        nf, route = _router(h, g_ffn[l].reshape(1, d).astype(F32), w_r_hi, w_r_lo, b_r, n_groups, n_experts,
                            "router")
        gate = route[:, :TOP_K]
        expert = route[:, TOP_K:2 * TOP_K].astype(jnp.int32).reshape(-1)
        buf_tok, blk_expert, blk_groups, n_real, slot = _layout(expert, n_experts, moe_blk)
        ys = _experts(nf, buf_tok, blk_expert, blk_groups, n_real, we_gate.reshape(n_experts, d, ff),
                      we_up.reshape(n_experts, d, ff), we_down.reshape(n_experts, ff, d), moe_blk, "moe_experts")
        h = _combine(h, gate, ys, slot, "moe_combine")
    return h.reshape(b, s, d)
```

```python
import functools
import math

import jax
import jax.numpy as jnp
from jax import lax
from jax.experimental import pallas as pl
from jax.experimental.pallas import tpu as pltpu

F32 = jnp.float32
BF16 = jnp.bfloat16

NORM_EPS = 1e-6
DA_HEAD_DIM = 128
DA_CHUNK = 64
DA_MAX_BOUNDED_LOGIT = 50.0
HG_DIM = 128
HG_CHUNK = 64
HG_FAST_MAX_DECAY = 60.0
X_HEAD_DIM = 256
TOP_K = 2
ROW_COPY_UNROLL = 8
LANES = 128
MASK_VALUE = -1e30
V7X_VMEM_BYTES = 64 * 1024 * 1024
VMEM_CAP_BYTES = V7X_VMEM_BYTES - 6 * 1024 * 1024


def _vmem_limit(block_bytes, scratch_bytes):
    return int(min(VMEM_CAP_BYTES, max(32 * 1024 * 1024, 2 * block_bytes + scratch_bytes + 16 * 1024 * 1024)))


def _params(semantics, block_bytes, scratch_bytes=0):
    return pltpu.CompilerParams(dimension_semantics=semantics,
                                vmem_limit_bytes=_vmem_limit(block_bytes, scratch_bytes))


def _nbytes(shape, dtype):
    return math.prod(shape) * jnp.dtype(dtype).itemsize


def _rms_kernel(x_ref, g_ref, o_ref):
    x = x_ref[...].astype(F32)
    ms = jnp.mean(x * x, axis=-1, keepdims=True)
    o_ref[...] = (x * lax.rsqrt(ms + NORM_EPS) * g_ref[...]).astype(o_ref.dtype)


def _rmsnorm(x, g, out_dtype, name):
    m, d = x.shape
    tm = min(256, m)
    return pl.pallas_call(
        _rms_kernel,
        out_shape=jax.ShapeDtypeStruct((m, d), out_dtype),
        grid=(m // tm,),
        in_specs=[pl.BlockSpec((tm, d), lambda i: (i, 0)), pl.BlockSpec((1, d), lambda i: (0, 0))],
        out_specs=pl.BlockSpec((tm, d), lambda i: (i, 0)),
        compiler_params=_params(("parallel",), _nbytes((tm, d), x.dtype) + _nbytes((tm, d), out_dtype)),
        name=name,
    )(x, g.reshape(1, d).astype(F32))


def _ep_plain(acc, o_ref):
    o_ref[...] = acc.astype(o_ref.dtype)


def _ep_group_norm(acc, o_ref, gain, *, group):
    for c in range(acc.shape[1] // group):
        sl = slice(c * group, (c + 1) * group)
        xg = acc[:, sl]
        ms = jnp.mean(xg * xg, axis=-1, keepdims=True)
        o_ref[:, sl] = (xg * lax.rsqrt(ms + NORM_EPS) * gain[:, sl]).astype(o_ref.dtype)


def _ep_bias_sigmoid(acc, o_ref, bias):
    o_ref[...] = jax.nn.sigmoid(acc + bias).astype(o_ref.dtype)


def _ep_residual(acc, o_ref, res):
    o_ref[...] = (res + acc).astype(o_ref.dtype)


def _row_block(i):
    return jnp.maximum(i - 1, 0)


def _mm_kernel(a_ref, w_ref, *rest, epilogue, side):
    if side:
        *extra, side_src_ref, o_ref, side_out_ref, wb_ref = rest
    else:
        *extra, o_ref, wb_ref = rest
    i = pl.program_id(1)

    @pl.when(i == 0)
    def _():
        wb_ref[...] = w_ref[...].astype(BF16)

    @pl.when(i > 0)
    def _():
        acc = jnp.dot(a_ref[...], wb_ref[...], preferred_element_type=F32)
        epilogue(acc, o_ref, *[e[...] for e in extra])
        if side:
            side_out_ref[...] = side_src_ref[...].astype(BF16)


def _matmul(a, w, *, col0=0, ncols=None, out_dtype, epilogue=_ep_plain, col_extras=(), full_extras=(),
            side_cast=None, name):
    m, k = a.shape
    ncols = w.shape[1] - col0 if ncols is None else ncols
    tm = min(1024, m)
    tn = min(512, ncols)
    assert m % tm == 0 and ncols % tn == 0 and col0 % tn == 0
    n_rb = m // tm
    in_specs = [pl.BlockSpec((tm, k), lambda j, i: (_row_block(i), 0)),
                pl.BlockSpec((k, tn), lambda j, i, o=col0 // tn: (0, j + o))]
    operands = [a, w]
    block_bytes = _nbytes((tm, k), a.dtype) + _nbytes((k, tn), w.dtype) + _nbytes((tm, tn), out_dtype)
    for vec, c0 in col_extras:
        assert c0 % tn == 0
        in_specs.append(pl.BlockSpec((1, tn), lambda j, i, o=c0 // tn: (0, j + o)))
        operands.append(vec)
    for arr, c0 in full_extras:
        assert c0 % tn == 0
        in_specs.append(pl.BlockSpec((tm, tn), lambda j, i, o=c0 // tn: (_row_block(i), j + o)))
        operands.append(arr)
        block_bytes += _nbytes((tm, tn), arr.dtype)
    out_shape = jax.ShapeDtypeStruct((m, ncols), out_dtype)
    out_specs = pl.BlockSpec((tm, tn), lambda j, i: (_row_block(i), j))
    if side_cast is not None:
        rows, cols = side_cast.shape
        n_steps = (ncols // tn) * n_rb
        chunk = rows // n_steps
        assert chunk * n_steps == rows and chunk % 16 == 0, (rows, n_steps)
        side_spec = pl.BlockSpec((chunk, cols), lambda j, i: (j * n_rb + _row_block(i), 0))
        in_specs.append(side_spec)
        operands.append(side_cast)
        out_shape = (out_shape, jax.ShapeDtypeStruct((rows, cols), BF16))
        out_specs = (out_specs, side_spec)
        block_bytes += _nbytes((chunk, cols), F32) + _nbytes((chunk, cols), BF16)
    return pl.pallas_call(
        functools.partial(_mm_kernel, epilogue=epilogue, side=side_cast is not None),
        out_shape=out_shape,
        grid=(ncols // tn, n_rb + 1),
        in_specs=in_specs,
        out_specs=out_specs,
        scratch_shapes=[pltpu.VMEM((k, tn), BF16)],
        compiler_params=_params(("parallel", "arbitrary"), block_bytes, _nbytes((k, tn), BF16)),
        name=name,
    )(*operands)


def _merge_kernel(ya_ref, wa_ref, yb_ref, wb_ref, ga_ref, gb_ref, o_ref, wa_bf_ref, wb_bf_ref):
    i = pl.program_id(1)

    @pl.when(i == 0)
    def _():
        wa_bf_ref[...] = wa_ref[...].astype(BF16)
        wb_bf_ref[...] = wb_ref[...].astype(BF16)

    @pl.when(i > 0)
    def _():
        pa = jnp.dot(ya_ref[...], wa_bf_ref[...], preferred_element_type=F32)
        pb = jnp.dot(yb_ref[...], wb_bf_ref[...], preferred_element_type=F32)
        o_ref[...] = (ga_ref[...].astype(F32) * pa + gb_ref[...].astype(F32) * pb).astype(o_ref.dtype)


def _merge(ya, wa, yb, wb, gates, name):
    m, ka = ya.shape
    kb = yb.shape[1]
    d = wa.shape[1]
    tm = min(512, m)
    tn = min(512, d)
    nb = d // tn
    block_bytes = (_nbytes((tm, ka), BF16) + _nbytes((ka, tn), F32) + _nbytes((tm, kb), BF16)
                   + _nbytes((kb, tn), F32) + 3 * _nbytes((tm, tn), BF16))
    return pl.pallas_call(
        _merge_kernel,
        out_shape=jax.ShapeDtypeStruct((m, d), BF16),
        grid=(nb, m // tm + 1),
        in_specs=[pl.BlockSpec((tm, ka), lambda j, i: (_row_block(i), 0)),
                  pl.BlockSpec((ka, tn), lambda j, i: (0, j)),
                  pl.BlockSpec((tm, kb), lambda j, i: (_row_block(i), 0)),
                  pl.BlockSpec((kb, tn), lambda j, i: (0, j)),
                  pl.BlockSpec((tm, tn), lambda j, i: (_row_block(i), j)),
                  pl.BlockSpec((tm, tn), lambda j, i: (_row_block(i), j + nb))],
        out_specs=pl.BlockSpec((tm, tn), lambda j, i: (_row_block(i), j)),
        scratch_shapes=[pltpu.VMEM((ka, tn), BF16), pltpu.VMEM((kb, tn), BF16)],
        compiler_params=_params(("parallel", "arbitrary"), block_bytes, _nbytes((ka + kb, tn), BF16)),
        name=name,
    )(ya, wa, yb, wb, gates, gates)


def _da_kernel(lamv_ref, gsub_ref, q_ref, k_ref, v_ref, o_ref, acc_ref, l_ref, *m_ref, tq, lam_init, bounded):
    qi = pl.program_id(2)
    hd = DA_HEAD_DIM
    m_ref = None if bounded else m_ref[0]
    if not bounded:
        m_ref[...] = jnp.full(m_ref.shape, MASK_VALUE, F32)
    l_ref[...] = jnp.zeros(l_ref.shape, F32)
    acc_ref[...] = jnp.zeros(acc_ref.shape, F32)
    q = q_ref[0]
    def visible(width):
        row_chunk = lax.broadcasted_iota(jnp.int32, (tq, width), 0) // DA_CHUNK
        col_chunk = lax.broadcasted_iota(jnp.int32, (tq, width), 1) // DA_CHUNK - (width - tq) // DA_CHUNK
        return col_chunk <= row_chunk

    def widen(x, width):
        return jnp.concatenate([x] * (width // LANES), axis=-1)

    def update(off, width, masked):
        kk = k_ref[0, pl.ds(off, width), :]
        vv = v_ref[0, pl.ds(off, width), :]
        for c in range(2):
            s = lax.dot_general(q[:, c * hd:(c + 1) * hd], kk[:, c * hd:(c + 1) * hd],
                                (((1,), (1,)), ((), ())), preferred_element_type=F32)
            if masked:
                s = jnp.where(visible(width), s, MASK_VALUE)
            if bounded:
                p = jnp.exp2(s)
            else:
                m_old = m_ref[c]
                m_new = jnp.maximum(m_old, jnp.max(s, axis=-1, keepdims=True))
                p = jnp.exp2(s - widen(m_new, width))
                alpha = jnp.exp2(m_old - m_new)
                m_ref[c] = m_new
            psum = p[:, :LANES]
            for g in range(1, width // LANES):
                psum = psum + p[:, g * LANES:(g + 1) * LANES]
            pv = jnp.dot(p.astype(BF16), vv, preferred_element_type=F32)
            if bounded:
                l_ref[c] = l_ref[c] + psum
                acc_ref[c] = acc_ref[c] + pv
            else:
                l_ref[c] = alpha * l_ref[c] + psum
                acc_ref[c] = widen(alpha, 2 * hd) * acc_ref[c] + pv

    def wide_body(j, carry):
        update(pl.multiple_of(j * 2 * tq, 2 * tq), 2 * tq, False)
        return carry

    lax.fori_loop(0, qi // 2, wide_body, 0)

    @pl.when(qi % 2 == 1)
    def _():
        update(pl.multiple_of((qi - 1) * tq, tq), tq, False)

    update(pl.multiple_of(qi * tq, tq), tq, True)

    lv = lamv_ref[...]
    lam = (jnp.exp(jnp.sum(lv[0:1] * lv[1:2], axis=-1, keepdims=True))
           - jnp.exp(jnp.sum(lv[2:3] * lv[3:4], axis=-1, keepdims=True)) + lam_init)
    l1 = jnp.sum(l_ref[0], axis=-1, keepdims=True)
    l2 = jnp.sum(l_ref[1], axis=-1, keepdims=True)
    o = acc_ref[0] / l1 - lam * (acc_ref[1] / l2)
    ms = jnp.mean(o * o, axis=-1, keepdims=True)
    o_ref[0] = ((o * lax.rsqrt(ms + NORM_EPS) * gsub_ref[...]) * (1.0 - lam_init)).astype(o_ref.dtype)


def _diff_attention(qk, v, lamv, g_sub, n_heads, lam_init, bounded, name):
    b, s, _ = v.shape
    hw = 2 * DA_HEAD_DIM
    tq = min(512, s)
    block_bytes = 2 * _nbytes((tq, hw), BF16) + 2 * _nbytes((s, hw), BF16) + 6 * _nbytes((tq, 2 * tq), F32)
    stats = [pltpu.VMEM((2, tq, LANES), F32)] * (1 if bounded else 2)
    return pl.pallas_call(
        functools.partial(_da_kernel, tq=tq, lam_init=lam_init, bounded=bounded),
        out_shape=jax.ShapeDtypeStruct((b, s, n_heads * hw), BF16),
        grid=(b, n_heads, s // tq),
        in_specs=[pl.BlockSpec((4, DA_HEAD_DIM), lambda bi, h, i: (0, 0)),
                  pl.BlockSpec((1, hw), lambda bi, h, i: (0, 0)),
                  pl.BlockSpec((1, tq, hw), lambda bi, h, i: (bi, i, h)),
                  pl.BlockSpec((1, s, hw), lambda bi, h, i: (bi, 0, n_heads + h)),
                  pl.BlockSpec((1, s, hw), lambda bi, h, i: (bi, 0, h))],
        out_specs=pl.BlockSpec((1, tq, hw), lambda bi, h, i: (bi, i, h)),
        scratch_shapes=[pltpu.VMEM((2, tq, hw), F32)] + stats,
        compiler_params=_params(("parallel", "parallel", "arbitrary"), block_bytes),
        name=name,
    )(lamv, g_sub, qk, qk, v)


def _diff_attention_any(qk, v, lamv, g_sub, g_q, g_k, q_scale, n_heads, lam_init):
    bound = 1.01 * DA_HEAD_DIM * q_scale * jnp.max(jnp.abs(g_q)) * jnp.max(jnp.abs(g_k))
    return lax.cond(
        bound < DA_MAX_BOUNDED_LOGIT,
        lambda: _diff_attention(qk, v, lamv, g_sub, n_heads, lam_init, True, "diff_attn"),
        lambda: _diff_attention(qk, v, lamv, g_sub, n_heads, lam_init, False, "diff_attn_online"))


def _hg_forget(lb_ref, f_ref, rows, sl):
    lb = lb_ref[:, sl]
    return lb + (1.0 - lb) * jax.nn.sigmoid(f_ref[0, rows, sl])


def _hg_finish(o, gout_ref, gate):
    ms = jnp.mean(o * o, axis=-1, keepdims=True)
    return (o * lax.rsqrt(ms + NORM_EPS) * gout_ref[...]) * (gate * jax.nn.sigmoid(gate))


def _hg_decay_kernel(lb_ref, f_ref, o_ref, *, steps):
    lb = lb_ref[...]
    nl = -jnp.log(lb + (1.0 - lb) * jax.nn.sigmoid(f_ref[...]))
    worst = jnp.zeros((1, nl.shape[1]), F32)
    for c in range(steps // HG_CHUNK):
        worst = jnp.maximum(worst, jnp.sum(nl[c * HG_CHUNK:(c + 1) * HG_CHUNK], axis=0, keepdims=True))
    o_ref[...] = jnp.broadcast_to(worst, o_ref.shape)


def _hg_max_decay(hg2, lower, hg_w, steps, name):
    t = hg2.shape[0]
    w = min(1024, hg_w)
    ncb = hg_w // w
    sub = 8
    out = pl.pallas_call(
        functools.partial(_hg_decay_kernel, steps=steps),
        out_shape=jax.ShapeDtypeStruct((t // steps * sub, hg_w), F32),
        grid=(t // steps, ncb),
        in_specs=[pl.BlockSpec((1, w), lambda i, j: (0, j)),
                  pl.BlockSpec((steps, w), lambda i, j: (i, ncb + j))],
        out_specs=pl.BlockSpec((sub, w), lambda i, j: (i, j)),
        compiler_params=_params(("parallel", "parallel"), 4 * _nbytes((steps, w), F32)),
        name=name,
    )(lower, hg2)
    return jnp.max(out)


def _hg_kernel(lb_ref, gout_ref, q_ref, f_ref, i_ref, g_ref, o_ref, st_ref, *, heads, steps):
    t = pl.program_id(2)
    ch, hd = HG_CHUNK, HG_DIM

    @pl.when(t == 0)
    def _():
        st_ref[...] = jnp.zeros(st_ref.shape, F32)

    r = lax.broadcasted_iota(jnp.int32, (ch, ch), 0)
    c = lax.broadcasted_iota(jnp.int32, (ch, ch), 1)
    causal = c <= r
    row = lax.broadcasted_iota(jnp.int32, (ch, hd), 0)
    nt = (((1,), (1,)), ((), ()))
    tn = (((0,), (0,)), ((), ()))

    def chunk(rows, sl, st):
        q = q_ref[0, rows, sl]
        f = _hg_forget(lb_ref, f_ref, rows, sl)
        key = 1.0 - f
        lf = jnp.log(f)
        cum = lf
        for sh in (1, 2, 4, 8, 16, 32):
            cum = cum + jnp.where(row >= sh, pltpu.roll(cum, sh, axis=0), 0.0)
        last = cum[ch - 1:ch]
        q_dec = (q * jnp.exp(cum)).astype(BF16)
        k_inv = (key * jnp.exp(-cum)).astype(BF16)
        k_state = (key * jnp.exp(last - cum)).astype(BF16)
        vals = i_ref[0, rows, sl].astype(BF16)
        scores = jnp.where(causal, lax.dot_general(q_dec, k_inv, nt, preferred_element_type=F32), 0.0)
        o = (jnp.dot(scores.astype(BF16), vals, preferred_element_type=F32)
             + lax.dot_general(q_dec, st.astype(BF16), nt, preferred_element_type=F32))
        st_next = jnp.exp(last) * st + lax.dot_general(vals, k_state, tn, preferred_element_type=F32)
        return _hg_finish(o, gout_ref, g_ref[0, rows, sl]), st_next

    for h in range(heads):
        sl = slice(h * hd, (h + 1) * hd)
        st = st_ref[h]
        for ci in range(steps // ch):
            rows = slice(ci * ch, (ci + 1) * ch)
            y, st = chunk(rows, sl, st)
            o_ref[0, rows, sl] = y.astype(o_ref.dtype)
        st_ref[h] = st


def _hg_seq_kernel(lb_ref, gout_ref, q_ref, f_ref, i_ref, g_ref, o_ref, st_ref, *, steps):
    t = pl.program_id(2)
    hd = HG_DIM

    @pl.when(t == 0)
    def _():
        st_ref[...] = jnp.zeros(st_ref.shape, F32)

    lane = lax.broadcasted_iota(jnp.int32, (hd, hd), 1)
    sl = slice(0, hd)

    def block(bi, carry):
        rows = pl.ds(pl.multiple_of(bi * hd, hd), hd)
        q = q_ref[0, rows, :]
        f = _hg_forget(lb_ref, f_ref, rows, sl)
        key = 1.0 - f
        vals_t = i_ref[0, rows, :].T
        st = st_ref[...]
        out_t = jnp.zeros((hd, hd), F32)
        for s in range(hd):
            st = f[s:s + 1] * st + vals_t[:, s:s + 1] * key[s:s + 1]
            out_t = jnp.where(lane == s, jnp.sum(st * q[s:s + 1], axis=1, keepdims=True), out_t)
        st_ref[...] = st
        o_ref[0, rows, :] = _hg_finish(out_t.T, gout_ref, g_ref[0, rows, :]).astype(o_ref.dtype)
        return carry

    lax.fori_loop(0, steps // hd, block, 0)


def _hgrn2(hg, lower, g_out, n_heads, heads, kernel_fn, name):
    b, s, _ = hg.shape
    steps = min(512, s)
    nhb = n_heads // heads
    w = heads * HG_DIM
    block_bytes = 4 * _nbytes((steps, w), F32) + _nbytes((steps, w), BF16)

    def spec(part):
        return pl.BlockSpec((1, steps, w), lambda bi, h, t, p=part: (bi, t, p * nhb + h))

    return pl.pallas_call(
        functools.partial(kernel_fn, steps=steps),
        out_shape=jax.ShapeDtypeStruct((b, s, n_heads * HG_DIM), BF16),
        grid=(b, nhb, s // steps),
        in_specs=[pl.BlockSpec((1, w), lambda bi, h, t: (0, h)),
                  pl.BlockSpec((1, HG_DIM), lambda bi, h, t: (0, 0)),
                  spec(0), spec(1), spec(2), spec(3)],
        out_specs=pl.BlockSpec((1, steps, w), lambda bi, h, t: (bi, t, h)),
        scratch_shapes=[pltpu.VMEM((heads, HG_DIM, HG_DIM) if heads > 1 else (HG_DIM, HG_DIM), F32)],
        compiler_params=_params(("parallel", "parallel", "arbitrary"), block_bytes),
        name=name,
    )(lower, g_out, hg, hg, hg, hg)


def _hgrn2_any(hg, lower, g_out, n_heads):
    b, s, c4 = hg.shape
    hg_w = c4 // 4
    heads = min(4, n_heads)
    decay = _hg_max_decay(hg.reshape(b * s, c4), lower, hg_w, min(512, s), "hgrn2_decay")
    return lax.cond(
        decay < HG_FAST_MAX_DECAY,
        lambda: _hgrn2(hg, lower, g_out, n_heads, heads, functools.partial(_hg_kernel, heads=heads), "hgrn2"),
        lambda: _hgrn2(hg, lower, g_out, n_heads, 1, _hg_seq_kernel, "hgrn2_stepwise"))


def _xattn_kernel(q_ref, k_ref, v_ref, o_ref, *, n_heads):
    hd = X_HEAD_DIM
    for h in range(n_heads):
        sl = slice(h * hd, (h + 1) * hd)
        s = lax.dot_general(q_ref[0, :, sl], k_ref[0, :, sl], (((1,), (1,)), ((), ())),
                            preferred_element_type=F32)
        p = jnp.exp(s - jnp.max(s, axis=-1, keepdims=True))
        denom = jnp.sum(p, axis=-1, keepdims=True)
        o = jnp.dot(p.astype(BF16), v_ref[0, :, sl], preferred_element_type=F32)
        o_ref[0, :, sl] = (o / denom).astype(o_ref.dtype)


def _cross_attention(q, k, v, n_heads, name):
    b, s, w = q.shape
    mlen = k.shape[1]
    tq = min(1024, s)
    block_bytes = 2 * _nbytes((tq, w), BF16) + 2 * _nbytes((mlen, w), BF16) + 3 * _nbytes((tq, mlen), F32)
    return pl.pallas_call(
        functools.partial(_xattn_kernel, n_heads=n_heads),
        out_shape=jax.ShapeDtypeStruct((b, s, w), BF16),
        grid=(b, s // tq),
        in_specs=[pl.BlockSpec((1, tq, w), lambda bi, i: (bi, i, 0)),
                  pl.BlockSpec((1, mlen, w), lambda bi, i: (bi, 0, 0)),
                  pl.BlockSpec((1, mlen, w), lambda bi, i: (bi, 0, 0))],
        out_specs=pl.BlockSpec((1, tq, w), lambda bi, i: (bi, i, 0)),
        compiler_params=_params(("parallel", "parallel"), block_bytes),
        name=name,
    )(q, k, v)


def _pack_bf16_pair(lo, hi):
    lo_bits = lax.bitcast_convert_type(lo.astype(BF16).astype(F32), jnp.uint32)
    hi_bits = lax.bitcast_convert_type(hi.astype(BF16).astype(F32), jnp.uint32)
    return (lo_bits >> 16) | (hi_bits & jnp.uint32(0xFFFF0000))


def _unpack_bf16_pair(words):
    lo = lax.bitcast_convert_type(words << 16, F32)
    hi = lax.bitcast_convert_type(words & jnp.uint32(0xFFFF0000), F32)
    return lo, hi


def _router_kernel(h_ref, g_ref, whi_ref, wlo_ref, b_ref, n_ref, rt_ref, *, n_groups, n_experts):
    x = h_ref[...]
    ms = jnp.mean(x * x, axis=-1, keepdims=True)
    n = x * lax.rsqrt(ms + NORM_EPS) * g_ref[...]
    n_hi = n.astype(BF16)
    n_lo = (n - n_hi.astype(F32)).astype(BF16)
    half = n.shape[1] // 2
    n_ref[...] = _pack_bf16_pair(n[:, :half], n[:, half:])
    logits = (jnp.dot(n_hi, whi_ref[...], preferred_element_type=F32)
              + jnp.dot(n_lo, whi_ref[...], preferred_element_type=F32)
              + jnp.dot(n_hi, wlo_ref[...], preferred_element_type=F32) + b_ref[...])
    epg = n_experts // n_groups
    lane = lax.broadcasted_iota(jnp.int32, logits.shape, 1)
    n_lanes = logits.shape[1]

    def top1(vals):
        best = jnp.max(vals, axis=-1, keepdims=True)
        return best, jnp.min(jnp.where(vals == best, lane, n_lanes), axis=-1, keepdims=True)

    is_grp = lane < n_groups
    g_max, grp = top1(jnp.where(is_grp, logits, -jnp.inf))
    grp_w = 1.0 / jnp.sum(jnp.where(is_grp, jnp.exp(logits - g_max), 0.0), axis=-1, keepdims=True)
    first = n_groups + grp * epg
    in_grp = jnp.where(jnp.logical_and(lane >= first, lane < first + epg), logits, -jnp.inf)
    v1, i1 = top1(in_grp)
    v2, i2 = top1(jnp.where(lane == i1, -jnp.inf, in_grp))
    e2 = jnp.exp(v2 - v1)
    p1 = 1.0 / (1.0 + e2)
    cols = [grp_w * p1, grp_w * (e2 * p1), (i1 - n_groups).astype(F32), (i2 - n_groups).astype(F32)]
    rt = jnp.zeros(logits.shape, F32)
    for c, val in enumerate(cols):
        rt = jnp.where(lane == c, val, rt)
    rt_ref[...] = rt


def _router(h, g, w_hi, w_lo, bias, n_groups, n_experts, name):
    m, d = h.shape
    nl = w_hi.shape[1]
    tm = min(256, m)
    block_bytes = 2 * _nbytes((tm, d), F32) + 2 * _nbytes((d, nl), BF16)
    return pl.pallas_call(
        functools.partial(_router_kernel, n_groups=n_groups, n_experts=n_experts),
        out_shape=(jax.ShapeDtypeStruct((m, d // 2), jnp.uint32), jax.ShapeDtypeStruct((m, nl), F32)),
        grid=(m // tm,),
        in_specs=[pl.BlockSpec((tm, d), lambda i: (i, 0)),
                  pl.BlockSpec((1, d), lambda i: (0, 0)),
                  pl.BlockSpec((d, nl), lambda i: (0, 0)),
                  pl.BlockSpec((d, nl), lambda i: (0, 0)),
                  pl.BlockSpec((1, nl), lambda i: (0, 0))],
        out_specs=(pl.BlockSpec((tm, d // 2), lambda i: (i, 0)), pl.BlockSpec((tm, nl), lambda i: (i, 0))),
        compiler_params=_params(("parallel",), block_bytes),
        name=name,
    )(h, g, w_hi, w_lo, bias)


def _row_copies(src_ref, dst_ref, idx_ref, idx0, idx_stride, dst0, n_groups, sem, *, start):
    def body(g, carry):
        for u in range(ROW_COPY_UNROLL):
            r = g * ROW_COPY_UNROLL + u
            cp = pltpu.make_async_copy(src_ref.at[pl.ds(idx_ref[idx0 + r * idx_stride], 1)],
                                       dst_ref.at[pl.ds(dst0 + r, 1)], sem)
            if start:
                cp.start(priority=1)
            else:
                cp.wait()
        return carry

    lax.fori_loop(0, n_groups, body, 0)


def _expert_kernel(tok_ref, be_ref, src0_ref, ng_ref, nreal_ref, nf_ref, wg_ref, wu_ref, wd_ref, o_ref, xbuf, sems):
    i = pl.program_id(0)
    n_real = nreal_ref[0]

    def gather(block, start):
        slot = block % 2
        _row_copies(nf_ref, xbuf.at[slot], tok_ref, src0_ref[block], 1, 0, ng_ref[block], sems.at[slot],
                    start=start)

    @pl.when(i == 0)
    def _():
        xbuf[...] = jnp.zeros(xbuf.shape, xbuf.dtype)

    @pl.when(jnp.logical_and(i == 0, n_real > 0))
    def _():
        gather(i, True)

    @pl.when(i + 1 < n_real)
    def _():
        gather(i + 1, True)

    @pl.when(i < n_real)
    def _():
        gather(i, False)
        x_lo, x_hi = _unpack_bf16_pair(xbuf[i % 2])
        x_lo, x_hi = x_lo.astype(BF16), x_hi.astype(BF16)
        half = x_lo.shape[1]

        def up(w_ref):
            return (jnp.dot(x_lo, w_ref[0, :half, :], preferred_element_type=F32)
                    + jnp.dot(x_hi, w_ref[0, half:, :], preferred_element_type=F32))

        hg = up(wg_ref)
        hid = (hg * jax.nn.sigmoid(hg) * up(wu_ref)).astype(BF16)
        y = jnp.dot(hid, wd_ref[0], preferred_element_type=F32)
        o_ref[...] = _pack_bf16_pair(y[:, :half], y[:, half:])

    @pl.when(i >= n_real)
    def _():
        o_ref[...] = jnp.zeros(o_ref.shape, o_ref.dtype)


def _experts(nf, sorted_tok, blk_expert, blk_src0, blk_groups, n_real, wg, wu, wd, blk, name):
    n_blk = blk_expert.shape[0]
    d = wg.shape[1]
    ff = wg.shape[2]
    half = d // 2
    block_bytes = 3 * _nbytes((d, ff), BF16) + _nbytes((blk, half), jnp.uint32) + 3 * _nbytes((blk, d), F32)

    def weights(i, tk, be, s0, ng, nr):
        return (be[i], 0, 0)

    return pl.pallas_call(
        _expert_kernel,
        out_shape=jax.ShapeDtypeStruct((n_blk * blk, half), jnp.uint32),
        grid_spec=pltpu.PrefetchScalarGridSpec(
            num_scalar_prefetch=5,
            grid=(n_blk,),
            in_specs=[pl.BlockSpec(memory_space=pl.ANY),
                      pl.BlockSpec((1, d, ff), weights),
                      pl.BlockSpec((1, d, ff), weights),
                      pl.BlockSpec((1, ff, d), weights)],
            out_specs=pl.BlockSpec((blk, half), lambda i, tk, be, s0, ng, nr: (i, 0)),
            scratch_shapes=[pltpu.VMEM((2, blk, half), jnp.uint32), pltpu.SemaphoreType.DMA((2,))]),
        compiler_params=_params(("arbitrary",), block_bytes, 2 * _nbytes((blk, half), jnp.uint32)),
        name=name,
    )(sorted_tok, blk_expert, blk_src0, blk_groups, n_real, nf, wg, wu, wd)


def _combine_kernel(slot_ref, h_ref, gate_ref, ys_ref, o_ref, ybuf, sems, *, tm):
    i = pl.program_id(0)

    def gather(block, start):
        s = block % 2
        for k in range(TOP_K):
            _row_copies(ys_ref, ybuf.at[s], slot_ref, block * tm * TOP_K + k, TOP_K, k * tm,
                        tm // ROW_COPY_UNROLL, sems.at[s], start=start)

    @pl.when(i == 0)
    def _():
        gather(i, True)

    @pl.when(i + 1 < pl.num_programs(0))
    def _():
        gather(i + 1, True)

    gather(i, False)
    half = ybuf.shape[2]
    gate = gate_ref[...]
    acc_lo = h_ref[:, :half]
    acc_hi = h_ref[:, half:]
    for k in range(TOP_K):
        y_lo, y_hi = _unpack_bf16_pair(ybuf[i % 2, pl.ds(k * tm, tm), :])
        acc_lo = acc_lo + gate[:, k:k + 1] * y_lo
        acc_hi = acc_hi + gate[:, k:k + 1] * y_hi
    o_ref[:, :half] = acc_lo
    o_ref[:, half:] = acc_hi


def _combine(h, gate, ys, slot, name):
    m, d = h.shape
    tm = min(128, m)
    block_bytes = 2 * _nbytes((tm, d), F32) + TOP_K * _nbytes((tm, d), F32)
    return pl.pallas_call(
        functools.partial(_combine_kernel, tm=tm),
        out_shape=jax.ShapeDtypeStruct((m, d), F32),
        grid_spec=pltpu.PrefetchScalarGridSpec(
            num_scalar_prefetch=1,
            grid=(m // tm,),
            in_specs=[pl.BlockSpec((tm, d), lambda i, sl: (i, 0)),
                      pl.BlockSpec((tm, TOP_K), lambda i, sl: (i, 0)),
                      pl.BlockSpec(memory_space=pl.ANY)],
            out_specs=pl.BlockSpec((tm, d), lambda i, sl: (i, 0)),
            scratch_shapes=[pltpu.VMEM((2, TOP_K * tm, d // 2), jnp.uint32), pltpu.SemaphoreType.DMA((2,))]),
        compiler_params=_params(("arbitrary",), block_bytes),
        name=name,
    )(slot, h, gate, ys)


def _layout(expert, n_experts, blk):
    n_assign = expert.shape[0]
    e_sorted, order = lax.sort_key_val(expert, jnp.arange(n_assign, dtype=jnp.int32))
    experts = jnp.arange(n_experts, dtype=jnp.int32)
    starts = jnp.searchsorted(e_sorted, experts, side='left').astype(jnp.int32)
    counts = jnp.searchsorted(e_sorted, experts, side='right').astype(jnp.int32) - starts
    padded = (counts + blk - 1) // blk * blk
    pad_ends = jnp.cumsum(padded)
    pad_starts = pad_ends - padded
    gap = pad_starts - starts
    gap_step = jnp.concatenate([gap[:1], gap[1:] - gap[:-1]])
    dest = jnp.arange(n_assign, dtype=jnp.int32) + jnp.cumsum(
        jnp.zeros((n_assign + 1,), jnp.int32).at[starts].add(gap_step))[:n_assign]
    _, slot = lax.sort_key_val(order, dest)
    cap = n_assign + n_experts * blk
    n_blk = cap // blk
    blk_row0 = jnp.arange(n_blk, dtype=jnp.int32) * blk
    blk_expert = jnp.minimum(jnp.searchsorted(pad_ends, blk_row0, side='right'), n_experts - 1).astype(jnp.int32)
    n_real = (pad_ends[-1] // blk).astype(jnp.int32)
    blk_src0 = jnp.clip(blk_row0 - gap[blk_expert], 0, n_assign).astype(jnp.int32)
    blk_rows = jnp.clip(counts[blk_expert] - (blk_row0 - pad_starts[blk_expert]), 0, blk)
    blk_rows = jnp.where(jnp.arange(n_blk) < n_real, blk_rows, 0)
    blk_groups = ((blk_rows + ROW_COPY_UNROLL - 1) // ROW_COPY_UNROLL).astype(jnp.int32)
    sorted_tok = jnp.concatenate([order // TOP_K, jnp.zeros((ROW_COPY_UNROLL,), jnp.int32)])
    return sorted_tok, blk_expert, blk_src0, blk_groups, n_real.reshape(1), slot.astype(jnp.int32)


def _diff_lambda_init(layer):
    return 0.8 - 0.6 * math.exp(-0.3 * layer)


def _tile_cols(v, n):
    return jnp.tile(v.reshape(1, -1).astype(F32), (1, n // v.size))


def kernel(x, mem, g_mix, w_in, g_da_q, g_da_k, lam_q1, lam_k1, lam_q2, lam_k2, g_da_sub, hg_lower, g_hg_out,
           w_up_a, w_up_b, w_gate, b_gate, w_out, g_cross, g_mem, w_xq, w_xkv, g_xq, g_xk, w_xo, g_ffn, w_grp,
           b_grp, w_erouter, b_erouter, w_e_gate, w_e_up, w_e_down):
    b, s, d = x.shape
    t = b * s
    depth = g_mix.shape[0]
    da_w = w_up_a.shape[1]
    hg_w = w_up_b.shape[1]
    da_heads = da_w // (2 * DA_HEAD_DIM)
    hg_heads = hg_w // HG_DIM
    x_w = w_xq.shape[2]
    x_heads = x_w // X_HEAD_DIM
    n_groups = w_grp.shape[2]
    n_experts = w_erouter.shape[2]
    moe_blk = min(256, t)

    lower_bounds = jnp.cumsum(jax.nn.softmax(hg_lower.astype(F32), axis=0), axis=0)
    h = x.reshape(t, d)
    mem2 = mem.reshape(b * mem.shape[1], d)
    for l in range(depth):
        lam_init = _diff_lambda_init(l)
        w_in_l = w_in[l]
        n = _rmsnorm(h, g_mix[l], BF16, "rms_mix")
        q_scale = DA_HEAD_DIM ** -0.5 * math.log2(math.e)
        qk_gain = jnp.concatenate([_tile_cols(g_da_q[l], da_w) * q_scale, _tile_cols(g_da_k[l], da_w)], axis=1)
        ff = w_e_gate.shape[3]
        qk, we_gate = _matmul(n, w_in_l, col0=0, ncols=2 * da_w, out_dtype=BF16,
                              epilogue=functools.partial(_ep_group_norm, group=DA_HEAD_DIM),
                              col_extras=[(qk_gain, 0)], side_cast=w_e_gate[l].reshape(n_experts * d, ff),
                              name="proj_qk")
        v = _matmul(n, w_in_l, col0=2 * da_w, ncols=da_w, out_dtype=BF16, name="proj_v")
        hg, we_down = _matmul(n, w_in_l, col0=3 * da_w, ncols=4 * hg_w, out_dtype=F32,
                              side_cast=w_e_down[l].reshape(n_experts * ff, d), name="proj_hg")
        gates, we_up = _matmul(n, w_gate[l], out_dtype=BF16, epilogue=_ep_bias_sigmoid,
                               col_extras=[(b_gate[l].reshape(1, -1).astype(F32), 0)],
                               side_cast=w_e_up[l].reshape(n_experts * d, ff), name="proj_gate")
        lamv = jnp.stack([lam_q1[l], lam_k1[l], lam_q2[l], lam_k2[l]]).astype(F32)
        y_a = _diff_attention_any(qk.reshape(b, s, 2 * da_w), v.reshape(b, s, da_w), lamv,
                                  g_da_sub[l].reshape(1, -1).astype(F32), g_da_q[l], g_da_k[l], q_scale,
                                  da_heads, lam_init)
        y_b = _hgrn2_any(hg.reshape(b, s, 4 * hg_w), lower_bounds[l].reshape(1, hg_w),
                         g_hg_out[l].reshape(1, HG_DIM).astype(F32), hg_heads)
        merged = _merge(y_a.reshape(t, da_w), w_up_a[l], y_b.reshape(t, hg_w), w_up_b[l], gates, "merge")
        h = _matmul(merged, w_out[l], out_dtype=F32, epilogue=_ep_residual,
                    full_extras=[(h, 0)], name="mix_out")
        nc = _rmsnorm(h, g_cross[l], BF16, "rms_cross")
        xq = _matmul(nc, w_xq[l], out_dtype=BF16,
                     epilogue=functools.partial(_ep_group_norm, group=X_HEAD_DIM),
                     col_extras=[(_tile_cols(g_xq[l], x_w) * X_HEAD_DIM ** -0.5, 0)], name="xattn_q")
        nm = _rmsnorm(mem2, g_mem[l], BF16, "rms_mem")
        w_xkv_l = w_xkv[l]
        xk = _matmul(nm, w_xkv_l, col0=0, ncols=x_w, out_dtype=BF16,
                     epilogue=functools.partial(_ep_group_norm, group=X_HEAD_DIM),
                     col_extras=[(_tile_cols(g_xk[l], x_w), 0)], name="xattn_k")
        xv = _matmul(nm, w_xkv_l, col0=x_w, ncols=x_w, out_dtype=BF16, name="xattn_v")
        xo = _cross_attention(xq.reshape(b, s, x_w), xk.reshape(b, -1, x_w), xv.reshape(b, -1, x_w), x_heads,
                              "xattn")
        h = _matmul(xo.reshape(t, x_w), w_xo[l], out_dtype=F32, epilogue=_ep_residual,
                    full_extras=[(h, 0)], name="xattn_out")
        n_logit = -(-(n_groups + n_experts) // LANES) * LANES
        w_r = jnp.zeros((d, n_logit), F32).at[:, :n_groups].set(w_grp[l]).at[:, n_groups:n_groups + n_experts].set(
            w_erouter[l])
        b_r = jnp.zeros((1, n_logit), F32).at[0, :n_groups].set(b_grp[l]).at[0, n_groups:n_groups + n_experts].set(
            b_erouter[l])
        w_r_hi = w_r.astype(BF16)
        w_r_lo = (w_r - w_r_hi.astype(F32)).astype(BF16)
        nf, route = _router(h, g_ffn[l].reshape(1, d).astype(F32), w_r_hi, w_r_lo, b_r, n_groups, n_experts,
                            "router")
        gate = route[:, :TOP_K]
        expert = route[:, TOP_K:2 * TOP_K].astype(jnp.int32).reshape(-1)
        sorted_tok, blk_expert, blk_src0, blk_groups, n_real, slot = _layout(expert, n_experts, moe_blk)
        ys = _experts(nf, sorted_tok, blk_expert, blk_src0, blk_groups, n_real, we_gate.reshape(n_experts, d, ff),
                      we_up.reshape(n_experts, d, ff), we_down.reshape(n_experts, ff, d), moe_blk, "moe_experts")
        h = _combine(h, gate, ys, slot, "moe_combine")
    return h.reshape(b, s, d)
```

```python
import functools
import math

import jax
import jax.numpy as jnp
from jax import lax
from jax.experimental import pallas as pl
from jax.experimental.pallas import tpu as pltpu

F32 = jnp.float32
BF16 = jnp.bfloat16

NORM_EPS = 1e-6
DA_HEAD_DIM = 128
DA_CHUNK = 64
DA_MAX_BOUNDED_LOGIT = 50.0
HG_DIM = 128
HG_CHUNK = 64
HG_FAST_MAX_DECAY = 60.0
X_HEAD_DIM = 256
TOP_K = 2
ROW_COPY_UNROLL = 8
LANES = 128
MASK_VALUE = -1e30
V7X_VMEM_BYTES = 64 * 1024 * 1024
VMEM_CAP_BYTES = V7X_VMEM_BYTES - 6 * 1024 * 1024


def _vmem_limit(block_bytes, scratch_bytes):
    return int(min(VMEM_CAP_BYTES, max(32 * 1024 * 1024, 2 * block_bytes + scratch_bytes + 16 * 1024 * 1024)))


def _params(semantics, block_bytes, scratch_bytes=0):
    return pltpu.CompilerParams(dimension_semantics=semantics,
                                vmem_limit_bytes=_vmem_limit(block_bytes, scratch_bytes))


def _nbytes(shape, dtype):
    return math.prod(shape) * jnp.dtype(dtype).itemsize


def _rms_kernel(x_ref, g_ref, o_ref):
    x = x_ref[...].astype(F32)
    ms = jnp.mean(x * x, axis=-1, keepdims=True)
    o_ref[...] = (x * lax.rsqrt(ms + NORM_EPS) * g_ref[...]).astype(o_ref.dtype)


def _rmsnorm(x, g, out_dtype, name):
    m, d = x.shape
    tm = min(256, m)
    return pl.pallas_call(
        _rms_kernel,
        out_shape=jax.ShapeDtypeStruct((m, d), out_dtype),
        grid=(m // tm,),
        in_specs=[pl.BlockSpec((tm, d), lambda i: (i, 0)), pl.BlockSpec((1, d), lambda i: (0, 0))],
        out_specs=pl.BlockSpec((tm, d), lambda i: (i, 0)),
        compiler_params=_params(("parallel",), _nbytes((tm, d), x.dtype) + _nbytes((tm, d), out_dtype)),
        name=name,
    )(x, g.reshape(1, d).astype(F32))


def _ep_plain(acc, o_ref):
    o_ref[...] = acc.astype(o_ref.dtype)


def _ep_group_norm(acc, o_ref, gain, *, group):
    for c in range(acc.shape[1] // group):
        sl = slice(c * group, (c + 1) * group)
        xg = acc[:, sl]
        ms = jnp.mean(xg * xg, axis=-1, keepdims=True)
        o_ref[:, sl] = (xg * lax.rsqrt(ms + NORM_EPS) * gain[:, sl]).astype(o_ref.dtype)


def _ep_bias_sigmoid(acc, o_ref, bias):
    o_ref[...] = jax.nn.sigmoid(acc + bias).astype(o_ref.dtype)


def _ep_residual(acc, o_ref, res):
    o_ref[...] = (res + acc).astype(o_ref.dtype)


def _row_block(i):
    return jnp.maximum(i - 1, 0)


def _mm_kernel(a_ref, w_ref, *rest, epilogue, side):
    if side:
        *extra, side_src_ref, o_ref, side_out_ref, wb_ref = rest
    else:
        *extra, o_ref, wb_ref = rest
    i = pl.program_id(1)

    @pl.when(i == 0)
    def _():
        wb_ref[...] = w_ref[...].astype(BF16)

    @pl.when(i > 0)
    def _():
        acc = jnp.dot(a_ref[...], wb_ref[...], preferred_element_type=F32)
        epilogue(acc, o_ref, *[e[...] for e in extra])
        if side:
            side_out_ref[...] = side_src_ref[...].astype(BF16)


def _matmul(a, w, *, col0=0, ncols=None, out_dtype, epilogue=_ep_plain, col_extras=(), full_extras=(),
            side_cast=None, name):
    m, k = a.shape
    ncols = w.shape[1] - col0 if ncols is None else ncols
    tm = min(1024, m)
    tn = min(512, ncols)
    assert m % tm == 0 and ncols % tn == 0 and col0 % tn == 0
    n_rb = m // tm
    in_specs = [pl.BlockSpec((tm, k), lambda j, i: (_row_block(i), 0)),
                pl.BlockSpec((k, tn), lambda j, i, o=col0 // tn: (0, j + o))]
    operands = [a, w]
    block_bytes = _nbytes((tm, k), a.dtype) + _nbytes((k, tn), w.dtype) + _nbytes((tm, tn), out_dtype)
    for vec, c0 in col_extras:
        assert c0 % tn == 0
        in_specs.append(pl.BlockSpec((1, tn), lambda j, i, o=c0 // tn: (0, j + o)))
        operands.append(vec)
    for arr, c0 in full_extras:
        assert c0 % tn == 0
        in_specs.append(pl.BlockSpec((tm, tn), lambda j, i, o=c0 // tn: (_row_block(i), j + o)))
        operands.append(arr)
        block_bytes += _nbytes((tm, tn), arr.dtype)
    out_shape = jax.ShapeDtypeStruct((m, ncols), out_dtype)
    out_specs = pl.BlockSpec((tm, tn), lambda j, i: (_row_block(i), j))
    if side_cast is not None:
        rows, cols = side_cast.shape
        n_steps = (ncols // tn) * n_rb
        chunk = rows // n_steps
        assert chunk * n_steps == rows and chunk % 16 == 0, (rows, n_steps)
        side_spec = pl.BlockSpec((chunk, cols), lambda j, i: (j * n_rb + _row_block(i), 0))
        in_specs.append(side_spec)
        operands.append(side_cast)
        out_shape = (out_shape, jax.ShapeDtypeStruct((rows, cols), BF16))
        out_specs = (out_specs, side_spec)
        block_bytes += _nbytes((chunk, cols), F32) + _nbytes((chunk, cols), BF16)
    return pl.pallas_call(
        functools.partial(_mm_kernel, epilogue=epilogue, side=side_cast is not None),
        out_shape=out_shape,
        grid=(ncols // tn, n_rb + 1),
        in_specs=in_specs,
        out_specs=out_specs,
        scratch_shapes=[pltpu.VMEM((k, tn), BF16)],
        compiler_params=_params(("parallel", "arbitrary"), block_bytes, _nbytes((k, tn), BF16)),
        name=name,
    )(*operands)


def _merge_kernel(ya_ref, wa_ref, yb_ref, wb_ref, ga_ref, gb_ref, o_ref, wa_bf_ref, wb_bf_ref):
    i = pl.program_id(1)

    @pl.when(i == 0)
    def _():
        wa_bf_ref[...] = wa_ref[...].astype(BF16)
        wb_bf_ref[...] = wb_ref[...].astype(BF16)

    @pl.when(i > 0)
    def _():
        pa = jnp.dot(ya_ref[...], wa_bf_ref[...], preferred_element_type=F32)
        pb = jnp.dot(yb_ref[...], wb_bf_ref[...], preferred_element_type=F32)
        o_ref[...] = (ga_ref[...].astype(F32) * pa + gb_ref[...].astype(F32) * pb).astype(o_ref.dtype)


def _merge(ya, wa, yb, wb, gates, name):
    m, ka = ya.shape
    kb = yb.shape[1]
    d = wa.shape[1]
    tm = min(512, m)
    tn = min(512, d)
    nb = d // tn
    block_bytes = (_nbytes((tm, ka), BF16) + _nbytes((ka, tn), F32) + _nbytes((tm, kb), BF16)
                   + _nbytes((kb, tn), F32) + 3 * _nbytes((tm, tn), BF16))
    return pl.pallas_call(
        _merge_kernel,
        out_shape=jax.ShapeDtypeStruct((m, d), BF16),
        grid=(nb, m // tm + 1),
        in_specs=[pl.BlockSpec((tm, ka), lambda j, i: (_row_block(i), 0)),
                  pl.BlockSpec((ka, tn), lambda j, i: (0, j)),
                  pl.BlockSpec((tm, kb), lambda j, i: (_row_block(i), 0)),
                  pl.BlockSpec((kb, tn), lambda j, i: (0, j)),
                  pl.BlockSpec((tm, tn), lambda j, i: (_row_block(i), j)),
                  pl.BlockSpec((tm, tn), lambda j, i: (_row_block(i), j + nb))],
        out_specs=pl.BlockSpec((tm, tn), lambda j, i: (_row_block(i), j)),
        scratch_shapes=[pltpu.VMEM((ka, tn), BF16), pltpu.VMEM((kb, tn), BF16)],
        compiler_params=_params(("parallel", "arbitrary"), block_bytes, _nbytes((ka + kb, tn), BF16)),
        name=name,
    )(ya, wa, yb, wb, gates, gates)


def _da_kernel(lamv_ref, gsub_ref, q_ref, k_ref, v_ref, o_ref, acc_ref, l_ref, *m_ref, tq, tsub, lam_init,
               bounded):
    qi = pl.program_id(2)
    hd = DA_HEAD_DIM
    m_ref = None if bounded else m_ref[0]
    if not bounded:
        m_ref[...] = jnp.full(m_ref.shape, MASK_VALUE, F32)
    l_ref[...] = jnp.zeros(l_ref.shape, F32)
    acc_ref[...] = jnp.zeros(acc_ref.shape, F32)
    def widen(x, width):
        return jnp.concatenate([x] * (width // LANES), axis=-1)

    def update(row0, nrows, off, width, lead):
        rows = slice(row0, row0 + nrows)
        qq = q_ref[0, rows, :]
        kk = k_ref[0, pl.ds(off, width), :]
        vv = v_ref[0, pl.ds(off, width), :]
        for c in range(2):
            s = lax.dot_general(qq[:, c * hd:(c + 1) * hd], kk[:, c * hd:(c + 1) * hd],
                                (((1,), (1,)), ((), ())), preferred_element_type=F32)
            if lead is not None:
                row_chunk = (lax.broadcasted_iota(jnp.int32, (nrows, width), 0) + lead) // DA_CHUNK
                col_chunk = lax.broadcasted_iota(jnp.int32, (nrows, width), 1) // DA_CHUNK
                s = jnp.where(col_chunk <= row_chunk, s, MASK_VALUE)
            if bounded:
                p = jnp.exp2(s)
            else:
                m_old = m_ref[c, rows]
                m_new = jnp.maximum(m_old, jnp.max(s, axis=-1, keepdims=True))
                p = jnp.exp2(s - widen(m_new, width))
                alpha = jnp.exp2(m_old - m_new)
                m_ref[c, rows] = m_new
            psum = p[:, :LANES]
            for g in range(1, width // LANES):
                psum = psum + p[:, g * LANES:(g + 1) * LANES]
            pv = jnp.dot(p.astype(BF16), vv, preferred_element_type=F32)
            if bounded:
                l_ref[c, rows] = l_ref[c, rows] + psum
                acc_ref[c, rows] = acc_ref[c, rows] + pv
            else:
                l_ref[c, rows] = alpha * l_ref[c, rows] + psum
                acc_ref[c, rows] = widen(alpha, 2 * hd) * acc_ref[c, rows] + pv

    def full_body(j, carry):
        update(0, tq, pl.multiple_of(j * tq, tq), tq, None)
        return carry

    lax.fori_loop(0, qi, full_body, 0)
    for r in range(tq // tsub):
        update(r * tsub, tsub, pl.multiple_of(qi * tq, tq), (r + 1) * tsub, r * tsub)

    lv = lamv_ref[...]
    lam = (jnp.exp(jnp.sum(lv[0:1] * lv[1:2], axis=-1, keepdims=True))
           - jnp.exp(jnp.sum(lv[2:3] * lv[3:4], axis=-1, keepdims=True)) + lam_init)
    l1 = jnp.sum(l_ref[0], axis=-1, keepdims=True)
    l2 = jnp.sum(l_ref[1], axis=-1, keepdims=True)
    o = acc_ref[0] / l1 - lam * (acc_ref[1] / l2)
    ms = jnp.mean(o * o, axis=-1, keepdims=True)
    o_ref[0] = ((o * lax.rsqrt(ms + NORM_EPS) * gsub_ref[...]) * (1.0 - lam_init)).astype(o_ref.dtype)


def _diff_attention(qk, v, lamv, g_sub, n_heads, lam_init, bounded, name):
    b, s, _ = v.shape
    hw = 2 * DA_HEAD_DIM
    tq = min(1024, s)
    tsub = min(256, tq)
    block_bytes = 2 * _nbytes((tq, hw), BF16) + 2 * _nbytes((s, hw), BF16) + 6 * _nbytes((tq, tq), F32)
    stats = [pltpu.VMEM((2, tq, LANES), F32)] * (1 if bounded else 2)
    return pl.pallas_call(
        functools.partial(_da_kernel, tq=tq, tsub=tsub, lam_init=lam_init, bounded=bounded),
        out_shape=jax.ShapeDtypeStruct((b, s, n_heads * hw), BF16),
        grid=(b, n_heads, s // tq),
        in_specs=[pl.BlockSpec((4, DA_HEAD_DIM), lambda bi, h, i: (0, 0)),
                  pl.BlockSpec((1, hw), lambda bi, h, i: (0, 0)),
                  pl.BlockSpec((1, tq, hw), lambda bi, h, i: (bi, i, h)),
                  pl.BlockSpec((1, s, hw), lambda bi, h, i: (bi, 0, n_heads + h)),
                  pl.BlockSpec((1, s, hw), lambda bi, h, i: (bi, 0, h))],
        out_specs=pl.BlockSpec((1, tq, hw), lambda bi, h, i: (bi, i, h)),
        scratch_shapes=[pltpu.VMEM((2, tq, hw), F32)] + stats,
        compiler_params=_params(("parallel", "parallel", "arbitrary"), block_bytes),
        name=name,
    )(lamv, g_sub, qk, qk, v)


def _diff_attention_any(qk, v, lamv, g_sub, g_q, g_k, q_scale, n_heads, lam_init):
    bound = 1.01 * DA_HEAD_DIM * q_scale * jnp.max(jnp.abs(g_q)) * jnp.max(jnp.abs(g_k))
    return lax.cond(
        bound < DA_MAX_BOUNDED_LOGIT,
        lambda: _diff_attention(qk, v, lamv, g_sub, n_heads, lam_init, True, "diff_attn"),
        lambda: _diff_attention(qk, v, lamv, g_sub, n_heads, lam_init, False, "diff_attn_online"))


def _hg_forget(lb_ref, f_ref, rows, sl):
    lb = lb_ref[:, sl]
    return lb + (1.0 - lb) * jax.nn.sigmoid(f_ref[0, rows, sl])


def _hg_finish(o, gout_ref, gate):
    ms = jnp.mean(o * o, axis=-1, keepdims=True)
    return (o * lax.rsqrt(ms + NORM_EPS) * gout_ref[...]) * (gate * jax.nn.sigmoid(gate))


def _hg_decay_kernel(lb_ref, f_ref, o_ref, *, steps):
    lb = lb_ref[...]
    nl = -jnp.log(lb + (1.0 - lb) * jax.nn.sigmoid(f_ref[...]))
    worst = jnp.zeros((1, nl.shape[1]), F32)
    for c in range(steps // HG_CHUNK):
        worst = jnp.maximum(worst, jnp.sum(nl[c * HG_CHUNK:(c + 1) * HG_CHUNK], axis=0, keepdims=True))
    o_ref[...] = jnp.broadcast_to(worst, o_ref.shape)


def _hg_max_decay(hg2, lower, hg_w, steps, name):
    t = hg2.shape[0]
    w = min(1024, hg_w)
    ncb = hg_w // w
    sub = 8
    out = pl.pallas_call(
        functools.partial(_hg_decay_kernel, steps=steps),
        out_shape=jax.ShapeDtypeStruct((t // steps * sub, hg_w), F32),
        grid=(t // steps, ncb),
        in_specs=[pl.BlockSpec((1, w), lambda i, j: (0, j)),
                  pl.BlockSpec((steps, w), lambda i, j: (i, ncb + j))],
        out_specs=pl.BlockSpec((sub, w), lambda i, j: (i, j)),
        compiler_params=_params(("parallel", "parallel"), 4 * _nbytes((steps, w), F32)),
        name=name,
    )(lower, hg2)
    return jnp.max(out)


def _hg_kernel(lb_ref, gout_ref, q_ref, f_ref, i_ref, g_ref, o_ref, st_ref, *, heads, steps):
    t = pl.program_id(2)
    ch, hd = HG_CHUNK, HG_DIM

    @pl.when(t == 0)
    def _():
        st_ref[...] = jnp.zeros(st_ref.shape, F32)

    r = lax.broadcasted_iota(jnp.int32, (ch, ch), 0)
    c = lax.broadcasted_iota(jnp.int32, (ch, ch), 1)
    causal = c <= r
    row = lax.broadcasted_iota(jnp.int32, (ch, hd), 0)
    nt = (((1,), (1,)), ((), ()))
    tn = (((0,), (0,)), ((), ()))

    def chunk(rows, sl, st):
        q = q_ref[0, rows, sl]
        f = _hg_forget(lb_ref, f_ref, rows, sl)
        key = 1.0 - f
        lf = jnp.log(f)
        cum = lf
        for sh in (1, 2, 4, 8, 16, 32):
            cum = cum + jnp.where(row >= sh, pltpu.roll(cum, sh, axis=0), 0.0)
        last = cum[ch - 1:ch]
        q_dec = (q * jnp.exp(cum)).astype(BF16)
        k_inv = (key * jnp.exp(-cum)).astype(BF16)
        k_state = (key * jnp.exp(last - cum)).astype(BF16)
        vals = i_ref[0, rows, sl].astype(BF16)
        scores = jnp.where(causal, lax.dot_general(q_dec, k_inv, nt, preferred_element_type=F32), 0.0)
        o = (jnp.dot(scores.astype(BF16), vals, preferred_element_type=F32)
             + lax.dot_general(q_dec, st.astype(BF16), nt, preferred_element_type=F32))
        st_next = jnp.exp(last) * st + lax.dot_general(vals, k_state, tn, preferred_element_type=F32)
        return _hg_finish(o, gout_ref, g_ref[0, rows, sl]), st_next

    for h in range(heads):
        sl = slice(h * hd, (h + 1) * hd)
        st = st_ref[h]
        for ci in range(steps // ch):
            rows = slice(ci * ch, (ci + 1) * ch)
            y, st = chunk(rows, sl, st)
            o_ref[0, rows, sl] = y.astype(o_ref.dtype)
        st_ref[h] = st


def _hg_seq_kernel(lb_ref, gout_ref, q_ref, f_ref, i_ref, g_ref, o_ref, st_ref, *, steps):
    t = pl.program_id(2)
    hd = HG_DIM

    @pl.when(t == 0)
    def _():
        st_ref[...] = jnp.zeros(st_ref.shape, F32)

    lane = lax.broadcasted_iota(jnp.int32, (hd, hd), 1)
    sl = slice(0, hd)

    def block(bi, carry):
        rows = pl.ds(pl.multiple_of(bi * hd, hd), hd)
        q = q_ref[0, rows, :]
        f = _hg_forget(lb_ref, f_ref, rows, sl)
        key = 1.0 - f
        vals_t = i_ref[0, rows, :].T
        st = st_ref[...]
        out_t = jnp.zeros((hd, hd), F32)
        for s in range(hd):
            st = f[s:s + 1] * st + vals_t[:, s:s + 1] * key[s:s + 1]
            out_t = jnp.where(lane == s, jnp.sum(st * q[s:s + 1], axis=1, keepdims=True), out_t)
        st_ref[...] = st
        o_ref[0, rows, :] = _hg_finish(out_t.T, gout_ref, g_ref[0, rows, :]).astype(o_ref.dtype)
        return carry

    lax.fori_loop(0, steps // hd, block, 0)


def _hgrn2(hg, lower, g_out, n_heads, heads, kernel_fn, name):
    b, s, _ = hg.shape
    steps = min(512, s)
    nhb = n_heads // heads
    w = heads * HG_DIM
    block_bytes = 4 * _nbytes((steps, w), F32) + _nbytes((steps, w), BF16)

    def spec(part):
        return pl.BlockSpec((1, steps, w), lambda bi, h, t, p=part: (bi, t, p * nhb + h))

    return pl.pallas_call(
        functools.partial(kernel_fn, steps=steps),
        out_shape=jax.ShapeDtypeStruct((b, s, n_heads * HG_DIM), BF16),
        grid=(b, nhb, s // steps),
        in_specs=[pl.BlockSpec((1, w), lambda bi, h, t: (0, h)),
                  pl.BlockSpec((1, HG_DIM), lambda bi, h, t: (0, 0)),
                  spec(0), spec(1), spec(2), spec(3)],
        out_specs=pl.BlockSpec((1, steps, w), lambda bi, h, t: (bi, t, h)),
        scratch_shapes=[pltpu.VMEM((heads, HG_DIM, HG_DIM) if heads > 1 else (HG_DIM, HG_DIM), F32)],
        compiler_params=_params(("parallel", "parallel", "arbitrary"), block_bytes),
        name=name,
    )(lower, g_out, hg, hg, hg, hg)


def _hgrn2_any(hg, lower, g_out, n_heads):
    b, s, c4 = hg.shape
    hg_w = c4 // 4
    heads = min(4, n_heads)
    decay = _hg_max_decay(hg.reshape(b * s, c4), lower, hg_w, min(512, s), "hgrn2_decay")
    return lax.cond(
        decay < HG_FAST_MAX_DECAY,
        lambda: _hgrn2(hg, lower, g_out, n_heads, heads, functools.partial(_hg_kernel, heads=heads), "hgrn2"),
        lambda: _hgrn2(hg, lower, g_out, n_heads, 1, _hg_seq_kernel, "hgrn2_stepwise"))


def _xattn_kernel(q_ref, k_ref, v_ref, o_ref, *, n_heads):
    hd = X_HEAD_DIM
    for h in range(n_heads):
        sl = slice(h * hd, (h + 1) * hd)
        s = lax.dot_general(q_ref[0, :, sl], k_ref[0, :, sl], (((1,), (1,)), ((), ())),
                            preferred_element_type=F32)
        p = jnp.exp(s - jnp.max(s, axis=-1, keepdims=True))
        denom = jnp.sum(p, axis=-1, keepdims=True)
        o = jnp.dot(p.astype(BF16), v_ref[0, :, sl], preferred_element_type=F32)
        o_ref[0, :, sl] = (o / denom).astype(o_ref.dtype)


def _cross_attention(q, k, v, n_heads, name):
    b, s, w = q.shape
    mlen = k.shape[1]
    tq = min(1024, s)
    block_bytes = 2 * _nbytes((tq, w), BF16) + 2 * _nbytes((mlen, w), BF16) + 3 * _nbytes((tq, mlen), F32)
    return pl.pallas_call(
        functools.partial(_xattn_kernel, n_heads=n_heads),
        out_shape=jax.ShapeDtypeStruct((b, s, w), BF16),
        grid=(b, s // tq),
        in_specs=[pl.BlockSpec((1, tq, w), lambda bi, i: (bi, i, 0)),
                  pl.BlockSpec((1, mlen, w), lambda bi, i: (bi, 0, 0)),
                  pl.BlockSpec((1, mlen, w), lambda bi, i: (bi, 0, 0))],
        out_specs=pl.BlockSpec((1, tq, w), lambda bi, i: (bi, i, 0)),
        compiler_params=_params(("parallel", "parallel"), block_bytes),
        name=name,
    )(q, k, v)


def _pack_bf16_pair(lo, hi):
    lo_bits = lax.bitcast_convert_type(lo.astype(BF16).astype(F32), jnp.uint32)
    hi_bits = lax.bitcast_convert_type(hi.astype(BF16).astype(F32), jnp.uint32)
    return (lo_bits >> 16) | (hi_bits & jnp.uint32(0xFFFF0000))


def _unpack_bf16_pair(words):
    lo = lax.bitcast_convert_type(words << 16, F32)
    hi = lax.bitcast_convert_type(words & jnp.uint32(0xFFFF0000), F32)
    return lo, hi


def _router_kernel(h_ref, g_ref, whi_ref, wlo_ref, b_ref, n_ref, rt_ref, *, n_groups, n_experts):
    x = h_ref[...]
    ms = jnp.mean(x * x, axis=-1, keepdims=True)
    n = x * lax.rsqrt(ms + NORM_EPS) * g_ref[...]
    n_hi = n.astype(BF16)
    n_lo = (n - n_hi.astype(F32)).astype(BF16)
    half = n.shape[1] // 2
    n_ref[...] = _pack_bf16_pair(n[:, :half], n[:, half:])
    logits = (jnp.dot(n_hi, whi_ref[...], preferred_element_type=F32)
              + jnp.dot(n_lo, whi_ref[...], preferred_element_type=F32)
              + jnp.dot(n_hi, wlo_ref[...], preferred_element_type=F32) + b_ref[...])
    epg = n_experts // n_groups
    lane = lax.broadcasted_iota(jnp.int32, logits.shape, 1)
    n_lanes = logits.shape[1]

    def top1(vals):
        best = jnp.max(vals, axis=-1, keepdims=True)
        return best, jnp.min(jnp.where(vals == best, lane, n_lanes), axis=-1, keepdims=True)

    is_grp = lane < n_groups
    g_max, grp = top1(jnp.where(is_grp, logits, -jnp.inf))
    grp_w = 1.0 / jnp.sum(jnp.where(is_grp, jnp.exp(logits - g_max), 0.0), axis=-1, keepdims=True)
    first = n_groups + grp * epg
    in_grp = jnp.where(jnp.logical_and(lane >= first, lane < first + epg), logits, -jnp.inf)
    v1, i1 = top1(in_grp)
    v2, i2 = top1(jnp.where(lane == i1, -jnp.inf, in_grp))
    e2 = jnp.exp(v2 - v1)
    p1 = 1.0 / (1.0 + e2)
    cols = [grp_w * p1, grp_w * (e2 * p1), (i1 - n_groups).astype(F32), (i2 - n_groups).astype(F32)]
    rt = jnp.zeros(logits.shape, F32)
    for c, val in enumerate(cols):
        rt = jnp.where(lane == c, val, rt)
    rt_ref[...] = rt


def _router(h, g, w_hi, w_lo, bias, n_groups, n_experts, name):
    m, d = h.shape
    nl = w_hi.shape[1]
    tm = min(256, m)
    block_bytes = 2 * _nbytes((tm, d), F32) + 2 * _nbytes((d, nl), BF16)
    return pl.pallas_call(
        functools.partial(_router_kernel, n_groups=n_groups, n_experts=n_experts),
        out_shape=(jax.ShapeDtypeStruct((m, d // 2), jnp.uint32), jax.ShapeDtypeStruct((m, nl), F32)),
        grid=(m // tm,),
        in_specs=[pl.BlockSpec((tm, d), lambda i: (i, 0)),
                  pl.BlockSpec((1, d), lambda i: (0, 0)),
                  pl.BlockSpec((d, nl), lambda i: (0, 0)),
                  pl.BlockSpec((d, nl), lambda i: (0, 0)),
                  pl.BlockSpec((1, nl), lambda i: (0, 0))],
        out_specs=(pl.BlockSpec((tm, d // 2), lambda i: (i, 0)), pl.BlockSpec((tm, nl), lambda i: (i, 0))),
        compiler_params=_params(("parallel",), block_bytes),
        name=name,
    )(h, g, w_hi, w_lo, bias)


def _row_copies(src_ref, dst_ref, idx_ref, idx0, idx_stride, dst0, n_groups, sem, *, start):
    def body(g, carry):
        for u in range(ROW_COPY_UNROLL):
            r = g * ROW_COPY_UNROLL + u
            cp = pltpu.make_async_copy(src_ref.at[pl.ds(idx_ref[idx0 + r * idx_stride], 1)],
                                       dst_ref.at[pl.ds(dst0 + r, 1)], sem)
            if start:
                cp.start(priority=1)
            else:
                cp.wait()
        return carry

    lax.fori_loop(0, n_groups, body, 0)


def _expert_kernel(tok_ref, be_ref, src0_ref, ng_ref, nreal_ref, nf_ref, wg_ref, wu_ref, wd_ref, o_ref, xbuf, sems):
    i = pl.program_id(0)
    n_real = nreal_ref[0]

    def gather(block, start):
        slot = block % 2
        _row_copies(nf_ref, xbuf.at[slot], tok_ref, src0_ref[block], 1, 0, ng_ref[block], sems.at[slot],
                    start=start)

    @pl.when(i == 0)
    def _():
        xbuf[...] = jnp.zeros(xbuf.shape, xbuf.dtype)

    @pl.when(jnp.logical_and(i == 0, n_real > 0))
    def _():
        gather(i, True)

    @pl.when(i + 1 < n_real)
    def _():
        gather(i + 1, True)

    @pl.when(i < n_real)
    def _():
        gather(i, False)
        x_lo, x_hi = _unpack_bf16_pair(xbuf[i % 2])
        x_lo, x_hi = x_lo.astype(BF16), x_hi.astype(BF16)
        half = x_lo.shape[1]

        def up(w_ref):
            return (jnp.dot(x_lo, w_ref[0, :half, :], preferred_element_type=F32)
                    + jnp.dot(x_hi, w_ref[0, half:, :], preferred_element_type=F32))

        hg = up(wg_ref)
        hid = (hg * jax.nn.sigmoid(hg) * up(wu_ref)).astype(BF16)
        y = jnp.dot(hid, wd_ref[0], preferred_element_type=F32)
        o_ref[...] = _pack_bf16_pair(y[:, :half], y[:, half:])

    @pl.when(i >= n_real)
    def _():
        o_ref[...] = jnp.zeros(o_ref.shape, o_ref.dtype)


def _experts(nf, sorted_tok, blk_expert, blk_src0, blk_groups, n_real, wg, wu, wd, blk, name):
    n_blk = blk_expert.shape[0]
    d = wg.shape[1]
    ff = wg.shape[2]
    half = d // 2
    block_bytes = 3 * _nbytes((d, ff), BF16) + _nbytes((blk, half), jnp.uint32) + 3 * _nbytes((blk, d), F32)

    def weights(i, tk, be, s0, ng, nr):
        return (be[i], 0, 0)

    return pl.pallas_call(
        _expert_kernel,
        out_shape=jax.ShapeDtypeStruct((n_blk * blk, half), jnp.uint32),
        grid_spec=pltpu.PrefetchScalarGridSpec(
            num_scalar_prefetch=5,
            grid=(n_blk,),
            in_specs=[pl.BlockSpec(memory_space=pl.ANY),
                      pl.BlockSpec((1, d, ff), weights),
                      pl.BlockSpec((1, d, ff), weights),
                      pl.BlockSpec((1, ff, d), weights)],
            out_specs=pl.BlockSpec((blk, half), lambda i, tk, be, s0, ng, nr: (i, 0)),
            scratch_shapes=[pltpu.VMEM((2, blk, half), jnp.uint32), pltpu.SemaphoreType.DMA((2,))]),
        compiler_params=_params(("arbitrary",), block_bytes, 2 * _nbytes((blk, half), jnp.uint32)),
        name=name,
    )(sorted_tok, blk_expert, blk_src0, blk_groups, n_real, nf, wg, wu, wd)


def _combine_kernel(slot_ref, h_ref, gate_ref, ys_ref, o_ref, ybuf, sems, *, tm):
    i = pl.program_id(0)

    def gather(block, start):
        s = block % 2
        for k in range(TOP_K):
            _row_copies(ys_ref, ybuf.at[s], slot_ref, block * tm * TOP_K + k, TOP_K, k * tm,
                        tm // ROW_COPY_UNROLL, sems.at[s], start=start)

    @pl.when(i == 0)
    def _():
        gather(i, True)

    @pl.when(i + 1 < pl.num_programs(0))
    def _():
        gather(i + 1, True)

    gather(i, False)
    half = ybuf.shape[2]
    gate = gate_ref[...]
    acc_lo = h_ref[:, :half]
    acc_hi = h_ref[:, half:]
    for k in range(TOP_K):
        y_lo, y_hi = _unpack_bf16_pair(ybuf[i % 2, pl.ds(k * tm, tm), :])
        acc_lo = acc_lo + gate[:, k:k + 1] * y_lo
        acc_hi = acc_hi + gate[:, k:k + 1] * y_hi
    o_ref[:, :half] = acc_lo
    o_ref[:, half:] = acc_hi


def _combine(h, gate, ys, slot, name):
    m, d = h.shape
    tm = min(128, m)
    block_bytes = 2 * _nbytes((tm, d), F32) + TOP_K * _nbytes((tm, d), F32)
    return pl.pallas_call(
        functools.partial(_combine_kernel, tm=tm),
        out_shape=jax.ShapeDtypeStruct((m, d), F32),
        grid_spec=pltpu.PrefetchScalarGridSpec(
            num_scalar_prefetch=1,
            grid=(m // tm,),
            in_specs=[pl.BlockSpec((tm, d), lambda i, sl: (i, 0)),
                      pl.BlockSpec((tm, TOP_K), lambda i, sl: (i, 0)),
                      pl.BlockSpec(memory_space=pl.ANY)],
            out_specs=pl.BlockSpec((tm, d), lambda i, sl: (i, 0)),
            scratch_shapes=[pltpu.VMEM((2, TOP_K * tm, d // 2), jnp.uint32), pltpu.SemaphoreType.DMA((2,))]),
        compiler_params=_params(("arbitrary",), block_bytes),
        name=name,
    )(slot, h, gate, ys)


def _layout(expert, n_experts, blk):
    n_assign = expert.shape[0]
    e_sorted, order = lax.sort_key_val(expert, jnp.arange(n_assign, dtype=jnp.int32))
    experts = jnp.arange(n_experts, dtype=jnp.int32)
    starts = jnp.searchsorted(e_sorted, experts, side='left').astype(jnp.int32)
    counts = jnp.searchsorted(e_sorted, experts, side='right').astype(jnp.int32) - starts
    padded = (counts + blk - 1) // blk * blk
    pad_ends = jnp.cumsum(padded)
    pad_starts = pad_ends - padded
    gap = pad_starts - starts
    gap_step = jnp.concatenate([gap[:1], gap[1:] - gap[:-1]])
    dest = jnp.arange(n_assign, dtype=jnp.int32) + jnp.cumsum(
        jnp.zeros((n_assign + 1,), jnp.int32).at[starts].add(gap_step))[:n_assign]
    _, slot = lax.sort_key_val(order, dest)
    cap = n_assign + n_experts * blk
    n_blk = cap // blk
    blk_row0 = jnp.arange(n_blk, dtype=jnp.int32) * blk
    blk_expert = jnp.minimum(jnp.searchsorted(pad_ends, blk_row0, side='right'), n_experts - 1).astype(jnp.int32)
    n_real = (pad_ends[-1] // blk).astype(jnp.int32)
    blk_src0 = jnp.clip(blk_row0 - gap[blk_expert], 0, n_assign).astype(jnp.int32)
    blk_rows = jnp.clip(counts[blk_expert] - (blk_row0 - pad_starts[blk_expert]), 0, blk)
    blk_rows = jnp.where(jnp.arange(n_blk) < n_real, blk_rows, 0)
    blk_groups = ((blk_rows + ROW_COPY_UNROLL - 1) // ROW_COPY_UNROLL).astype(jnp.int32)
    sorted_tok = jnp.concatenate([order // TOP_K, jnp.zeros((ROW_COPY_UNROLL,), jnp.int32)])
    return sorted_tok, blk_expert, blk_src0, blk_groups, n_real.reshape(1), slot.astype(jnp.int32)


def _diff_lambda_init(layer):
    return 0.8 - 0.6 * math.exp(-0.3 * layer)


def _tile_cols(v, n):
    return jnp.tile(v.reshape(1, -1).astype(F32), (1, n // v.size))


def kernel(x, mem, g_mix, w_in, g_da_q, g_da_k, lam_q1, lam_k1, lam_q2, lam_k2, g_da_sub, hg_lower, g_hg_out,
           w_up_a, w_up_b, w_gate, b_gate, w_out, g_cross, g_mem, w_xq, w_xkv, g_xq, g_xk, w_xo, g_ffn, w_grp,
           b_grp, w_erouter, b_erouter, w_e_gate, w_e_up, w_e_down):
    b, s, d = x.shape
    t = b * s
    depth = g_mix.shape[0]
    da_w = w_up_a.shape[1]
    hg_w = w_up_b.shape[1]
    da_heads = da_w // (2 * DA_HEAD_DIM)
    hg_heads = hg_w // HG_DIM
    x_w = w_xq.shape[2]
    x_heads = x_w // X_HEAD_DIM
    n_groups = w_grp.shape[2]
    n_experts = w_erouter.shape[2]
    moe_blk = min(256, t)

    lower_bounds = jnp.cumsum(jax.nn.softmax(hg_lower.astype(F32), axis=0), axis=0)
    h = x.reshape(t, d)
    mem2 = mem.reshape(b * mem.shape[1], d)
    for l in range(depth):
        lam_init = _diff_lambda_init(l)
        w_in_l = w_in[l]
        n = _rmsnorm(h, g_mix[l], BF16, "rms_mix")
        q_scale = DA_HEAD_DIM ** -0.5 * math.log2(math.e)
        qk_gain = jnp.concatenate([_tile_cols(g_da_q[l], da_w) * q_scale, _tile_cols(g_da_k[l], da_w)], axis=1)
        ff = w_e_gate.shape[3]
        qk, we_gate = _matmul(n, w_in_l, col0=0, ncols=2 * da_w, out_dtype=BF16,
                              epilogue=functools.partial(_ep_group_norm, group=DA_HEAD_DIM),
                              col_extras=[(qk_gain, 0)], side_cast=w_e_gate[l].reshape(n_experts * d, ff),
                              name="proj_qk")
        v = _matmul(n, w_in_l, col0=2 * da_w, ncols=da_w, out_dtype=BF16, name="proj_v")
        hg, we_down = _matmul(n, w_in_l, col0=3 * da_w, ncols=4 * hg_w, out_dtype=F32,
                              side_cast=w_e_down[l].reshape(n_experts * ff, d), name="proj_hg")
        gates, we_up = _matmul(n, w_gate[l], out_dtype=BF16, epilogue=_ep_bias_sigmoid,
                               col_extras=[(b_gate[l].reshape(1, -1).astype(F32), 0)],
                               side_cast=w_e_up[l].reshape(n_experts * d, ff), name="proj_gate")
        lamv = jnp.stack([lam_q1[l], lam_k1[l], lam_q2[l], lam_k2[l]]).astype(F32)
        y_a = _diff_attention_any(qk.reshape(b, s, 2 * da_w), v.reshape(b, s, da_w), lamv,
                                  g_da_sub[l].reshape(1, -1).astype(F32), g_da_q[l], g_da_k[l], q_scale,
                                  da_heads, lam_init)
        y_b = _hgrn2_any(hg.reshape(b, s, 4 * hg_w), lower_bounds[l].reshape(1, hg_w),
                         g_hg_out[l].reshape(1, HG_DIM).astype(F32), hg_heads)
        merged = _merge(y_a.reshape(t, da_w), w_up_a[l], y_b.reshape(t, hg_w), w_up_b[l], gates, "merge")
        h = _matmul(merged, w_out[l], out_dtype=F32, epilogue=_ep_residual,
                    full_extras=[(h, 0)], name="mix_out")
        nc = _rmsnorm(h, g_cross[l], BF16, "rms_cross")
        xq = _matmul(nc, w_xq[l], out_dtype=BF16,
                     epilogue=functools.partial(_ep_group_norm, group=X_HEAD_DIM),
                     col_extras=[(_tile_cols(g_xq[l], x_w) * X_HEAD_DIM ** -0.5, 0)], name="xattn_q")
        nm = _rmsnorm(mem2, g_mem[l], BF16, "rms_mem")
        w_xkv_l = w_xkv[l]
        xk = _matmul(nm, w_xkv_l, col0=0, ncols=x_w, out_dtype=BF16,
                     epilogue=functools.partial(_ep_group_norm, group=X_HEAD_DIM),
                     col_extras=[(_tile_cols(g_xk[l], x_w), 0)], name="xattn_k")
        xv = _matmul(nm, w_xkv_l, col0=x_w, ncols=x_w, out_dtype=BF16, name="xattn_v")
        xo = _cross_attention(xq.reshape(b, s, x_w), xk.reshape(b, -1, x_w), xv.reshape(b, -1, x_w), x_heads,
                              "xattn")
        h = _matmul(xo.reshape(t, x_w), w_xo[l], out_dtype=F32, epilogue=_ep_residual,
                    full_extras=[(h, 0)], name="xattn_out")
        n_logit = -(-(n_groups + n_experts) // LANES) * LANES
        w_r = jnp.zeros((d, n_logit), F32).at[:, :n_groups].set(w_grp[l]).at[:, n_groups:n_groups + n_experts].set(
            w_erouter[l])
        b_r = jnp.zeros((1, n_logit), F32).at[0, :n_groups].set(b_grp[l]).at[0, n_groups:n_groups + n_experts].set(
            b_erouter[l])
        w_r_hi = w_r.astype(BF16)
        w_r_lo = (w_r - w_r_hi.astype(F32)).astype(BF16)
        nf, route = _router(h, g_ffn[l].reshape(1, d).astype(F32), w_r_hi, w_r_lo, b_r, n_groups, n_experts,
                            "router")
        gate = route[:, :TOP_K]
        expert = route[:, TOP_K:2 * TOP_K].astype(jnp.int32).reshape(-1)
        sorted_tok, blk_expert, blk_src0, blk_groups, n_real, slot = _layout(expert, n_experts, moe_blk)
        ys = _experts(nf, sorted_tok, blk_expert, blk_src0, blk_groups, n_real, we_gate.reshape(n_experts, d, ff),
                      we_up.reshape(n_experts, d, ff), we_down.reshape(n_experts, ff, d), moe_blk, "moe_experts")
        h = _combine(h, gate, ys, slot, "moe_combine")
    return h.reshape(b, s, d)
```

```python
import functools
import math

import jax
import jax.numpy as jnp
from jax import lax
from jax.experimental import pallas as pl
from jax.experimental.pallas import tpu as pltpu

F32 = jnp.float32
BF16 = jnp.bfloat16

NORM_EPS = 1e-6
DA_HEAD_DIM = 128
DA_CHUNK = 64
DA_MAX_BOUNDED_LOGIT = 50.0
HG_DIM = 128
HG_CHUNK = 64
HG_FAST_MAX_DECAY = 60.0
X_HEAD_DIM = 256
TOP_K = 2
ROW_COPY_UNROLL = 8
LANES = 128
MASK_VALUE = -1e30
V7X_VMEM_BYTES = 64 * 1024 * 1024
VMEM_CAP_BYTES = V7X_VMEM_BYTES - 6 * 1024 * 1024


def _vmem_limit(block_bytes, scratch_bytes):
    return int(min(VMEM_CAP_BYTES, max(32 * 1024 * 1024, 2 * block_bytes + scratch_bytes + 16 * 1024 * 1024)))


def _params(semantics, block_bytes, scratch_bytes=0):
    return pltpu.CompilerParams(dimension_semantics=semantics,
                                vmem_limit_bytes=_vmem_limit(block_bytes, scratch_bytes))


def _nbytes(shape, dtype):
    return math.prod(shape) * jnp.dtype(dtype).itemsize


def _rms_kernel(x_ref, g_ref, o_ref):
    x = x_ref[...].astype(F32)
    ms = jnp.mean(x * x, axis=-1, keepdims=True)
    o_ref[...] = (x * lax.rsqrt(ms + NORM_EPS) * g_ref[...]).astype(o_ref.dtype)


def _rmsnorm(x, g, out_dtype, name):
    m, d = x.shape
    tm = min(256, m)
    return pl.pallas_call(
        _rms_kernel,
        out_shape=jax.ShapeDtypeStruct((m, d), out_dtype),
        grid=(m // tm,),
        in_specs=[pl.BlockSpec((tm, d), lambda i: (i, 0)), pl.BlockSpec((1, d), lambda i: (0, 0))],
        out_specs=pl.BlockSpec((tm, d), lambda i: (i, 0)),
        compiler_params=_params(("parallel",), _nbytes((tm, d), x.dtype) + _nbytes((tm, d), out_dtype)),
        name=name,
    )(x, g.reshape(1, d).astype(F32))


def _ep_plain(acc, o_ref):
    o_ref[...] = acc.astype(o_ref.dtype)


def _ep_group_norm(acc, o_ref, gain, *, group):
    for c in range(acc.shape[1] // group):
        sl = slice(c * group, (c + 1) * group)
        xg = acc[:, sl]
        ms = jnp.mean(xg * xg, axis=-1, keepdims=True)
        o_ref[:, sl] = (xg * lax.rsqrt(ms + NORM_EPS) * gain[:, sl]).astype(o_ref.dtype)


def _ep_bias_sigmoid(acc, o_ref, bias):
    o_ref[...] = jax.nn.sigmoid(acc + bias).astype(o_ref.dtype)


def _ep_residual(acc, o_ref, res):
    o_ref[...] = (res + acc).astype(o_ref.dtype)


def _row_block(i):
    return jnp.maximum(i - 1, 0)


def _mm_kernel(a_ref, w_ref, *rest, epilogue, side):
    if side:
        *extra, side_src_ref, o_ref, side_out_ref, wb_ref = rest
    else:
        *extra, o_ref, wb_ref = rest
    i = pl.program_id(1)

    @pl.when(i == 0)
    def _():
        wb_ref[...] = w_ref[...].astype(BF16)

    @pl.when(i > 0)
    def _():
        acc = jnp.dot(a_ref[...], wb_ref[...], preferred_element_type=F32)
        epilogue(acc, o_ref, *[e[...] for e in extra])
        if side:
            side_out_ref[...] = side_src_ref[...].astype(BF16)


def _matmul(a, w, *, col0=0, ncols=None, out_dtype, epilogue=_ep_plain, col_extras=(), full_extras=(),
            side_cast=None, name):
    m, k = a.shape
    ncols = w.shape[1] - col0 if ncols is None else ncols
    tm = min(1024, m)
    tn = min(512 if k > 1024 else 1024, ncols)
    while ncols % tn or col0 % tn:
        tn //= 2
    assert m % tm == 0 and tn % LANES == 0
    n_rb = m // tm
    in_specs = [pl.BlockSpec((tm, k), lambda j, i: (_row_block(i), 0)),
                pl.BlockSpec((k, tn), lambda j, i, o=col0 // tn: (0, j + o))]
    operands = [a, w]
    block_bytes = _nbytes((tm, k), a.dtype) + _nbytes((k, tn), w.dtype) + _nbytes((tm, tn), out_dtype)
    for vec, c0 in col_extras:
        assert c0 % tn == 0
        in_specs.append(pl.BlockSpec((1, tn), lambda j, i, o=c0 // tn: (0, j + o)))
        operands.append(vec)
    for arr, c0 in full_extras:
        assert c0 % tn == 0
        in_specs.append(pl.BlockSpec((tm, tn), lambda j, i, o=c0 // tn: (_row_block(i), j + o)))
        operands.append(arr)
        block_bytes += _nbytes((tm, tn), arr.dtype)
    out_shape = jax.ShapeDtypeStruct((m, ncols), out_dtype)
    out_specs = pl.BlockSpec((tm, tn), lambda j, i: (_row_block(i), j))
    if side_cast is not None:
        rows, cols = side_cast.shape
        n_steps = (ncols // tn) * n_rb
        chunk = rows // n_steps
        assert chunk * n_steps == rows and chunk % 16 == 0, (rows, n_steps)
        side_spec = pl.BlockSpec((chunk, cols), lambda j, i: (j * n_rb + _row_block(i), 0))
        in_specs.append(side_spec)
        operands.append(side_cast)
        out_shape = (out_shape, jax.ShapeDtypeStruct((rows, cols), BF16))
        out_specs = (out_specs, side_spec)
        block_bytes += _nbytes((chunk, cols), F32) + _nbytes((chunk, cols), BF16)
    return pl.pallas_call(
        functools.partial(_mm_kernel, epilogue=epilogue, side=side_cast is not None),
        out_shape=out_shape,
        grid=(ncols // tn, n_rb + 1),
        in_specs=in_specs,
        out_specs=out_specs,
        scratch_shapes=[pltpu.VMEM((k, tn), BF16)],
        compiler_params=_params(("parallel", "arbitrary"), block_bytes, _nbytes((k, tn), BF16)),
        name=name,
    )(*operands)


def _merge_kernel(ya_ref, wa_ref, yb_ref, wb_ref, ga_ref, gb_ref, o_ref, wa_bf_ref, wb_bf_ref):
    i = pl.program_id(1)

    @pl.when(i == 0)
    def _():
        wa_bf_ref[...] = wa_ref[...].astype(BF16)
        wb_bf_ref[...] = wb_ref[...].astype(BF16)

    @pl.when(i > 0)
    def _():
        pa = jnp.dot(ya_ref[...], wa_bf_ref[...], preferred_element_type=F32)
        pb = jnp.dot(yb_ref[...], wb_bf_ref[...], preferred_element_type=F32)
        o_ref[...] = (ga_ref[...].astype(F32) * pa + gb_ref[...].astype(F32) * pb).astype(o_ref.dtype)


def _merge(ya, wa, yb, wb, gates, name):
    m, ka = ya.shape
    kb = yb.shape[1]
    d = wa.shape[1]
    tm = min(512, m)
    tn = min(512, d)
    nb = d // tn
    block_bytes = (_nbytes((tm, ka), BF16) + _nbytes((ka, tn), F32) + _nbytes((tm, kb), BF16)
                   + _nbytes((kb, tn), F32) + 3 * _nbytes((tm, tn), BF16))
    return pl.pallas_call(
        _merge_kernel,
        out_shape=jax.ShapeDtypeStruct((m, d), BF16),
        grid=(nb, m // tm + 1),
        in_specs=[pl.BlockSpec((tm, ka), lambda j, i: (_row_block(i), 0)),
                  pl.BlockSpec((ka, tn), lambda j, i: (0, j)),
                  pl.BlockSpec((tm, kb), lambda j, i: (_row_block(i), 0)),
                  pl.BlockSpec((kb, tn), lambda j, i: (0, j)),
                  pl.BlockSpec((tm, tn), lambda j, i: (_row_block(i), j)),
                  pl.BlockSpec((tm, tn), lambda j, i: (_row_block(i), j + nb))],
        out_specs=pl.BlockSpec((tm, tn), lambda j, i: (_row_block(i), j)),
        scratch_shapes=[pltpu.VMEM((ka, tn), BF16), pltpu.VMEM((kb, tn), BF16)],
        compiler_params=_params(("parallel", "arbitrary"), block_bytes, _nbytes((ka + kb, tn), BF16)),
        name=name,
    )(ya, wa, yb, wb, gates, gates)


def _da_kernel(lamv_ref, gsub_ref, q_ref, k_ref, v_ref, o_ref, acc_ref, l_ref, *m_ref, tq, tsub, lam_init,
               bounded):
    qi = pl.program_id(2)
    hd = DA_HEAD_DIM
    m_ref = None if bounded else m_ref[0]
    if not bounded:
        m_ref[...] = jnp.full(m_ref.shape, MASK_VALUE, F32)
    l_ref[...] = jnp.zeros(l_ref.shape, F32)
    acc_ref[...] = jnp.zeros(acc_ref.shape, F32)
    def widen(x, width):
        return jnp.concatenate([x] * (width // LANES), axis=-1)

    def update(row0, nrows, off, width, lead):
        rows = slice(row0, row0 + nrows)
        qq = q_ref[0, rows, :]
        kk = k_ref[0, pl.ds(off, width), :]
        vv = v_ref[0, pl.ds(off, width), :]
        for c in range(2):
            s = lax.dot_general(qq[:, c * hd:(c + 1) * hd], kk[:, c * hd:(c + 1) * hd],
                                (((1,), (1,)), ((), ())), preferred_element_type=F32)
            if lead is not None:
                row_chunk = (lax.broadcasted_iota(jnp.int32, (nrows, width), 0) + lead) // DA_CHUNK
                col_chunk = lax.broadcasted_iota(jnp.int32, (nrows, width), 1) // DA_CHUNK
                s = jnp.where(col_chunk <= row_chunk, s, MASK_VALUE)
            if bounded:
                p = jnp.exp2(s)
            else:
                m_old = m_ref[c, rows]
                m_new = jnp.maximum(m_old, jnp.max(s, axis=-1, keepdims=True))
                p = jnp.exp2(s - widen(m_new, width))
                alpha = jnp.exp2(m_old - m_new)
                m_ref[c, rows] = m_new
            psum = p[:, :LANES]
            for g in range(1, width // LANES):
                psum = psum + p[:, g * LANES:(g + 1) * LANES]
            pv = jnp.dot(p.astype(BF16), vv, preferred_element_type=F32)
            if bounded:
                l_ref[c, rows] = l_ref[c, rows] + psum
                acc_ref[c, rows] = acc_ref[c, rows] + pv
            else:
                l_ref[c, rows] = alpha * l_ref[c, rows] + psum
                acc_ref[c, rows] = widen(alpha, 2 * hd) * acc_ref[c, rows] + pv

    def full_body(j, carry):
        update(0, tq, pl.multiple_of(j * tq, tq), tq, None)
        return carry

    lax.fori_loop(0, qi, full_body, 0)
    for r in range(tq // tsub):
        update(r * tsub, tsub, pl.multiple_of(qi * tq, tq), (r + 1) * tsub, r * tsub)

    lv = lamv_ref[...]
    lam = (jnp.exp(jnp.sum(lv[0:1] * lv[1:2], axis=-1, keepdims=True))
           - jnp.exp(jnp.sum(lv[2:3] * lv[3:4], axis=-1, keepdims=True)) + lam_init)
    l1 = jnp.sum(l_ref[0], axis=-1, keepdims=True)
    l2 = jnp.sum(l_ref[1], axis=-1, keepdims=True)
    o = acc_ref[0] / l1 - lam * (acc_ref[1] / l2)
    ms = jnp.mean(o * o, axis=-1, keepdims=True)
    o_ref[0] = ((o * lax.rsqrt(ms + NORM_EPS) * gsub_ref[...]) * (1.0 - lam_init)).astype(o_ref.dtype)


def _diff_attention(qk, v, lamv, g_sub, n_heads, lam_init, bounded, name):
    b, s, _ = v.shape
    hw = 2 * DA_HEAD_DIM
    tq = min(1024, s)
    tsub = min(256, tq)
    block_bytes = 2 * _nbytes((tq, hw), BF16) + 2 * _nbytes((s, hw), BF16) + 6 * _nbytes((tq, tq), F32)
    stats = [pltpu.VMEM((2, tq, LANES), F32)] * (1 if bounded else 2)
    return pl.pallas_call(
        functools.partial(_da_kernel, tq=tq, tsub=tsub, lam_init=lam_init, bounded=bounded),
        out_shape=jax.ShapeDtypeStruct((b, s, n_heads * hw), BF16),
        grid=(b, n_heads, s // tq),
        in_specs=[pl.BlockSpec((4, DA_HEAD_DIM), lambda bi, h, i: (0, 0)),
                  pl.BlockSpec((1, hw), lambda bi, h, i: (0, 0)),
                  pl.BlockSpec((1, tq, hw), lambda bi, h, i: (bi, i, h)),
                  pl.BlockSpec((1, s, hw), lambda bi, h, i: (bi, 0, n_heads + h)),
                  pl.BlockSpec((1, s, hw), lambda bi, h, i: (bi, 0, h))],
        out_specs=pl.BlockSpec((1, tq, hw), lambda bi, h, i: (bi, i, h)),
        scratch_shapes=[pltpu.VMEM((2, tq, hw), F32)] + stats,
        compiler_params=_params(("parallel", "parallel", "arbitrary"), block_bytes),
        name=name,
    )(lamv, g_sub, qk, qk, v)


def _diff_attention_any(qk, v, lamv, g_sub, g_q, g_k, q_scale, n_heads, lam_init):
    bound = 1.01 * DA_HEAD_DIM * q_scale * jnp.max(jnp.abs(g_q)) * jnp.max(jnp.abs(g_k))
    return lax.cond(
        bound < DA_MAX_BOUNDED_LOGIT,
        lambda: _diff_attention(qk, v, lamv, g_sub, n_heads, lam_init, True, "diff_attn"),
        lambda: _diff_attention(qk, v, lamv, g_sub, n_heads, lam_init, False, "diff_attn_online"))


def _hg_forget(lb_ref, f_ref, rows, sl):
    lb = lb_ref[:, sl]
    return lb + (1.0 - lb) * jax.nn.sigmoid(f_ref[0, rows, sl])


def _hg_finish(o, gout_ref, gate):
    ms = jnp.mean(o * o, axis=-1, keepdims=True)
    return (o * lax.rsqrt(ms + NORM_EPS) * gout_ref[...]) * (gate * jax.nn.sigmoid(gate))


def _hg_kernel(lb_ref, gout_ref, q_ref, f_ref, i_ref, g_ref, o_ref, dec_ref, st_ref, *, heads, steps):
    t = pl.program_id(2)
    ch, hd = HG_CHUNK, HG_DIM

    @pl.when(t == 0)
    def _():
        st_ref[...] = jnp.zeros(st_ref.shape, F32)

    r = lax.broadcasted_iota(jnp.int32, (ch, ch), 0)
    c = lax.broadcasted_iota(jnp.int32, (ch, ch), 1)
    causal = c <= r
    row = lax.broadcasted_iota(jnp.int32, (ch, hd), 0)
    nt = (((1,), (1,)), ((), ()))
    tn = (((0,), (0,)), ((), ()))

    def chunk(rows, sl, st):
        q = q_ref[0, rows, sl]
        f = _hg_forget(lb_ref, f_ref, rows, sl)
        key = 1.0 - f
        lf = jnp.log(f)
        cum = lf
        for sh in (1, 2, 4, 8, 16, 32):
            cum = cum + jnp.where(row >= sh, pltpu.roll(cum, sh, axis=0), 0.0)
        last = cum[ch - 1:ch]
        q_dec = (q * jnp.exp(cum)).astype(BF16)
        k_inv = (key * jnp.exp(-cum)).astype(BF16)
        k_state = (key * jnp.exp(last - cum)).astype(BF16)
        vals = i_ref[0, rows, sl].astype(BF16)
        scores = jnp.where(causal, lax.dot_general(q_dec, k_inv, nt, preferred_element_type=F32), 0.0)
        o = (jnp.dot(scores.astype(BF16), vals, preferred_element_type=F32)
             + lax.dot_general(q_dec, st.astype(BF16), nt, preferred_element_type=F32))
        st_next = jnp.exp(last) * st + lax.dot_general(vals, k_state, tn, preferred_element_type=F32)
        return _hg_finish(o, gout_ref, g_ref[0, rows, sl]), st_next, -last

    worst = jnp.zeros((1, hd), F32)
    for h in range(heads):
        sl = slice(h * hd, (h + 1) * hd)
        st = st_ref[h]
        for ci in range(steps // ch):
            rows = slice(ci * ch, (ci + 1) * ch)
            y, st, decay = chunk(rows, sl, st)
            o_ref[0, rows, sl] = y.astype(o_ref.dtype)
            worst = jnp.maximum(worst, decay)
        st_ref[h] = st
    dec_ref[...] = jnp.broadcast_to(worst, dec_ref.shape)


def _hg_seq_kernel(lb_ref, gout_ref, q_ref, f_ref, i_ref, g_ref, o_ref, st_ref, *, steps):
    t = pl.program_id(2)
    hd = HG_DIM

    @pl.when(t == 0)
    def _():
        st_ref[...] = jnp.zeros(st_ref.shape, F32)

    lane = lax.broadcasted_iota(jnp.int32, (hd, hd), 1)
    sl = slice(0, hd)

    def block(bi, carry):
        rows = pl.ds(pl.multiple_of(bi * hd, hd), hd)
        q = q_ref[0, rows, :]
        f = _hg_forget(lb_ref, f_ref, rows, sl)
        key = 1.0 - f
        vals_t = i_ref[0, rows, :].T
        st = st_ref[...]
        out_t = jnp.zeros((hd, hd), F32)
        for s in range(hd):
            st = f[s:s + 1] * st + vals_t[:, s:s + 1] * key[s:s + 1]
            out_t = jnp.where(lane == s, jnp.sum(st * q[s:s + 1], axis=1, keepdims=True), out_t)
        st_ref[...] = st
        o_ref[0, rows, :] = _hg_finish(out_t.T, gout_ref, g_ref[0, rows, :]).astype(o_ref.dtype)
        return carry

    lax.fori_loop(0, steps // hd, block, 0)


def _hgrn2(hg, lower, g_out, n_heads, heads, kernel_fn, with_decay, name):
    b, s, _ = hg.shape
    steps = min(512, s)
    nhb = n_heads // heads
    nt = s // steps
    w = heads * HG_DIM
    block_bytes = 4 * _nbytes((steps, w), F32) + _nbytes((steps, w), BF16)

    def spec(part):
        return pl.BlockSpec((1, steps, w), lambda bi, h, t, p=part: (bi, t, p * nhb + h))

    out_shape = jax.ShapeDtypeStruct((b, s, n_heads * HG_DIM), BF16)
    out_specs = pl.BlockSpec((1, steps, w), lambda bi, h, t: (bi, t, h))
    if with_decay:
        sub = 8
        out_shape = (out_shape, jax.ShapeDtypeStruct((b * nhb * nt * sub, LANES), F32))
        out_specs = (out_specs, pl.BlockSpec((sub, LANES), lambda bi, h, t: ((bi * nhb + h) * nt + t, 0)))
    return pl.pallas_call(
        functools.partial(kernel_fn, steps=steps),
        out_shape=out_shape,
        grid=(b, nhb, nt),
        in_specs=[pl.BlockSpec((1, w), lambda bi, h, t: (0, h)),
                  pl.BlockSpec((1, HG_DIM), lambda bi, h, t: (0, 0)),
                  spec(0), spec(1), spec(2), spec(3)],
        out_specs=out_specs,
        scratch_shapes=[pltpu.VMEM((heads, HG_DIM, HG_DIM) if heads > 1 else (HG_DIM, HG_DIM), F32)],
        compiler_params=_params(("parallel", "parallel", "arbitrary"), block_bytes),
        name=name,
    )(lower, g_out, hg, hg, hg, hg)


def _hgrn2_any(hg, lower, g_out, n_heads):
    heads = min(8, n_heads)
    y, decay = _hgrn2(hg, lower, g_out, n_heads, heads, functools.partial(_hg_kernel, heads=heads), True,
                      "hgrn2")
    return lax.cond(
        jnp.max(decay) < HG_FAST_MAX_DECAY,
        lambda: y,
        lambda: _hgrn2(hg, lower, g_out, n_heads, 1, _hg_seq_kernel, False, "hgrn2_stepwise"))


def _xattn_kernel(q_ref, k_ref, v_ref, o_ref, *, n_heads):
    hd = X_HEAD_DIM
    for h in range(n_heads):
        sl = slice(h * hd, (h + 1) * hd)
        s = lax.dot_general(q_ref[0, :, sl], k_ref[0, :, sl], (((1,), (1,)), ((), ())),
                            preferred_element_type=F32)
        p = jnp.exp(s - jnp.max(s, axis=-1, keepdims=True))
        denom = jnp.sum(p, axis=-1, keepdims=True)
        o = jnp.dot(p.astype(BF16), v_ref[0, :, sl], preferred_element_type=F32)
        o_ref[0, :, sl] = (o / denom).astype(o_ref.dtype)


def _cross_attention(q, k, v, n_heads, name):
    b, s, w = q.shape
    mlen = k.shape[1]
    tq = min(1024, s)
    block_bytes = 2 * _nbytes((tq, w), BF16) + 2 * _nbytes((mlen, w), BF16) + 3 * _nbytes((tq, mlen), F32)
    return pl.pallas_call(
        functools.partial(_xattn_kernel, n_heads=n_heads),
        out_shape=jax.ShapeDtypeStruct((b, s, w), BF16),
        grid=(b, s // tq),
        in_specs=[pl.BlockSpec((1, tq, w), lambda bi, i: (bi, i, 0)),
                  pl.BlockSpec((1, mlen, w), lambda bi, i: (bi, 0, 0)),
                  pl.BlockSpec((1, mlen, w), lambda bi, i: (bi, 0, 0))],
        out_specs=pl.BlockSpec((1, tq, w), lambda bi, i: (bi, i, 0)),
        compiler_params=_params(("parallel", "parallel"), block_bytes),
        name=name,
    )(q, k, v)


def _pack_bf16_pair(lo, hi):
    lo_bits = lax.bitcast_convert_type(lo.astype(BF16).astype(F32), jnp.uint32)
    hi_bits = lax.bitcast_convert_type(hi.astype(BF16).astype(F32), jnp.uint32)
    return (lo_bits >> 16) | (hi_bits & jnp.uint32(0xFFFF0000))


def _unpack_bf16_pair(words):
    lo = lax.bitcast_convert_type(words << 16, F32)
    hi = lax.bitcast_convert_type(words & jnp.uint32(0xFFFF0000), F32)
    return lo, hi


def _router_kernel(h_ref, g_ref, whi_ref, wlo_ref, b_ref, n_ref, rt_ref, *, n_groups, n_experts):
    x = h_ref[...]
    ms = jnp.mean(x * x, axis=-1, keepdims=True)
    n = x * lax.rsqrt(ms + NORM_EPS) * g_ref[...]
    n_hi = n.astype(BF16)
    n_lo = (n - n_hi.astype(F32)).astype(BF16)
    half = n.shape[1] // 2
    n_ref[...] = _pack_bf16_pair(n[:, :half], n[:, half:])
    logits = (jnp.dot(n_hi, whi_ref[...], preferred_element_type=F32)
              + jnp.dot(n_lo, whi_ref[...], preferred_element_type=F32)
              + jnp.dot(n_hi, wlo_ref[...], preferred_element_type=F32) + b_ref[...])
    epg = n_experts // n_groups
    lane = lax.broadcasted_iota(jnp.int32, logits.shape, 1)
    n_lanes = logits.shape[1]

    def top1(vals):
        best = jnp.max(vals, axis=-1, keepdims=True)
        return best, jnp.min(jnp.where(vals == best, lane, n_lanes), axis=-1, keepdims=True)

    is_grp = lane < n_groups
    g_max, grp = top1(jnp.where(is_grp, logits, -jnp.inf))
    grp_w = 1.0 / jnp.sum(jnp.where(is_grp, jnp.exp(logits - g_max), 0.0), axis=-1, keepdims=True)
    first = n_groups + grp * epg
    in_grp = jnp.where(jnp.logical_and(lane >= first, lane < first + epg), logits, -jnp.inf)
    v1, i1 = top1(in_grp)
    v2, i2 = top1(jnp.where(lane == i1, -jnp.inf, in_grp))
    e2 = jnp.exp(v2 - v1)
    p1 = 1.0 / (1.0 + e2)
    cols = [grp_w * p1, grp_w * (e2 * p1), (i1 - n_groups).astype(F32), (i2 - n_groups).astype(F32)]
    rt = jnp.zeros(logits.shape, F32)
    for c, val in enumerate(cols):
        rt = jnp.where(lane == c, val, rt)
    rt_ref[...] = rt


def _router(h, g, w_hi, w_lo, bias, n_groups, n_experts, name):
    m, d = h.shape
    nl = w_hi.shape[1]
    tm = min(256, m)
    block_bytes = 2 * _nbytes((tm, d), F32) + 2 * _nbytes((d, nl), BF16)
    return pl.pallas_call(
        functools.partial(_router_kernel, n_groups=n_groups, n_experts=n_experts),
        out_shape=(jax.ShapeDtypeStruct((m, d // 2), jnp.uint32), jax.ShapeDtypeStruct((m, nl), F32)),
        grid=(m // tm,),
        in_specs=[pl.BlockSpec((tm, d), lambda i: (i, 0)),
                  pl.BlockSpec((1, d), lambda i: (0, 0)),
                  pl.BlockSpec((d, nl), lambda i: (0, 0)),
                  pl.BlockSpec((d, nl), lambda i: (0, 0)),
                  pl.BlockSpec((1, nl), lambda i: (0, 0))],
        out_specs=(pl.BlockSpec((tm, d // 2), lambda i: (i, 0)), pl.BlockSpec((tm, nl), lambda i: (i, 0))),
        compiler_params=_params(("parallel",), block_bytes),
        name=name,
    )(h, g, w_hi, w_lo, bias)


def _row_copies(src_ref, dst_ref, idx_ref, idx0, idx_stride, dst0, n_groups, sem, *, start):
    def body(g, carry):
        for u in range(ROW_COPY_UNROLL):
            r = g * ROW_COPY_UNROLL + u
            cp = pltpu.make_async_copy(src_ref.at[pl.ds(idx_ref[idx0 + r * idx_stride], 1)],
                                       dst_ref.at[pl.ds(dst0 + r, 1)], sem)
            if start:
                cp.start(priority=1)
            else:
                cp.wait()
        return carry

    lax.fori_loop(0, n_groups, body, 0)


def _expert_kernel(tok_ref, be_ref, src0_ref, ng_ref, nreal_ref, nf_ref, wg_ref, wu_ref, wd_ref, o_ref, xbuf, sems):
    i = pl.program_id(0)
    n_real = nreal_ref[0]

    def gather(block, start):
        slot = block % 2
        _row_copies(nf_ref, xbuf.at[slot], tok_ref, src0_ref[block], 1, 0, ng_ref[block], sems.at[slot],
                    start=start)

    @pl.when(i == 0)
    def _():
        xbuf[...] = jnp.zeros(xbuf.shape, xbuf.dtype)

    @pl.when(jnp.logical_and(i == 0, n_real > 0))
    def _():
        gather(i, True)

    @pl.when(i + 1 < n_real)
    def _():
        gather(i + 1, True)

    @pl.when(i < n_real)
    def _():
        gather(i, False)
        x_lo, x_hi = _unpack_bf16_pair(xbuf[i % 2])
        x_lo, x_hi = x_lo.astype(BF16), x_hi.astype(BF16)
        half = x_lo.shape[1]

        def up(w_ref):
            return (jnp.dot(x_lo, w_ref[0, :half, :], preferred_element_type=F32)
                    + jnp.dot(x_hi, w_ref[0, half:, :], preferred_element_type=F32))

        hg = up(wg_ref)
        hid = (hg * jax.nn.sigmoid(hg) * up(wu_ref)).astype(BF16)
        y = jnp.dot(hid, wd_ref[0], preferred_element_type=F32)
        o_ref[...] = _pack_bf16_pair(y[:, :half], y[:, half:])

    @pl.when(i >= n_real)
    def _():
        o_ref[...] = jnp.zeros(o_ref.shape, o_ref.dtype)


def _experts(nf, sorted_tok, blk_expert, blk_src0, blk_groups, n_real, wg, wu, wd, blk, name):
    n_blk = blk_expert.shape[0]
    d = wg.shape[1]
    ff = wg.shape[2]
    half = d // 2
    block_bytes = 3 * _nbytes((d, ff), BF16) + _nbytes((blk, half), jnp.uint32) + 3 * _nbytes((blk, d), F32)

    def weights(i, tk, be, s0, ng, nr):
        return (be[i], 0, 0)

    return pl.pallas_call(
        _expert_kernel,
        out_shape=jax.ShapeDtypeStruct((n_blk * blk, half), jnp.uint32),
        grid_spec=pltpu.PrefetchScalarGridSpec(
            num_scalar_prefetch=5,
            grid=(n_blk,),
            in_specs=[pl.BlockSpec(memory_space=pl.ANY),
                      pl.BlockSpec((1, d, ff), weights),
                      pl.BlockSpec((1, d, ff), weights),
                      pl.BlockSpec((1, ff, d), weights)],
            out_specs=pl.BlockSpec((blk, half), lambda i, tk, be, s0, ng, nr: (i, 0)),
            scratch_shapes=[pltpu.VMEM((2, blk, half), jnp.uint32), pltpu.SemaphoreType.DMA((2,))]),
        compiler_params=_params(("arbitrary",), block_bytes, 2 * _nbytes((blk, half), jnp.uint32)),
        name=name,
    )(sorted_tok, blk_expert, blk_src0, blk_groups, n_real, nf, wg, wu, wd)


def _combine_kernel(slot_ref, h_ref, gate_ref, ys_ref, o_ref, ybuf, sems, *, tm):
    i = pl.program_id(0)

    def gather(block, start):
        s = block % 2
        for k in range(TOP_K):
            _row_copies(ys_ref, ybuf.at[s], slot_ref, block * tm * TOP_K + k, TOP_K, k * tm,
                        tm // ROW_COPY_UNROLL, sems.at[s], start=start)

    @pl.when(i == 0)
    def _():
        gather(i, True)

    @pl.when(i + 1 < pl.num_programs(0))
    def _():
        gather(i + 1, True)

    gather(i, False)
    half = ybuf.shape[2]
    gate = gate_ref[...]
    acc_lo = h_ref[:, :half]
    acc_hi = h_ref[:, half:]
    for k in range(TOP_K):
        y_lo, y_hi = _unpack_bf16_pair(ybuf[i % 2, pl.ds(k * tm, tm), :])
        acc_lo = acc_lo + gate[:, k:k + 1] * y_lo
        acc_hi = acc_hi + gate[:, k:k + 1] * y_hi
    o_ref[:, :half] = acc_lo
    o_ref[:, half:] = acc_hi


def _combine(h, gate, ys, slot, name):
    m, d = h.shape
    tm = min(128, m)
    block_bytes = 2 * _nbytes((tm, d), F32) + TOP_K * _nbytes((tm, d), F32)
    return pl.pallas_call(
        functools.partial(_combine_kernel, tm=tm),
        out_shape=jax.ShapeDtypeStruct((m, d), F32),
        grid_spec=pltpu.PrefetchScalarGridSpec(
            num_scalar_prefetch=1,
            grid=(m // tm,),
            in_specs=[pl.BlockSpec((tm, d), lambda i, sl: (i, 0)),
                      pl.BlockSpec((tm, TOP_K), lambda i, sl: (i, 0)),
                      pl.BlockSpec(memory_space=pl.ANY)],
            out_specs=pl.BlockSpec((tm, d), lambda i, sl: (i, 0)),
            scratch_shapes=[pltpu.VMEM((2, TOP_K * tm, d // 2), jnp.uint32), pltpu.SemaphoreType.DMA((2,))]),
        compiler_params=_params(("arbitrary",), block_bytes),
        name=name,
    )(slot, h, gate, ys)


def _layout(expert, n_experts, blk):
    n_assign = expert.shape[0]
    e_sorted, order = lax.sort_key_val(expert, jnp.arange(n_assign, dtype=jnp.int32))
    experts = jnp.arange(n_experts, dtype=jnp.int32)
    starts = jnp.searchsorted(e_sorted, experts, side='left').astype(jnp.int32)
    counts = jnp.searchsorted(e_sorted, experts, side='right').astype(jnp.int32) - starts
    padded = (counts + blk - 1) // blk * blk
    pad_ends = jnp.cumsum(padded)
    pad_starts = pad_ends - padded
    gap = pad_starts - starts
    gap_step = jnp.concatenate([gap[:1], gap[1:] - gap[:-1]])
    dest = jnp.arange(n_assign, dtype=jnp.int32) + jnp.cumsum(
        jnp.zeros((n_assign + 1,), jnp.int32).at[starts].add(gap_step))[:n_assign]
    _, slot = lax.sort_key_val(order, dest)
    cap = n_assign + n_experts * blk
    n_blk = cap // blk
    blk_row0 = jnp.arange(n_blk, dtype=jnp.int32) * blk
    blk_expert = jnp.minimum(jnp.searchsorted(pad_ends, blk_row0, side='right'), n_experts - 1).astype(jnp.int32)
    n_real = (pad_ends[-1] // blk).astype(jnp.int32)
    blk_src0 = jnp.clip(blk_row0 - gap[blk_expert], 0, n_assign).astype(jnp.int32)
    blk_rows = jnp.clip(counts[blk_expert] - (blk_row0 - pad_starts[blk_expert]), 0, blk)
    blk_rows = jnp.where(jnp.arange(n_blk) < n_real, blk_rows, 0)
    blk_groups = ((blk_rows + ROW_COPY_UNROLL - 1) // ROW_COPY_UNROLL).astype(jnp.int32)
    sorted_tok = jnp.concatenate([order // TOP_K, jnp.zeros((ROW_COPY_UNROLL,), jnp.int32)])
    return sorted_tok, blk_expert, blk_src0, blk_groups, n_real.reshape(1), slot.astype(jnp.int32)


def _diff_lambda_init(layer):
    return 0.8 - 0.6 * math.exp(-0.3 * layer)


def _tile_cols(v, n):
    return jnp.tile(v.reshape(1, -1).astype(F32), (1, n // v.size))


def kernel(x, mem, g_mix, w_in, g_da_q, g_da_k, lam_q1, lam_k1, lam_q2, lam_k2, g_da_sub, hg_lower, g_hg_out,
           w_up_a, w_up_b, w_gate, b_gate, w_out, g_cross, g_mem, w_xq, w_xkv, g_xq, g_xk, w_xo, g_ffn, w_grp,
           b_grp, w_erouter, b_erouter, w_e_gate, w_e_up, w_e_down):
    b, s, d = x.shape
    t = b * s
    depth = g_mix.shape[0]
    da_w = w_up_a.shape[1]
    hg_w = w_up_b.shape[1]
    da_heads = da_w // (2 * DA_HEAD_DIM)
    hg_heads = hg_w // HG_DIM
    x_w = w_xq.shape[2]
    x_heads = x_w // X_HEAD_DIM
    n_groups = w_grp.shape[2]
    n_experts = w_erouter.shape[2]
    moe_blk = min(256, t)

    lower_bounds = jnp.cumsum(jax.nn.softmax(hg_lower.astype(F32), axis=0), axis=0)
    h = x.reshape(t, d)
    mem2 = mem.reshape(b * mem.shape[1], d)
    for l in range(depth):
        lam_init = _diff_lambda_init(l)
        w_in_l = w_in[l]
        n = _rmsnorm(h, g_mix[l], BF16, "rms_mix")
        q_scale = DA_HEAD_DIM ** -0.5 * math.log2(math.e)
        qk_gain = jnp.concatenate([_tile_cols(g_da_q[l], da_w) * q_scale, _tile_cols(g_da_k[l], da_w)], axis=1)
        ff = w_e_gate.shape[3]
        qk, we_gate = _matmul(n, w_in_l, col0=0, ncols=2 * da_w, out_dtype=BF16,
                              epilogue=functools.partial(_ep_group_norm, group=DA_HEAD_DIM),
                              col_extras=[(qk_gain, 0)], side_cast=w_e_gate[l].reshape(n_experts * d, ff),
                              name="proj_qk")
        v = _matmul(n, w_in_l, col0=2 * da_w, ncols=da_w, out_dtype=BF16, name="proj_v")
        hg, we_down = _matmul(n, w_in_l, col0=3 * da_w, ncols=4 * hg_w, out_dtype=F32,
                              side_cast=w_e_down[l].reshape(n_experts * ff, d), name="proj_hg")
        gates, we_up = _matmul(n, w_gate[l], out_dtype=BF16, epilogue=_ep_bias_sigmoid,
                               col_extras=[(b_gate[l].reshape(1, -1).astype(F32), 0)],
                               side_cast=w_e_up[l].reshape(n_experts * d, ff), name="proj_gate")
        lamv = jnp.stack([lam_q1[l], lam_k1[l], lam_q2[l], lam_k2[l]]).astype(F32)
        y_a = _diff_attention_any(qk.reshape(b, s, 2 * da_w), v.reshape(b, s, da_w), lamv,
                                  g_da_sub[l].reshape(1, -1).astype(F32), g_da_q[l], g_da_k[l], q_scale,
                                  da_heads, lam_init)
        y_b = _hgrn2_any(hg.reshape(b, s, 4 * hg_w), lower_bounds[l].reshape(1, hg_w),
                         g_hg_out[l].reshape(1, HG_DIM).astype(F32), hg_heads)
        merged = _merge(y_a.reshape(t, da_w), w_up_a[l], y_b.reshape(t, hg_w), w_up_b[l], gates, "merge")
        h = _matmul(merged, w_out[l], out_dtype=F32, epilogue=_ep_residual,
                    full_extras=[(h, 0)], name="mix_out")
        nc = _rmsnorm(h, g_cross[l], BF16, "rms_cross")
        xq = _matmul(nc, w_xq[l], out_dtype=BF16,
                     epilogue=functools.partial(_ep_group_norm, group=X_HEAD_DIM),
                     col_extras=[(_tile_cols(g_xq[l], x_w) * X_HEAD_DIM ** -0.5, 0)], name="xattn_q")
        nm = _rmsnorm(mem2, g_mem[l], BF16, "rms_mem")
        w_xkv_l = w_xkv[l]
        xk = _matmul(nm, w_xkv_l, col0=0, ncols=x_w, out_dtype=BF16,
                     epilogue=functools.partial(_ep_group_norm, group=X_HEAD_DIM),
                     col_extras=[(_tile_cols(g_xk[l], x_w), 0)], name="xattn_k")
        xv = _matmul(nm, w_xkv_l, col0=x_w, ncols=x_w, out_dtype=BF16, name="xattn_v")
        xo = _cross_attention(xq.reshape(b, s, x_w), xk.reshape(b, -1, x_w), xv.reshape(b, -1, x_w), x_heads,
                              "xattn")
        h = _matmul(xo.reshape(t, x_w), w_xo[l], out_dtype=F32, epilogue=_ep_residual,
                    full_extras=[(h, 0)], name="xattn_out")
        n_logit = -(-(n_groups + n_experts) // LANES) * LANES
        w_r = jnp.zeros((d, n_logit), F32).at[:, :n_groups].set(w_grp[l]).at[:, n_groups:n_groups + n_experts].set(
            w_erouter[l])
        b_r = jnp.zeros((1, n_logit), F32).at[0, :n_groups].set(b_grp[l]).at[0, n_groups:n_groups + n_experts].set(
            b_erouter[l])
        w_r_hi = w_r.astype(BF16)
        w_r_lo = (w_r - w_r_hi.astype(F32)).astype(BF16)
        nf, route = _router(h, g_ffn[l].reshape(1, d).astype(F32), w_r_hi, w_r_lo, b_r, n_groups, n_experts,
                            "router")
        gate = route[:, :TOP_K]
        expert = route[:, TOP_K:2 * TOP_K].astype(jnp.int32).reshape(-1)
        sorted_tok, blk_expert, blk_src0, blk_groups, n_real, slot = _layout(expert, n_experts, moe_blk)
        ys = _experts(nf, sorted_tok, blk_expert, blk_src0, blk_groups, n_real, we_gate.reshape(n_experts, d, ff),
                      we_up.reshape(n_experts, d, ff), we_down.reshape(n_experts, ff, d), moe_blk, "moe_experts")
        h = _combine(h, gate, ys, slot, "moe_combine")
    return h.reshape(b, s, d)
```

```python
import functools
import math

import jax
import jax.numpy as jnp
from jax import lax
from jax.experimental import pallas as pl
from jax.experimental.pallas import tpu as pltpu

F32 = jnp.float32
BF16 = jnp.bfloat16

NORM_EPS = 1e-6
DA_HEAD_DIM = 128
DA_CHUNK = 64
DA_MAX_BOUNDED_LOGIT = 50.0
HG_DIM = 128
HG_CHUNK = 64
HG_FAST_MAX_DECAY = 60.0
X_HEAD_DIM = 256
TOP_K = 2
ROW_COPY_UNROLL = 8
LANES = 128
MASK_VALUE = -1e30
V7X_VMEM_BYTES = 64 * 1024 * 1024
VMEM_CAP_BYTES = V7X_VMEM_BYTES - 6 * 1024 * 1024


def _vmem_limit(block_bytes, scratch_bytes):
    return int(min(VMEM_CAP_BYTES, max(32 * 1024 * 1024, 2 * block_bytes + scratch_bytes + 16 * 1024 * 1024)))


def _params(semantics, block_bytes, scratch_bytes=0):
    return pltpu.CompilerParams(dimension_semantics=semantics,
                                vmem_limit_bytes=_vmem_limit(block_bytes, scratch_bytes))


def _nbytes(shape, dtype):
    return math.prod(shape) * jnp.dtype(dtype).itemsize


def _rms_kernel(x_ref, g_ref, o_ref):
    x = x_ref[...].astype(F32)
    ms = jnp.mean(x * x, axis=-1, keepdims=True)
    o_ref[...] = (x * lax.rsqrt(ms + NORM_EPS) * g_ref[...]).astype(o_ref.dtype)


def _rmsnorm(x, g, out_dtype, name):
    m, d = x.shape
    tm = min(256, m)
    return pl.pallas_call(
        _rms_kernel,
        out_shape=jax.ShapeDtypeStruct((m, d), out_dtype),
        grid=(m // tm,),
        in_specs=[pl.BlockSpec((tm, d), lambda i: (i, 0)), pl.BlockSpec((1, d), lambda i: (0, 0))],
        out_specs=pl.BlockSpec((tm, d), lambda i: (i, 0)),
        compiler_params=_params(("parallel",), _nbytes((tm, d), x.dtype) + _nbytes((tm, d), out_dtype)),
        name=name,
    )(x, g.reshape(1, d).astype(F32))


def _ep_plain(acc, o_ref):
    o_ref[...] = acc.astype(o_ref.dtype)


def _ep_group_norm(acc, o_ref, gain, *, group):
    for c in range(acc.shape[1] // group):
        sl = slice(c * group, (c + 1) * group)
        xg = acc[:, sl]
        ms = jnp.mean(xg * xg, axis=-1, keepdims=True)
        o_ref[:, sl] = (xg * lax.rsqrt(ms + NORM_EPS) * gain[:, sl]).astype(o_ref.dtype)


def _ep_bias_sigmoid(acc, o_ref, bias):
    o_ref[...] = jax.nn.sigmoid(acc + bias).astype(o_ref.dtype)


def _ep_residual(acc, o_ref, res):
    o_ref[...] = (res + acc).astype(o_ref.dtype)


def _row_block(i):
    return jnp.maximum(i - 1, 0)


def _mm_kernel(a_ref, w_ref, *rest, epilogue, side):
    if side:
        *extra, side_src_ref, o_ref, side_out_ref, wb_ref = rest
    else:
        *extra, o_ref, wb_ref = rest
    i = pl.program_id(1)

    @pl.when(i == 0)
    def _():
        wb_ref[...] = w_ref[...].astype(BF16)

    @pl.when(i > 0)
    def _():
        acc = jnp.dot(a_ref[...], wb_ref[...], preferred_element_type=F32)
        epilogue(acc, o_ref, *[e[...] for e in extra])
        if side:
            side_out_ref[...] = side_src_ref[...].astype(BF16)


def _matmul(a, w, *, col0=0, ncols=None, out_dtype, epilogue=_ep_plain, col_extras=(), full_extras=(),
            side_cast=None, name):
    m, k = a.shape
    ncols = w.shape[1] - col0 if ncols is None else ncols
    tm = min(1024, m)
    tn = min(512 if k > 1024 else 1024, ncols)
    while ncols % tn or col0 % tn:
        tn //= 2
    assert m % tm == 0 and tn % LANES == 0
    n_rb = m // tm
    in_specs = [pl.BlockSpec((tm, k), lambda j, i: (_row_block(i), 0)),
                pl.BlockSpec((k, tn), lambda j, i, o=col0 // tn: (0, j + o))]
    operands = [a, w]
    block_bytes = _nbytes((tm, k), a.dtype) + _nbytes((k, tn), w.dtype) + _nbytes((tm, tn), out_dtype)
    for vec, c0 in col_extras:
        assert c0 % tn == 0
        in_specs.append(pl.BlockSpec((1, tn), lambda j, i, o=c0 // tn: (0, j + o)))
        operands.append(vec)
    for arr, c0 in full_extras:
        assert c0 % tn == 0
        in_specs.append(pl.BlockSpec((tm, tn), lambda j, i, o=c0 // tn: (_row_block(i), j + o)))
        operands.append(arr)
        block_bytes += _nbytes((tm, tn), arr.dtype)
    out_shape = jax.ShapeDtypeStruct((m, ncols), out_dtype)
    out_specs = pl.BlockSpec((tm, tn), lambda j, i: (_row_block(i), j))
    if side_cast is not None:
        rows, cols = side_cast.shape
        n_steps = (ncols // tn) * n_rb
        chunk = rows // n_steps
        assert chunk * n_steps == rows and chunk % 16 == 0, (rows, n_steps)
        side_spec = pl.BlockSpec((chunk, cols), lambda j, i: (j * n_rb + _row_block(i), 0))
        in_specs.append(side_spec)
        operands.append(side_cast)
        out_shape = (out_shape, jax.ShapeDtypeStruct((rows, cols), BF16))
        out_specs = (out_specs, side_spec)
        block_bytes += _nbytes((chunk, cols), F32) + _nbytes((chunk, cols), BF16)
    return pl.pallas_call(
        functools.partial(_mm_kernel, epilogue=epilogue, side=side_cast is not None),
        out_shape=out_shape,
        grid=(ncols // tn, n_rb + 1),
        in_specs=in_specs,
        out_specs=out_specs,
        scratch_shapes=[pltpu.VMEM((k, tn), BF16)],
        compiler_params=_params(("parallel", "arbitrary"), block_bytes, _nbytes((k, tn), BF16)),
        name=name,
    )(*operands)


def _merge_kernel(ya_ref, wa_ref, yb_ref, wb_ref, ga_ref, gb_ref, o_ref, wa_bf_ref, wb_bf_ref):
    i = pl.program_id(1)

    @pl.when(i == 0)
    def _():
        wa_bf_ref[...] = wa_ref[...].astype(BF16)
        wb_bf_ref[...] = wb_ref[...].astype(BF16)

    @pl.when(i > 0)
    def _():
        pa = jnp.dot(ya_ref[...], wa_bf_ref[...], preferred_element_type=F32)
        pb = jnp.dot(yb_ref[...], wb_bf_ref[...], preferred_element_type=F32)
        o_ref[...] = (ga_ref[...].astype(F32) * pa + gb_ref[...].astype(F32) * pb).astype(o_ref.dtype)


def _merge(ya, wa, yb, wb, gates, name):
    m, ka = ya.shape
    kb = yb.shape[1]
    d = wa.shape[1]
    tm = min(512, m)
    tn = min(512, d)
    nb = d // tn
    block_bytes = (_nbytes((tm, ka), BF16) + _nbytes((ka, tn), F32) + _nbytes((tm, kb), BF16)
                   + _nbytes((kb, tn), F32) + 3 * _nbytes((tm, tn), BF16))
    return pl.pallas_call(
        _merge_kernel,
        out_shape=jax.ShapeDtypeStruct((m, d), BF16),
        grid=(nb, m // tm + 1),
        in_specs=[pl.BlockSpec((tm, ka), lambda j, i: (_row_block(i), 0)),
                  pl.BlockSpec((ka, tn), lambda j, i: (0, j)),
                  pl.BlockSpec((tm, kb), lambda j, i: (_row_block(i), 0)),
                  pl.BlockSpec((kb, tn), lambda j, i: (0, j)),
                  pl.BlockSpec((tm, tn), lambda j, i: (_row_block(i), j)),
                  pl.BlockSpec((tm, tn), lambda j, i: (_row_block(i), j + nb))],
        out_specs=pl.BlockSpec((tm, tn), lambda j, i: (_row_block(i), j)),
        scratch_shapes=[pltpu.VMEM((ka, tn), BF16), pltpu.VMEM((kb, tn), BF16)],
        compiler_params=_params(("parallel", "arbitrary"), block_bytes, _nbytes((ka + kb, tn), BF16)),
        name=name,
    )(ya, wa, yb, wb, gates, gates)


def _da_kernel(lamv_ref, gsub_ref, q_ref, k_ref, v_ref, o_ref, acc_ref, l_ref, *m_ref, tq, tsub, lam_init,
               bounded):
    qi = pl.program_id(2)
    hd = DA_HEAD_DIM
    m_ref = None if bounded else m_ref[0]
    if not bounded:
        m_ref[...] = jnp.full(m_ref.shape, MASK_VALUE, F32)
    l_ref[...] = jnp.zeros(l_ref.shape, F32)
    acc_ref[...] = jnp.zeros(acc_ref.shape, F32)
    def widen(x, width):
        return jnp.concatenate([x] * (width // LANES), axis=-1)

    def update(row0, nrows, off, width, lead):
        rows = slice(row0, row0 + nrows)
        qq = q_ref[0, rows, :]
        kk = k_ref[0, pl.ds(off, width), :]
        vv = v_ref[0, pl.ds(off, width), :]
        for c in range(2):
            s = lax.dot_general(qq[:, c * hd:(c + 1) * hd], kk[:, c * hd:(c + 1) * hd],
                                (((1,), (1,)), ((), ())), preferred_element_type=F32)
            if lead is not None:
                row_chunk = (lax.broadcasted_iota(jnp.int32, (nrows, width), 0) + lead) // DA_CHUNK
                col_chunk = lax.broadcasted_iota(jnp.int32, (nrows, width), 1) // DA_CHUNK
                s = jnp.where(col_chunk <= row_chunk, s, MASK_VALUE)
            if bounded:
                p = jnp.exp2(s)
            else:
                m_old = m_ref[c, rows]
                m_new = jnp.maximum(m_old, jnp.max(s, axis=-1, keepdims=True))
                p = jnp.exp2(s - widen(m_new, width))
                alpha = jnp.exp2(m_old - m_new)
                m_ref[c, rows] = m_new
            psum = p[:, :LANES]
            for g in range(1, width // LANES):
                psum = psum + p[:, g * LANES:(g + 1) * LANES]
            pv = jnp.dot(p.astype(BF16), vv, preferred_element_type=F32)
            if bounded:
                l_ref[c, rows] = l_ref[c, rows] + psum
                acc_ref[c, rows] = acc_ref[c, rows] + pv
            else:
                l_ref[c, rows] = alpha * l_ref[c, rows] + psum
                acc_ref[c, rows] = widen(alpha, 2 * hd) * acc_ref[c, rows] + pv

    def full_body(j, carry):
        update(0, tq, pl.multiple_of(j * tq, tq), tq, None)
        return carry

    lax.fori_loop(0, qi, full_body, 0)
    for r in range(tq // tsub):
        update(r * tsub, tsub, pl.multiple_of(qi * tq, tq), (r + 1) * tsub, r * tsub)

    lv = lamv_ref[...]
    lam = (jnp.exp(jnp.sum(lv[0:1] * lv[1:2], axis=-1, keepdims=True))
           - jnp.exp(jnp.sum(lv[2:3] * lv[3:4], axis=-1, keepdims=True)) + lam_init)
    l1 = jnp.sum(l_ref[0], axis=-1, keepdims=True)
    l2 = jnp.sum(l_ref[1], axis=-1, keepdims=True)
    o = acc_ref[0] / l1 - lam * (acc_ref[1] / l2)
    ms = jnp.mean(o * o, axis=-1, keepdims=True)
    o_ref[0] = ((o * lax.rsqrt(ms + NORM_EPS) * gsub_ref[...]) * (1.0 - lam_init)).astype(o_ref.dtype)


def _diff_attention(qk, v, lamv, g_sub, n_heads, lam_init, bounded, name):
    b, s, _ = v.shape
    hw = 2 * DA_HEAD_DIM
    tq = min(1024, s)
    tsub = min(256, tq)
    block_bytes = 2 * _nbytes((tq, hw), BF16) + 2 * _nbytes((s, hw), BF16) + 6 * _nbytes((tq, tq), F32)
    stats = [pltpu.VMEM((2, tq, LANES), F32)] * (1 if bounded else 2)
    return pl.pallas_call(
        functools.partial(_da_kernel, tq=tq, tsub=tsub, lam_init=lam_init, bounded=bounded),
        out_shape=jax.ShapeDtypeStruct((b, s, n_heads * hw), BF16),
        grid=(b, n_heads, s // tq),
        in_specs=[pl.BlockSpec((4, DA_HEAD_DIM), lambda bi, h, i: (0, 0)),
                  pl.BlockSpec((1, hw), lambda bi, h, i: (0, 0)),
                  pl.BlockSpec((1, tq, hw), lambda bi, h, i: (bi, i, h)),
                  pl.BlockSpec((1, s, hw), lambda bi, h, i: (bi, 0, n_heads + h)),
                  pl.BlockSpec((1, s, hw), lambda bi, h, i: (bi, 0, h))],
        out_specs=pl.BlockSpec((1, tq, hw), lambda bi, h, i: (bi, i, h)),
        scratch_shapes=[pltpu.VMEM((2, tq, hw), F32)] + stats,
        compiler_params=_params(("parallel", "parallel", "arbitrary"), block_bytes),
        name=name,
    )(lamv, g_sub, qk, qk, v)


def _diff_attention_any(qk, v, lamv, g_sub, g_q, g_k, q_scale, n_heads, lam_init):
    bound = 1.01 * DA_HEAD_DIM * q_scale * jnp.max(jnp.abs(g_q)) * jnp.max(jnp.abs(g_k))
    return lax.cond(
        bound < DA_MAX_BOUNDED_LOGIT,
        lambda: _diff_attention(qk, v, lamv, g_sub, n_heads, lam_init, True, "diff_attn"),
        lambda: _diff_attention(qk, v, lamv, g_sub, n_heads, lam_init, False, "diff_attn_online"))


def _hg_forget(lb_ref, f_ref, rows, sl):
    lb = lb_ref[:, sl]
    return lb + (1.0 - lb) * jax.nn.sigmoid(f_ref[0, rows, sl])


def _hg_finish(o, gout_ref, gate):
    ms = jnp.mean(o * o, axis=-1, keepdims=True)
    return (o * lax.rsqrt(ms + NORM_EPS) * gout_ref[...]) * (gate * jax.nn.sigmoid(gate))


def _hg_kernel(lb_ref, gout_ref, q_ref, f_ref, i_ref, g_ref, o_ref, dec_ref, st_ref, *, heads, steps):
    t = pl.program_id(2)
    ch, hd = HG_CHUNK, HG_DIM

    @pl.when(t == 0)
    def _():
        st_ref[...] = jnp.zeros(st_ref.shape, F32)

    r = lax.broadcasted_iota(jnp.int32, (ch, ch), 0)
    c = lax.broadcasted_iota(jnp.int32, (ch, ch), 1)
    causal = c <= r
    row = lax.broadcasted_iota(jnp.int32, (ch, hd), 0)
    nt = (((1,), (1,)), ((), ()))
    tn = (((0,), (0,)), ((), ()))

    def chunk(rows, sl, st):
        q = q_ref[0, rows, sl]
        f = _hg_forget(lb_ref, f_ref, rows, sl)
        key = 1.0 - f
        lf = jnp.log(f)
        cum = lf
        for sh in (1, 2, 4, 8, 16, 32):
            cum = cum + jnp.where(row >= sh, pltpu.roll(cum, sh, axis=0), 0.0)
        last = cum[ch - 1:ch]
        q_dec = (q * jnp.exp(cum)).astype(BF16)
        k_inv = (key * jnp.exp(-cum)).astype(BF16)
        k_state = (key * jnp.exp(last - cum)).astype(BF16)
        vals = i_ref[0, rows, sl].astype(BF16)
        scores = jnp.where(causal, lax.dot_general(q_dec, k_inv, nt, preferred_element_type=F32), 0.0)
        o = (jnp.dot(scores.astype(BF16), vals, preferred_element_type=F32)
             + lax.dot_general(q_dec, st.astype(BF16), nt, preferred_element_type=F32))
        st_next = jnp.exp(last) * st + lax.dot_general(vals, k_state, tn, preferred_element_type=F32)
        return _hg_finish(o, gout_ref, g_ref[0, rows, sl]), st_next, -last

    worst = jnp.zeros((1, hd), F32)
    for h in range(heads):
        sl = slice(h * hd, (h + 1) * hd)
        st = st_ref[h]
        for ci in range(steps // ch):
            rows = slice(ci * ch, (ci + 1) * ch)
            y, st, decay = chunk(rows, sl, st)
            o_ref[0, rows, sl] = y.astype(o_ref.dtype)
            worst = jnp.maximum(worst, decay)
        st_ref[h] = st
    dec_ref[...] = jnp.broadcast_to(worst, dec_ref.shape)


def _hg_seq_kernel(lb_ref, gout_ref, q_ref, f_ref, i_ref, g_ref, o_ref, st_ref, *, steps):
    t = pl.program_id(2)
    hd = HG_DIM

    @pl.when(t == 0)
    def _():
        st_ref[...] = jnp.zeros(st_ref.shape, F32)

    lane = lax.broadcasted_iota(jnp.int32, (hd, hd), 1)
    sl = slice(0, hd)

    def block(bi, carry):
        rows = pl.ds(pl.multiple_of(bi * hd, hd), hd)
        q = q_ref[0, rows, :]
        f = _hg_forget(lb_ref, f_ref, rows, sl)
        key = 1.0 - f
        vals_t = i_ref[0, rows, :].T
        st = st_ref[...]
        out_t = jnp.zeros((hd, hd), F32)
        for s in range(hd):
            st = f[s:s + 1] * st + vals_t[:, s:s + 1] * key[s:s + 1]
            out_t = jnp.where(lane == s, jnp.sum(st * q[s:s + 1], axis=1, keepdims=True), out_t)
        st_ref[...] = st
        o_ref[0, rows, :] = _hg_finish(out_t.T, gout_ref, g_ref[0, rows, :]).astype(o_ref.dtype)
        return carry

    lax.fori_loop(0, steps // hd, block, 0)


def _hgrn2(hg, lower, g_out, n_heads, heads, kernel_fn, with_decay, name):
    b, s, _ = hg.shape
    steps = min(512, s)
    nhb = n_heads // heads
    nt = s // steps
    w = heads * HG_DIM
    block_bytes = 4 * _nbytes((steps, w), F32) + _nbytes((steps, w), BF16)

    def spec(part):
        return pl.BlockSpec((1, steps, w), lambda bi, h, t, p=part: (bi, t, p * nhb + h))

    out_shape = jax.ShapeDtypeStruct((b, s, n_heads * HG_DIM), BF16)
    out_specs = pl.BlockSpec((1, steps, w), lambda bi, h, t: (bi, t, h))
    if with_decay:
        sub = 8
        out_shape = (out_shape, jax.ShapeDtypeStruct((b * nhb * nt * sub, LANES), F32))
        out_specs = (out_specs, pl.BlockSpec((sub, LANES), lambda bi, h, t: ((bi * nhb + h) * nt + t, 0)))
    return pl.pallas_call(
        functools.partial(kernel_fn, steps=steps),
        out_shape=out_shape,
        grid=(b, nhb, nt),
        in_specs=[pl.BlockSpec((1, w), lambda bi, h, t: (0, h)),
                  pl.BlockSpec((1, HG_DIM), lambda bi, h, t: (0, 0)),
                  spec(0), spec(1), spec(2), spec(3)],
        out_specs=out_specs,
        scratch_shapes=[pltpu.VMEM((heads, HG_DIM, HG_DIM) if heads > 1 else (HG_DIM, HG_DIM), F32)],
        compiler_params=_params(("parallel", "parallel", "arbitrary"), block_bytes),
        name=name,
    )(lower, g_out, hg, hg, hg, hg)


def _hgrn2_any(hg, lower, g_out, n_heads):
    heads = min(8, n_heads)
    y, decay = _hgrn2(hg, lower, g_out, n_heads, heads, functools.partial(_hg_kernel, heads=heads), True,
                      "hgrn2")
    return lax.cond(
        jnp.max(decay) < HG_FAST_MAX_DECAY,
        lambda: y,
        lambda: _hgrn2(hg, lower, g_out, n_heads, 1, _hg_seq_kernel, False, "hgrn2_stepwise"))


def _xattn_kernel(q_ref, k_ref, v_ref, o_ref, *, n_heads):
    hd = X_HEAD_DIM
    for h in range(n_heads):
        sl = slice(h * hd, (h + 1) * hd)
        s = lax.dot_general(q_ref[0, :, sl], k_ref[0, :, sl], (((1,), (1,)), ((), ())),
                            preferred_element_type=F32)
        p = jnp.exp(s - jnp.max(s, axis=-1, keepdims=True))
        denom = jnp.sum(p, axis=-1, keepdims=True)
        o = jnp.dot(p.astype(BF16), v_ref[0, :, sl], preferred_element_type=F32)
        o_ref[0, :, sl] = (o / denom).astype(o_ref.dtype)


def _cross_attention(q, k, v, n_heads, name):
    b, s, w = q.shape
    mlen = k.shape[1]
    tq = min(1024, s)
    block_bytes = 2 * _nbytes((tq, w), BF16) + 2 * _nbytes((mlen, w), BF16) + 3 * _nbytes((tq, mlen), F32)
    return pl.pallas_call(
        functools.partial(_xattn_kernel, n_heads=n_heads),
        out_shape=jax.ShapeDtypeStruct((b, s, w), BF16),
        grid=(b, s // tq),
        in_specs=[pl.BlockSpec((1, tq, w), lambda bi, i: (bi, i, 0)),
                  pl.BlockSpec((1, mlen, w), lambda bi, i: (bi, 0, 0)),
                  pl.BlockSpec((1, mlen, w), lambda bi, i: (bi, 0, 0))],
        out_specs=pl.BlockSpec((1, tq, w), lambda bi, i: (bi, i, 0)),
        compiler_params=_params(("parallel", "parallel"), block_bytes),
        name=name,
    )(q, k, v)


def _pack_bf16_pair(lo, hi):
    lo_bits = lax.bitcast_convert_type(lo.astype(BF16).astype(F32), jnp.uint32)
    hi_bits = lax.bitcast_convert_type(hi.astype(BF16).astype(F32), jnp.uint32)
    return (lo_bits >> 16) | (hi_bits & jnp.uint32(0xFFFF0000))


def _unpack_bf16_pair(words):
    lo = lax.bitcast_convert_type(words << 16, F32)
    hi = lax.bitcast_convert_type(words & jnp.uint32(0xFFFF0000), F32)
    return lo, hi


def _router_kernel(xo_ref, wo_ref, res_ref, g_ref, whi_ref, wlo_ref, b_ref, h_ref, n_ref, rt_ref, *, n_groups,
                   n_experts):
    x = res_ref[...] + jnp.dot(xo_ref[...], wo_ref[...], preferred_element_type=F32)
    h_ref[...] = x
    ms = jnp.mean(x * x, axis=-1, keepdims=True)
    n = x * lax.rsqrt(ms + NORM_EPS) * g_ref[...]
    n_hi = n.astype(BF16)
    n_lo = (n - n_hi.astype(F32)).astype(BF16)
    half = n.shape[1] // 2
    n_ref[...] = _pack_bf16_pair(n[:, :half], n[:, half:])
    logits = (jnp.dot(n_hi, whi_ref[...], preferred_element_type=F32)
              + jnp.dot(n_lo, whi_ref[...], preferred_element_type=F32)
              + jnp.dot(n_hi, wlo_ref[...], preferred_element_type=F32) + b_ref[...])
    epg = n_experts // n_groups
    lane = lax.broadcasted_iota(jnp.int32, logits.shape, 1)
    n_lanes = logits.shape[1]

    def top1(vals):
        best = jnp.max(vals, axis=-1, keepdims=True)
        return best, jnp.min(jnp.where(vals == best, lane, n_lanes), axis=-1, keepdims=True)

    is_grp = lane < n_groups
    g_max, grp = top1(jnp.where(is_grp, logits, -jnp.inf))
    grp_w = 1.0 / jnp.sum(jnp.where(is_grp, jnp.exp(logits - g_max), 0.0), axis=-1, keepdims=True)
    first = n_groups + grp * epg
    in_grp = jnp.where(jnp.logical_and(lane >= first, lane < first + epg), logits, -jnp.inf)
    v1, i1 = top1(in_grp)
    v2, i2 = top1(jnp.where(lane == i1, -jnp.inf, in_grp))
    e2 = jnp.exp(v2 - v1)
    p1 = 1.0 / (1.0 + e2)
    cols = [grp_w * p1, grp_w * (e2 * p1), (i1 - n_groups).astype(F32), (i2 - n_groups).astype(F32)]
    rt = jnp.zeros(logits.shape, F32)
    for c, val in enumerate(cols):
        rt = jnp.where(lane == c, val, rt)
    rt_ref[...] = rt


def _xout_router(xo, w_o, res, g, w_hi, w_lo, bias, n_groups, n_experts, name):
    m, d = res.shape
    kx = xo.shape[1]
    nl = w_hi.shape[1]
    tm = min(256, m)
    block_bytes = (_nbytes((tm, kx), BF16) + _nbytes((kx, d), BF16) + 3 * _nbytes((tm, d), F32)
                   + 2 * _nbytes((d, nl), BF16))
    return pl.pallas_call(
        functools.partial(_router_kernel, n_groups=n_groups, n_experts=n_experts),
        out_shape=(jax.ShapeDtypeStruct((m, d), F32), jax.ShapeDtypeStruct((m, d // 2), jnp.uint32),
                   jax.ShapeDtypeStruct((m, nl), F32)),
        grid=(m // tm,),
        in_specs=[pl.BlockSpec((tm, kx), lambda i: (i, 0)),
                  pl.BlockSpec((kx, d), lambda i: (0, 0)),
                  pl.BlockSpec((tm, d), lambda i: (i, 0)),
                  pl.BlockSpec((1, d), lambda i: (0, 0)),
                  pl.BlockSpec((d, nl), lambda i: (0, 0)),
                  pl.BlockSpec((d, nl), lambda i: (0, 0)),
                  pl.BlockSpec((1, nl), lambda i: (0, 0))],
        out_specs=(pl.BlockSpec((tm, d), lambda i: (i, 0)), pl.BlockSpec((tm, d // 2), lambda i: (i, 0)),
                   pl.BlockSpec((tm, nl), lambda i: (i, 0))),
        compiler_params=_params(("parallel",), block_bytes),
        name=name,
    )(xo, w_o, res, g, w_hi, w_lo, bias)


def _row_copies(src_ref, dst_ref, idx_ref, idx0, idx_stride, dst0, n_groups, sem, *, start):
    def body(g, carry):
        for u in range(ROW_COPY_UNROLL):
            r = g * ROW_COPY_UNROLL + u
            cp = pltpu.make_async_copy(src_ref.at[pl.ds(idx_ref[idx0 + r * idx_stride], 1)],
                                       dst_ref.at[pl.ds(dst0 + r, 1)], sem)
            if start:
                cp.start(priority=1)
            else:
                cp.wait()
        return carry

    lax.fori_loop(0, n_groups, body, 0)


def _expert_kernel(tok_ref, be_ref, src0_ref, ng_ref, nreal_ref, nf_ref, wg_ref, wu_ref, wd_ref, o_ref, xbuf, sems):
    i = pl.program_id(0)
    n_real = nreal_ref[0]

    def gather(block, start):
        slot = block % 2
        _row_copies(nf_ref, xbuf.at[slot], tok_ref, src0_ref[block], 1, 0, ng_ref[block], sems.at[slot],
                    start=start)

    @pl.when(i == 0)
    def _():
        xbuf[...] = jnp.zeros(xbuf.shape, xbuf.dtype)

    @pl.when(jnp.logical_and(i == 0, n_real > 0))
    def _():
        gather(i, True)

    @pl.when(i + 1 < n_real)
    def _():
        gather(i + 1, True)

    @pl.when(i < n_real)
    def _():
        gather(i, False)
        x_lo, x_hi = _unpack_bf16_pair(xbuf[i % 2])
        x_lo, x_hi = x_lo.astype(BF16), x_hi.astype(BF16)
        half = x_lo.shape[1]

        def up(w_ref):
            return (jnp.dot(x_lo, w_ref[0, :half, :], preferred_element_type=F32)
                    + jnp.dot(x_hi, w_ref[0, half:, :], preferred_element_type=F32))

        hg = up(wg_ref)
        hid = (hg * jax.nn.sigmoid(hg) * up(wu_ref)).astype(BF16)
        y = jnp.dot(hid, wd_ref[0], preferred_element_type=F32)
        o_ref[...] = _pack_bf16_pair(y[:, :half], y[:, half:])

    @pl.when(i >= n_real)
    def _():
        o_ref[...] = jnp.zeros(o_ref.shape, o_ref.dtype)


def _experts(nf, sorted_tok, blk_expert, blk_src0, blk_groups, n_real, wg, wu, wd, blk, name):
    n_blk = blk_expert.shape[0]
    d = wg.shape[1]
    ff = wg.shape[2]
    half = d // 2
    block_bytes = 3 * _nbytes((d, ff), BF16) + _nbytes((blk, half), jnp.uint32) + 3 * _nbytes((blk, d), F32)

    def weights(i, tk, be, s0, ng, nr):
        return (be[i], 0, 0)

    return pl.pallas_call(
        _expert_kernel,
        out_shape=jax.ShapeDtypeStruct((n_blk * blk, half), jnp.uint32),
        grid_spec=pltpu.PrefetchScalarGridSpec(
            num_scalar_prefetch=5,
            grid=(n_blk,),
            in_specs=[pl.BlockSpec(memory_space=pl.ANY),
                      pl.BlockSpec((1, d, ff), weights),
                      pl.BlockSpec((1, d, ff), weights),
                      pl.BlockSpec((1, ff, d), weights)],
            out_specs=pl.BlockSpec((blk, half), lambda i, tk, be, s0, ng, nr: (i, 0)),
            scratch_shapes=[pltpu.VMEM((2, blk, half), jnp.uint32), pltpu.SemaphoreType.DMA((2,))]),
        compiler_params=_params(("arbitrary",), block_bytes, 2 * _nbytes((blk, half), jnp.uint32)),
        name=name,
    )(sorted_tok, blk_expert, blk_src0, blk_groups, n_real, nf, wg, wu, wd)


def _combine_kernel(slot_ref, h_ref, gate_ref, ys_ref, o_ref, ybuf, sems, *, tm):
    i = pl.program_id(0)

    def gather(block, start):
        s = block % 2
        for k in range(TOP_K):
            _row_copies(ys_ref, ybuf.at[s], slot_ref, block * tm * TOP_K + k, TOP_K, k * tm,
                        tm // ROW_COPY_UNROLL, sems.at[s], start=start)

    @pl.when(i == 0)
    def _():
        gather(i, True)

    @pl.when(i + 1 < pl.num_programs(0))
    def _():
        gather(i + 1, True)

    gather(i, False)
    half = ybuf.shape[2]
    gate = gate_ref[...]
    acc_lo = h_ref[:, :half]
    acc_hi = h_ref[:, half:]
    for k in range(TOP_K):
        y_lo, y_hi = _unpack_bf16_pair(ybuf[i % 2, pl.ds(k * tm, tm), :])
        acc_lo = acc_lo + gate[:, k:k + 1] * y_lo
        acc_hi = acc_hi + gate[:, k:k + 1] * y_hi
    o_ref[:, :half] = acc_lo
    o_ref[:, half:] = acc_hi


def _combine(h, gate, ys, slot, name):
    m, d = h.shape
    tm = min(128, m)
    block_bytes = 2 * _nbytes((tm, d), F32) + TOP_K * _nbytes((tm, d), F32)
    return pl.pallas_call(
        functools.partial(_combine_kernel, tm=tm),
        out_shape=jax.ShapeDtypeStruct((m, d), F32),
        grid_spec=pltpu.PrefetchScalarGridSpec(
            num_scalar_prefetch=1,
            grid=(m // tm,),
            in_specs=[pl.BlockSpec((tm, d), lambda i, sl: (i, 0)),
                      pl.BlockSpec((tm, TOP_K), lambda i, sl: (i, 0)),
                      pl.BlockSpec(memory_space=pl.ANY)],
            out_specs=pl.BlockSpec((tm, d), lambda i, sl: (i, 0)),
            scratch_shapes=[pltpu.VMEM((2, TOP_K * tm, d // 2), jnp.uint32), pltpu.SemaphoreType.DMA((2,))]),
        compiler_params=_params(("arbitrary",), block_bytes),
        name=name,
    )(slot, h, gate, ys)


def _layout(expert, n_experts, blk):
    n_assign = expert.shape[0]
    e_sorted, order = lax.sort_key_val(expert, jnp.arange(n_assign, dtype=jnp.int32))
    experts = jnp.arange(n_experts, dtype=jnp.int32)
    starts = jnp.searchsorted(e_sorted, experts, side='left').astype(jnp.int32)
    counts = jnp.searchsorted(e_sorted, experts, side='right').astype(jnp.int32) - starts
    padded = (counts + blk - 1) // blk * blk
    pad_ends = jnp.cumsum(padded)
    pad_starts = pad_ends - padded
    gap = pad_starts - starts
    gap_step = jnp.concatenate([gap[:1], gap[1:] - gap[:-1]])
    dest = jnp.arange(n_assign, dtype=jnp.int32) + jnp.cumsum(
        jnp.zeros((n_assign + 1,), jnp.int32).at[starts].add(gap_step))[:n_assign]
    _, slot = lax.sort_key_val(order, dest)
    cap = n_assign + n_experts * blk
    n_blk = cap // blk
    blk_row0 = jnp.arange(n_blk, dtype=jnp.int32) * blk
    blk_expert = jnp.minimum(jnp.searchsorted(pad_ends, blk_row0, side='right'), n_experts - 1).astype(jnp.int32)
    n_real = (pad_ends[-1] // blk).astype(jnp.int32)
    blk_src0 = jnp.clip(blk_row0 - gap[blk_expert], 0, n_assign).astype(jnp.int32)
    blk_rows = jnp.clip(counts[blk_expert] - (blk_row0 - pad_starts[blk_expert]), 0, blk)
    blk_rows = jnp.where(jnp.arange(n_blk) < n_real, blk_rows, 0)
    blk_groups = ((blk_rows + ROW_COPY_UNROLL - 1) // ROW_COPY_UNROLL).astype(jnp.int32)
    sorted_tok = jnp.concatenate([order // TOP_K, jnp.zeros((ROW_COPY_UNROLL,), jnp.int32)])
    return sorted_tok, blk_expert, blk_src0, blk_groups, n_real.reshape(1), slot.astype(jnp.int32)


def _diff_lambda_init(layer):
    return 0.8 - 0.6 * math.exp(-0.3 * layer)


def _tile_cols(v, n):
    return jnp.tile(v.reshape(1, -1).astype(F32), (1, n // v.size))


def kernel(x, mem, g_mix, w_in, g_da_q, g_da_k, lam_q1, lam_k1, lam_q2, lam_k2, g_da_sub, hg_lower, g_hg_out,
           w_up_a, w_up_b, w_gate, b_gate, w_out, g_cross, g_mem, w_xq, w_xkv, g_xq, g_xk, w_xo, g_ffn, w_grp,
           b_grp, w_erouter, b_erouter, w_e_gate, w_e_up, w_e_down):
    b, s, d = x.shape
    t = b * s
    depth = g_mix.shape[0]
    da_w = w_up_a.shape[1]
    hg_w = w_up_b.shape[1]
    da_heads = da_w // (2 * DA_HEAD_DIM)
    hg_heads = hg_w // HG_DIM
    x_w = w_xq.shape[2]
    x_heads = x_w // X_HEAD_DIM
    n_groups = w_grp.shape[2]
    n_experts = w_erouter.shape[2]
    moe_blk = min(256, t)

    lower_bounds = jnp.cumsum(jax.nn.softmax(hg_lower.astype(F32), axis=0), axis=0)
    h = x.reshape(t, d)
    mem2 = mem.reshape(b * mem.shape[1], d)
    for l in range(depth):
        lam_init = _diff_lambda_init(l)
        w_in_l = w_in[l]
        n = _rmsnorm(h, g_mix[l], BF16, "rms_mix")
        q_scale = DA_HEAD_DIM ** -0.5 * math.log2(math.e)
        qk_gain = jnp.concatenate([_tile_cols(g_da_q[l], da_w) * q_scale, _tile_cols(g_da_k[l], da_w)], axis=1)
        ff = w_e_gate.shape[3]
        qk, we_gate = _matmul(n, w_in_l, col0=0, ncols=2 * da_w, out_dtype=BF16,
                              epilogue=functools.partial(_ep_group_norm, group=DA_HEAD_DIM),
                              col_extras=[(qk_gain, 0)], side_cast=w_e_gate[l].reshape(n_experts * d, ff),
                              name="proj_qk")
        v = _matmul(n, w_in_l, col0=2 * da_w, ncols=da_w, out_dtype=BF16, name="proj_v")
        hg, we_down = _matmul(n, w_in_l, col0=3 * da_w, ncols=4 * hg_w, out_dtype=F32,
                              side_cast=w_e_down[l].reshape(n_experts * ff, d), name="proj_hg")
        gates, we_up = _matmul(n, w_gate[l], out_dtype=BF16, epilogue=_ep_bias_sigmoid,
                               col_extras=[(b_gate[l].reshape(1, -1).astype(F32), 0)],
                               side_cast=w_e_up[l].reshape(n_experts * d, ff), name="proj_gate")
        lamv = jnp.stack([lam_q1[l], lam_k1[l], lam_q2[l], lam_k2[l]]).astype(F32)
        y_a = _diff_attention_any(qk.reshape(b, s, 2 * da_w), v.reshape(b, s, da_w), lamv,
                                  g_da_sub[l].reshape(1, -1).astype(F32), g_da_q[l], g_da_k[l], q_scale,
                                  da_heads, lam_init)
        y_b = _hgrn2_any(hg.reshape(b, s, 4 * hg_w), lower_bounds[l].reshape(1, hg_w),
                         g_hg_out[l].reshape(1, HG_DIM).astype(F32), hg_heads)
        merged = _merge(y_a.reshape(t, da_w), w_up_a[l], y_b.reshape(t, hg_w), w_up_b[l], gates, "merge")
        h = _matmul(merged, w_out[l], out_dtype=F32, epilogue=_ep_residual,
                    full_extras=[(h, 0)], name="mix_out")
        nc = _rmsnorm(h, g_cross[l], BF16, "rms_cross")
        xq, w_xo_bf = _matmul(nc, w_xq[l], out_dtype=BF16,
                              epilogue=functools.partial(_ep_group_norm, group=X_HEAD_DIM),
                              col_extras=[(_tile_cols(g_xq[l], x_w) * X_HEAD_DIM ** -0.5, 0)],
                              side_cast=w_xo[l], name="xattn_q")
        nm = _rmsnorm(mem2, g_mem[l], BF16, "rms_mem")
        w_xkv_l = w_xkv[l]
        xk = _matmul(nm, w_xkv_l, col0=0, ncols=x_w, out_dtype=BF16,
                     epilogue=functools.partial(_ep_group_norm, group=X_HEAD_DIM),
                     col_extras=[(_tile_cols(g_xk[l], x_w), 0)], name="xattn_k")
        xv = _matmul(nm, w_xkv_l, col0=x_w, ncols=x_w, out_dtype=BF16, name="xattn_v")
        xo = _cross_attention(xq.reshape(b, s, x_w), xk.reshape(b, -1, x_w), xv.reshape(b, -1, x_w), x_heads,
                              "xattn")
        n_logit = -(-(n_groups + n_experts) // LANES) * LANES
        w_r = jnp.zeros((d, n_logit), F32).at[:, :n_groups].set(w_grp[l]).at[:, n_groups:n_groups + n_experts].set(
            w_erouter[l])
        b_r = jnp.zeros((1, n_logit), F32).at[0, :n_groups].set(b_grp[l]).at[0, n_groups:n_groups + n_experts].set(
            b_erouter[l])
        w_r_hi = w_r.astype(BF16)
        w_r_lo = (w_r - w_r_hi.astype(F32)).astype(BF16)
        h, nf, route = _xout_router(xo.reshape(t, x_w), w_xo_bf, h, g_ffn[l].reshape(1, d).astype(F32), w_r_hi,
                                    w_r_lo, b_r, n_groups, n_experts, "xattn_out_router")
        gate = route[:, :TOP_K]
        expert = route[:, TOP_K:2 * TOP_K].astype(jnp.int32).reshape(-1)
        sorted_tok, blk_expert, blk_src0, blk_groups, n_real, slot = _layout(expert, n_experts, moe_blk)
        ys = _experts(nf, sorted_tok, blk_expert, blk_src0, blk_groups, n_real, we_gate.reshape(n_experts, d, ff),
                      we_up.reshape(n_experts, d, ff), we_down.reshape(n_experts, ff, d), moe_blk, "moe_experts")
        h = _combine(h, gate, ys, slot, "moe_combine")
    return h.reshape(b, s, d)
```

```python
import functools
import math

import jax
import jax.numpy as jnp
from jax import lax
from jax.experimental import pallas as pl
from jax.experimental.pallas import tpu as pltpu

F32 = jnp.float32
BF16 = jnp.bfloat16

NORM_EPS = 1e-6
DA_HEAD_DIM = 128
DA_CHUNK = 64
DA_MAX_BOUNDED_LOGIT = 50.0
HG_DIM = 128
HG_CHUNK = 64
HG_FAST_MAX_DECAY = 60.0
X_HEAD_DIM = 256
TOP_K = 2
ROW_COPY_UNROLL = 8
LANES = 128
MASK_VALUE = -1e30
V7X_VMEM_BYTES = 64 * 1024 * 1024
VMEM_CAP_BYTES = V7X_VMEM_BYTES - 6 * 1024 * 1024


def _vmem_limit(block_bytes, scratch_bytes):
    return int(min(VMEM_CAP_BYTES, max(32 * 1024 * 1024, 2 * block_bytes + scratch_bytes + 16 * 1024 * 1024)))


def _params(semantics, block_bytes, scratch_bytes=0):
    return pltpu.CompilerParams(dimension_semantics=semantics,
                                vmem_limit_bytes=_vmem_limit(block_bytes, scratch_bytes))


def _nbytes(shape, dtype):
    return math.prod(shape) * jnp.dtype(dtype).itemsize


def _rms_kernel(x_ref, g_ref, o_ref):
    x = x_ref[...].astype(F32)
    ms = jnp.mean(x * x, axis=-1, keepdims=True)
    o_ref[...] = (x * lax.rsqrt(ms + NORM_EPS) * g_ref[...]).astype(o_ref.dtype)


def _rmsnorm(x, g, out_dtype, name):
    m, d = x.shape
    tm = min(256, m)
    return pl.pallas_call(
        _rms_kernel,
        out_shape=jax.ShapeDtypeStruct((m, d), out_dtype),
        grid=(m // tm,),
        in_specs=[pl.BlockSpec((tm, d), lambda i: (i, 0)), pl.BlockSpec((1, d), lambda i: (0, 0))],
        out_specs=pl.BlockSpec((tm, d), lambda i: (i, 0)),
        compiler_params=_params(("parallel",), _nbytes((tm, d), x.dtype) + _nbytes((tm, d), out_dtype)),
        name=name,
    )(x, g.reshape(1, d).astype(F32))


def _ep_plain(acc, o_ref):
    o_ref[...] = acc.astype(o_ref.dtype)


def _ep_group_norm(acc, o_ref, gain, *, group):
    for c in range(acc.shape[1] // group):
        sl = slice(c * group, (c + 1) * group)
        xg = acc[:, sl]
        ms = jnp.mean(xg * xg, axis=-1, keepdims=True)
        o_ref[:, sl] = (xg * lax.rsqrt(ms + NORM_EPS) * gain[:, sl]).astype(o_ref.dtype)


def _ep_bias_sigmoid(acc, o_ref, bias):
    o_ref[...] = jax.nn.sigmoid(acc + bias).astype(o_ref.dtype)


def _ep_residual(acc, o_ref, res):
    o_ref[...] = (res + acc).astype(o_ref.dtype)


def _row_block(i):
    return jnp.maximum(i - 1, 0)


def _mm_kernel(a_ref, w_ref, *rest, epilogue, side):
    if side:
        *extra, side_src_ref, o_ref, side_out_ref, wb_ref = rest
    else:
        *extra, o_ref, wb_ref = rest
    i = pl.program_id(1)

    @pl.when(i == 0)
    def _():
        wb_ref[...] = w_ref[...].astype(BF16)

    @pl.when(i > 0)
    def _():
        acc = jnp.dot(a_ref[...], wb_ref[...], preferred_element_type=F32)
        epilogue(acc, o_ref, *[e[...] for e in extra])
        if side:
            side_out_ref[...] = side_src_ref[...].astype(BF16)


def _matmul(a, w, *, col0=0, ncols=None, out_dtype, epilogue=_ep_plain, col_extras=(), full_extras=(),
            side_cast=None, name):
    m, k = a.shape
    ncols = w.shape[1] - col0 if ncols is None else ncols
    tm = min(1024, m)
    tn = min(512 if k > 1024 else 1024, ncols)
    while ncols % tn or col0 % tn:
        tn //= 2
    assert m % tm == 0 and tn % LANES == 0
    n_rb = m // tm
    in_specs = [pl.BlockSpec((tm, k), lambda j, i: (_row_block(i), 0)),
                pl.BlockSpec((k, tn), lambda j, i, o=col0 // tn: (0, j + o))]
    operands = [a, w]
    block_bytes = _nbytes((tm, k), a.dtype) + _nbytes((k, tn), w.dtype) + _nbytes((tm, tn), out_dtype)
    for vec, c0 in col_extras:
        assert c0 % tn == 0
        in_specs.append(pl.BlockSpec((1, tn), lambda j, i, o=c0 // tn: (0, j + o)))
        operands.append(vec)
    for arr, c0 in full_extras:
        assert c0 % tn == 0
        in_specs.append(pl.BlockSpec((tm, tn), lambda j, i, o=c0 // tn: (_row_block(i), j + o)))
        operands.append(arr)
        block_bytes += _nbytes((tm, tn), arr.dtype)
    out_shape = jax.ShapeDtypeStruct((m, ncols), out_dtype)
    out_specs = pl.BlockSpec((tm, tn), lambda j, i: (_row_block(i), j))
    if side_cast is not None:
        rows, cols = side_cast.shape
        n_steps = (ncols // tn) * n_rb
        chunk = rows // n_steps
        assert chunk * n_steps == rows and chunk % 16 == 0, (rows, n_steps)
        side_spec = pl.BlockSpec((chunk, cols), lambda j, i: (j * n_rb + _row_block(i), 0))
        in_specs.append(side_spec)
        operands.append(side_cast)
        out_shape = (out_shape, jax.ShapeDtypeStruct((rows, cols), BF16))
        out_specs = (out_specs, side_spec)
        block_bytes += _nbytes((chunk, cols), F32) + _nbytes((chunk, cols), BF16)
    return pl.pallas_call(
        functools.partial(_mm_kernel, epilogue=epilogue, side=side_cast is not None),
        out_shape=out_shape,
        grid=(ncols // tn, n_rb + 1),
        in_specs=in_specs,
        out_specs=out_specs,
        scratch_shapes=[pltpu.VMEM((k, tn), BF16)],
        compiler_params=_params(("parallel", "arbitrary"), block_bytes, _nbytes((k, tn), BF16)),
        name=name,
    )(*operands)


def _nmm_kernel(a_ref, g_ref, w_ref, gain_ref, side_src_ref, o_ref, side_out_ref, *, epilogue):
    a = a_ref[...]
    ms = jnp.mean(a * a, axis=-1, keepdims=True)
    n = (a * lax.rsqrt(ms + NORM_EPS) * g_ref[...]).astype(BF16)
    epilogue(jnp.dot(n, w_ref[...], preferred_element_type=F32), o_ref, gain_ref[...])
    side_out_ref[...] = side_src_ref[...].astype(BF16)


def _norm_matmul(a, g, w, gain, side_cast, *, out_dtype, epilogue, name):
    m, k = a.shape
    n = w.shape[1]
    tm = min(512, m)
    n_steps = m // tm
    rows, cols = side_cast.shape
    chunk = rows // n_steps
    assert m % tm == 0 and chunk * n_steps == rows and chunk % 16 == 0
    side_spec = pl.BlockSpec((chunk, cols), lambda i: (i, 0))
    block_bytes = (_nbytes((tm, k), F32) + _nbytes((k, n), BF16) + _nbytes((tm, n), out_dtype)
                   + _nbytes((chunk, cols), F32) + _nbytes((chunk, cols), BF16))
    return pl.pallas_call(
        functools.partial(_nmm_kernel, epilogue=epilogue),
        out_shape=(jax.ShapeDtypeStruct((m, n), out_dtype), jax.ShapeDtypeStruct((rows, cols), BF16)),
        grid=(n_steps,),
        in_specs=[pl.BlockSpec((tm, k), lambda i: (i, 0)),
                  pl.BlockSpec((1, k), lambda i: (0, 0)),
                  pl.BlockSpec((k, n), lambda i: (0, 0)),
                  pl.BlockSpec((1, n), lambda i: (0, 0)),
                  side_spec],
        out_specs=(pl.BlockSpec((tm, n), lambda i: (i, 0)), side_spec),
        compiler_params=_params(("parallel",), block_bytes),
        name=name,
    )(a, g, w, gain, side_cast)


def _merge_kernel(ya_ref, wa_ref, yb_ref, wb_ref, ga_ref, gb_ref, o_ref, wa_bf_ref, wb_bf_ref):
    i = pl.program_id(1)

    @pl.when(i == 0)
    def _():
        wa_bf_ref[...] = wa_ref[...].astype(BF16)
        wb_bf_ref[...] = wb_ref[...].astype(BF16)

    @pl.when(i > 0)
    def _():
        pa = jnp.dot(ya_ref[...], wa_bf_ref[...], preferred_element_type=F32)
        pb = jnp.dot(yb_ref[...], wb_bf_ref[...], preferred_element_type=F32)
        o_ref[...] = (ga_ref[...].astype(F32) * pa + gb_ref[...].astype(F32) * pb).astype(o_ref.dtype)


def _merge(ya, wa, yb, wb, gates, name):
    m, ka = ya.shape
    kb = yb.shape[1]
    d = wa.shape[1]
    tm = min(512, m)
    tn = min(512, d)
    nb = d // tn
    block_bytes = (_nbytes((tm, ka), BF16) + _nbytes((ka, tn), F32) + _nbytes((tm, kb), BF16)
                   + _nbytes((kb, tn), F32) + 3 * _nbytes((tm, tn), BF16))
    return pl.pallas_call(
        _merge_kernel,
        out_shape=jax.ShapeDtypeStruct((m, d), BF16),
        grid=(nb, m // tm + 1),
        in_specs=[pl.BlockSpec((tm, ka), lambda j, i: (_row_block(i), 0)),
                  pl.BlockSpec((ka, tn), lambda j, i: (0, j)),
                  pl.BlockSpec((tm, kb), lambda j, i: (_row_block(i), 0)),
                  pl.BlockSpec((kb, tn), lambda j, i: (0, j)),
                  pl.BlockSpec((tm, tn), lambda j, i: (_row_block(i), j)),
                  pl.BlockSpec((tm, tn), lambda j, i: (_row_block(i), j + nb))],
        out_specs=pl.BlockSpec((tm, tn), lambda j, i: (_row_block(i), j)),
        scratch_shapes=[pltpu.VMEM((ka, tn), BF16), pltpu.VMEM((kb, tn), BF16)],
        compiler_params=_params(("parallel", "arbitrary"), block_bytes, _nbytes((ka + kb, tn), BF16)),
        name=name,
    )(ya, wa, yb, wb, gates, gates)


def _da_kernel(lamv_ref, gsub_ref, q_ref, k_ref, v_ref, o_ref, acc_ref, l_ref, *m_ref, tq, tsub, lam_init,
               bounded):
    qi = pl.program_id(2)
    hd = DA_HEAD_DIM
    m_ref = None if bounded else m_ref[0]
    if not bounded:
        m_ref[...] = jnp.full(m_ref.shape, MASK_VALUE, F32)
    l_ref[...] = jnp.zeros(l_ref.shape, F32)
    acc_ref[...] = jnp.zeros(acc_ref.shape, F32)
    def widen(x, width):
        return jnp.concatenate([x] * (width // LANES), axis=-1)

    def update(row0, nrows, off, width, lead):
        rows = slice(row0, row0 + nrows)
        qq = q_ref[0, rows, :]
        kk = k_ref[0, pl.ds(off, width), :]
        vv = v_ref[0, pl.ds(off, width), :]
        for c in range(2):
            s = lax.dot_general(qq[:, c * hd:(c + 1) * hd], kk[:, c * hd:(c + 1) * hd],
                                (((1,), (1,)), ((), ())), preferred_element_type=F32)
            if lead is not None:
                row_chunk = (lax.broadcasted_iota(jnp.int32, (nrows, width), 0) + lead) // DA_CHUNK
                col_chunk = lax.broadcasted_iota(jnp.int32, (nrows, width), 1) // DA_CHUNK
                s = jnp.where(col_chunk <= row_chunk, s, MASK_VALUE)
            if bounded:
                p = jnp.exp2(s)
            else:
                m_old = m_ref[c, rows]
                m_new = jnp.maximum(m_old, jnp.max(s, axis=-1, keepdims=True))
                p = jnp.exp2(s - widen(m_new, width))
                alpha = jnp.exp2(m_old - m_new)
                m_ref[c, rows] = m_new
            psum = p[:, :LANES]
            for g in range(1, width // LANES):
                psum = psum + p[:, g * LANES:(g + 1) * LANES]
            pv = jnp.dot(p.astype(BF16), vv, preferred_element_type=F32)
            if bounded:
                l_ref[c, rows] = l_ref[c, rows] + psum
                acc_ref[c, rows] = acc_ref[c, rows] + pv
            else:
                l_ref[c, rows] = alpha * l_ref[c, rows] + psum
                acc_ref[c, rows] = widen(alpha, 2 * hd) * acc_ref[c, rows] + pv

    def full_body(j, carry):
        update(0, tq, pl.multiple_of(j * tq, tq), tq, None)
        return carry

    lax.fori_loop(0, qi, full_body, 0)
    for r in range(tq // tsub):
        update(r * tsub, tsub, pl.multiple_of(qi * tq, tq), (r + 1) * tsub, r * tsub)

    lv = lamv_ref[...]
    lam = (jnp.exp(jnp.sum(lv[0:1] * lv[1:2], axis=-1, keepdims=True))
           - jnp.exp(jnp.sum(lv[2:3] * lv[3:4], axis=-1, keepdims=True)) + lam_init)
    l1 = jnp.sum(l_ref[0], axis=-1, keepdims=True)
    l2 = jnp.sum(l_ref[1], axis=-1, keepdims=True)
    o = acc_ref[0] / l1 - lam * (acc_ref[1] / l2)
    ms = jnp.mean(o * o, axis=-1, keepdims=True)
    o_ref[0] = ((o * lax.rsqrt(ms + NORM_EPS) * gsub_ref[...]) * (1.0 - lam_init)).astype(o_ref.dtype)


def _diff_attention(qk, v, lamv, g_sub, n_heads, lam_init, bounded, name):
    b, s, _ = v.shape
    hw = 2 * DA_HEAD_DIM
    tq = min(1024, s)
    tsub = min(256, tq)
    block_bytes = 2 * _nbytes((tq, hw), BF16) + 2 * _nbytes((s, hw), BF16) + 6 * _nbytes((tq, tq), F32)
    stats = [pltpu.VMEM((2, tq, LANES), F32)] * (1 if bounded else 2)
    return pl.pallas_call(
        functools.partial(_da_kernel, tq=tq, tsub=tsub, lam_init=lam_init, bounded=bounded),
        out_shape=jax.ShapeDtypeStruct((b, s, n_heads * hw), BF16),
        grid=(b, n_heads, s // tq),
        in_specs=[pl.BlockSpec((4, DA_HEAD_DIM), lambda bi, h, i: (0, 0)),
                  pl.BlockSpec((1, hw), lambda bi, h, i: (0, 0)),
                  pl.BlockSpec((1, tq, hw), lambda bi, h, i: (bi, i, h)),
                  pl.BlockSpec((1, s, hw), lambda bi, h, i: (bi, 0, n_heads + h)),
                  pl.BlockSpec((1, s, hw), lambda bi, h, i: (bi, 0, h))],
        out_specs=pl.BlockSpec((1, tq, hw), lambda bi, h, i: (bi, i, h)),
        scratch_shapes=[pltpu.VMEM((2, tq, hw), F32)] + stats,
        compiler_params=_params(("parallel", "parallel", "arbitrary"), block_bytes),
        name=name,
    )(lamv, g_sub, qk, qk, v)


def _diff_attention_any(qk, v, lamv, g_sub, g_q, g_k, q_scale, n_heads, lam_init):
    bound = 1.01 * DA_HEAD_DIM * q_scale * jnp.max(jnp.abs(g_q)) * jnp.max(jnp.abs(g_k))
    return lax.cond(
        bound < DA_MAX_BOUNDED_LOGIT,
        lambda: _diff_attention(qk, v, lamv, g_sub, n_heads, lam_init, True, "diff_attn"),
        lambda: _diff_attention(qk, v, lamv, g_sub, n_heads, lam_init, False, "diff_attn_online"))


def _hg_forget(lb_ref, f_ref, rows, sl):
    lb = lb_ref[:, sl]
    return lb + (1.0 - lb) * jax.nn.sigmoid(f_ref[0, rows, sl])


def _hg_finish(o, gout_ref, gate):
    ms = jnp.mean(o * o, axis=-1, keepdims=True)
    return (o * lax.rsqrt(ms + NORM_EPS) * gout_ref[...]) * (gate * jax.nn.sigmoid(gate))


def _hg_kernel(lb_ref, gout_ref, q_ref, f_ref, i_ref, g_ref, o_ref, dec_ref, st_ref, *, heads, steps):
    t = pl.program_id(2)
    ch, hd = HG_CHUNK, HG_DIM

    @pl.when(t == 0)
    def _():
        st_ref[...] = jnp.zeros(st_ref.shape, F32)

    r = lax.broadcasted_iota(jnp.int32, (ch, ch), 0)
    c = lax.broadcasted_iota(jnp.int32, (ch, ch), 1)
    causal = c <= r
    row = lax.broadcasted_iota(jnp.int32, (ch, hd), 0)
    nt = (((1,), (1,)), ((), ()))
    tn = (((0,), (0,)), ((), ()))

    def chunk(rows, sl, st):
        q = q_ref[0, rows, sl]
        f = _hg_forget(lb_ref, f_ref, rows, sl)
        key = 1.0 - f
        lf = jnp.log(f)
        cum = lf
        for sh in (1, 2, 4, 8, 16, 32):
            cum = cum + jnp.where(row >= sh, pltpu.roll(cum, sh, axis=0), 0.0)
        last = cum[ch - 1:ch]
        q_dec = (q * jnp.exp(cum)).astype(BF16)
        k_inv = (key * jnp.exp(-cum)).astype(BF16)
        k_state = (key * jnp.exp(last - cum)).astype(BF16)
        vals = i_ref[0, rows, sl].astype(BF16)
        scores = jnp.where(causal, lax.dot_general(q_dec, k_inv, nt, preferred_element_type=F32), 0.0)
        o = (jnp.dot(scores.astype(BF16), vals, preferred_element_type=F32)
             + lax.dot_general(q_dec, st.astype(BF16), nt, preferred_element_type=F32))
        st_next = jnp.exp(last) * st + lax.dot_general(vals, k_state, tn, preferred_element_type=F32)
        return _hg_finish(o, gout_ref, g_ref[0, rows, sl]), st_next, -last

    worst = jnp.zeros((1, hd), F32)
    for h in range(heads):
        sl = slice(h * hd, (h + 1) * hd)
        st = st_ref[h]
        for ci in range(steps // ch):
            rows = slice(ci * ch, (ci + 1) * ch)
            y, st, decay = chunk(rows, sl, st)
            o_ref[0, rows, sl] = y.astype(o_ref.dtype)
            worst = jnp.maximum(worst, decay)
        st_ref[h] = st
    dec_ref[...] = jnp.broadcast_to(worst, dec_ref.shape)


def _hg_seq_kernel(lb_ref, gout_ref, q_ref, f_ref, i_ref, g_ref, o_ref, st_ref, *, steps):
    t = pl.program_id(2)
    hd = HG_DIM

    @pl.when(t == 0)
    def _():
        st_ref[...] = jnp.zeros(st_ref.shape, F32)

    lane = lax.broadcasted_iota(jnp.int32, (hd, hd), 1)
    sl = slice(0, hd)

    def block(bi, carry):
        rows = pl.ds(pl.multiple_of(bi * hd, hd), hd)
        q = q_ref[0, rows, :]
        f = _hg_forget(lb_ref, f_ref, rows, sl)
        key = 1.0 - f
        vals_t = i_ref[0, rows, :].T
        st = st_ref[...]
        out_t = jnp.zeros((hd, hd), F32)
        for s in range(hd):
            st = f[s:s + 1] * st + vals_t[:, s:s + 1] * key[s:s + 1]
            out_t = jnp.where(lane == s, jnp.sum(st * q[s:s + 1], axis=1, keepdims=True), out_t)
        st_ref[...] = st
        o_ref[0, rows, :] = _hg_finish(out_t.T, gout_ref, g_ref[0, rows, :]).astype(o_ref.dtype)
        return carry

    lax.fori_loop(0, steps // hd, block, 0)


def _hgrn2(hg, lower, g_out, n_heads, heads, kernel_fn, with_decay, name):
    b, s, _ = hg.shape
    steps = min(512, s)
    nhb = n_heads // heads
    nt = s // steps
    w = heads * HG_DIM
    block_bytes = 4 * _nbytes((steps, w), F32) + _nbytes((steps, w), BF16)

    def spec(part):
        return pl.BlockSpec((1, steps, w), lambda bi, h, t, p=part: (bi, t, p * nhb + h))

    out_shape = jax.ShapeDtypeStruct((b, s, n_heads * HG_DIM), BF16)
    out_specs = pl.BlockSpec((1, steps, w), lambda bi, h, t: (bi, t, h))
    if with_decay:
        sub = 8
        out_shape = (out_shape, jax.ShapeDtypeStruct((b * nhb * nt * sub, LANES), F32))
        out_specs = (out_specs, pl.BlockSpec((sub, LANES), lambda bi, h, t: ((bi * nhb + h) * nt + t, 0)))
    return pl.pallas_call(
        functools.partial(kernel_fn, steps=steps),
        out_shape=out_shape,
        grid=(b, nhb, nt),
        in_specs=[pl.BlockSpec((1, w), lambda bi, h, t: (0, h)),
                  pl.BlockSpec((1, HG_DIM), lambda bi, h, t: (0, 0)),
                  spec(0), spec(1), spec(2), spec(3)],
        out_specs=out_specs,
        scratch_shapes=[pltpu.VMEM((heads, HG_DIM, HG_DIM) if heads > 1 else (HG_DIM, HG_DIM), F32)],
        compiler_params=_params(("parallel", "parallel", "arbitrary"), block_bytes),
        name=name,
    )(lower, g_out, hg, hg, hg, hg)


def _hgrn2_any(hg, lower, g_out, n_heads):
    heads = min(8, n_heads)
    y, decay = _hgrn2(hg, lower, g_out, n_heads, heads, functools.partial(_hg_kernel, heads=heads), True,
                      "hgrn2")
    return lax.cond(
        jnp.max(decay) < HG_FAST_MAX_DECAY,
        lambda: y,
        lambda: _hgrn2(hg, lower, g_out, n_heads, 1, _hg_seq_kernel, False, "hgrn2_stepwise"))


def _xattn_kernel(q_ref, k_ref, v_ref, o_ref, *, n_heads):
    hd = X_HEAD_DIM
    for h in range(n_heads):
        sl = slice(h * hd, (h + 1) * hd)
        s = lax.dot_general(q_ref[0, :, sl], k_ref[0, :, sl], (((1,), (1,)), ((), ())),
                            preferred_element_type=F32)
        p = jnp.exp(s - jnp.max(s, axis=-1, keepdims=True))
        denom = jnp.sum(p, axis=-1, keepdims=True)
        o = jnp.dot(p.astype(BF16), v_ref[0, :, sl], preferred_element_type=F32)
        o_ref[0, :, sl] = (o / denom).astype(o_ref.dtype)


def _cross_attention(q, k, v, n_heads, name):
    b, s, w = q.shape
    mlen = k.shape[1]
    tq = min(1024, s)
    block_bytes = 2 * _nbytes((tq, w), BF16) + 2 * _nbytes((mlen, w), BF16) + 3 * _nbytes((tq, mlen), F32)
    return pl.pallas_call(
        functools.partial(_xattn_kernel, n_heads=n_heads),
        out_shape=jax.ShapeDtypeStruct((b, s, w), BF16),
        grid=(b, s // tq),
        in_specs=[pl.BlockSpec((1, tq, w), lambda bi, i: (bi, i, 0)),
                  pl.BlockSpec((1, mlen, w), lambda bi, i: (bi, 0, 0)),
                  pl.BlockSpec((1, mlen, w), lambda bi, i: (bi, 0, 0))],
        out_specs=pl.BlockSpec((1, tq, w), lambda bi, i: (bi, i, 0)),
        compiler_params=_params(("parallel", "parallel"), block_bytes),
        name=name,
    )(q, k, v)


def _pack_bf16_pair(lo, hi):
    lo_bits = lax.bitcast_convert_type(lo.astype(BF16).astype(F32), jnp.uint32)
    hi_bits = lax.bitcast_convert_type(hi.astype(BF16).astype(F32), jnp.uint32)
    return (lo_bits >> 16) | (hi_bits & jnp.uint32(0xFFFF0000))


def _unpack_bf16_pair(words):
    lo = lax.bitcast_convert_type(words << 16, F32)
    hi = lax.bitcast_convert_type(words & jnp.uint32(0xFFFF0000), F32)
    return lo, hi


def _router_kernel(xo_ref, wo_ref, res_ref, g_ref, whi_ref, wlo_ref, b_ref, h_ref, n_ref, rt_ref, *, n_groups,
                   n_experts):
    x = res_ref[...] + jnp.dot(xo_ref[...], wo_ref[...], preferred_element_type=F32)
    h_ref[...] = x
    ms = jnp.mean(x * x, axis=-1, keepdims=True)
    n = x * lax.rsqrt(ms + NORM_EPS) * g_ref[...]
    n_hi = n.astype(BF16)
    n_lo = (n - n_hi.astype(F32)).astype(BF16)
    half = n.shape[1] // 2
    n_ref[...] = _pack_bf16_pair(n[:, :half], n[:, half:])
    logits = (jnp.dot(n_hi, whi_ref[...], preferred_element_type=F32)
              + jnp.dot(n_lo, whi_ref[...], preferred_element_type=F32)
              + jnp.dot(n_hi, wlo_ref[...], preferred_element_type=F32) + b_ref[...])
    epg = n_experts // n_groups
    lane = lax.broadcasted_iota(jnp.int32, logits.shape, 1)
    n_lanes = logits.shape[1]

    def top1(vals):
        best = jnp.max(vals, axis=-1, keepdims=True)
        return best, jnp.min(jnp.where(vals == best, lane, n_lanes), axis=-1, keepdims=True)

    is_grp = lane < n_groups
    g_max, grp = top1(jnp.where(is_grp, logits, -jnp.inf))
    grp_w = 1.0 / jnp.sum(jnp.where(is_grp, jnp.exp(logits - g_max), 0.0), axis=-1, keepdims=True)
    first = n_groups + grp * epg
    in_grp = jnp.where(jnp.logical_and(lane >= first, lane < first + epg), logits, -jnp.inf)
    v1, i1 = top1(in_grp)
    v2, i2 = top1(jnp.where(lane == i1, -jnp.inf, in_grp))
    e2 = jnp.exp(v2 - v1)
    p1 = 1.0 / (1.0 + e2)
    cols = [grp_w * p1, grp_w * (e2 * p1), (i1 - n_groups).astype(F32), (i2 - n_groups).astype(F32)]
    rt = jnp.zeros(logits.shape, F32)
    for c, val in enumerate(cols):
        rt = jnp.where(lane == c, val, rt)
    rt_ref[...] = rt


def _xout_router(xo, w_o, res, g, w_hi, w_lo, bias, n_groups, n_experts, name):
    m, d = res.shape
    kx = xo.shape[1]
    nl = w_hi.shape[1]
    tm = min(256, m)
    block_bytes = (_nbytes((tm, kx), BF16) + _nbytes((kx, d), BF16) + 3 * _nbytes((tm, d), F32)
                   + 2 * _nbytes((d, nl), BF16))
    return pl.pallas_call(
        functools.partial(_router_kernel, n_groups=n_groups, n_experts=n_experts),
        out_shape=(jax.ShapeDtypeStruct((m, d), F32), jax.ShapeDtypeStruct((m, d // 2), jnp.uint32),
                   jax.ShapeDtypeStruct((m, nl), F32)),
        grid=(m // tm,),
        in_specs=[pl.BlockSpec((tm, kx), lambda i: (i, 0)),
                  pl.BlockSpec((kx, d), lambda i: (0, 0)),
                  pl.BlockSpec((tm, d), lambda i: (i, 0)),
                  pl.BlockSpec((1, d), lambda i: (0, 0)),
                  pl.BlockSpec((d, nl), lambda i: (0, 0)),
                  pl.BlockSpec((d, nl), lambda i: (0, 0)),
                  pl.BlockSpec((1, nl), lambda i: (0, 0))],
        out_specs=(pl.BlockSpec((tm, d), lambda i: (i, 0)), pl.BlockSpec((tm, d // 2), lambda i: (i, 0)),
                   pl.BlockSpec((tm, nl), lambda i: (i, 0))),
        compiler_params=_params(("parallel",), block_bytes),
        name=name,
    )(xo, w_o, res, g, w_hi, w_lo, bias)


def _row_copies(src_ref, dst_ref, idx_ref, idx0, idx_stride, dst0, n_groups, sem, *, start):
    def body(g, carry):
        for u in range(ROW_COPY_UNROLL):
            r = g * ROW_COPY_UNROLL + u
            cp = pltpu.make_async_copy(src_ref.at[pl.ds(idx_ref[idx0 + r * idx_stride], 1)],
                                       dst_ref.at[pl.ds(dst0 + r, 1)], sem)
            if start:
                cp.start(priority=1)
            else:
                cp.wait()
        return carry

    lax.fori_loop(0, n_groups, body, 0)


def _expert_kernel(tok_ref, be_ref, src0_ref, ng_ref, nreal_ref, nf_ref, wg_ref, wu_ref, wd_ref, o_ref, xbuf, sems):
    i = pl.program_id(0)
    n_real = nreal_ref[0]

    def gather(block, start):
        slot = block % 2
        _row_copies(nf_ref, xbuf.at[slot], tok_ref, src0_ref[block], 1, 0, ng_ref[block], sems.at[slot],
                    start=start)

    @pl.when(i == 0)
    def _():
        xbuf[...] = jnp.zeros(xbuf.shape, xbuf.dtype)

    @pl.when(jnp.logical_and(i == 0, n_real > 0))
    def _():
        gather(i, True)

    @pl.when(i + 1 < n_real)
    def _():
        gather(i + 1, True)

    @pl.when(i < n_real)
    def _():
        gather(i, False)
        x_lo, x_hi = _unpack_bf16_pair(xbuf[i % 2])
        x_lo, x_hi = x_lo.astype(BF16), x_hi.astype(BF16)
        half = x_lo.shape[1]

        def up(w_ref):
            return (jnp.dot(x_lo, w_ref[0, :half, :], preferred_element_type=F32)
                    + jnp.dot(x_hi, w_ref[0, half:, :], preferred_element_type=F32))

        hg = up(wg_ref)
        hid = (hg * jax.nn.sigmoid(hg) * up(wu_ref)).astype(BF16)
        y = jnp.dot(hid, wd_ref[0], preferred_element_type=F32)
        o_ref[...] = _pack_bf16_pair(y[:, :half], y[:, half:])

    @pl.when(i >= n_real)
    def _():
        o_ref[...] = jnp.zeros(o_ref.shape, o_ref.dtype)


def _experts(nf, sorted_tok, blk_expert, blk_src0, blk_groups, n_real, wg, wu, wd, blk, name):
    n_blk = blk_expert.shape[0]
    d = wg.shape[1]
    ff = wg.shape[2]
    half = d // 2
    block_bytes = 3 * _nbytes((d, ff), BF16) + _nbytes((blk, half), jnp.uint32) + 3 * _nbytes((blk, d), F32)

    def weights(i, tk, be, s0, ng, nr):
        return (be[i], 0, 0)

    return pl.pallas_call(
        _expert_kernel,
        out_shape=jax.ShapeDtypeStruct((n_blk * blk, half), jnp.uint32),
        grid_spec=pltpu.PrefetchScalarGridSpec(
            num_scalar_prefetch=5,
            grid=(n_blk,),
            in_specs=[pl.BlockSpec(memory_space=pl.ANY),
                      pl.BlockSpec((1, d, ff), weights),
                      pl.BlockSpec((1, d, ff), weights),
                      pl.BlockSpec((1, ff, d), weights)],
            out_specs=pl.BlockSpec((blk, half), lambda i, tk, be, s0, ng, nr: (i, 0)),
            scratch_shapes=[pltpu.VMEM((2, blk, half), jnp.uint32), pltpu.SemaphoreType.DMA((2,))]),
        compiler_params=_params(("arbitrary",), block_bytes, 2 * _nbytes((blk, half), jnp.uint32)),
        name=name,
    )(sorted_tok, blk_expert, blk_src0, blk_groups, n_real, nf, wg, wu, wd)


def _combine_kernel(slot_ref, h_ref, gate_ref, ys_ref, o_ref, ybuf, sems, *, tm):
    i = pl.program_id(0)

    def gather(block, start):
        s = block % 2
        for k in range(TOP_K):
            _row_copies(ys_ref, ybuf.at[s], slot_ref, block * tm * TOP_K + k, TOP_K, k * tm,
                        tm // ROW_COPY_UNROLL, sems.at[s], start=start)

    @pl.when(i == 0)
    def _():
        gather(i, True)

    @pl.when(i + 1 < pl.num_programs(0))
    def _():
        gather(i + 1, True)

    gather(i, False)
    half = ybuf.shape[2]
    gate = gate_ref[...]
    acc_lo = h_ref[:, :half]
    acc_hi = h_ref[:, half:]
    for k in range(TOP_K):
        y_lo, y_hi = _unpack_bf16_pair(ybuf[i % 2, pl.ds(k * tm, tm), :])
        acc_lo = acc_lo + gate[:, k:k + 1] * y_lo
        acc_hi = acc_hi + gate[:, k:k + 1] * y_hi
    o_ref[:, :half] = acc_lo
    o_ref[:, half:] = acc_hi


def _combine(h, gate, ys, slot, name):
    m, d = h.shape
    tm = min(128, m)
    block_bytes = 2 * _nbytes((tm, d), F32) + TOP_K * _nbytes((tm, d), F32)
    return pl.pallas_call(
        functools.partial(_combine_kernel, tm=tm),
        out_shape=jax.ShapeDtypeStruct((m, d), F32),
        grid_spec=pltpu.PrefetchScalarGridSpec(
            num_scalar_prefetch=1,
            grid=(m // tm,),
            in_specs=[pl.BlockSpec((tm, d), lambda i, sl: (i, 0)),
                      pl.BlockSpec((tm, TOP_K), lambda i, sl: (i, 0)),
                      pl.BlockSpec(memory_space=pl.ANY)],
            out_specs=pl.BlockSpec((tm, d), lambda i, sl: (i, 0)),
            scratch_shapes=[pltpu.VMEM((2, TOP_K * tm, d // 2), jnp.uint32), pltpu.SemaphoreType.DMA((2,))]),
        compiler_params=_params(("arbitrary",), block_bytes),
        name=name,
    )(slot, h, gate, ys)


def _layout(expert, n_experts, blk):
    n_assign = expert.shape[0]
    e_sorted, order = lax.sort_key_val(expert, jnp.arange(n_assign, dtype=jnp.int32))
    experts = jnp.arange(n_experts, dtype=jnp.int32)
    starts = jnp.searchsorted(e_sorted, experts, side='left').astype(jnp.int32)
    counts = jnp.searchsorted(e_sorted, experts, side='right').astype(jnp.int32) - starts
    padded = (counts + blk - 1) // blk * blk
    pad_ends = jnp.cumsum(padded)
    pad_starts = pad_ends - padded
    gap = pad_starts - starts
    gap_step = jnp.concatenate([gap[:1], gap[1:] - gap[:-1]])
    dest = jnp.arange(n_assign, dtype=jnp.int32) + jnp.cumsum(
        jnp.zeros((n_assign + 1,), jnp.int32).at[starts].add(gap_step))[:n_assign]
    _, slot = lax.sort_key_val(order, dest)
    cap = n_assign + n_experts * blk
    n_blk = cap // blk
    blk_row0 = jnp.arange(n_blk, dtype=jnp.int32) * blk
    blk_expert = jnp.minimum(jnp.searchsorted(pad_ends, blk_row0, side='right'), n_experts - 1).astype(jnp.int32)
    n_real = (pad_ends[-1] // blk).astype(jnp.int32)
    blk_src0 = jnp.clip(blk_row0 - gap[blk_expert], 0, n_assign).astype(jnp.int32)
    blk_rows = jnp.clip(counts[blk_expert] - (blk_row0 - pad_starts[blk_expert]), 0, blk)
    blk_rows = jnp.where(jnp.arange(n_blk) < n_real, blk_rows, 0)
    blk_groups = ((blk_rows + ROW_COPY_UNROLL - 1) // ROW_COPY_UNROLL).astype(jnp.int32)
    sorted_tok = jnp.concatenate([order // TOP_K, jnp.zeros((ROW_COPY_UNROLL,), jnp.int32)])
    return sorted_tok, blk_expert, blk_src0, blk_groups, n_real.reshape(1), slot.astype(jnp.int32)


def _diff_lambda_init(layer):
    return 0.8 - 0.6 * math.exp(-0.3 * layer)


def _tile_cols(v, n):
    return jnp.tile(v.reshape(1, -1).astype(F32), (1, n // v.size))


def kernel(x, mem, g_mix, w_in, g_da_q, g_da_k, lam_q1, lam_k1, lam_q2, lam_k2, g_da_sub, hg_lower, g_hg_out,
           w_up_a, w_up_b, w_gate, b_gate, w_out, g_cross, g_mem, w_xq, w_xkv, g_xq, g_xk, w_xo, g_ffn, w_grp,
           b_grp, w_erouter, b_erouter, w_e_gate, w_e_up, w_e_down):
    b, s, d = x.shape
    t = b * s
    depth = g_mix.shape[0]
    da_w = w_up_a.shape[1]
    hg_w = w_up_b.shape[1]
    da_heads = da_w // (2 * DA_HEAD_DIM)
    hg_heads = hg_w // HG_DIM
    x_w = w_xq.shape[2]
    x_heads = x_w // X_HEAD_DIM
    n_groups = w_grp.shape[2]
    n_experts = w_erouter.shape[2]
    moe_blk = min(256, t)

    lower_bounds = jnp.cumsum(jax.nn.softmax(hg_lower.astype(F32), axis=0), axis=0)
    h = x.reshape(t, d)
    mem2 = mem.reshape(b * mem.shape[1], d)
    for l in range(depth):
        lam_init = _diff_lambda_init(l)
        w_in_l = w_in[l]
        n = _rmsnorm(h, g_mix[l], BF16, "rms_mix")
        q_scale = DA_HEAD_DIM ** -0.5 * math.log2(math.e)
        qk_gain = jnp.concatenate([_tile_cols(g_da_q[l], da_w) * q_scale, _tile_cols(g_da_k[l], da_w)], axis=1)
        ff = w_e_gate.shape[3]
        qk, we_gate = _matmul(n, w_in_l, col0=0, ncols=2 * da_w, out_dtype=BF16,
                              epilogue=functools.partial(_ep_group_norm, group=DA_HEAD_DIM),
                              col_extras=[(qk_gain, 0)], side_cast=w_e_gate[l].reshape(n_experts * d, ff),
                              name="proj_qk")
        v = _matmul(n, w_in_l, col0=2 * da_w, ncols=da_w, out_dtype=BF16, name="proj_v")
        hg, we_down = _matmul(n, w_in_l, col0=3 * da_w, ncols=4 * hg_w, out_dtype=F32,
                              side_cast=w_e_down[l].reshape(n_experts * ff, d), name="proj_hg")
        gates, we_up = _matmul(n, w_gate[l], out_dtype=BF16, epilogue=_ep_bias_sigmoid,
                               col_extras=[(b_gate[l].reshape(1, -1).astype(F32), 0)],
                               side_cast=w_e_up[l].reshape(n_experts * d, ff), name="proj_gate")
        lamv = jnp.stack([lam_q1[l], lam_k1[l], lam_q2[l], lam_k2[l]]).astype(F32)
        y_a = _diff_attention_any(qk.reshape(b, s, 2 * da_w), v.reshape(b, s, da_w), lamv,
                                  g_da_sub[l].reshape(1, -1).astype(F32), g_da_q[l], g_da_k[l], q_scale,
                                  da_heads, lam_init)
        y_b = _hgrn2_any(hg.reshape(b, s, 4 * hg_w), lower_bounds[l].reshape(1, hg_w),
                         g_hg_out[l].reshape(1, HG_DIM).astype(F32), hg_heads)
        merged = _merge(y_a.reshape(t, da_w), w_up_a[l], y_b.reshape(t, hg_w), w_up_b[l], gates, "merge")
        h, w_xq_bf = _matmul(merged, w_out[l], out_dtype=F32, epilogue=_ep_residual,
                             full_extras=[(h, 0)], side_cast=w_xq[l], name="mix_out")
        xq, w_xo_bf = _norm_matmul(h, g_cross[l].reshape(1, d).astype(F32), w_xq_bf,
                                   _tile_cols(g_xq[l], x_w) * X_HEAD_DIM ** -0.5, w_xo[l], out_dtype=BF16,
                                   epilogue=functools.partial(_ep_group_norm, group=X_HEAD_DIM), name="xattn_q")
        nm = _rmsnorm(mem2, g_mem[l], BF16, "rms_mem")
        w_xkv_l = w_xkv[l]
        xk = _matmul(nm, w_xkv_l, col0=0, ncols=x_w, out_dtype=BF16,
                     epilogue=functools.partial(_ep_group_norm, group=X_HEAD_DIM),
                     col_extras=[(_tile_cols(g_xk[l], x_w), 0)], name="xattn_k")
        xv = _matmul(nm, w_xkv_l, col0=x_w, ncols=x_w, out_dtype=BF16, name="xattn_v")
        xo = _cross_attention(xq.reshape(b, s, x_w), xk.reshape(b, -1, x_w), xv.reshape(b, -1, x_w), x_heads,
                              "xattn")
        n_logit = -(-(n_groups + n_experts) // LANES) * LANES
        w_r = jnp.zeros((d, n_logit), F32).at[:, :n_groups].set(w_grp[l]).at[:, n_groups:n_groups + n_experts].set(
            w_erouter[l])
        b_r = jnp.zeros((1, n_logit), F32).at[0, :n_groups].set(b_grp[l]).at[0, n_groups:n_groups + n_experts].set(
            b_erouter[l])
        w_r_hi = w_r.astype(BF16)
        w_r_lo = (w_r - w_r_hi.astype(F32)).astype(BF16)
        h, nf, route = _xout_router(xo.reshape(t, x_w), w_xo_bf, h, g_ffn[l].reshape(1, d).astype(F32), w_r_hi,
                                    w_r_lo, b_r, n_groups, n_experts, "xattn_out_router")
        gate = route[:, :TOP_K]
        expert = route[:, TOP_K:2 * TOP_K].astype(jnp.int32).reshape(-1)
        sorted_tok, blk_expert, blk_src0, blk_groups, n_real, slot = _layout(expert, n_experts, moe_blk)
        ys = _experts(nf, sorted_tok, blk_expert, blk_src0, blk_groups, n_real, we_gate.reshape(n_experts, d, ff),
                      we_up.reshape(n_experts, d, ff), we_down.reshape(n_experts, ff, d), moe_blk, "moe_experts")
        h = _combine(h, gate, ys, slot, "moe_combine")
    return h.reshape(b, s, d)
```

```python
import functools
import math

import jax
import jax.numpy as jnp
from jax import lax
from jax.experimental import pallas as pl
from jax.experimental.pallas import tpu as pltpu

F32 = jnp.float32
BF16 = jnp.bfloat16

NORM_EPS = 1e-6
DA_HEAD_DIM = 128
DA_CHUNK = 64
DA_MAX_BOUNDED_LOGIT = 50.0
HG_DIM = 128
HG_CHUNK = 64
HG_FAST_MAX_DECAY = 60.0
X_HEAD_DIM = 256
TOP_K = 2
ROW_COPY_UNROLL = 8
LANES = 128
MASK_VALUE = -1e30
V7X_VMEM_BYTES = 64 * 1024 * 1024
VMEM_CAP_BYTES = V7X_VMEM_BYTES - 6 * 1024 * 1024


def _vmem_limit(block_bytes, scratch_bytes):
    return int(min(VMEM_CAP_BYTES, max(32 * 1024 * 1024, 2 * block_bytes + scratch_bytes + 16 * 1024 * 1024)))


def _params(semantics, block_bytes, scratch_bytes=0):
    return pltpu.CompilerParams(dimension_semantics=semantics,
                                vmem_limit_bytes=_vmem_limit(block_bytes, scratch_bytes))


def _nbytes(shape, dtype):
    return math.prod(shape) * jnp.dtype(dtype).itemsize


def _rms_kernel(x_ref, g_ref, o_ref):
    x = x_ref[...].astype(F32)
    ms = jnp.mean(x * x, axis=-1, keepdims=True)
    o_ref[...] = (x * lax.rsqrt(ms + NORM_EPS) * g_ref[...]).astype(o_ref.dtype)


def _rmsnorm(x, g, out_dtype, name):
    m, d = x.shape
    tm = min(256, m)
    return pl.pallas_call(
        _rms_kernel,
        out_shape=jax.ShapeDtypeStruct((m, d), out_dtype),
        grid=(m // tm,),
        in_specs=[pl.BlockSpec((tm, d), lambda i: (i, 0)), pl.BlockSpec((1, d), lambda i: (0, 0))],
        out_specs=pl.BlockSpec((tm, d), lambda i: (i, 0)),
        compiler_params=_params(("parallel",), _nbytes((tm, d), x.dtype) + _nbytes((tm, d), out_dtype)),
        name=name,
    )(x, g.reshape(1, d).astype(F32))


def _ep_plain(acc, o_ref):
    o_ref[...] = acc.astype(o_ref.dtype)


def _ep_group_norm(acc, o_ref, gain, *, group):
    for c in range(acc.shape[1] // group):
        sl = slice(c * group, (c + 1) * group)
        xg = acc[:, sl]
        ms = jnp.mean(xg * xg, axis=-1, keepdims=True)
        o_ref[:, sl] = (xg * lax.rsqrt(ms + NORM_EPS) * gain[:, sl]).astype(o_ref.dtype)


def _ep_bias_sigmoid(acc, o_ref, bias):
    o_ref[...] = jax.nn.sigmoid(acc + bias).astype(o_ref.dtype)


def _ep_residual(acc, o_ref, res):
    o_ref[...] = (res + acc).astype(o_ref.dtype)


def _row_block(i):
    return jnp.maximum(i - 1, 0)


def _mm_kernel(a_ref, w_ref, *rest, epilogue, side):
    if side:
        *extra, side_src_ref, o_ref, side_out_ref, wb_ref = rest
    else:
        *extra, o_ref, wb_ref = rest
    i = pl.program_id(1)

    @pl.when(i == 0)
    def _():
        wb_ref[...] = w_ref[...].astype(BF16)

    @pl.when(i > 0)
    def _():
        acc = jnp.dot(a_ref[...], wb_ref[...], preferred_element_type=F32)
        epilogue(acc, o_ref, *[e[...] for e in extra])
        if side:
            side_out_ref[...] = side_src_ref[...].astype(BF16)


def _matmul(a, w, *, col0=0, ncols=None, out_dtype, epilogue=_ep_plain, col_extras=(), full_extras=(),
            side_cast=None, name):
    m, k = a.shape
    ncols = w.shape[1] - col0 if ncols is None else ncols
    tm = min(1024, m)
    tn = min(512 if k > 1024 else 1024, ncols)
    while ncols % tn or col0 % tn:
        tn //= 2
    assert m % tm == 0 and tn % LANES == 0
    n_rb = m // tm
    in_specs = [pl.BlockSpec((tm, k), lambda j, i: (_row_block(i), 0)),
                pl.BlockSpec((k, tn), lambda j, i, o=col0 // tn: (0, j + o))]
    operands = [a, w]
    block_bytes = _nbytes((tm, k), a.dtype) + _nbytes((k, tn), w.dtype) + _nbytes((tm, tn), out_dtype)
    for vec, c0 in col_extras:
        assert c0 % tn == 0
        in_specs.append(pl.BlockSpec((1, tn), lambda j, i, o=c0 // tn: (0, j + o)))
        operands.append(vec)
    for arr, c0 in full_extras:
        assert c0 % tn == 0
        in_specs.append(pl.BlockSpec((tm, tn), lambda j, i, o=c0 // tn: (_row_block(i), j + o)))
        operands.append(arr)
        block_bytes += _nbytes((tm, tn), arr.dtype)
    out_shape = jax.ShapeDtypeStruct((m, ncols), out_dtype)
    out_specs = pl.BlockSpec((tm, tn), lambda j, i: (_row_block(i), j))
    if side_cast is not None:
        rows, cols = side_cast.shape
        n_steps = (ncols // tn) * n_rb
        chunk = rows // n_steps
        assert chunk * n_steps == rows and chunk % 16 == 0, (rows, n_steps)
        side_spec = pl.BlockSpec((chunk, cols), lambda j, i: (j * n_rb + _row_block(i), 0))
        in_specs.append(side_spec)
        operands.append(side_cast)
        out_shape = (out_shape, jax.ShapeDtypeStruct((rows, cols), BF16))
        out_specs = (out_specs, side_spec)
        block_bytes += _nbytes((chunk, cols), F32) + _nbytes((chunk, cols), BF16)
    return pl.pallas_call(
        functools.partial(_mm_kernel, epilogue=epilogue, side=side_cast is not None),
        out_shape=out_shape,
        grid=(ncols // tn, n_rb + 1),
        in_specs=in_specs,
        out_specs=out_specs,
        scratch_shapes=[pltpu.VMEM((k, tn), BF16)],
        compiler_params=_params(("parallel", "arbitrary"), block_bytes, _nbytes((k, tn), BF16)),
        name=name,
    )(*operands)


def _nmm_kernel(a_ref, g_ref, w_ref, gain_ref, side_src_ref, o_ref, side_out_ref, *, epilogue):
    a = a_ref[...]
    ms = jnp.mean(a * a, axis=-1, keepdims=True)
    n = (a * lax.rsqrt(ms + NORM_EPS) * g_ref[...]).astype(BF16)
    epilogue(jnp.dot(n, w_ref[...], preferred_element_type=F32), o_ref, gain_ref[...])
    side_out_ref[...] = side_src_ref[...].astype(BF16)


def _norm_matmul(a, g, w, gain, side_cast, *, out_dtype, epilogue, name):
    m, k = a.shape
    n = w.shape[1]
    tm = min(512, m)
    n_steps = m // tm
    rows, cols = side_cast.shape
    chunk = rows // n_steps
    assert m % tm == 0 and chunk * n_steps == rows and chunk % 16 == 0
    side_spec = pl.BlockSpec((chunk, cols), lambda i: (i, 0))
    block_bytes = (_nbytes((tm, k), F32) + _nbytes((k, n), BF16) + _nbytes((tm, n), out_dtype)
                   + _nbytes((chunk, cols), F32) + _nbytes((chunk, cols), BF16))
    return pl.pallas_call(
        functools.partial(_nmm_kernel, epilogue=epilogue),
        out_shape=(jax.ShapeDtypeStruct((m, n), out_dtype), jax.ShapeDtypeStruct((rows, cols), BF16)),
        grid=(n_steps,),
        in_specs=[pl.BlockSpec((tm, k), lambda i: (i, 0)),
                  pl.BlockSpec((1, k), lambda i: (0, 0)),
                  pl.BlockSpec((k, n), lambda i: (0, 0)),
                  pl.BlockSpec((1, n), lambda i: (0, 0)),
                  side_spec],
        out_specs=(pl.BlockSpec((tm, n), lambda i: (i, 0)), side_spec),
        compiler_params=_params(("parallel",), block_bytes),
        name=name,
    )(a, g, w, gain, side_cast)


def _merge_kernel(ya_ref, wa_ref, yb_ref, wb_ref, ga_ref, gb_ref, o_ref, wa_bf_ref, wb_bf_ref):
    i = pl.program_id(1)

    @pl.when(i == 0)
    def _():
        wa_bf_ref[...] = wa_ref[...].astype(BF16)
        wb_bf_ref[...] = wb_ref[...].astype(BF16)

    @pl.when(i > 0)
    def _():
        pa = jnp.dot(ya_ref[...], wa_bf_ref[...], preferred_element_type=F32)
        pb = jnp.dot(yb_ref[...], wb_bf_ref[...], preferred_element_type=F32)
        o_ref[...] = (ga_ref[...].astype(F32) * pa + gb_ref[...].astype(F32) * pb).astype(o_ref.dtype)


def _merge(ya, wa, yb, wb, gates, name):
    m, ka = ya.shape
    kb = yb.shape[1]
    d = wa.shape[1]
    tm = min(512, m)
    tn = min(512, d)
    nb = d // tn
    block_bytes = (_nbytes((tm, ka), BF16) + _nbytes((ka, tn), F32) + _nbytes((tm, kb), BF16)
                   + _nbytes((kb, tn), F32) + 3 * _nbytes((tm, tn), BF16))
    return pl.pallas_call(
        _merge_kernel,
        out_shape=jax.ShapeDtypeStruct((m, d), BF16),
        grid=(nb, m // tm + 1),
        in_specs=[pl.BlockSpec((tm, ka), lambda j, i: (_row_block(i), 0)),
                  pl.BlockSpec((ka, tn), lambda j, i: (0, j)),
                  pl.BlockSpec((tm, kb), lambda j, i: (_row_block(i), 0)),
                  pl.BlockSpec((kb, tn), lambda j, i: (0, j)),
                  pl.BlockSpec((tm, tn), lambda j, i: (_row_block(i), j)),
                  pl.BlockSpec((tm, tn), lambda j, i: (_row_block(i), j + nb))],
        out_specs=pl.BlockSpec((tm, tn), lambda j, i: (_row_block(i), j)),
        scratch_shapes=[pltpu.VMEM((ka, tn), BF16), pltpu.VMEM((kb, tn), BF16)],
        compiler_params=_params(("parallel", "arbitrary"), block_bytes, _nbytes((ka + kb, tn), BF16)),
        name=name,
    )(ya, wa, yb, wb, gates, gates)


def _da_kernel(lamv_ref, gsub_ref, q_ref, k_ref, v_ref, o_ref, acc_ref, l_ref, *m_ref, tq, tsub, lam_init,
               bounded):
    qi = pl.program_id(2)
    hd = DA_HEAD_DIM
    m_ref = None if bounded else m_ref[0]
    if not bounded:
        m_ref[...] = jnp.full(m_ref.shape, MASK_VALUE, F32)
    l_ref[...] = jnp.zeros(l_ref.shape, F32)
    acc_ref[...] = jnp.zeros(acc_ref.shape, F32)
    def widen(x, width):
        return jnp.concatenate([x] * (width // LANES), axis=-1)

    def update(row0, nrows, off, width, lead):
        rows = slice(row0, row0 + nrows)
        qq = q_ref[0, rows, :]
        kk = k_ref[0, pl.ds(off, width), :]
        vv = v_ref[0, pl.ds(off, width), :]
        for c in range(2):
            s = lax.dot_general(qq[:, c * hd:(c + 1) * hd], kk[:, c * hd:(c + 1) * hd],
                                (((1,), (1,)), ((), ())), preferred_element_type=F32)
            if lead is not None:
                row_chunk = (lax.broadcasted_iota(jnp.int32, (nrows, width), 0) + lead) // DA_CHUNK
                col_chunk = lax.broadcasted_iota(jnp.int32, (nrows, width), 1) // DA_CHUNK
                s = jnp.where(col_chunk <= row_chunk, s, MASK_VALUE)
            if bounded:
                p = jnp.exp2(s)
            else:
                m_old = m_ref[c, rows]
                m_new = jnp.maximum(m_old, jnp.max(s, axis=-1, keepdims=True))
                p = jnp.exp2(s - widen(m_new, width))
                alpha = jnp.exp2(m_old - m_new)
                m_ref[c, rows] = m_new
            psum = p[:, :LANES]
            for g in range(1, width // LANES):
                psum = psum + p[:, g * LANES:(g + 1) * LANES]
            pv = jnp.dot(p.astype(BF16), vv, preferred_element_type=F32)
            if bounded:
                l_ref[c, rows] = l_ref[c, rows] + psum
                acc_ref[c, rows] = acc_ref[c, rows] + pv
            else:
                l_ref[c, rows] = alpha * l_ref[c, rows] + psum
                acc_ref[c, rows] = widen(alpha, 2 * hd) * acc_ref[c, rows] + pv

    def full_body(j, carry):
        update(0, tq, pl.multiple_of(j * tq, tq), tq, None)
        return carry

    lax.fori_loop(0, qi, full_body, 0)
    for r in range(tq // tsub):
        update(r * tsub, tsub, pl.multiple_of(qi * tq, tq), (r + 1) * tsub, r * tsub)

    lv = lamv_ref[...]
    lam = (jnp.exp(jnp.sum(lv[0:1] * lv[1:2], axis=-1, keepdims=True))
           - jnp.exp(jnp.sum(lv[2:3] * lv[3:4], axis=-1, keepdims=True)) + lam_init)
    l1 = jnp.sum(l_ref[0], axis=-1, keepdims=True)
    l2 = jnp.sum(l_ref[1], axis=-1, keepdims=True)
    o = acc_ref[0] / l1 - lam * (acc_ref[1] / l2)
    ms = jnp.mean(o * o, axis=-1, keepdims=True)
    o_ref[0] = ((o * lax.rsqrt(ms + NORM_EPS) * gsub_ref[...]) * (1.0 - lam_init)).astype(o_ref.dtype)


def _diff_attention(qk, v, lamv, g_sub, n_heads, lam_init, bounded, name):
    b, s, _ = v.shape
    hw = 2 * DA_HEAD_DIM
    tq = min(2048, s)
    tsub = min(256, tq)
    block_bytes = 2 * _nbytes((tq, hw), BF16) + 2 * _nbytes((s, hw), BF16) + 6 * _nbytes((tq, tq), F32)
    stats = [pltpu.VMEM((2, tq, LANES), F32)] * (1 if bounded else 2)
    return pl.pallas_call(
        functools.partial(_da_kernel, tq=tq, tsub=tsub, lam_init=lam_init, bounded=bounded),
        out_shape=jax.ShapeDtypeStruct((b, s, n_heads * hw), BF16),
        grid=(b, n_heads, s // tq),
        in_specs=[pl.BlockSpec((4, DA_HEAD_DIM), lambda bi, h, i: (0, 0)),
                  pl.BlockSpec((1, hw), lambda bi, h, i: (0, 0)),
                  pl.BlockSpec((1, tq, hw), lambda bi, h, i: (bi, i, h)),
                  pl.BlockSpec((1, s, hw), lambda bi, h, i: (bi, 0, n_heads + h)),
                  pl.BlockSpec((1, s, hw), lambda bi, h, i: (bi, 0, h))],
        out_specs=pl.BlockSpec((1, tq, hw), lambda bi, h, i: (bi, i, h)),
        scratch_shapes=[pltpu.VMEM((2, tq, hw), F32)] + stats,
        compiler_params=_params(("parallel", "parallel", "arbitrary"), block_bytes),
        name=name,
    )(lamv, g_sub, qk, qk, v)


def _diff_attention_any(qk, v, lamv, g_sub, g_q, g_k, q_scale, n_heads, lam_init):
    bound = 1.01 * DA_HEAD_DIM * q_scale * jnp.max(jnp.abs(g_q)) * jnp.max(jnp.abs(g_k))
    return lax.cond(
        bound < DA_MAX_BOUNDED_LOGIT,
        lambda: _diff_attention(qk, v, lamv, g_sub, n_heads, lam_init, True, "diff_attn"),
        lambda: _diff_attention(qk, v, lamv, g_sub, n_heads, lam_init, False, "diff_attn_online"))


def _hg_forget(lb_ref, f_ref, rows, sl):
    lb = lb_ref[:, sl]
    return lb + (1.0 - lb) * jax.nn.sigmoid(f_ref[0, rows, sl])


def _hg_finish(o, gout_ref, gate):
    ms = jnp.mean(o * o, axis=-1, keepdims=True)
    return (o * lax.rsqrt(ms + NORM_EPS) * gout_ref[...]) * (gate * jax.nn.sigmoid(gate))


def _hg_kernel(lb_ref, gout_ref, q_ref, f_ref, i_ref, g_ref, o_ref, dec_ref, st_ref, *, heads, steps):
    t = pl.program_id(2)
    ch, hd = HG_CHUNK, HG_DIM

    @pl.when(t == 0)
    def _():
        st_ref[...] = jnp.zeros(st_ref.shape, F32)

    r = lax.broadcasted_iota(jnp.int32, (ch, ch), 0)
    c = lax.broadcasted_iota(jnp.int32, (ch, ch), 1)
    causal = c <= r
    row = lax.broadcasted_iota(jnp.int32, (ch, hd), 0)
    nt = (((1,), (1,)), ((), ()))
    tn = (((0,), (0,)), ((), ()))

    def chunk(rows, sl, st):
        q = q_ref[0, rows, sl]
        f = _hg_forget(lb_ref, f_ref, rows, sl)
        key = 1.0 - f
        lf = jnp.log(f)
        cum = lf
        for sh in (1, 2, 4, 8, 16, 32):
            cum = cum + jnp.where(row >= sh, pltpu.roll(cum, sh, axis=0), 0.0)
        last = cum[ch - 1:ch]
        q_dec = (q * jnp.exp(cum)).astype(BF16)
        k_inv = (key * jnp.exp(-cum)).astype(BF16)
        k_state = (key * jnp.exp(last - cum)).astype(BF16)
        vals = i_ref[0, rows, sl].astype(BF16)
        scores = jnp.where(causal, lax.dot_general(q_dec, k_inv, nt, preferred_element_type=F32), 0.0)
        o = (jnp.dot(scores.astype(BF16), vals, preferred_element_type=F32)
             + lax.dot_general(q_dec, st.astype(BF16), nt, preferred_element_type=F32))
        st_next = jnp.exp(last) * st + lax.dot_general(vals, k_state, tn, preferred_element_type=F32)
        return _hg_finish(o, gout_ref, g_ref[0, rows, sl]), st_next, -last

    worst = jnp.zeros((1, hd), F32)
    for h in range(heads):
        sl = slice(h * hd, (h + 1) * hd)
        st = st_ref[h]
        for ci in range(steps // ch):
            rows = slice(ci * ch, (ci + 1) * ch)
            y, st, decay = chunk(rows, sl, st)
            o_ref[0, rows, sl] = y.astype(o_ref.dtype)
            worst = jnp.maximum(worst, decay)
        st_ref[h] = st
    dec_ref[...] = jnp.broadcast_to(worst, dec_ref.shape)


def _hg_seq_kernel(lb_ref, gout_ref, q_ref, f_ref, i_ref, g_ref, o_ref, st_ref, *, steps):
    t = pl.program_id(2)
    hd = HG_DIM

    @pl.when(t == 0)
    def _():
        st_ref[...] = jnp.zeros(st_ref.shape, F32)

    lane = lax.broadcasted_iota(jnp.int32, (hd, hd), 1)
    sl = slice(0, hd)

    def block(bi, carry):
        rows = pl.ds(pl.multiple_of(bi * hd, hd), hd)
        q = q_ref[0, rows, :]
        f = _hg_forget(lb_ref, f_ref, rows, sl)
        key = 1.0 - f
        vals_t = i_ref[0, rows, :].T
        st = st_ref[...]
        out_t = jnp.zeros((hd, hd), F32)
        for s in range(hd):
            st = f[s:s + 1] * st + vals_t[:, s:s + 1] * key[s:s + 1]
            out_t = jnp.where(lane == s, jnp.sum(st * q[s:s + 1], axis=1, keepdims=True), out_t)
        st_ref[...] = st
        o_ref[0, rows, :] = _hg_finish(out_t.T, gout_ref, g_ref[0, rows, :]).astype(o_ref.dtype)
        return carry

    lax.fori_loop(0, steps // hd, block, 0)


def _hgrn2(hg, lower, g_out, n_heads, heads, kernel_fn, with_decay, name):
    b, s, _ = hg.shape
    steps = min(512, s)
    nhb = n_heads // heads
    nt = s // steps
    w = heads * HG_DIM
    block_bytes = 4 * _nbytes((steps, w), F32) + _nbytes((steps, w), BF16)

    def spec(part):
        return pl.BlockSpec((1, steps, w), lambda bi, h, t, p=part: (bi, t, p * nhb + h))

    out_shape = jax.ShapeDtypeStruct((b, s, n_heads * HG_DIM), BF16)
    out_specs = pl.BlockSpec((1, steps, w), lambda bi, h, t: (bi, t, h))
    if with_decay:
        sub = 8
        out_shape = (out_shape, jax.ShapeDtypeStruct((b * nhb * nt * sub, LANES), F32))
        out_specs = (out_specs, pl.BlockSpec((sub, LANES), lambda bi, h, t: ((bi * nhb + h) * nt + t, 0)))
    return pl.pallas_call(
        functools.partial(kernel_fn, steps=steps),
        out_shape=out_shape,
        grid=(b, nhb, nt),
        in_specs=[pl.BlockSpec((1, w), lambda bi, h, t: (0, h)),
                  pl.BlockSpec((1, HG_DIM), lambda bi, h, t: (0, 0)),
                  spec(0), spec(1), spec(2), spec(3)],
        out_specs=out_specs,
        scratch_shapes=[pltpu.VMEM((heads, HG_DIM, HG_DIM) if heads > 1 else (HG_DIM, HG_DIM), F32)],
        compiler_params=_params(("parallel", "parallel", "arbitrary"), block_bytes),
        name=name,
    )(lower, g_out, hg, hg, hg, hg)


def _hgrn2_any(hg, lower, g_out, n_heads):
    heads = min(8, n_heads)
    y, decay = _hgrn2(hg, lower, g_out, n_heads, heads, functools.partial(_hg_kernel, heads=heads), True,
                      "hgrn2")
    return lax.cond(
        jnp.max(decay) < HG_FAST_MAX_DECAY,
        lambda: y,
        lambda: _hgrn2(hg, lower, g_out, n_heads, 1, _hg_seq_kernel, False, "hgrn2_stepwise"))


def _xattn_kernel(q_ref, k_ref, v_ref, o_ref, *, n_heads):
    hd = X_HEAD_DIM
    for h in range(n_heads):
        sl = slice(h * hd, (h + 1) * hd)
        s = lax.dot_general(q_ref[0, :, sl], k_ref[0, :, sl], (((1,), (1,)), ((), ())),
                            preferred_element_type=F32)
        p = jnp.exp(s - jnp.max(s, axis=-1, keepdims=True))
        denom = jnp.sum(p, axis=-1, keepdims=True)
        o = jnp.dot(p.astype(BF16), v_ref[0, :, sl], preferred_element_type=F32)
        o_ref[0, :, sl] = (o / denom).astype(o_ref.dtype)


def _cross_attention(q, k, v, n_heads, name):
    b, s, w = q.shape
    mlen = k.shape[1]
    tq = min(1024, s)
    block_bytes = 2 * _nbytes((tq, w), BF16) + 2 * _nbytes((mlen, w), BF16) + 3 * _nbytes((tq, mlen), F32)
    return pl.pallas_call(
        functools.partial(_xattn_kernel, n_heads=n_heads),
        out_shape=jax.ShapeDtypeStruct((b, s, w), BF16),
        grid=(b, s // tq),
        in_specs=[pl.BlockSpec((1, tq, w), lambda bi, i: (bi, i, 0)),
                  pl.BlockSpec((1, mlen, w), lambda bi, i: (bi, 0, 0)),
                  pl.BlockSpec((1, mlen, w), lambda bi, i: (bi, 0, 0))],
        out_specs=pl.BlockSpec((1, tq, w), lambda bi, i: (bi, i, 0)),
        compiler_params=_params(("parallel", "parallel"), block_bytes),
        name=name,
    )(q, k, v)


def _pack_bf16_pair(lo, hi):
    lo_bits = lax.bitcast_convert_type(lo.astype(BF16).astype(F32), jnp.uint32)
    hi_bits = lax.bitcast_convert_type(hi.astype(BF16).astype(F32), jnp.uint32)
    return (lo_bits >> 16) | (hi_bits & jnp.uint32(0xFFFF0000))


def _unpack_bf16_pair(words):
    lo = lax.bitcast_convert_type(words << 16, F32)
    hi = lax.bitcast_convert_type(words & jnp.uint32(0xFFFF0000), F32)
    return lo, hi


def _router_kernel(xo_ref, wo_ref, res_ref, g_ref, whi_ref, wlo_ref, b_ref, h_ref, n_ref, rt_ref, *, n_groups,
                   n_experts):
    x = res_ref[...] + jnp.dot(xo_ref[...], wo_ref[...], preferred_element_type=F32)
    h_ref[...] = x
    ms = jnp.mean(x * x, axis=-1, keepdims=True)
    n = x * lax.rsqrt(ms + NORM_EPS) * g_ref[...]
    n_hi = n.astype(BF16)
    n_lo = (n - n_hi.astype(F32)).astype(BF16)
    half = n.shape[1] // 2
    n_ref[...] = _pack_bf16_pair(n[:, :half], n[:, half:])
    logits = (jnp.dot(n_hi, whi_ref[...], preferred_element_type=F32)
              + jnp.dot(n_lo, whi_ref[...], preferred_element_type=F32)
              + jnp.dot(n_hi, wlo_ref[...], preferred_element_type=F32) + b_ref[...])
    epg = n_experts // n_groups
    lane = lax.broadcasted_iota(jnp.int32, logits.shape, 1)
    n_lanes = logits.shape[1]

    def top1(vals):
        best = jnp.max(vals, axis=-1, keepdims=True)
        return best, jnp.min(jnp.where(vals == best, lane, n_lanes), axis=-1, keepdims=True)

    is_grp = lane < n_groups
    g_max, grp = top1(jnp.where(is_grp, logits, -jnp.inf))
    grp_w = 1.0 / jnp.sum(jnp.where(is_grp, jnp.exp(logits - g_max), 0.0), axis=-1, keepdims=True)
    first = n_groups + grp * epg
    in_grp = jnp.where(jnp.logical_and(lane >= first, lane < first + epg), logits, -jnp.inf)
    v1, i1 = top1(in_grp)
    v2, i2 = top1(jnp.where(lane == i1, -jnp.inf, in_grp))
    e2 = jnp.exp(v2 - v1)
    p1 = 1.0 / (1.0 + e2)
    cols = [grp_w * p1, grp_w * (e2 * p1), (i1 - n_groups).astype(F32), (i2 - n_groups).astype(F32)]
    rt = jnp.zeros(logits.shape, F32)
    for c, val in enumerate(cols):
        rt = jnp.where(lane == c, val, rt)
    rt_ref[...] = rt


def _xout_router(xo, w_o, res, g, w_hi, w_lo, bias, n_groups, n_experts, name):
    m, d = res.shape
    kx = xo.shape[1]
    nl = w_hi.shape[1]
    tm = min(256, m)
    block_bytes = (_nbytes((tm, kx), BF16) + _nbytes((kx, d), BF16) + 3 * _nbytes((tm, d), F32)
                   + 2 * _nbytes((d, nl), BF16))
    return pl.pallas_call(
        functools.partial(_router_kernel, n_groups=n_groups, n_experts=n_experts),
        out_shape=(jax.ShapeDtypeStruct((m, d), F32), jax.ShapeDtypeStruct((m, d // 2), jnp.uint32),
                   jax.ShapeDtypeStruct((m, nl), F32)),
        grid=(m // tm,),
        in_specs=[pl.BlockSpec((tm, kx), lambda i: (i, 0)),
                  pl.BlockSpec((kx, d), lambda i: (0, 0)),
                  pl.BlockSpec((tm, d), lambda i: (i, 0)),
                  pl.BlockSpec((1, d), lambda i: (0, 0)),
                  pl.BlockSpec((d, nl), lambda i: (0, 0)),
                  pl.BlockSpec((d, nl), lambda i: (0, 0)),
                  pl.BlockSpec((1, nl), lambda i: (0, 0))],
        out_specs=(pl.BlockSpec((tm, d), lambda i: (i, 0)), pl.BlockSpec((tm, d // 2), lambda i: (i, 0)),
                   pl.BlockSpec((tm, nl), lambda i: (i, 0))),
        compiler_params=_params(("parallel",), block_bytes),
        name=name,
    )(xo, w_o, res, g, w_hi, w_lo, bias)


def _row_copies(src_ref, dst_ref, idx_ref, idx0, idx_stride, dst0, n_groups, sem, *, start):
    def body(g, carry):
        for u in range(ROW_COPY_UNROLL):
            r = g * ROW_COPY_UNROLL + u
            cp = pltpu.make_async_copy(src_ref.at[pl.ds(idx_ref[idx0 + r * idx_stride], 1)],
                                       dst_ref.at[pl.ds(dst0 + r, 1)], sem)
            if start:
                cp.start(priority=1)
            else:
                cp.wait()
        return carry

    lax.fori_loop(0, n_groups, body, 0)


def _expert_kernel(tok_ref, be_ref, src0_ref, ng_ref, nreal_ref, nf_ref, wg_ref, wu_ref, wd_ref, o_ref, xbuf, sems):
    i = pl.program_id(0)
    n_real = nreal_ref[0]

    def gather(block, start):
        slot = block % 2
        _row_copies(nf_ref, xbuf.at[slot], tok_ref, src0_ref[block], 1, 0, ng_ref[block], sems.at[slot],
                    start=start)

    @pl.when(i == 0)
    def _():
        xbuf[...] = jnp.zeros(xbuf.shape, xbuf.dtype)

    @pl.when(jnp.logical_and(i == 0, n_real > 0))
    def _():
        gather(i, True)

    @pl.when(i + 1 < n_real)
    def _():
        gather(i + 1, True)

    @pl.when(i < n_real)
    def _():
        gather(i, False)
        x_lo, x_hi = _unpack_bf16_pair(xbuf[i % 2])
        x_lo, x_hi = x_lo.astype(BF16), x_hi.astype(BF16)
        half = x_lo.shape[1]

        def up(w_ref):
            return (jnp.dot(x_lo, w_ref[0, :half, :], preferred_element_type=F32)
                    + jnp.dot(x_hi, w_ref[0, half:, :], preferred_element_type=F32))

        hg = up(wg_ref)
        hid = (hg * jax.nn.sigmoid(hg) * up(wu_ref)).astype(BF16)
        y = jnp.dot(hid, wd_ref[0], preferred_element_type=F32)
        o_ref[...] = _pack_bf16_pair(y[:, :half], y[:, half:])

    @pl.when(i >= n_real)
    def _():
        o_ref[...] = jnp.zeros(o_ref.shape, o_ref.dtype)


def _experts(nf, sorted_tok, blk_expert, blk_src0, blk_groups, n_real, wg, wu, wd, blk, name):
    n_blk = blk_expert.shape[0]
    d = wg.shape[1]
    ff = wg.shape[2]
    half = d // 2
    block_bytes = 3 * _nbytes((d, ff), BF16) + _nbytes((blk, half), jnp.uint32) + 3 * _nbytes((blk, d), F32)

    def weights(i, tk, be, s0, ng, nr):
        return (be[i], 0, 0)

    return pl.pallas_call(
        _expert_kernel,
        out_shape=jax.ShapeDtypeStruct((n_blk * blk, half), jnp.uint32),
        grid_spec=pltpu.PrefetchScalarGridSpec(
            num_scalar_prefetch=5,
            grid=(n_blk,),
            in_specs=[pl.BlockSpec(memory_space=pl.ANY),
                      pl.BlockSpec((1, d, ff), weights),
                      pl.BlockSpec((1, d, ff), weights),
                      pl.BlockSpec((1, ff, d), weights)],
            out_specs=pl.BlockSpec((blk, half), lambda i, tk, be, s0, ng, nr: (i, 0)),
            scratch_shapes=[pltpu.VMEM((2, blk, half), jnp.uint32), pltpu.SemaphoreType.DMA((2,))]),
        compiler_params=_params(("arbitrary",), block_bytes, 2 * _nbytes((blk, half), jnp.uint32)),
        name=name,
    )(sorted_tok, blk_expert, blk_src0, blk_groups, n_real, nf, wg, wu, wd)


def _combine_kernel(slot_ref, h_ref, gate_ref, ys_ref, o_ref, ybuf, sems, *, tm):
    i = pl.program_id(0)

    def gather(block, start):
        s = block % 2
        for k in range(TOP_K):
            _row_copies(ys_ref, ybuf.at[s], slot_ref, block * tm * TOP_K + k, TOP_K, k * tm,
                        tm // ROW_COPY_UNROLL, sems.at[s], start=start)

    @pl.when(i == 0)
    def _():
        gather(i, True)

    @pl.when(i + 1 < pl.num_programs(0))
    def _():
        gather(i + 1, True)

    gather(i, False)
    half = ybuf.shape[2]
    gate = gate_ref[...]
    acc_lo = h_ref[:, :half]
    acc_hi = h_ref[:, half:]
    for k in range(TOP_K):
        y_lo, y_hi = _unpack_bf16_pair(ybuf[i % 2, pl.ds(k * tm, tm), :])
        acc_lo = acc_lo + gate[:, k:k + 1] * y_lo
        acc_hi = acc_hi + gate[:, k:k + 1] * y_hi
    o_ref[:, :half] = acc_lo
    o_ref[:, half:] = acc_hi


def _combine(h, gate, ys, slot, name):
    m, d = h.shape
    tm = min(128, m)
    block_bytes = 2 * _nbytes((tm, d), F32) + TOP_K * _nbytes((tm, d), F32)
    return pl.pallas_call(
        functools.partial(_combine_kernel, tm=tm),
        out_shape=jax.ShapeDtypeStruct((m, d), F32),
        grid_spec=pltpu.PrefetchScalarGridSpec(
            num_scalar_prefetch=1,
            grid=(m // tm,),
            in_specs=[pl.BlockSpec((tm, d), lambda i, sl: (i, 0)),
                      pl.BlockSpec((tm, TOP_K), lambda i, sl: (i, 0)),
                      pl.BlockSpec(memory_space=pl.ANY)],
            out_specs=pl.BlockSpec((tm, d), lambda i, sl: (i, 0)),
            scratch_shapes=[pltpu.VMEM((2, TOP_K * tm, d // 2), jnp.uint32), pltpu.SemaphoreType.DMA((2,))]),
        compiler_params=_params(("arbitrary",), block_bytes),
        name=name,
    )(slot, h, gate, ys)


def _layout(expert, n_experts, blk):
    n_assign = expert.shape[0]
    e_sorted, order = lax.sort_key_val(expert, jnp.arange(n_assign, dtype=jnp.int32))
    experts = jnp.arange(n_experts, dtype=jnp.int32)
    starts = jnp.searchsorted(e_sorted, experts, side='left').astype(jnp.int32)
    counts = jnp.searchsorted(e_sorted, experts, side='right').astype(jnp.int32) - starts
    padded = (counts + blk - 1) // blk * blk
    pad_ends = jnp.cumsum(padded)
    pad_starts = pad_ends - padded
    gap = pad_starts - starts
    gap_step = jnp.concatenate([gap[:1], gap[1:] - gap[:-1]])
    dest = jnp.arange(n_assign, dtype=jnp.int32) + jnp.cumsum(
        jnp.zeros((n_assign + 1,), jnp.int32).at[starts].add(gap_step))[:n_assign]
    _, slot = lax.sort_key_val(order, dest)
    cap = n_assign + n_experts * blk
    n_blk = cap // blk
    blk_row0 = jnp.arange(n_blk, dtype=jnp.int32) * blk
    blk_expert = jnp.minimum(jnp.searchsorted(pad_ends, blk_row0, side='right'), n_experts - 1).astype(jnp.int32)
    n_real = (pad_ends[-1] // blk).astype(jnp.int32)
    blk_src0 = jnp.clip(blk_row0 - gap[blk_expert], 0, n_assign).astype(jnp.int32)
    blk_rows = jnp.clip(counts[blk_expert] - (blk_row0 - pad_starts[blk_expert]), 0, blk)
    blk_rows = jnp.where(jnp.arange(n_blk) < n_real, blk_rows, 0)
    blk_groups = ((blk_rows + ROW_COPY_UNROLL - 1) // ROW_COPY_UNROLL).astype(jnp.int32)
    sorted_tok = jnp.concatenate([order // TOP_K, jnp.zeros((ROW_COPY_UNROLL,), jnp.int32)])
    return sorted_tok, blk_expert, blk_src0, blk_groups, n_real.reshape(1), slot.astype(jnp.int32)


def _diff_lambda_init(layer):
    return 0.8 - 0.6 * math.exp(-0.3 * layer)


def _tile_cols(v, n):
    return jnp.tile(v.reshape(1, -1).astype(F32), (1, n // v.size))


def kernel(x, mem, g_mix, w_in, g_da_q, g_da_k, lam_q1, lam_k1, lam_q2, lam_k2, g_da_sub, hg_lower, g_hg_out,
           w_up_a, w_up_b, w_gate, b_gate, w_out, g_cross, g_mem, w_xq, w_xkv, g_xq, g_xk, w_xo, g_ffn, w_grp,
           b_grp, w_erouter, b_erouter, w_e_gate, w_e_up, w_e_down):
    b, s, d = x.shape
    t = b * s
    depth = g_mix.shape[0]
    da_w = w_up_a.shape[1]
    hg_w = w_up_b.shape[1]
    da_heads = da_w // (2 * DA_HEAD_DIM)
    hg_heads = hg_w // HG_DIM
    x_w = w_xq.shape[2]
    x_heads = x_w // X_HEAD_DIM
    n_groups = w_grp.shape[2]
    n_experts = w_erouter.shape[2]
    moe_blk = min(256, t)

    lower_bounds = jnp.cumsum(jax.nn.softmax(hg_lower.astype(F32), axis=0), axis=0)
    h = x.reshape(t, d)
    mem2 = mem.reshape(b * mem.shape[1], d)
    for l in range(depth):
        lam_init = _diff_lambda_init(l)
        w_in_l = w_in[l]
        n = _rmsnorm(h, g_mix[l], BF16, "rms_mix")
        q_scale = DA_HEAD_DIM ** -0.5 * math.log2(math.e)
        qk_gain = jnp.concatenate([_tile_cols(g_da_q[l], da_w) * q_scale, _tile_cols(g_da_k[l], da_w)], axis=1)
        ff = w_e_gate.shape[3]
        qk, we_gate = _matmul(n, w_in_l, col0=0, ncols=2 * da_w, out_dtype=BF16,
                              epilogue=functools.partial(_ep_group_norm, group=DA_HEAD_DIM),
                              col_extras=[(qk_gain, 0)], side_cast=w_e_gate[l].reshape(n_experts * d, ff),
                              name="proj_qk")
        v = _matmul(n, w_in_l, col0=2 * da_w, ncols=da_w, out_dtype=BF16, name="proj_v")
        hg, we_down = _matmul(n, w_in_l, col0=3 * da_w, ncols=4 * hg_w, out_dtype=F32,
                              side_cast=w_e_down[l].reshape(n_experts * ff, d), name="proj_hg")
        gates, we_up = _matmul(n, w_gate[l], out_dtype=BF16, epilogue=_ep_bias_sigmoid,
                               col_extras=[(b_gate[l].reshape(1, -1).astype(F32), 0)],
                               side_cast=w_e_up[l].reshape(n_experts * d, ff), name="proj_gate")
        lamv = jnp.stack([lam_q1[l], lam_k1[l], lam_q2[l], lam_k2[l]]).astype(F32)
        y_a = _diff_attention_any(qk.reshape(b, s, 2 * da_w), v.reshape(b, s, da_w), lamv,
                                  g_da_sub[l].reshape(1, -1).astype(F32), g_da_q[l], g_da_k[l], q_scale,
                                  da_heads, lam_init)
        y_b = _hgrn2_any(hg.reshape(b, s, 4 * hg_w), lower_bounds[l].reshape(1, hg_w),
                         g_hg_out[l].reshape(1, HG_DIM).astype(F32), hg_heads)
        merged = _merge(y_a.reshape(t, da_w), w_up_a[l], y_b.reshape(t, hg_w), w_up_b[l], gates, "merge")
        h, w_xq_bf = _matmul(merged, w_out[l], out_dtype=F32, epilogue=_ep_residual,
                             full_extras=[(h, 0)], side_cast=w_xq[l], name="mix_out")
        xq, w_xo_bf = _norm_matmul(h, g_cross[l].reshape(1, d).astype(F32), w_xq_bf,
                                   _tile_cols(g_xq[l], x_w) * X_HEAD_DIM ** -0.5, w_xo[l], out_dtype=BF16,
                                   epilogue=functools.partial(_ep_group_norm, group=X_HEAD_DIM), name="xattn_q")
        nm = _rmsnorm(mem2, g_mem[l], BF16, "rms_mem")
        w_xkv_l = w_xkv[l]
        xk = _matmul(nm, w_xkv_l, col0=0, ncols=x_w, out_dtype=BF16,
                     epilogue=functools.partial(_ep_group_norm, group=X_HEAD_DIM),
                     col_extras=[(_tile_cols(g_xk[l], x_w), 0)], name="xattn_k")
        xv = _matmul(nm, w_xkv_l, col0=x_w, ncols=x_w, out_dtype=BF16, name="xattn_v")
        xo = _cross_attention(xq.reshape(b, s, x_w), xk.reshape(b, -1, x_w), xv.reshape(b, -1, x_w), x_heads,
                              "xattn")
        n_logit = -(-(n_groups + n_experts) // LANES) * LANES
        w_r = jnp.zeros((d, n_logit), F32).at[:, :n_groups].set(w_grp[l]).at[:, n_groups:n_groups + n_experts].set(
            w_erouter[l])
        b_r = jnp.zeros((1, n_logit), F32).at[0, :n_groups].set(b_grp[l]).at[0, n_groups:n_groups + n_experts].set(
            b_erouter[l])
        w_r_hi = w_r.astype(BF16)
        w_r_lo = (w_r - w_r_hi.astype(F32)).astype(BF16)
        h, nf, route = _xout_router(xo.reshape(t, x_w), w_xo_bf, h, g_ffn[l].reshape(1, d).astype(F32), w_r_hi,
                                    w_r_lo, b_r, n_groups, n_experts, "xattn_out_router")
        gate = route[:, :TOP_K]
        expert = route[:, TOP_K:2 * TOP_K].astype(jnp.int32).reshape(-1)
        sorted_tok, blk_expert, blk_src0, blk_groups, n_real, slot = _layout(expert, n_experts, moe_blk)
        ys = _experts(nf, sorted_tok, blk_expert, blk_src0, blk_groups, n_real, we_gate.reshape(n_experts, d, ff),
                      we_up.reshape(n_experts, d, ff), we_down.reshape(n_experts, ff, d), moe_blk, "moe_experts")
        h = _combine(h, gate, ys, slot, "moe_combine")
    return h.reshape(b, s, d)
```

```python
import functools
import math

import jax
import jax.numpy as jnp
from jax import lax
from jax.experimental import pallas as pl
from jax.experimental.pallas import tpu as pltpu

F32 = jnp.float32
BF16 = jnp.bfloat16

NORM_EPS = 1e-6
DA_HEAD_DIM = 128
DA_CHUNK = 64
DA_MAX_BOUNDED_LOGIT = 50.0
HG_DIM = 128
HG_CHUNK = 64
HG_FAST_MAX_DECAY = 60.0
X_HEAD_DIM = 256
TOP_K = 2
ROW_COPY_UNROLL = 8
LANES = 128
MASK_VALUE = -1e30
V7X_VMEM_BYTES = 64 * 1024 * 1024
VMEM_CAP_BYTES = V7X_VMEM_BYTES - 6 * 1024 * 1024


def _vmem_limit(block_bytes, scratch_bytes):
    return int(min(VMEM_CAP_BYTES, max(32 * 1024 * 1024, 2 * block_bytes + scratch_bytes + 16 * 1024 * 1024)))


def _params(semantics, block_bytes, scratch_bytes=0):
    return pltpu.CompilerParams(dimension_semantics=semantics,
                                vmem_limit_bytes=_vmem_limit(block_bytes, scratch_bytes))


def _nbytes(shape, dtype):
    return math.prod(shape) * jnp.dtype(dtype).itemsize


def _rms_kernel(x_ref, g_ref, o_ref):
    x = x_ref[...].astype(F32)
    ms = jnp.mean(x * x, axis=-1, keepdims=True)
    o_ref[...] = (x * lax.rsqrt(ms + NORM_EPS) * g_ref[...]).astype(o_ref.dtype)


def _rmsnorm(x, g, out_dtype, name):
    m, d = x.shape
    tm = min(256, m)
    return pl.pallas_call(
        _rms_kernel,
        out_shape=jax.ShapeDtypeStruct((m, d), out_dtype),
        grid=(m // tm,),
        in_specs=[pl.BlockSpec((tm, d), lambda i: (i, 0)), pl.BlockSpec((1, d), lambda i: (0, 0))],
        out_specs=pl.BlockSpec((tm, d), lambda i: (i, 0)),
        compiler_params=_params(("parallel",), _nbytes((tm, d), x.dtype) + _nbytes((tm, d), out_dtype)),
        name=name,
    )(x, g.reshape(1, d).astype(F32))


def _ep_plain(acc, o_ref):
    o_ref[...] = acc.astype(o_ref.dtype)


def _ep_group_norm(acc, o_ref, gain, *, group):
    for c in range(acc.shape[1] // group):
        sl = slice(c * group, (c + 1) * group)
        xg = acc[:, sl]
        ms = jnp.mean(xg * xg, axis=-1, keepdims=True)
        o_ref[:, sl] = (xg * lax.rsqrt(ms + NORM_EPS) * gain[:, sl]).astype(o_ref.dtype)


def _ep_bias_sigmoid(acc, o_ref, bias):
    o_ref[...] = jax.nn.sigmoid(acc + bias).astype(o_ref.dtype)


def _ep_residual(acc, o_ref, res):
    o_ref[...] = (res + acc).astype(o_ref.dtype)


def _row_block(i):
    return jnp.maximum(i - 1, 0)


def _mm_kernel(a_ref, w_ref, *rest, epilogue, side):
    if side:
        *extra, side_src_ref, o_ref, side_out_ref, wb_ref = rest
    else:
        *extra, o_ref, wb_ref = rest
    i = pl.program_id(1)

    @pl.when(i == 0)
    def _():
        wb_ref[...] = w_ref[...].astype(BF16)

    @pl.when(i > 0)
    def _():
        acc = jnp.dot(a_ref[...], wb_ref[...], preferred_element_type=F32)
        epilogue(acc, o_ref, *[e[...] for e in extra])
        if side:
            side_out_ref[...] = side_src_ref[...].astype(BF16)


def _matmul(a, w, *, col0=0, ncols=None, out_dtype, epilogue=_ep_plain, col_extras=(), full_extras=(),
            side_cast=None, name):
    m, k = a.shape
    ncols = w.shape[1] - col0 if ncols is None else ncols
    tm = min(1024, m)
    tn = min(512 if k > 1024 else 1024, ncols)
    while ncols % tn or col0 % tn:
        tn //= 2
    assert m % tm == 0 and tn % LANES == 0
    n_rb = m // tm
    in_specs = [pl.BlockSpec((tm, k), lambda j, i: (_row_block(i), 0)),
                pl.BlockSpec((k, tn), lambda j, i, o=col0 // tn: (0, j + o))]
    operands = [a, w]
    block_bytes = _nbytes((tm, k), a.dtype) + _nbytes((k, tn), w.dtype) + _nbytes((tm, tn), out_dtype)
    for vec, c0 in col_extras:
        assert c0 % tn == 0
        in_specs.append(pl.BlockSpec((1, tn), lambda j, i, o=c0 // tn: (0, j + o)))
        operands.append(vec)
    for arr, c0 in full_extras:
        assert c0 % tn == 0
        in_specs.append(pl.BlockSpec((tm, tn), lambda j, i, o=c0 // tn: (_row_block(i), j + o)))
        operands.append(arr)
        block_bytes += _nbytes((tm, tn), arr.dtype)
    out_shape = jax.ShapeDtypeStruct((m, ncols), out_dtype)
    out_specs = pl.BlockSpec((tm, tn), lambda j, i: (_row_block(i), j))
    if side_cast is not None:
        rows, cols = side_cast.shape
        n_steps = (ncols // tn) * n_rb
        chunk = rows // n_steps
        assert chunk * n_steps == rows and chunk % 16 == 0, (rows, n_steps)
        side_spec = pl.BlockSpec((chunk, cols), lambda j, i: (j * n_rb + _row_block(i), 0))
        in_specs.append(side_spec)
        operands.append(side_cast)
        out_shape = (out_shape, jax.ShapeDtypeStruct((rows, cols), BF16))
        out_specs = (out_specs, side_spec)
        block_bytes += _nbytes((chunk, cols), F32) + _nbytes((chunk, cols), BF16)
    return pl.pallas_call(
        functools.partial(_mm_kernel, epilogue=epilogue, side=side_cast is not None),
        out_shape=out_shape,
        grid=(ncols // tn, n_rb + 1),
        in_specs=in_specs,
        out_specs=out_specs,
        scratch_shapes=[pltpu.VMEM((k, tn), BF16)],
        compiler_params=_params(("parallel", "arbitrary"), block_bytes, _nbytes((k, tn), BF16)),
        name=name,
    )(*operands)


def _nmm_kernel(a_ref, g_ref, w_ref, gain_ref, side_src_ref, o_ref, side_out_ref, *, epilogue):
    a = a_ref[...]
    ms = jnp.mean(a * a, axis=-1, keepdims=True)
    n = (a * lax.rsqrt(ms + NORM_EPS) * g_ref[...]).astype(BF16)
    epilogue(jnp.dot(n, w_ref[...], preferred_element_type=F32), o_ref, gain_ref[...])
    side_out_ref[...] = side_src_ref[...].astype(BF16)


def _norm_matmul(a, g, w, gain, side_cast, *, out_dtype, epilogue, name):
    m, k = a.shape
    n = w.shape[1]
    tm = min(512, m)
    n_steps = m // tm
    rows, cols = side_cast.shape
    chunk = rows // n_steps
    assert m % tm == 0 and chunk * n_steps == rows and chunk % 16 == 0
    side_spec = pl.BlockSpec((chunk, cols), lambda i: (i, 0))
    block_bytes = (_nbytes((tm, k), F32) + _nbytes((k, n), BF16) + _nbytes((tm, n), out_dtype)
                   + _nbytes((chunk, cols), F32) + _nbytes((chunk, cols), BF16))
    return pl.pallas_call(
        functools.partial(_nmm_kernel, epilogue=epilogue),
        out_shape=(jax.ShapeDtypeStruct((m, n), out_dtype), jax.ShapeDtypeStruct((rows, cols), BF16)),
        grid=(n_steps,),
        in_specs=[pl.BlockSpec((tm, k), lambda i: (i, 0)),
                  pl.BlockSpec((1, k), lambda i: (0, 0)),
                  pl.BlockSpec((k, n), lambda i: (0, 0)),
                  pl.BlockSpec((1, n), lambda i: (0, 0)),
                  side_spec],
        out_specs=(pl.BlockSpec((tm, n), lambda i: (i, 0)), side_spec),
        compiler_params=_params(("parallel",), block_bytes),
        name=name,
    )(a, g, w, gain, side_cast)


def _merge_kernel(ya_ref, wa_ref, yb_ref, wb_ref, ga_ref, gb_ref, o_ref, wa_bf_ref, wb_bf_ref):
    i = pl.program_id(1)

    @pl.when(i == 0)
    def _():
        wa_bf_ref[...] = wa_ref[...].astype(BF16)
        wb_bf_ref[...] = wb_ref[...].astype(BF16)

    @pl.when(i > 0)
    def _():
        pa = jnp.dot(ya_ref[...], wa_bf_ref[...], preferred_element_type=F32)
        pb = jnp.dot(yb_ref[...], wb_bf_ref[...], preferred_element_type=F32)
        o_ref[...] = (ga_ref[...].astype(F32) * pa + gb_ref[...].astype(F32) * pb).astype(o_ref.dtype)


def _merge(ya, wa, yb, wb, gates, name):
    m, ka = ya.shape
    kb = yb.shape[1]
    d = wa.shape[1]
    tm = min(512, m)
    tn = min(512, d)
    nb = d // tn
    block_bytes = (_nbytes((tm, ka), BF16) + _nbytes((ka, tn), F32) + _nbytes((tm, kb), BF16)
                   + _nbytes((kb, tn), F32) + 3 * _nbytes((tm, tn), BF16))
    return pl.pallas_call(
        _merge_kernel,
        out_shape=jax.ShapeDtypeStruct((m, d), BF16),
        grid=(nb, m // tm + 1),
        in_specs=[pl.BlockSpec((tm, ka), lambda j, i: (_row_block(i), 0)),
                  pl.BlockSpec((ka, tn), lambda j, i: (0, j)),
                  pl.BlockSpec((tm, kb), lambda j, i: (_row_block(i), 0)),
                  pl.BlockSpec((kb, tn), lambda j, i: (0, j)),
                  pl.BlockSpec((tm, tn), lambda j, i: (_row_block(i), j)),
                  pl.BlockSpec((tm, tn), lambda j, i: (_row_block(i), j + nb))],
        out_specs=pl.BlockSpec((tm, tn), lambda j, i: (_row_block(i), j)),
        scratch_shapes=[pltpu.VMEM((ka, tn), BF16), pltpu.VMEM((kb, tn), BF16)],
        compiler_params=_params(("parallel", "arbitrary"), block_bytes, _nbytes((ka + kb, tn), BF16)),
        name=name,
    )(ya, wa, yb, wb, gates, gates)


def _da_kernel(lamv_ref, gsub_ref, q_ref, k_ref, v_ref, o_ref, acc_ref, l_ref, *m_ref, tq, tsub, lam_init,
               bounded):
    qi = pl.program_id(2)
    hd = DA_HEAD_DIM
    m_ref = None if bounded else m_ref[0]
    if not bounded:
        m_ref[...] = jnp.full(m_ref.shape, MASK_VALUE, F32)
    l_ref[...] = jnp.zeros(l_ref.shape, F32)
    acc_ref[...] = jnp.zeros(acc_ref.shape, F32)
    def widen(x, width):
        return jnp.concatenate([x] * (width // LANES), axis=-1)

    def update(row0, nrows, off, width, lead):
        rows = slice(row0, row0 + nrows)
        qq = q_ref[0, rows, :]
        kk = k_ref[0, pl.ds(off, width), :]
        vv = v_ref[0, pl.ds(off, width), :]
        for c in range(2):
            s = lax.dot_general(qq[:, c * hd:(c + 1) * hd], kk[:, c * hd:(c + 1) * hd],
                                (((1,), (1,)), ((), ())), preferred_element_type=F32)
            if lead is not None:
                row_chunk = (lax.broadcasted_iota(jnp.int32, (nrows, width), 0) + lead) // DA_CHUNK
                col_chunk = lax.broadcasted_iota(jnp.int32, (nrows, width), 1) // DA_CHUNK
                s = jnp.where(col_chunk <= row_chunk, s, MASK_VALUE)
            if bounded:
                p = jnp.exp2(s)
            else:
                m_old = m_ref[c, rows]
                m_new = jnp.maximum(m_old, jnp.max(s, axis=-1, keepdims=True))
                p = jnp.exp2(s - widen(m_new, width))
                alpha = jnp.exp2(m_old - m_new)
                m_ref[c, rows] = m_new
            psum = p[:, :LANES]
            for g in range(1, width // LANES):
                psum = psum + p[:, g * LANES:(g + 1) * LANES]
            pv = jnp.dot(p.astype(BF16), vv, preferred_element_type=F32)
            if bounded:
                l_ref[c, rows] = l_ref[c, rows] + psum
                acc_ref[c, rows] = acc_ref[c, rows] + pv
            else:
                l_ref[c, rows] = alpha * l_ref[c, rows] + psum
                acc_ref[c, rows] = widen(alpha, 2 * hd) * acc_ref[c, rows] + pv

    def full_body(j, carry):
        update(0, tq, pl.multiple_of(j * tq, tq), tq, None)
        return carry

    lax.fori_loop(0, qi, full_body, 0)
    for r in range(tq // tsub):
        update(r * tsub, tsub, pl.multiple_of(qi * tq, tq), (r + 1) * tsub, r * tsub)

    lv = lamv_ref[...]
    lam = (jnp.exp(jnp.sum(lv[0:1] * lv[1:2], axis=-1, keepdims=True))
           - jnp.exp(jnp.sum(lv[2:3] * lv[3:4], axis=-1, keepdims=True)) + lam_init)
    l1 = jnp.sum(l_ref[0], axis=-1, keepdims=True)
    l2 = jnp.sum(l_ref[1], axis=-1, keepdims=True)
    o = acc_ref[0] / l1 - lam * (acc_ref[1] / l2)
    ms = jnp.mean(o * o, axis=-1, keepdims=True)
    o_ref[0] = ((o * lax.rsqrt(ms + NORM_EPS) * gsub_ref[...]) * (1.0 - lam_init)).astype(o_ref.dtype)


def _diff_attention(qk, v, lamv, g_sub, n_heads, lam_init, bounded, name):
    b, s, _ = v.shape
    hw = 2 * DA_HEAD_DIM
    tq = min(2048, s)
    tsub = min(256, tq)
    block_bytes = 2 * _nbytes((tq, hw), BF16) + 2 * _nbytes((s, hw), BF16) + 6 * _nbytes((tq, tq), F32)
    stats = [pltpu.VMEM((2, tq, LANES), F32)] * (1 if bounded else 2)
    return pl.pallas_call(
        functools.partial(_da_kernel, tq=tq, tsub=tsub, lam_init=lam_init, bounded=bounded),
        out_shape=jax.ShapeDtypeStruct((b, s, n_heads * hw), BF16),
        grid=(b, n_heads, s // tq),
        in_specs=[pl.BlockSpec((4, DA_HEAD_DIM), lambda bi, h, i: (0, 0)),
                  pl.BlockSpec((1, hw), lambda bi, h, i: (0, 0)),
                  pl.BlockSpec((1, tq, hw), lambda bi, h, i: (bi, i, h)),
                  pl.BlockSpec((1, s, hw), lambda bi, h, i: (bi, 0, n_heads + h)),
                  pl.BlockSpec((1, s, hw), lambda bi, h, i: (bi, 0, h))],
        out_specs=pl.BlockSpec((1, tq, hw), lambda bi, h, i: (bi, i, h)),
        scratch_shapes=[pltpu.VMEM((2, tq, hw), F32)] + stats,
        compiler_params=_params(("parallel", "parallel", "arbitrary"), block_bytes),
        name=name,
    )(lamv, g_sub, qk, qk, v)


def _diff_attention_any(qk, v, lamv, g_sub, g_q, g_k, q_scale, n_heads, lam_init):
    bound = 1.01 * DA_HEAD_DIM * q_scale * jnp.max(jnp.abs(g_q)) * jnp.max(jnp.abs(g_k))
    return lax.cond(
        bound < DA_MAX_BOUNDED_LOGIT,
        lambda: _diff_attention(qk, v, lamv, g_sub, n_heads, lam_init, True, "diff_attn"),
        lambda: _diff_attention(qk, v, lamv, g_sub, n_heads, lam_init, False, "diff_attn_online"))


def _hg_forget(lb_ref, f_ref, rows, sl):
    lb = lb_ref[:, sl]
    return lb + (1.0 - lb) * jax.nn.sigmoid(f_ref[0, rows, sl])


def _hg_finish(o, gout_ref, gate):
    ms = jnp.mean(o * o, axis=-1, keepdims=True)
    return (o * lax.rsqrt(ms + NORM_EPS) * gout_ref[...]) * (gate * jax.nn.sigmoid(gate))


def _hg_kernel(lb_ref, gout_ref, q_ref, f_ref, i_ref, g_ref, o_ref, dec_ref, st_ref, *, heads, steps):
    t = pl.program_id(2)
    ch, hd = HG_CHUNK, HG_DIM

    @pl.when(t == 0)
    def _():
        st_ref[...] = jnp.zeros(st_ref.shape, F32)

    r = lax.broadcasted_iota(jnp.int32, (ch, ch), 0)
    c = lax.broadcasted_iota(jnp.int32, (ch, ch), 1)
    causal = c <= r
    row = lax.broadcasted_iota(jnp.int32, (ch, hd), 0)
    nt = (((1,), (1,)), ((), ()))
    tn = (((0,), (0,)), ((), ()))

    def chunk(rows, sl, st):
        q = q_ref[0, rows, sl]
        f = _hg_forget(lb_ref, f_ref, rows, sl)
        key = 1.0 - f
        lf = jnp.log(f)
        cum = lf
        for sh in (1, 2, 4, 8, 16, 32):
            cum = cum + jnp.where(row >= sh, pltpu.roll(cum, sh, axis=0), 0.0)
        last = cum[ch - 1:ch]
        q_dec = (q * jnp.exp(cum)).astype(BF16)
        k_inv = (key * jnp.exp(-cum)).astype(BF16)
        k_state = (key * jnp.exp(last - cum)).astype(BF16)
        vals = i_ref[0, rows, sl].astype(BF16)
        scores = jnp.where(causal, lax.dot_general(q_dec, k_inv, nt, preferred_element_type=F32), 0.0)
        o = (jnp.dot(scores.astype(BF16), vals, preferred_element_type=F32)
             + lax.dot_general(q_dec, st.astype(BF16), nt, preferred_element_type=F32))
        st_next = jnp.exp(last) * st + lax.dot_general(vals, k_state, tn, preferred_element_type=F32)
        return _hg_finish(o, gout_ref, g_ref[0, rows, sl]), st_next, -last

    worst = jnp.zeros((1, hd), F32)
    for h in range(heads):
        sl = slice(h * hd, (h + 1) * hd)
        st = st_ref[h]
        for ci in range(steps // ch):
            rows = slice(ci * ch, (ci + 1) * ch)
            y, st, decay = chunk(rows, sl, st)
            o_ref[0, rows, sl] = y.astype(o_ref.dtype)
            worst = jnp.maximum(worst, decay)
        st_ref[h] = st
    dec_ref[...] = jnp.broadcast_to(worst, dec_ref.shape)


def _hg_seq_kernel(lb_ref, gout_ref, q_ref, f_ref, i_ref, g_ref, o_ref, st_ref, *, steps):
    t = pl.program_id(2)
    hd = HG_DIM

    @pl.when(t == 0)
    def _():
        st_ref[...] = jnp.zeros(st_ref.shape, F32)

    lane = lax.broadcasted_iota(jnp.int32, (hd, hd), 1)
    sl = slice(0, hd)

    def block(bi, carry):
        rows = pl.ds(pl.multiple_of(bi * hd, hd), hd)
        q = q_ref[0, rows, :]
        f = _hg_forget(lb_ref, f_ref, rows, sl)
        key = 1.0 - f
        vals_t = i_ref[0, rows, :].T
        st = st_ref[...]
        out_t = jnp.zeros((hd, hd), F32)
        for s in range(hd):
            st = f[s:s + 1] * st + vals_t[:, s:s + 1] * key[s:s + 1]
            out_t = jnp.where(lane == s, jnp.sum(st * q[s:s + 1], axis=1, keepdims=True), out_t)
        st_ref[...] = st
        o_ref[0, rows, :] = _hg_finish(out_t.T, gout_ref, g_ref[0, rows, :]).astype(o_ref.dtype)
        return carry

    lax.fori_loop(0, steps // hd, block, 0)


def _hgrn2(hg, lower, g_out, n_heads, heads, kernel_fn, with_decay, name):
    b, s, _ = hg.shape
    steps = min(512, s)
    nhb = n_heads // heads
    nt = s // steps
    w = heads * HG_DIM
    block_bytes = 4 * _nbytes((steps, w), F32) + _nbytes((steps, w), BF16)

    def spec(part):
        return pl.BlockSpec((1, steps, w), lambda bi, h, t, p=part: (bi, t, p * nhb + h))

    out_shape = jax.ShapeDtypeStruct((b, s, n_heads * HG_DIM), BF16)
    out_specs = pl.BlockSpec((1, steps, w), lambda bi, h, t: (bi, t, h))
    if with_decay:
        sub = 8
        out_shape = (out_shape, jax.ShapeDtypeStruct((b * nhb * nt * sub, LANES), F32))
        out_specs = (out_specs, pl.BlockSpec((sub, LANES), lambda bi, h, t: ((bi * nhb + h) * nt + t, 0)))
    return pl.pallas_call(
        functools.partial(kernel_fn, steps=steps),
        out_shape=out_shape,
        grid=(b, nhb, nt),
        in_specs=[pl.BlockSpec((1, w), lambda bi, h, t: (0, h)),
                  pl.BlockSpec((1, HG_DIM), lambda bi, h, t: (0, 0)),
                  spec(0), spec(1), spec(2), spec(3)],
        out_specs=out_specs,
        scratch_shapes=[pltpu.VMEM((heads, HG_DIM, HG_DIM) if heads > 1 else (HG_DIM, HG_DIM), F32)],
        compiler_params=_params(("parallel", "parallel", "arbitrary"), block_bytes),
        name=name,
    )(lower, g_out, hg, hg, hg, hg)


def _hgrn2_any(hg, lower, g_out, n_heads):
    heads = min(8, n_heads)
    y, decay = _hgrn2(hg, lower, g_out, n_heads, heads, functools.partial(_hg_kernel, heads=heads), True,
                      "hgrn2")
    return lax.cond(
        jnp.max(decay) < HG_FAST_MAX_DECAY,
        lambda: y,
        lambda: _hgrn2(hg, lower, g_out, n_heads, 1, _hg_seq_kernel, False, "hgrn2_stepwise"))


def _xattn_kernel(q_ref, k_ref, v_ref, o_ref, *, n_heads):
    hd = X_HEAD_DIM
    for h in range(n_heads):
        sl = slice(h * hd, (h + 1) * hd)
        s = lax.dot_general(q_ref[0, :, sl], k_ref[0, :, sl], (((1,), (1,)), ((), ())),
                            preferred_element_type=F32)
        p = jnp.exp(s - jnp.max(s, axis=-1, keepdims=True))
        denom = jnp.sum(p, axis=-1, keepdims=True)
        o = jnp.dot(p.astype(BF16), v_ref[0, :, sl], preferred_element_type=F32)
        o_ref[0, :, sl] = (o / denom).astype(o_ref.dtype)


def _cross_attention(q, k, v, n_heads, name):
    b, s, w = q.shape
    mlen = k.shape[1]
    tq = min(1024, s)
    block_bytes = 2 * _nbytes((tq, w), BF16) + 2 * _nbytes((mlen, w), BF16) + 3 * _nbytes((tq, mlen), F32)
    return pl.pallas_call(
        functools.partial(_xattn_kernel, n_heads=n_heads),
        out_shape=jax.ShapeDtypeStruct((b, s, w), BF16),
        grid=(b, s // tq),
        in_specs=[pl.BlockSpec((1, tq, w), lambda bi, i: (bi, i, 0)),
                  pl.BlockSpec((1, mlen, w), lambda bi, i: (bi, 0, 0)),
                  pl.BlockSpec((1, mlen, w), lambda bi, i: (bi, 0, 0))],
        out_specs=pl.BlockSpec((1, tq, w), lambda bi, i: (bi, i, 0)),
        compiler_params=_params(("parallel", "parallel"), block_bytes),
        name=name,
    )(q, k, v)


def _pack_bf16_pair(lo, hi):
    lo_bits = lax.bitcast_convert_type(lo.astype(BF16).astype(F32), jnp.uint32)
    hi_bits = lax.bitcast_convert_type(hi.astype(BF16).astype(F32), jnp.uint32)
    return (lo_bits >> 16) | (hi_bits & jnp.uint32(0xFFFF0000))


def _unpack_bf16_pair(words):
    lo = lax.bitcast_convert_type(words << 16, F32)
    hi = lax.bitcast_convert_type(words & jnp.uint32(0xFFFF0000), F32)
    return lo, hi


def _router_kernel(xo_ref, wo_ref, res_ref, g_ref, whi_ref, wlo_ref, b_ref, h_ref, n_ref, rt_ref, *, n_groups,
                   n_experts):
    x = res_ref[...] + jnp.dot(xo_ref[...], wo_ref[...], preferred_element_type=F32)
    h_ref[...] = x
    ms = jnp.mean(x * x, axis=-1, keepdims=True)
    n = x * lax.rsqrt(ms + NORM_EPS) * g_ref[...]
    n_hi = n.astype(BF16)
    n_lo = (n - n_hi.astype(F32)).astype(BF16)
    half = n.shape[1] // 2
    n_ref[...] = _pack_bf16_pair(n[:, :half], n[:, half:])
    logits = (jnp.dot(n_hi, whi_ref[...], preferred_element_type=F32)
              + jnp.dot(n_lo, whi_ref[...], preferred_element_type=F32)
              + jnp.dot(n_hi, wlo_ref[...], preferred_element_type=F32) + b_ref[...])
    epg = n_experts // n_groups
    lane = lax.broadcasted_iota(jnp.int32, logits.shape, 1)
    n_lanes = logits.shape[1]

    def top1(vals):
        best = jnp.max(vals, axis=-1, keepdims=True)
        return best, jnp.min(jnp.where(vals == best, lane, n_lanes), axis=-1, keepdims=True)

    is_grp = lane < n_groups
    g_max, grp = top1(jnp.where(is_grp, logits, -jnp.inf))
    grp_w = 1.0 / jnp.sum(jnp.where(is_grp, jnp.exp(logits - g_max), 0.0), axis=-1, keepdims=True)
    first = n_groups + grp * epg
    in_grp = jnp.where(jnp.logical_and(lane >= first, lane < first + epg), logits, -jnp.inf)
    v1, i1 = top1(in_grp)
    v2, i2 = top1(jnp.where(lane == i1, -jnp.inf, in_grp))
    e2 = jnp.exp(v2 - v1)
    p1 = 1.0 / (1.0 + e2)
    cols = [grp_w * p1, grp_w * (e2 * p1), (i1 - n_groups).astype(F32), (i2 - n_groups).astype(F32)]
    rt = jnp.zeros(logits.shape, F32)
    for c, val in enumerate(cols):
        rt = jnp.where(lane == c, val, rt)
    rt_ref[...] = rt


def _xout_router(xo, w_o, res, g, w_hi, w_lo, bias, n_groups, n_experts, name):
    m, d = res.shape
    kx = xo.shape[1]
    nl = w_hi.shape[1]
    tm = min(256, m)
    block_bytes = (_nbytes((tm, kx), BF16) + _nbytes((kx, d), BF16) + 3 * _nbytes((tm, d), F32)
                   + 2 * _nbytes((d, nl), BF16))
    return pl.pallas_call(
        functools.partial(_router_kernel, n_groups=n_groups, n_experts=n_experts),
        out_shape=(jax.ShapeDtypeStruct((m, d), F32), jax.ShapeDtypeStruct((m, d // 2), jnp.uint32),
                   jax.ShapeDtypeStruct((m, nl), F32)),
        grid=(m // tm,),
        in_specs=[pl.BlockSpec((tm, kx), lambda i: (i, 0)),
                  pl.BlockSpec((kx, d), lambda i: (0, 0)),
                  pl.BlockSpec((tm, d), lambda i: (i, 0)),
                  pl.BlockSpec((1, d), lambda i: (0, 0)),
                  pl.BlockSpec((d, nl), lambda i: (0, 0)),
                  pl.BlockSpec((d, nl), lambda i: (0, 0)),
                  pl.BlockSpec((1, nl), lambda i: (0, 0))],
        out_specs=(pl.BlockSpec((tm, d), lambda i: (i, 0)), pl.BlockSpec((tm, d // 2), lambda i: (i, 0)),
                   pl.BlockSpec((tm, nl), lambda i: (i, 0))),
        compiler_params=_params(("parallel",), block_bytes),
        name=name,
    )(xo, w_o, res, g, w_hi, w_lo, bias)


def _row_copies(src_ref, dst_ref, idx_ref, idx0, idx_stride, dst0, n_groups, sem, *, start):
    def body(g, carry):
        for u in range(ROW_COPY_UNROLL):
            r = g * ROW_COPY_UNROLL + u
            cp = pltpu.make_async_copy(src_ref.at[pl.ds(idx_ref[idx0 + r * idx_stride], 1)],
                                       dst_ref.at[pl.ds(dst0 + r, 1)], sem)
            if start:
                cp.start(priority=1)
            else:
                cp.wait()
        return carry

    lax.fori_loop(0, n_groups, body, 0)


def _expert_kernel(tok_ref, be_ref, src0_ref, ng_ref, nreal_ref, nf_ref, wg_ref, wu_ref, wd_ref, o_ref, xbuf, sems):
    i = pl.program_id(0)
    n_real = nreal_ref[0]

    def gather(block, start):
        slot = block % 2
        _row_copies(nf_ref, xbuf.at[slot], tok_ref, src0_ref[block], 1, 0, ng_ref[block], sems.at[slot],
                    start=start)

    @pl.when(i == 0)
    def _():
        xbuf[...] = jnp.zeros(xbuf.shape, xbuf.dtype)

    @pl.when(jnp.logical_and(i == 0, n_real > 0))
    def _():
        gather(i, True)

    @pl.when(i + 1 < n_real)
    def _():
        gather(i + 1, True)

    @pl.when(i < n_real)
    def _():
        gather(i, False)
        x_lo, x_hi = _unpack_bf16_pair(xbuf[i % 2])
        x_lo, x_hi = x_lo.astype(BF16), x_hi.astype(BF16)
        half = x_lo.shape[1]

        def up(w_ref):
            return (jnp.dot(x_lo, w_ref[0, :half, :], preferred_element_type=F32)
                    + jnp.dot(x_hi, w_ref[0, half:, :], preferred_element_type=F32))

        hg = up(wg_ref)
        hid = (hg * jax.nn.sigmoid(hg) * up(wu_ref)).astype(BF16)
        y = jnp.dot(hid, wd_ref[0], preferred_element_type=F32)
        o_ref[...] = _pack_bf16_pair(y[:, :half], y[:, half:])

    @pl.when(i >= n_real)
    def _():
        o_ref[...] = jnp.zeros(o_ref.shape, o_ref.dtype)


def _experts(nf, sorted_tok, blk_expert, blk_src0, blk_groups, n_real, wg, wu, wd, blk, name):
    n_blk = blk_expert.shape[0]
    d = wg.shape[1]
    ff = wg.shape[2]
    half = d // 2
    block_bytes = 3 * _nbytes((d, ff), BF16) + _nbytes((blk, half), jnp.uint32) + 3 * _nbytes((blk, d), F32)

    def weights(i, tk, be, s0, ng, nr):
        return (be[i], 0, 0)

    return pl.pallas_call(
        _expert_kernel,
        out_shape=jax.ShapeDtypeStruct((n_blk * blk, half), jnp.uint32),
        grid_spec=pltpu.PrefetchScalarGridSpec(
            num_scalar_prefetch=5,
            grid=(n_blk,),
            in_specs=[pl.BlockSpec(memory_space=pl.ANY),
                      pl.BlockSpec((1, d, ff), weights),
                      pl.BlockSpec((1, d, ff), weights),
                      pl.BlockSpec((1, ff, d), weights)],
            out_specs=pl.BlockSpec((blk, half), lambda i, tk, be, s0, ng, nr: (i, 0)),
            scratch_shapes=[pltpu.VMEM((2, blk, half), jnp.uint32), pltpu.SemaphoreType.DMA((2,))]),
        compiler_params=_params(("arbitrary",), block_bytes, 2 * _nbytes((blk, half), jnp.uint32)),
        name=name,
    )(sorted_tok, blk_expert, blk_src0, blk_groups, n_real, nf, wg, wu, wd)


def _combine_kernel(slot_ref, h_ref, gate_ref, ys_ref, o_ref, ybuf, sems, *, tm):
    i = pl.program_id(0)

    def gather(block, start):
        s = block % 2
        for k in range(TOP_K):
            _row_copies(ys_ref, ybuf.at[s], slot_ref, block * tm * TOP_K + k, TOP_K, k * tm,
                        tm // ROW_COPY_UNROLL, sems.at[s], start=start)

    @pl.when(i == 0)
    def _():
        gather(i, True)

    @pl.when(i + 1 < pl.num_programs(0))
    def _():
        gather(i + 1, True)

    gather(i, False)
    half = ybuf.shape[2]
    gate = gate_ref[...]
    acc_lo = h_ref[:, :half]
    acc_hi = h_ref[:, half:]
    for k in range(TOP_K):
        y_lo, y_hi = _unpack_bf16_pair(ybuf[i % 2, pl.ds(k * tm, tm), :])
        acc_lo = acc_lo + gate[:, k:k + 1] * y_lo
        acc_hi = acc_hi + gate[:, k:k + 1] * y_hi
    o_ref[:, :half] = acc_lo
    o_ref[:, half:] = acc_hi


def _combine(h, gate, ys, slot, name):
    m, d = h.shape
    tm = min(128, m)
    block_bytes = 2 * _nbytes((tm, d), F32) + TOP_K * _nbytes((tm, d), F32)
    return pl.pallas_call(
        functools.partial(_combine_kernel, tm=tm),
        out_shape=jax.ShapeDtypeStruct((m, d), F32),
        grid_spec=pltpu.PrefetchScalarGridSpec(
            num_scalar_prefetch=1,
            grid=(m // tm,),
            in_specs=[pl.BlockSpec((tm, d), lambda i, sl: (i, 0)),
                      pl.BlockSpec((tm, TOP_K), lambda i, sl: (i, 0)),
                      pl.BlockSpec(memory_space=pl.ANY)],
            out_specs=pl.BlockSpec((tm, d), lambda i, sl: (i, 0)),
            scratch_shapes=[pltpu.VMEM((2, TOP_K * tm, d // 2), jnp.uint32), pltpu.SemaphoreType.DMA((2,))]),
        compiler_params=_params(("arbitrary",), block_bytes),
        name=name,
    )(slot, h, gate, ys)


def _layout(expert, n_experts, blk):
    n_assign = expert.shape[0]
    e_sorted, order = lax.sort_key_val(expert, jnp.arange(n_assign, dtype=jnp.int32))
    experts = jnp.arange(n_experts, dtype=jnp.int32)
    counts = jnp.sum((expert[:, None] == experts[None, :]).astype(jnp.int32), axis=0)
    starts = jnp.cumsum(counts) - counts
    padded = (counts + blk - 1) // blk * blk
    pad_ends = jnp.cumsum(padded)
    pad_starts = pad_ends - padded
    gap = pad_starts - starts
    gap_step = jnp.concatenate([gap[:1], gap[1:] - gap[:-1]])
    dest = jnp.arange(n_assign, dtype=jnp.int32) + jnp.cumsum(
        jnp.zeros((n_assign + 1,), jnp.int32).at[starts].add(gap_step))[:n_assign]
    _, slot = lax.sort_key_val(order, dest)
    cap = n_assign + n_experts * blk
    n_blk = cap // blk
    blk_row0 = jnp.arange(n_blk, dtype=jnp.int32) * blk
    blk_expert = jnp.minimum(jnp.sum((pad_ends[None, :] <= blk_row0[:, None]).astype(jnp.int32), axis=1),
                             n_experts - 1).astype(jnp.int32)
    n_real = (pad_ends[-1] // blk).astype(jnp.int32)
    blk_src0 = jnp.clip(blk_row0 - gap[blk_expert], 0, n_assign).astype(jnp.int32)
    blk_rows = jnp.clip(counts[blk_expert] - (blk_row0 - pad_starts[blk_expert]), 0, blk)
    blk_rows = jnp.where(jnp.arange(n_blk) < n_real, blk_rows, 0)
    blk_groups = ((blk_rows + ROW_COPY_UNROLL - 1) // ROW_COPY_UNROLL).astype(jnp.int32)
    sorted_tok = jnp.concatenate([order // TOP_K, jnp.zeros((ROW_COPY_UNROLL,), jnp.int32)])
    return sorted_tok, blk_expert, blk_src0, blk_groups, n_real.reshape(1), slot.astype(jnp.int32)


def _diff_lambda_init(layer):
    return 0.8 - 0.6 * math.exp(-0.3 * layer)


def _tile_cols(v, n):
    return jnp.tile(v.reshape(1, -1).astype(F32), (1, n // v.size))


def kernel(x, mem, g_mix, w_in, g_da_q, g_da_k, lam_q1, lam_k1, lam_q2, lam_k2, g_da_sub, hg_lower, g_hg_out,
           w_up_a, w_up_b, w_gate, b_gate, w_out, g_cross, g_mem, w_xq, w_xkv, g_xq, g_xk, w_xo, g_ffn, w_grp,
           b_grp, w_erouter, b_erouter, w_e_gate, w_e_up, w_e_down):
    b, s, d = x.shape
    t = b * s
    depth = g_mix.shape[0]
    da_w = w_up_a.shape[1]
    hg_w = w_up_b.shape[1]
    da_heads = da_w // (2 * DA_HEAD_DIM)
    hg_heads = hg_w // HG_DIM
    x_w = w_xq.shape[2]
    x_heads = x_w // X_HEAD_DIM
    n_groups = w_grp.shape[2]
    n_experts = w_erouter.shape[2]
    moe_blk = min(256, t)

    lower_bounds = jnp.cumsum(jax.nn.softmax(hg_lower.astype(F32), axis=0), axis=0)
    h = x.reshape(t, d)
    mem2 = mem.reshape(b * mem.shape[1], d)
    for l in range(depth):
        lam_init = _diff_lambda_init(l)
        w_in_l = w_in[l]
        n = _rmsnorm(h, g_mix[l], BF16, "rms_mix")
        q_scale = DA_HEAD_DIM ** -0.5 * math.log2(math.e)
        qk_gain = jnp.concatenate([_tile_cols(g_da_q[l], da_w) * q_scale, _tile_cols(g_da_k[l], da_w)], axis=1)
        ff = w_e_gate.shape[3]
        qk, we_gate = _matmul(n, w_in_l, col0=0, ncols=2 * da_w, out_dtype=BF16,
                              epilogue=functools.partial(_ep_group_norm, group=DA_HEAD_DIM),
                              col_extras=[(qk_gain, 0)], side_cast=w_e_gate[l].reshape(n_experts * d, ff),
                              name="proj_qk")
        v = _matmul(n, w_in_l, col0=2 * da_w, ncols=da_w, out_dtype=BF16, name="proj_v")
        hg, we_down = _matmul(n, w_in_l, col0=3 * da_w, ncols=4 * hg_w, out_dtype=F32,
                              side_cast=w_e_down[l].reshape(n_experts * ff, d), name="proj_hg")
        gates, we_up = _matmul(n, w_gate[l], out_dtype=BF16, epilogue=_ep_bias_sigmoid,
                               col_extras=[(b_gate[l].reshape(1, -1).astype(F32), 0)],
                               side_cast=w_e_up[l].reshape(n_experts * d, ff), name="proj_gate")
        lamv = jnp.stack([lam_q1[l], lam_k1[l], lam_q2[l], lam_k2[l]]).astype(F32)
        y_a = _diff_attention_any(qk.reshape(b, s, 2 * da_w), v.reshape(b, s, da_w), lamv,
                                  g_da_sub[l].reshape(1, -1).astype(F32), g_da_q[l], g_da_k[l], q_scale,
                                  da_heads, lam_init)
        y_b = _hgrn2_any(hg.reshape(b, s, 4 * hg_w), lower_bounds[l].reshape(1, hg_w),
                         g_hg_out[l].reshape(1, HG_DIM).astype(F32), hg_heads)
        merged = _merge(y_a.reshape(t, da_w), w_up_a[l], y_b.reshape(t, hg_w), w_up_b[l], gates, "merge")
        h, w_xq_bf = _matmul(merged, w_out[l], out_dtype=F32, epilogue=_ep_residual,
                             full_extras=[(h, 0)], side_cast=w_xq[l], name="mix_out")
        xq, w_xo_bf = _norm_matmul(h, g_cross[l].reshape(1, d).astype(F32), w_xq_bf,
                                   _tile_cols(g_xq[l], x_w) * X_HEAD_DIM ** -0.5, w_xo[l], out_dtype=BF16,
                                   epilogue=functools.partial(_ep_group_norm, group=X_HEAD_DIM), name="xattn_q")
        nm = _rmsnorm(mem2, g_mem[l], BF16, "rms_mem")
        w_xkv_l = w_xkv[l]
        xk = _matmul(nm, w_xkv_l, col0=0, ncols=x_w, out_dtype=BF16,
                     epilogue=functools.partial(_ep_group_norm, group=X_HEAD_DIM),
                     col_extras=[(_tile_cols(g_xk[l], x_w), 0)], name="xattn_k")
        xv = _matmul(nm, w_xkv_l, col0=x_w, ncols=x_w, out_dtype=BF16, name="xattn_v")
        xo = _cross_attention(xq.reshape(b, s, x_w), xk.reshape(b, -1, x_w), xv.reshape(b, -1, x_w), x_heads,
                              "xattn")
        n_logit = -(-(n_groups + n_experts) // LANES) * LANES
        w_r = jnp.zeros((d, n_logit), F32).at[:, :n_groups].set(w_grp[l]).at[:, n_groups:n_groups + n_experts].set(
            w_erouter[l])
        b_r = jnp.zeros((1, n_logit), F32).at[0, :n_groups].set(b_grp[l]).at[0, n_groups:n_groups + n_experts].set(
            b_erouter[l])
        w_r_hi = w_r.astype(BF16)
        w_r_lo = (w_r - w_r_hi.astype(F32)).astype(BF16)
        h, nf, route = _xout_router(xo.reshape(t, x_w), w_xo_bf, h, g_ffn[l].reshape(1, d).astype(F32), w_r_hi,
                                    w_r_lo, b_r, n_groups, n_experts, "xattn_out_router")
        gate = route[:, :TOP_K]
        expert = route[:, TOP_K:2 * TOP_K].astype(jnp.int32).reshape(-1)
        sorted_tok, blk_expert, blk_src0, blk_groups, n_real, slot = _layout(expert, n_experts, moe_blk)
        ys = _experts(nf, sorted_tok, blk_expert, blk_src0, blk_groups, n_real, we_gate.reshape(n_experts, d, ff),
                      we_up.reshape(n_experts, d, ff), we_down.reshape(n_experts, ff, d), moe_blk, "moe_experts")
        h = _combine(h, gate, ys, slot, "moe_combine")
    return h.reshape(b, s, d)
```

```python
import functools
import math

import jax
import jax.numpy as jnp
from jax import lax
from jax.experimental import pallas as pl
from jax.experimental.pallas import tpu as pltpu

F32 = jnp.float32
BF16 = jnp.bfloat16

NORM_EPS = 1e-6
DA_HEAD_DIM = 128
DA_CHUNK = 64
DA_MAX_BOUNDED_LOGIT = 50.0
HG_DIM = 128
HG_CHUNK = 64
HG_FAST_MAX_DECAY = 60.0
X_HEAD_DIM = 256
TOP_K = 2
ROW_COPY_UNROLL = 8
LANES = 128
MASK_VALUE = -1e30
V7X_VMEM_BYTES = 64 * 1024 * 1024
VMEM_CAP_BYTES = V7X_VMEM_BYTES - 6 * 1024 * 1024


def _vmem_limit(block_bytes, scratch_bytes):
    return int(min(VMEM_CAP_BYTES, max(32 * 1024 * 1024, 2 * block_bytes + scratch_bytes + 16 * 1024 * 1024)))


def _params(semantics, block_bytes, scratch_bytes=0):
    return pltpu.CompilerParams(dimension_semantics=semantics,
                                vmem_limit_bytes=_vmem_limit(block_bytes, scratch_bytes))


def _nbytes(shape, dtype):
    return math.prod(shape) * jnp.dtype(dtype).itemsize


def _rms_kernel(x_ref, g_ref, o_ref):
    x = x_ref[...].astype(F32)
    ms = jnp.mean(x * x, axis=-1, keepdims=True)
    o_ref[...] = (x * lax.rsqrt(ms + NORM_EPS) * g_ref[...]).astype(o_ref.dtype)


def _rmsnorm(x, g, out_dtype, name):
    m, d = x.shape
    tm = min(256, m)
    return pl.pallas_call(
        _rms_kernel,
        out_shape=jax.ShapeDtypeStruct((m, d), out_dtype),
        grid=(m // tm,),
        in_specs=[pl.BlockSpec((tm, d), lambda i: (i, 0)), pl.BlockSpec((1, d), lambda i: (0, 0))],
        out_specs=pl.BlockSpec((tm, d), lambda i: (i, 0)),
        compiler_params=_params(("parallel",), _nbytes((tm, d), x.dtype) + _nbytes((tm, d), out_dtype)),
        name=name,
    )(x, g.reshape(1, d).astype(F32))


def _ep_plain(acc, o_ref):
    o_ref[...] = acc.astype(o_ref.dtype)


def _ep_group_norm(acc, o_ref, gain, *, group):
    for c in range(acc.shape[1] // group):
        sl = slice(c * group, (c + 1) * group)
        xg = acc[:, sl]
        ms = jnp.mean(xg * xg, axis=-1, keepdims=True)
        o_ref[:, sl] = (xg * lax.rsqrt(ms + NORM_EPS) * gain[:, sl]).astype(o_ref.dtype)


def _ep_bias_sigmoid(acc, o_ref, bias):
    o_ref[...] = (0.5 * jnp.tanh(0.5 * (acc + bias)) + 0.5).astype(o_ref.dtype)


def _ep_residual(acc, o_ref, res):
    o_ref[...] = (res + acc).astype(o_ref.dtype)


def _row_block(i):
    return jnp.maximum(i - 1, 0)


def _mm_kernel(a_ref, w_ref, *rest, epilogue, side):
    if side:
        *extra, side_src_ref, o_ref, side_out_ref, wb_ref = rest
    else:
        *extra, o_ref, wb_ref = rest
    i = pl.program_id(1)

    @pl.when(i == 0)
    def _():
        wb_ref[...] = w_ref[...].astype(BF16)

    @pl.when(i > 0)
    def _():
        acc = jnp.dot(a_ref[...], wb_ref[...], preferred_element_type=F32)
        epilogue(acc, o_ref, *[e[...] for e in extra])
        if side:
            side_out_ref[...] = side_src_ref[...].astype(BF16)


def _matmul(a, w, *, col0=0, ncols=None, out_dtype, epilogue=_ep_plain, col_extras=(), full_extras=(),
            side_cast=None, name):
    m, k = a.shape
    ncols = w.shape[1] - col0 if ncols is None else ncols
    tm = min(1024, m)
    tn = min(512 if k > 1024 else 1024, ncols)
    while ncols % tn or col0 % tn:
        tn //= 2
    assert m % tm == 0 and tn % LANES == 0
    n_rb = m // tm
    in_specs = [pl.BlockSpec((tm, k), lambda j, i: (_row_block(i), 0)),
                pl.BlockSpec((k, tn), lambda j, i, o=col0 // tn: (0, j + o))]
    operands = [a, w]
    block_bytes = _nbytes((tm, k), a.dtype) + _nbytes((k, tn), w.dtype) + _nbytes((tm, tn), out_dtype)
    for vec, c0 in col_extras:
        assert c0 % tn == 0
        in_specs.append(pl.BlockSpec((1, tn), lambda j, i, o=c0 // tn: (0, j + o)))
        operands.append(vec)
    for arr, c0 in full_extras:
        assert c0 % tn == 0
        in_specs.append(pl.BlockSpec((tm, tn), lambda j, i, o=c0 // tn: (_row_block(i), j + o)))
        operands.append(arr)
        block_bytes += _nbytes((tm, tn), arr.dtype)
    out_shape = jax.ShapeDtypeStruct((m, ncols), out_dtype)
    out_specs = pl.BlockSpec((tm, tn), lambda j, i: (_row_block(i), j))
    if side_cast is not None:
        rows, cols = side_cast.shape
        n_steps = (ncols // tn) * n_rb
        chunk = rows // n_steps
        assert chunk * n_steps == rows and chunk % 16 == 0, (rows, n_steps)
        side_spec = pl.BlockSpec((chunk, cols), lambda j, i: (j * n_rb + _row_block(i), 0))
        in_specs.append(side_spec)
        operands.append(side_cast)
        out_shape = (out_shape, jax.ShapeDtypeStruct((rows, cols), BF16))
        out_specs = (out_specs, side_spec)
        block_bytes += _nbytes((chunk, cols), F32) + _nbytes((chunk, cols), BF16)
    return pl.pallas_call(
        functools.partial(_mm_kernel, epilogue=epilogue, side=side_cast is not None),
        out_shape=out_shape,
        grid=(ncols // tn, n_rb + 1),
        in_specs=in_specs,
        out_specs=out_specs,
        scratch_shapes=[pltpu.VMEM((k, tn), BF16)],
        compiler_params=_params(("parallel", "arbitrary"), block_bytes, _nbytes((k, tn), BF16)),
        name=name,
    )(*operands)


def _nmm_kernel(a_ref, g_ref, w_ref, gain_ref, side_src_ref, o_ref, side_out_ref, *, epilogue):
    a = a_ref[...]
    ms = jnp.mean(a * a, axis=-1, keepdims=True)
    n = (a * lax.rsqrt(ms + NORM_EPS) * g_ref[...]).astype(BF16)
    epilogue(jnp.dot(n, w_ref[...], preferred_element_type=F32), o_ref, gain_ref[...])
    side_out_ref[...] = side_src_ref[...].astype(BF16)


def _norm_matmul(a, g, w, gain, side_cast, *, out_dtype, epilogue, name):
    m, k = a.shape
    n = w.shape[1]
    tm = min(512, m)
    n_steps = m // tm
    rows, cols = side_cast.shape
    chunk = rows // n_steps
    assert m % tm == 0 and chunk * n_steps == rows and chunk % 16 == 0
    side_spec = pl.BlockSpec((chunk, cols), lambda i: (i, 0))
    block_bytes = (_nbytes((tm, k), F32) + _nbytes((k, n), BF16) + _nbytes((tm, n), out_dtype)
                   + _nbytes((chunk, cols), F32) + _nbytes((chunk, cols), BF16))
    return pl.pallas_call(
        functools.partial(_nmm_kernel, epilogue=epilogue),
        out_shape=(jax.ShapeDtypeStruct((m, n), out_dtype), jax.ShapeDtypeStruct((rows, cols), BF16)),
        grid=(n_steps,),
        in_specs=[pl.BlockSpec((tm, k), lambda i: (i, 0)),
                  pl.BlockSpec((1, k), lambda i: (0, 0)),
                  pl.BlockSpec((k, n), lambda i: (0, 0)),
                  pl.BlockSpec((1, n), lambda i: (0, 0)),
                  side_spec],
        out_specs=(pl.BlockSpec((tm, n), lambda i: (i, 0)), side_spec),
        compiler_params=_params(("parallel",), block_bytes),
        name=name,
    )(a, g, w, gain, side_cast)


def _merge_kernel(ya_ref, wa_ref, yb_ref, wb_ref, ga_ref, gb_ref, o_ref, wa_bf_ref, wb_bf_ref):
    i = pl.program_id(1)

    @pl.when(i == 0)
    def _():
        wa_bf_ref[...] = wa_ref[...].astype(BF16)
        wb_bf_ref[...] = wb_ref[...].astype(BF16)

    @pl.when(i > 0)
    def _():
        pa = jnp.dot(ya_ref[...], wa_bf_ref[...], preferred_element_type=F32)
        pb = jnp.dot(yb_ref[...], wb_bf_ref[...], preferred_element_type=F32)
        o_ref[...] = (ga_ref[...].astype(F32) * pa + gb_ref[...].astype(F32) * pb).astype(o_ref.dtype)


def _merge(ya, wa, yb, wb, gates, name):
    m, ka = ya.shape
    kb = yb.shape[1]
    d = wa.shape[1]
    tm = min(512, m)
    tn = min(512, d)
    nb = d // tn
    block_bytes = (_nbytes((tm, ka), BF16) + _nbytes((ka, tn), F32) + _nbytes((tm, kb), BF16)
                   + _nbytes((kb, tn), F32) + 3 * _nbytes((tm, tn), BF16))
    return pl.pallas_call(
        _merge_kernel,
        out_shape=jax.ShapeDtypeStruct((m, d), BF16),
        grid=(nb, m // tm + 1),
        in_specs=[pl.BlockSpec((tm, ka), lambda j, i: (_row_block(i), 0)),
                  pl.BlockSpec((ka, tn), lambda j, i: (0, j)),
                  pl.BlockSpec((tm, kb), lambda j, i: (_row_block(i), 0)),
                  pl.BlockSpec((kb, tn), lambda j, i: (0, j)),
                  pl.BlockSpec((tm, tn), lambda j, i: (_row_block(i), j)),
                  pl.BlockSpec((tm, tn), lambda j, i: (_row_block(i), j + nb))],
        out_specs=pl.BlockSpec((tm, tn), lambda j, i: (_row_block(i), j)),
        scratch_shapes=[pltpu.VMEM((ka, tn), BF16), pltpu.VMEM((kb, tn), BF16)],
        compiler_params=_params(("parallel", "arbitrary"), block_bytes, _nbytes((ka + kb, tn), BF16)),
        name=name,
    )(ya, wa, yb, wb, gates, gates)


def _da_kernel(lamv_ref, gsub_ref, q_ref, k_ref, v_ref, o_ref, acc_ref, l_ref, *m_ref, tq, tsub, lam_init,
               bounded):
    qi = pl.program_id(2)
    hd = DA_HEAD_DIM
    m_ref = None if bounded else m_ref[0]
    if not bounded:
        m_ref[...] = jnp.full(m_ref.shape, MASK_VALUE, F32)
    l_ref[...] = jnp.zeros(l_ref.shape, F32)
    acc_ref[...] = jnp.zeros(acc_ref.shape, F32)
    def widen(x, width):
        return jnp.concatenate([x] * (width // LANES), axis=-1)

    def update(row0, nrows, off, width, lead):
        rows = slice(row0, row0 + nrows)
        qq = q_ref[0, rows, :]
        kk = k_ref[0, pl.ds(off, width), :]
        vv = v_ref[0, pl.ds(off, width), :]
        for c in range(2):
            s = lax.dot_general(qq[:, c * hd:(c + 1) * hd], kk[:, c * hd:(c + 1) * hd],
                                (((1,), (1,)), ((), ())), preferred_element_type=F32)
            if lead is not None:
                row_chunk = (lax.broadcasted_iota(jnp.int32, (nrows, width), 0) + lead) // DA_CHUNK
                col_chunk = lax.broadcasted_iota(jnp.int32, (nrows, width), 1) // DA_CHUNK
                s = jnp.where(col_chunk <= row_chunk, s, MASK_VALUE)
            if bounded:
                p = jnp.exp2(s)
            else:
                m_old = m_ref[c, rows]
                m_new = jnp.maximum(m_old, jnp.max(s, axis=-1, keepdims=True))
                p = jnp.exp2(s - widen(m_new, width))
                alpha = jnp.exp2(m_old - m_new)
                m_ref[c, rows] = m_new
            psum = p[:, :LANES]
            for g in range(1, width // LANES):
                psum = psum + p[:, g * LANES:(g + 1) * LANES]
            pv = jnp.dot(p.astype(BF16), vv, preferred_element_type=F32)
            if bounded:
                l_ref[c, rows] = l_ref[c, rows] + psum
                acc_ref[c, rows] = acc_ref[c, rows] + pv
            else:
                l_ref[c, rows] = alpha * l_ref[c, rows] + psum
                acc_ref[c, rows] = widen(alpha, 2 * hd) * acc_ref[c, rows] + pv

    def full_body(j, carry):
        update(0, tq, pl.multiple_of(j * tq, tq), tq, None)
        return carry

    lax.fori_loop(0, qi, full_body, 0)
    for r in range(tq // tsub):
        update(r * tsub, tsub, pl.multiple_of(qi * tq, tq), (r + 1) * tsub, r * tsub)

    lv = lamv_ref[...]
    lam = (jnp.exp(jnp.sum(lv[0:1] * lv[1:2], axis=-1, keepdims=True))
           - jnp.exp(jnp.sum(lv[2:3] * lv[3:4], axis=-1, keepdims=True)) + lam_init)
    l1 = jnp.sum(l_ref[0], axis=-1, keepdims=True)
    l2 = jnp.sum(l_ref[1], axis=-1, keepdims=True)
    o = acc_ref[0] / l1 - lam * (acc_ref[1] / l2)
    ms = jnp.mean(o * o, axis=-1, keepdims=True)
    o_ref[0] = ((o * lax.rsqrt(ms + NORM_EPS) * gsub_ref[...]) * (1.0 - lam_init)).astype(o_ref.dtype)


def _diff_attention(qk, v, lamv, g_sub, n_heads, lam_init, bounded, name):
    b, s, _ = v.shape
    hw = 2 * DA_HEAD_DIM
    tq = min(2048, s)
    tsub = min(256, tq)
    block_bytes = 2 * _nbytes((tq, hw), BF16) + 2 * _nbytes((s, hw), BF16) + 6 * _nbytes((tq, tq), F32)
    stats = [pltpu.VMEM((2, tq, LANES), F32)] * (1 if bounded else 2)
    return pl.pallas_call(
        functools.partial(_da_kernel, tq=tq, tsub=tsub, lam_init=lam_init, bounded=bounded),
        out_shape=jax.ShapeDtypeStruct((b, s, n_heads * hw), BF16),
        grid=(b, n_heads, s // tq),
        in_specs=[pl.BlockSpec((4, DA_HEAD_DIM), lambda bi, h, i: (0, 0)),
                  pl.BlockSpec((1, hw), lambda bi, h, i: (0, 0)),
                  pl.BlockSpec((1, tq, hw), lambda bi, h, i: (bi, i, h)),
                  pl.BlockSpec((1, s, hw), lambda bi, h, i: (bi, 0, n_heads + h)),
                  pl.BlockSpec((1, s, hw), lambda bi, h, i: (bi, 0, h))],
        out_specs=pl.BlockSpec((1, tq, hw), lambda bi, h, i: (bi, i, h)),
        scratch_shapes=[pltpu.VMEM((2, tq, hw), F32)] + stats,
        compiler_params=_params(("parallel", "parallel", "arbitrary"), block_bytes),
        name=name,
    )(lamv, g_sub, qk, qk, v)


def _diff_attention_any(qk, v, lamv, g_sub, g_q, g_k, q_scale, n_heads, lam_init):
    bound = 1.01 * DA_HEAD_DIM * q_scale * jnp.max(jnp.abs(g_q)) * jnp.max(jnp.abs(g_k))
    return lax.cond(
        bound < DA_MAX_BOUNDED_LOGIT,
        lambda: _diff_attention(qk, v, lamv, g_sub, n_heads, lam_init, True, "diff_attn"),
        lambda: _diff_attention(qk, v, lamv, g_sub, n_heads, lam_init, False, "diff_attn_online"))


def _hg_forget(lb_ref, f_ref, rows, sl):
    lb = lb_ref[:, sl]
    return lb + (1.0 - lb) * jax.nn.sigmoid(f_ref[0, rows, sl])


def _hg_finish(o, gout_ref, gate):
    ms = jnp.mean(o * o, axis=-1, keepdims=True)
    return (o * lax.rsqrt(ms + NORM_EPS) * gout_ref[...]) * (gate * jax.nn.sigmoid(gate))


def _hg_kernel(lb_ref, gout_ref, q_ref, f_ref, i_ref, g_ref, o_ref, dec_ref, st_ref, *, heads, steps):
    t = pl.program_id(2)
    ch, hd = HG_CHUNK, HG_DIM

    @pl.when(t == 0)
    def _():
        st_ref[...] = jnp.zeros(st_ref.shape, F32)

    r = lax.broadcasted_iota(jnp.int32, (ch, ch), 0)
    c = lax.broadcasted_iota(jnp.int32, (ch, ch), 1)
    causal = c <= r
    row = lax.broadcasted_iota(jnp.int32, (ch, hd), 0)
    nt = (((1,), (1,)), ((), ()))
    tn = (((0,), (0,)), ((), ()))

    def chunk(rows, sl, st):
        q = q_ref[0, rows, sl]
        f = _hg_forget(lb_ref, f_ref, rows, sl)
        key = 1.0 - f
        lf = jnp.log(f)
        cum = lf
        for sh in (1, 2, 4, 8, 16, 32):
            cum = cum + jnp.where(row >= sh, pltpu.roll(cum, sh, axis=0), 0.0)
        last = cum[ch - 1:ch]
        q_dec = (q * jnp.exp(cum)).astype(BF16)
        k_inv = (key * jnp.exp(-cum)).astype(BF16)
        k_state = (key * jnp.exp(last - cum)).astype(BF16)
        vals = i_ref[0, rows, sl].astype(BF16)
        scores = jnp.where(causal, lax.dot_general(q_dec, k_inv, nt, preferred_element_type=F32), 0.0)
        o = (jnp.dot(scores.astype(BF16), vals, preferred_element_type=F32)
             + lax.dot_general(q_dec, st.astype(BF16), nt, preferred_element_type=F32))
        st_next = jnp.exp(last) * st + lax.dot_general(vals, k_state, tn, preferred_element_type=F32)
        return _hg_finish(o, gout_ref, g_ref[0, rows, sl]), st_next, -last

    worst = jnp.zeros((1, hd), F32)
    for h in range(heads):
        sl = slice(h * hd, (h + 1) * hd)
        st = st_ref[h]
        for ci in range(steps // ch):
            rows = slice(ci * ch, (ci + 1) * ch)
            y, st, decay = chunk(rows, sl, st)
            o_ref[0, rows, sl] = y.astype(o_ref.dtype)
            worst = jnp.maximum(worst, decay)
        st_ref[h] = st
    dec_ref[...] = jnp.broadcast_to(worst, dec_ref.shape)


def _hg_seq_kernel(lb_ref, gout_ref, q_ref, f_ref, i_ref, g_ref, o_ref, st_ref, *, steps):
    t = pl.program_id(2)
    hd = HG_DIM

    @pl.when(t == 0)
    def _():
        st_ref[...] = jnp.zeros(st_ref.shape, F32)

    lane = lax.broadcasted_iota(jnp.int32, (hd, hd), 1)
    sl = slice(0, hd)

    def block(bi, carry):
        rows = pl.ds(pl.multiple_of(bi * hd, hd), hd)
        q = q_ref[0, rows, :]
        f = _hg_forget(lb_ref, f_ref, rows, sl)
        key = 1.0 - f
        vals_t = i_ref[0, rows, :].T
        st = st_ref[...]
        out_t = jnp.zeros((hd, hd), F32)
        for s in range(hd):
            st = f[s:s + 1] * st + vals_t[:, s:s + 1] * key[s:s + 1]
            out_t = jnp.where(lane == s, jnp.sum(st * q[s:s + 1], axis=1, keepdims=True), out_t)
        st_ref[...] = st
        o_ref[0, rows, :] = _hg_finish(out_t.T, gout_ref, g_ref[0, rows, :]).astype(o_ref.dtype)
        return carry

    lax.fori_loop(0, steps // hd, block, 0)


def _hgrn2(hg, lower, g_out, n_heads, heads, kernel_fn, with_decay, name):
    b, s, _ = hg.shape
    steps = min(512, s)
    nhb = n_heads // heads
    nt = s // steps
    w = heads * HG_DIM
    block_bytes = 4 * _nbytes((steps, w), F32) + _nbytes((steps, w), BF16)

    def spec(part):
        return pl.BlockSpec((1, steps, w), lambda bi, h, t, p=part: (bi, t, p * nhb + h))

    out_shape = jax.ShapeDtypeStruct((b, s, n_heads * HG_DIM), BF16)
    out_specs = pl.BlockSpec((1, steps, w), lambda bi, h, t: (bi, t, h))
    if with_decay:
        sub = 8
        out_shape = (out_shape, jax.ShapeDtypeStruct((b * nhb * nt * sub, LANES), F32))
        out_specs = (out_specs, pl.BlockSpec((sub, LANES), lambda bi, h, t: ((bi * nhb + h) * nt + t, 0)))
    return pl.pallas_call(
        functools.partial(kernel_fn, steps=steps),
        out_shape=out_shape,
        grid=(b, nhb, nt),
        in_specs=[pl.BlockSpec((1, w), lambda bi, h, t: (0, h)),
                  pl.BlockSpec((1, HG_DIM), lambda bi, h, t: (0, 0)),
                  spec(0), spec(1), spec(2), spec(3)],
        out_specs=out_specs,
        scratch_shapes=[pltpu.VMEM((heads, HG_DIM, HG_DIM) if heads > 1 else (HG_DIM, HG_DIM), F32)],
        compiler_params=_params(("parallel", "parallel", "arbitrary"), block_bytes),
        name=name,
    )(lower, g_out, hg, hg, hg, hg)


def _hgrn2_any(hg, lower, g_out, n_heads):
    heads = min(8, n_heads)
    y, decay = _hgrn2(hg, lower, g_out, n_heads, heads, functools.partial(_hg_kernel, heads=heads), True,
                      "hgrn2")
    return lax.cond(
        jnp.max(decay) < HG_FAST_MAX_DECAY,
        lambda: y,
        lambda: _hgrn2(hg, lower, g_out, n_heads, 1, _hg_seq_kernel, False, "hgrn2_stepwise"))


def _xattn_kernel(q_ref, k_ref, v_ref, o_ref, *, n_heads):
    hd = X_HEAD_DIM
    for h in range(n_heads):
        sl = slice(h * hd, (h + 1) * hd)
        s = lax.dot_general(q_ref[0, :, sl], k_ref[0, :, sl], (((1,), (1,)), ((), ())),
                            preferred_element_type=F32)
        p = jnp.exp(s - jnp.max(s, axis=-1, keepdims=True))
        denom = jnp.sum(p, axis=-1, keepdims=True)
        o = jnp.dot(p.astype(BF16), v_ref[0, :, sl], preferred_element_type=F32)
        o_ref[0, :, sl] = (o / denom).astype(o_ref.dtype)


def _cross_attention(q, k, v, n_heads, name):
    b, s, w = q.shape
    mlen = k.shape[1]
    tq = min(1024, s)
    block_bytes = 2 * _nbytes((tq, w), BF16) + 2 * _nbytes((mlen, w), BF16) + 3 * _nbytes((tq, mlen), F32)
    return pl.pallas_call(
        functools.partial(_xattn_kernel, n_heads=n_heads),
        out_shape=jax.ShapeDtypeStruct((b, s, w), BF16),
        grid=(b, s // tq),
        in_specs=[pl.BlockSpec((1, tq, w), lambda bi, i: (bi, i, 0)),
                  pl.BlockSpec((1, mlen, w), lambda bi, i: (bi, 0, 0)),
                  pl.BlockSpec((1, mlen, w), lambda bi, i: (bi, 0, 0))],
        out_specs=pl.BlockSpec((1, tq, w), lambda bi, i: (bi, i, 0)),
        compiler_params=_params(("parallel", "parallel"), block_bytes),
        name=name,
    )(q, k, v)


def _pack_bf16_pair(lo, hi):
    lo_bits = lax.bitcast_convert_type(lo.astype(BF16).astype(F32), jnp.uint32)
    hi_bits = lax.bitcast_convert_type(hi.astype(BF16).astype(F32), jnp.uint32)
    return (lo_bits >> 16) | (hi_bits & jnp.uint32(0xFFFF0000))


def _unpack_bf16_pair(words):
    lo = lax.bitcast_convert_type(words << 16, F32)
    hi = lax.bitcast_convert_type(words & jnp.uint32(0xFFFF0000), F32)
    return lo, hi


def _router_kernel(xo_ref, wo_ref, res_ref, g_ref, whi_ref, wlo_ref, b_ref, h_ref, n_ref, rt_ref, *, n_groups,
                   n_experts):
    x = res_ref[...] + jnp.dot(xo_ref[...], wo_ref[...], preferred_element_type=F32)
    h_ref[...] = x
    ms = jnp.mean(x * x, axis=-1, keepdims=True)
    n = x * lax.rsqrt(ms + NORM_EPS) * g_ref[...]
    n_hi = n.astype(BF16)
    n_lo = (n - n_hi.astype(F32)).astype(BF16)
    half = n.shape[1] // 2
    n_ref[...] = _pack_bf16_pair(n[:, :half], n[:, half:])
    logits = (jnp.dot(n_hi, whi_ref[...], preferred_element_type=F32)
              + jnp.dot(n_lo, whi_ref[...], preferred_element_type=F32)
              + jnp.dot(n_hi, wlo_ref[...], preferred_element_type=F32) + b_ref[...])
    epg = n_experts // n_groups
    lane = lax.broadcasted_iota(jnp.int32, logits.shape, 1)
    n_lanes = logits.shape[1]

    def top1(vals):
        best = jnp.max(vals, axis=-1, keepdims=True)
        return best, jnp.min(jnp.where(vals == best, lane, n_lanes), axis=-1, keepdims=True)

    is_grp = lane < n_groups
    g_max, grp = top1(jnp.where(is_grp, logits, -jnp.inf))
    grp_w = 1.0 / jnp.sum(jnp.where(is_grp, jnp.exp(logits - g_max), 0.0), axis=-1, keepdims=True)
    first = n_groups + grp * epg
    in_grp = jnp.where(jnp.logical_and(lane >= first, lane < first + epg), logits, -jnp.inf)
    v1, i1 = top1(in_grp)
    v2, i2 = top1(jnp.where(lane == i1, -jnp.inf, in_grp))
    e2 = jnp.exp(v2 - v1)
    p1 = 1.0 / (1.0 + e2)
    cols = [grp_w * p1, grp_w * (e2 * p1), (i1 - n_groups).astype(F32), (i2 - n_groups).astype(F32)]
    rt = jnp.zeros(logits.shape, F32)
    for c, val in enumerate(cols):
        rt = jnp.where(lane == c, val, rt)
    rt_ref[...] = rt


def _xout_router(xo, w_o, res, g, w_hi, w_lo, bias, n_groups, n_experts, name):
    m, d = res.shape
    kx = xo.shape[1]
    nl = w_hi.shape[1]
    tm = min(256, m)
    block_bytes = (_nbytes((tm, kx), BF16) + _nbytes((kx, d), BF16) + 3 * _nbytes((tm, d), F32)
                   + 2 * _nbytes((d, nl), BF16))
    return pl.pallas_call(
        functools.partial(_router_kernel, n_groups=n_groups, n_experts=n_experts),
        out_shape=(jax.ShapeDtypeStruct((m, d), F32), jax.ShapeDtypeStruct((m, d // 2), jnp.uint32),
                   jax.ShapeDtypeStruct((m, nl), F32)),
        grid=(m // tm,),
        in_specs=[pl.BlockSpec((tm, kx), lambda i: (i, 0)),
                  pl.BlockSpec((kx, d), lambda i: (0, 0)),
                  pl.BlockSpec((tm, d), lambda i: (i, 0)),
                  pl.BlockSpec((1, d), lambda i: (0, 0)),
                  pl.BlockSpec((d, nl), lambda i: (0, 0)),
                  pl.BlockSpec((d, nl), lambda i: (0, 0)),
                  pl.BlockSpec((1, nl), lambda i: (0, 0))],
        out_specs=(pl.BlockSpec((tm, d), lambda i: (i, 0)), pl.BlockSpec((tm, d // 2), lambda i: (i, 0)),
                   pl.BlockSpec((tm, nl), lambda i: (i, 0))),
        compiler_params=_params(("parallel",), block_bytes),
        name=name,
    )(xo, w_o, res, g, w_hi, w_lo, bias)


def _row_copies(src_ref, dst_ref, idx_ref, idx0, idx_stride, dst0, n_groups, sem, *, start):
    def body(g, carry):
        for u in range(ROW_COPY_UNROLL):
            r = g * ROW_COPY_UNROLL + u
            cp = pltpu.make_async_copy(src_ref.at[pl.ds(idx_ref[idx0 + r * idx_stride], 1)],
                                       dst_ref.at[pl.ds(dst0 + r, 1)], sem)
            if start:
                cp.start(priority=1)
            else:
                cp.wait()
        return carry

    lax.fori_loop(0, n_groups, body, 0)


def _expert_kernel(tok_ref, be_ref, src0_ref, ng_ref, nreal_ref, nf_ref, wg_ref, wu_ref, wd_ref, o_ref, xbuf, sems):
    i = pl.program_id(0)
    n_real = nreal_ref[0]

    def gather(block, start):
        slot = block % 2
        _row_copies(nf_ref, xbuf.at[slot], tok_ref, src0_ref[block], 1, 0, ng_ref[block], sems.at[slot],
                    start=start)

    @pl.when(i == 0)
    def _():
        xbuf[...] = jnp.zeros(xbuf.shape, xbuf.dtype)

    @pl.when(jnp.logical_and(i == 0, n_real > 0))
    def _():
        gather(i, True)

    @pl.when(i + 1 < n_real)
    def _():
        gather(i + 1, True)

    @pl.when(i < n_real)
    def _():
        gather(i, False)
        x_lo, x_hi = _unpack_bf16_pair(xbuf[i % 2])
        x_lo, x_hi = x_lo.astype(BF16), x_hi.astype(BF16)
        half = x_lo.shape[1]

        def up(w_ref):
            return (jnp.dot(x_lo, w_ref[0, :half, :], preferred_element_type=F32)
                    + jnp.dot(x_hi, w_ref[0, half:, :], preferred_element_type=F32))

        hg = up(wg_ref)
        hid = (hg * jax.nn.sigmoid(hg) * up(wu_ref)).astype(BF16)
        y = jnp.dot(hid, wd_ref[0], preferred_element_type=F32)
        o_ref[...] = _pack_bf16_pair(y[:, :half], y[:, half:])

    @pl.when(i >= n_real)
    def _():
        o_ref[...] = jnp.zeros(o_ref.shape, o_ref.dtype)


def _experts(nf, sorted_tok, blk_expert, blk_src0, blk_groups, n_real, wg, wu, wd, blk, name):
    n_blk = blk_expert.shape[0]
    d = wg.shape[1]
    ff = wg.shape[2]
    half = d // 2
    block_bytes = 3 * _nbytes((d, ff), BF16) + _nbytes((blk, half), jnp.uint32) + 3 * _nbytes((blk, d), F32)

    def weights(i, tk, be, s0, ng, nr):
        return (be[i], 0, 0)

    return pl.pallas_call(
        _expert_kernel,
        out_shape=jax.ShapeDtypeStruct((n_blk * blk, half), jnp.uint32),
        grid_spec=pltpu.PrefetchScalarGridSpec(
            num_scalar_prefetch=5,
            grid=(n_blk,),
            in_specs=[pl.BlockSpec(memory_space=pl.ANY),
                      pl.BlockSpec((1, d, ff), weights),
                      pl.BlockSpec((1, d, ff), weights),
                      pl.BlockSpec((1, ff, d), weights)],
            out_specs=pl.BlockSpec((blk, half), lambda i, tk, be, s0, ng, nr: (i, 0)),
            scratch_shapes=[pltpu.VMEM((2, blk, half), jnp.uint32), pltpu.SemaphoreType.DMA((2,))]),
        compiler_params=_params(("arbitrary",), block_bytes, 2 * _nbytes((blk, half), jnp.uint32)),
        name=name,
    )(sorted_tok, blk_expert, blk_src0, blk_groups, n_real, nf, wg, wu, wd)


def _combine_kernel(slot_ref, h_ref, gate_ref, ys_ref, o_ref, ybuf, sems, *, tm):
    i = pl.program_id(0)

    def gather(block, start):
        s = block % 2
        for k in range(TOP_K):
            _row_copies(ys_ref, ybuf.at[s], slot_ref, block * tm * TOP_K + k, TOP_K, k * tm,
                        tm // ROW_COPY_UNROLL, sems.at[s], start=start)

    @pl.when(i == 0)
    def _():
        gather(i, True)

    @pl.when(i + 1 < pl.num_programs(0))
    def _():
        gather(i + 1, True)

    gather(i, False)
    half = ybuf.shape[2]
    gate = gate_ref[...]
    acc_lo = h_ref[:, :half]
    acc_hi = h_ref[:, half:]
    for k in range(TOP_K):
        y_lo, y_hi = _unpack_bf16_pair(ybuf[i % 2, pl.ds(k * tm, tm), :])
        acc_lo = acc_lo + gate[:, k:k + 1] * y_lo
        acc_hi = acc_hi + gate[:, k:k + 1] * y_hi
    o_ref[:, :half] = acc_lo
    o_ref[:, half:] = acc_hi


def _combine(h, gate, ys, slot, name):
    m, d = h.shape
    tm = min(128, m)
    block_bytes = 2 * _nbytes((tm, d), F32) + TOP_K * _nbytes((tm, d), F32)
    return pl.pallas_call(
        functools.partial(_combine_kernel, tm=tm),
        out_shape=jax.ShapeDtypeStruct((m, d), F32),
        grid_spec=pltpu.PrefetchScalarGridSpec(
            num_scalar_prefetch=1,
            grid=(m // tm,),
            in_specs=[pl.BlockSpec((tm, d), lambda i, sl: (i, 0)),
                      pl.BlockSpec((tm, TOP_K), lambda i, sl: (i, 0)),
                      pl.BlockSpec(memory_space=pl.ANY)],
            out_specs=pl.BlockSpec((tm, d), lambda i, sl: (i, 0)),
            scratch_shapes=[pltpu.VMEM((2, TOP_K * tm, d // 2), jnp.uint32), pltpu.SemaphoreType.DMA((2,))]),
        compiler_params=_params(("arbitrary",), block_bytes),
        name=name,
    )(slot, h, gate, ys)


def _layout(expert, n_experts, blk):
    n_assign = expert.shape[0]
    e_sorted, order = lax.sort_key_val(expert, jnp.arange(n_assign, dtype=jnp.int32))
    experts = jnp.arange(n_experts, dtype=jnp.int32)
    counts = jnp.sum((expert[:, None] == experts[None, :]).astype(jnp.int32), axis=0)
    starts = jnp.cumsum(counts) - counts
    padded = (counts + blk - 1) // blk * blk
    pad_ends = jnp.cumsum(padded)
    pad_starts = pad_ends - padded
    gap = pad_starts - starts
    gap_step = jnp.concatenate([gap[:1], gap[1:] - gap[:-1]])
    dest = jnp.arange(n_assign, dtype=jnp.int32) + jnp.cumsum(
        jnp.zeros((n_assign + 1,), jnp.int32).at[starts].add(gap_step))[:n_assign]
    _, slot = lax.sort_key_val(order, dest)
    cap = n_assign + n_experts * blk
    n_blk = cap // blk
    blk_row0 = jnp.arange(n_blk, dtype=jnp.int32) * blk
    blk_expert = jnp.minimum(jnp.sum((pad_ends[None, :] <= blk_row0[:, None]).astype(jnp.int32), axis=1),
                             n_experts - 1).astype(jnp.int32)
    n_real = (pad_ends[-1] // blk).astype(jnp.int32)
    blk_src0 = jnp.clip(blk_row0 - gap[blk_expert], 0, n_assign).astype(jnp.int32)
    blk_rows = jnp.clip(counts[blk_expert] - (blk_row0 - pad_starts[blk_expert]), 0, blk)
    blk_rows = jnp.where(jnp.arange(n_blk) < n_real, blk_rows, 0)
    blk_groups = ((blk_rows + ROW_COPY_UNROLL - 1) // ROW_COPY_UNROLL).astype(jnp.int32)
    sorted_tok = jnp.concatenate([order // TOP_K, jnp.zeros((ROW_COPY_UNROLL,), jnp.int32)])
    return sorted_tok, blk_expert, blk_src0, blk_groups, n_real.reshape(1), slot.astype(jnp.int32)


def _diff_lambda_init(layer):
    return 0.8 - 0.6 * math.exp(-0.3 * layer)


def _tile_cols(v, n):
    return jnp.tile(v.reshape(1, -1).astype(F32), (1, n // v.size))


def kernel(x, mem, g_mix, w_in, g_da_q, g_da_k, lam_q1, lam_k1, lam_q2, lam_k2, g_da_sub, hg_lower, g_hg_out,
           w_up_a, w_up_b, w_gate, b_gate, w_out, g_cross, g_mem, w_xq, w_xkv, g_xq, g_xk, w_xo, g_ffn, w_grp,
           b_grp, w_erouter, b_erouter, w_e_gate, w_e_up, w_e_down):
    b, s, d = x.shape
    t = b * s
    depth = g_mix.shape[0]
    da_w = w_up_a.shape[1]
    hg_w = w_up_b.shape[1]
    da_heads = da_w // (2 * DA_HEAD_DIM)
    hg_heads = hg_w // HG_DIM
    x_w = w_xq.shape[2]
    x_heads = x_w // X_HEAD_DIM
    n_groups = w_grp.shape[2]
    n_experts = w_erouter.shape[2]
    moe_blk = min(256, t)

    lower_bounds = jnp.cumsum(jax.nn.softmax(hg_lower.astype(F32), axis=0), axis=0)
    h = x.reshape(t, d)
    mem2 = mem.reshape(b * mem.shape[1], d)
    for l in range(depth):
        lam_init = _diff_lambda_init(l)
        w_in_l = w_in[l]
        n = _rmsnorm(h, g_mix[l], BF16, "rms_mix")
        q_scale = DA_HEAD_DIM ** -0.5 * math.log2(math.e)
        qk_gain = jnp.concatenate([_tile_cols(g_da_q[l], da_w) * q_scale, _tile_cols(g_da_k[l], da_w)], axis=1)
        ff = w_e_gate.shape[3]
        qk, we_gate = _matmul(n, w_in_l, col0=0, ncols=2 * da_w, out_dtype=BF16,
                              epilogue=functools.partial(_ep_group_norm, group=DA_HEAD_DIM),
                              col_extras=[(qk_gain, 0)], side_cast=w_e_gate[l].reshape(n_experts * d, ff),
                              name="proj_qk")
        v = _matmul(n, w_in_l, col0=2 * da_w, ncols=da_w, out_dtype=BF16, name="proj_v")
        hg, we_down = _matmul(n, w_in_l, col0=3 * da_w, ncols=4 * hg_w, out_dtype=F32,
                              side_cast=w_e_down[l].reshape(n_experts * ff, d), name="proj_hg")
        gates, we_up = _matmul(n, w_gate[l], out_dtype=BF16, epilogue=_ep_bias_sigmoid,
                               col_extras=[(b_gate[l].reshape(1, -1).astype(F32), 0)],
                               side_cast=w_e_up[l].reshape(n_experts * d, ff), name="proj_gate")
        lamv = jnp.stack([lam_q1[l], lam_k1[l], lam_q2[l], lam_k2[l]]).astype(F32)
        y_a = _diff_attention_any(qk.reshape(b, s, 2 * da_w), v.reshape(b, s, da_w), lamv,
                                  g_da_sub[l].reshape(1, -1).astype(F32), g_da_q[l], g_da_k[l], q_scale,
                                  da_heads, lam_init)
        y_b = _hgrn2_any(hg.reshape(b, s, 4 * hg_w), lower_bounds[l].reshape(1, hg_w),
                         g_hg_out[l].reshape(1, HG_DIM).astype(F32), hg_heads)
        merged = _merge(y_a.reshape(t, da_w), w_up_a[l], y_b.reshape(t, hg_w), w_up_b[l], gates, "merge")
        h, w_xq_bf = _matmul(merged, w_out[l], out_dtype=F32, epilogue=_ep_residual,
                             full_extras=[(h, 0)], side_cast=w_xq[l], name="mix_out")
        xq, w_xo_bf = _norm_matmul(h, g_cross[l].reshape(1, d).astype(F32), w_xq_bf,
                                   _tile_cols(g_xq[l], x_w) * X_HEAD_DIM ** -0.5, w_xo[l], out_dtype=BF16,
                                   epilogue=functools.partial(_ep_group_norm, group=X_HEAD_DIM), name="xattn_q")
        nm = _rmsnorm(mem2, g_mem[l], BF16, "rms_mem")
        w_xkv_l = w_xkv[l]
        xk = _matmul(nm, w_xkv_l, col0=0, ncols=x_w, out_dtype=BF16,
                     epilogue=functools.partial(_ep_group_norm, group=X_HEAD_DIM),
                     col_extras=[(_tile_cols(g_xk[l], x_w), 0)], name="xattn_k")
        xv = _matmul(nm, w_xkv_l, col0=x_w, ncols=x_w, out_dtype=BF16, name="xattn_v")
        xo = _cross_attention(xq.reshape(b, s, x_w), xk.reshape(b, -1, x_w), xv.reshape(b, -1, x_w), x_heads,
                              "xattn")
        n_logit = -(-(n_groups + n_experts) // LANES) * LANES
        w_r = jnp.zeros((d, n_logit), F32).at[:, :n_groups].set(w_grp[l]).at[:, n_groups:n_groups + n_experts].set(
            w_erouter[l])
        b_r = jnp.zeros((1, n_logit), F32).at[0, :n_groups].set(b_grp[l]).at[0, n_groups:n_groups + n_experts].set(
            b_erouter[l])
        w_r_hi = w_r.astype(BF16)
        w_r_lo = (w_r - w_r_hi.astype(F32)).astype(BF16)
        h, nf, route = _xout_router(xo.reshape(t, x_w), w_xo_bf, h, g_ffn[l].reshape(1, d).astype(F32), w_r_hi,
                                    w_r_lo, b_r, n_groups, n_experts, "xattn_out_router")
        gate = route[:, :TOP_K]
        expert = route[:, TOP_K:2 * TOP_K].astype(jnp.int32).reshape(-1)
        sorted_tok, blk_expert, blk_src0, blk_groups, n_real, slot = _layout(expert, n_experts, moe_blk)
        ys = _experts(nf, sorted_tok, blk_expert, blk_src0, blk_groups, n_real, we_gate.reshape(n_experts, d, ff),
                      we_up.reshape(n_experts, d, ff), we_down.reshape(n_experts, ff, d), moe_blk, "moe_experts")
        h = _combine(h, gate, ys, slot, "moe_combine")
    return h.reshape(b, s, d)
```
